```python
import math
import jax
import jax.numpy as jnp
from jax import lax
import numpy as np

D_MODEL = 2048
BATCH = 2
SEQ = 4096
DEPTH = 2

PLE_DIM = 256
N_BRANCH = 3
BRANCH_W = D_MODEL // 2

W_A = BRANCH_W
S5_GROUP = 16
S5_GROUPS = W_A // S5_GROUP
S5_STATE = 64
S5_DT_MIN = 0.001
S5_DT_MAX = 0.1

W_B = BRANCH_W
HY_ORDER = 2
HY_EMB = 33
HY_FF = 64
HY_DECAY_TARGET = 0.01
HY_FAST_DECAY = 0.3
HY_SLOW_DECAY = 1.5

MLA_HEADS = BRANCH_W // 128
MLA_NOPE = 128
MLA_ROPE = 64
MLA_V = 128
MLA_Q_LORA = D_MODEL // 4
MLA_KV_LORA = D_MODEL // 4
W_C = MLA_HEADS * MLA_V
ROPE_BASE = 10000.0
Q_BLOCK = 128

LN_EPS = 1e-5
RMS_EPS = 1e-6
DEEPNORM_ALPHA = (2 * DEPTH) ** 0.25
DEEPNORM_BETA = (8 * DEPTH) ** -0.25

SPLIT_SIZES = (W_A, W_A, 3 * W_B, W_B, MLA_Q_LORA, MLA_KV_LORA, MLA_ROPE, W_C,
               N_BRANCH * D_MODEL, D_MODEL)
N_IN = sum(SPLIT_SIZES)

kernel_name = "hybrid_s5_hyena_mla_deepnorm_encoder"


def layer_norm(x, g, b):
    xf = x.astype(jnp.float32)
    mu = jnp.mean(xf, axis=-1, keepdims=True)
    var = jnp.mean(jnp.square(xf - mu), axis=-1, keepdims=True)
    return ((xf - mu) * lax.rsqrt(var + LN_EPS) * g.astype(jnp.float32) + b.astype(jnp.float32)).astype(x.dtype)


def rms_norm(x, g):
    xf = x.astype(jnp.float32)
    ms = jnp.mean(jnp.square(xf), axis=-1, keepdims=True)
    return (xf * lax.rsqrt(ms + RMS_EPS) * g.astype(jnp.float32)).astype(x.dtype)


def apply_rope(x, pos):
    half = x.shape[-1] // 2
    inv = ROPE_BASE ** (-jnp.arange(half, dtype=jnp.float32) / half)
    ang = pos.astype(jnp.float32)[:, :, None, None] * inv
    cos, sin = jnp.cos(ang), jnp.sin(ang)
    xf = x.astype(jnp.float32)
    x1, x2 = xf[..., :half], xf[..., half:]
    return jnp.concatenate([x1 * cos - x2 * sin, x1 * sin + x2 * cos], axis=-1).astype(x.dtype)


def short_conv3(u, w, b):
    up = jnp.pad(u, ((0, 0), (1, 1), (0, 0)))
    return up[:, :-2] * w[0] + up[:, 1:-1] * w[1] + up[:, 2:] * w[2] + b


def _linear_recurrence_op(left, right):
    a_l, b_l = left
    a_r, b_r = right
    return a_l * a_r, a_r * b_l + b_r


def s5_mixer(u, lam_re, lam_im, log_dt, b_re, b_im, c_re, c_im, d, w_glu, b_glu):
    f32 = jnp.float32
    bsz, L, _ = u.shape
    uf = u.astype(f32).reshape(bsz, L, S5_GROUPS, S5_GROUP)
    y = uf * d.astype(f32).reshape(S5_GROUPS, S5_GROUP)
    for direction in range(2):
        lam = lax.complex(lam_re[direction].astype(f32), lam_im[direction].astype(f32))
        dt = jnp.exp(log_dt[direction].astype(f32))[:, None]
        lam_bar = jnp.exp(lam * dt)
        b_mat = lax.complex(b_re[direction].astype(f32), b_im[direction].astype(f32))
        b_bar = ((lam_bar - 1.0) / lam)[:, :, None] * b_mat
        bu = jnp.einsum('blgh,gph->blgp', uf, b_bar)
        a = jnp.broadcast_to(lam_bar, bu.shape)
        _, state = lax.associative_scan(_linear_recurrence_op, (a, bu), axis=1,
                                        reverse=(direction == 1))
        c_mat = lax.complex(c_re[direction].astype(f32), c_im[direction].astype(f32))
        y = y + jnp.real(jnp.einsum('blgp,ghp->blgh', state, c_mat))
    y = jax.nn.gelu(y.reshape(bsz, L, W_A))
    y = y * jax.nn.sigmoid(y @ w_glu.astype(f32) + b_glu.astype(f32))
    return y


def hyena_filters(L, w1, b1, w2, b2, freq, w3, b3):
    f32 = jnp.float32
    n_ch = HY_ORDER * W_B
    bands = (HY_EMB - 1) // 2
    t = jnp.linspace(0.0, 1.0, L, dtype=f32)[:, None]
    w = 2.0 * math.pi * jnp.arange(L, dtype=f32)[:, None] / L
    f = jnp.linspace(1e-4, bands - 1, bands, dtype=f32)[None, :]
    feats = jnp.concatenate([t, jnp.cos(f * w), -jnp.sin(f * w)], axis=-1)
    h = jnp.sin(freq[0].astype(f32) * (feats @ w1.astype(f32) + b1.astype(f32)))
    h = jnp.sin(freq[1].astype(f32) * (h @ w2.astype(f32) + b2.astype(f32)))
    h = (h @ w3.astype(f32) + b3.astype(f32)).reshape(L, 2, n_ch)
    deltas = jnp.linspace(math.log(HY_DECAY_TARGET) / HY_SLOW_DECAY,
                          math.log(HY_DECAY_TARGET) / HY_FAST_DECAY, n_ch, dtype=f32)
    h = h * jnp.exp(-t[:, :, None] * jnp.abs(deltas))
    h_full = jnp.concatenate([h[:, 0], jnp.zeros((1, n_ch), f32), h[1:, 1][::-1]], axis=0)
    h_full = h_full / jnp.sum(jnp.abs(h_full), axis=0, keepdims=True)
    return jnp.fft.rfft(h_full, axis=0).reshape(L + 1, HY_ORDER, W_B)


def hyena_mixer(u3, conv_w, conv_b, filt_f, bias):
    f32 = jnp.float32
    L = u3.shape[1]
    uc = short_conv3(u3.astype(f32), conv_w.astype(f32), conv_b.astype(f32))
    v, x1, x2 = jnp.split(uc, 3, axis=-1)
    z = v
    for n, gate in enumerate((x1, x2)):
        z_f = jnp.fft.rfft(z, n=2 * L, axis=1)
        conv = jnp.fft.irfft(z_f * filt_f[:, n], n=2 * L, axis=1)[:, :L]
        z = gate * (conv + bias[n].astype(f32) * z)
    return z


def mla_mixer(c_q, c_kv, k_r, pos, q_norm_g, w_uq, kv_norm_g, w_ukv):
    f32 = jnp.float32
    bsz, L, _ = c_q.shape
    dqk = MLA_NOPE + MLA_ROPE
    q = (rms_norm(c_q, q_norm_g) @ w_uq).reshape(bsz, L, MLA_HEADS, dqk)
    q = jnp.concatenate([q[..., :MLA_NOPE], apply_rope(q[..., MLA_NOPE:], pos)], axis=-1)
    kv = (rms_norm(c_kv, kv_norm_g) @ w_ukv).reshape(bsz, L, MLA_HEADS, MLA_NOPE + MLA_V)
    k_nope, v = kv[..., :MLA_NOPE], kv[..., MLA_NOPE:]
    k_rope = apply_rope(k_r[:, :, None, :], pos)
    k = jnp.concatenate([k_nope, jnp.broadcast_to(k_rope, (bsz, L, MLA_HEADS, MLA_ROPE))], axis=-1)
    k = k.astype(f32)
    v = v.astype(f32)
    scale = dqk ** -0.5
    n_blocks = L // Q_BLOCK
    q_blocks = q.astype(f32).reshape(bsz, n_blocks, Q_BLOCK, MLA_HEADS, dqk).transpose(1, 0, 2, 3, 4)

    def attend(qb):
        s = jnp.einsum('bqhd,bkhd->bhqk', qb, k) * scale
        pr = jax.nn.softmax(s, axis=-1)
        return jnp.einsum('bhqk,bkhd->bqhd', pr, v)

    o = lax.map(attend, q_blocks)
    return o.transpose(1, 0, 2, 3, 4).reshape(bsz, L, W_C)


def setup_inputs(seed: int = 0) -> dict:
    key = jax.random.key(seed)
    ks = iter(jax.random.split(key, 48))
    f32 = jnp.float32

    def nrm(shape, scale):
        return jax.random.normal(next(ks), shape, f32) * scale

    G, P, H = S5_GROUPS, S5_STATE, S5_GROUP
    x = nrm((BATCH, SEQ, D_MODEL), 1.0)
    p = nrm((DEPTH, BATCH, SEQ, PLE_DIM), 1.0)
    offsets = jax.random.randint(next(ks), (BATCH, 1), 0, 1024, dtype=jnp.int32)
    positions = jnp.arange(SEQ, dtype=jnp.int32)[None, :] + offsets
    w_in = nrm((DEPTH, D_MODEL, N_IN), D_MODEL ** -0.5)
    n_idx = jnp.arange(P, dtype=f32)
    s5_lambda_re = -0.5 + nrm((DEPTH, 2, G, P), 0.01)
    s5_lambda_im = math.pi * n_idx + nrm((DEPTH, 2, G, P), 0.01)
    s5_log_dt = jax.random.uniform(next(ks), (DEPTH, 2, G), f32,
                                   math.log(S5_DT_MIN), math.log(S5_DT_MAX))
    s5_b_re = nrm((DEPTH, 2, G, P, H), (2 * H) ** -0.5)
    s5_b_im = nrm((DEPTH, 2, G, P, H), (2 * H) ** -0.5)
    s5_c_re = nrm((DEPTH, 2, G, H, P), P ** -0.5)
    s5_c_im = nrm((DEPTH, 2, G, H, P), P ** -0.5)
    s5_d = nrm((DEPTH, W_A), 1.0)
    s5_w_glu = nrm((DEPTH, W_A, W_A), W_A ** -0.5)
    s5_b_glu = nrm((DEPTH, W_A), 0.01)
    hy_conv_w = nrm((DEPTH, 3, 3 * W_B), 3 ** -0.5)
    hy_conv_b = nrm((DEPTH, 3 * W_B), 0.01)
    hy_w1 = nrm((DEPTH, HY_EMB, HY_FF), HY_EMB ** -0.5)
    hy_b1 = nrm((DEPTH, HY_FF), 0.01)
    hy_w2 = nrm((DEPTH, HY_FF, HY_FF), HY_FF ** -0.5)
    hy_b2 = nrm((DEPTH, HY_FF), 0.01)
    hy_freq = 1.0 + nrm((DEPTH, 2, HY_FF), 0.01)
    hy_w3 = nrm((DEPTH, HY_FF, 2 * HY_ORDER * W_B), HY_FF ** -0.5)
    hy_b3 = nrm((DEPTH, 2 * HY_ORDER * W_B), 0.01)
    hy_bias = nrm((DEPTH, HY_ORDER, W_B), 1.0)
    mla_q_norm = 1.0 + nrm((DEPTH, MLA_Q_LORA), 0.01)
    mla_w_uq = nrm((DEPTH, MLA_Q_LORA, MLA_HEADS * (MLA_NOPE + MLA_ROPE)), MLA_Q_LORA ** -0.5)
    mla_kv_norm = 1.0 + nrm((DEPTH, MLA_KV_LORA), 0.01)
    mla_w_ukv = nrm((DEPTH, MLA_KV_LORA, MLA_HEADS * (MLA_NOPE + MLA_V)), MLA_KV_LORA ** -0.5)
    w_lift = nrm((DEPTH, N_BRANCH, BRANCH_W, D_MODEL), BRANCH_W ** -0.5 * DEEPNORM_BETA)
    w_out = nrm((DEPTH, D_MODEL, D_MODEL), D_MODEL ** -0.5 * DEEPNORM_BETA)
    w_ple = nrm((DEPTH, PLE_DIM, D_MODEL), PLE_DIM ** -0.5)
    ln_g = 1.0 + nrm((DEPTH, D_MODEL), 0.01)
    ln_b = nrm((DEPTH, D_MODEL), 0.01)
    return {"x": x, "p": p, "positions": positions, "w_in": w_in,
            "s5_lambda_re": s5_lambda_re, "s5_lambda_im": s5_lambda_im, "s5_log_dt": s5_log_dt,
            "s5_b_re": s5_b_re, "s5_b_im": s5_b_im, "s5_c_re": s5_c_re, "s5_c_im": s5_c_im,
            "s5_d": s5_d, "s5_w_glu": s5_w_glu, "s5_b_glu": s5_b_glu,
            "hy_conv_w": hy_conv_w, "hy_conv_b": hy_conv_b, "hy_w1": hy_w1, "hy_b1": hy_b1,
            "hy_w2": hy_w2, "hy_b2": hy_b2, "hy_freq": hy_freq, "hy_w3": hy_w3, "hy_b3": hy_b3,
            "hy_bias": hy_bias, "mla_q_norm": mla_q_norm, "mla_w_uq": mla_w_uq,
            "mla_kv_norm": mla_kv_norm, "mla_w_ukv": mla_w_ukv, "w_lift": w_lift, "w_out": w_out,
            "w_ple": w_ple, "ln_g": ln_g, "ln_b": ln_b}


def reference(x, p, positions, w_in, s5_lambda_re, s5_lambda_im, s5_log_dt, s5_b_re, s5_b_im,
              s5_c_re, s5_c_im, s5_d, s5_w_glu, s5_b_glu, hy_conv_w, hy_conv_b, hy_w1, hy_b1,
              hy_w2, hy_b2, hy_freq, hy_w3, hy_b3, hy_bias, mla_q_norm, mla_w_uq, mla_kv_norm,
              mla_w_ukv, w_lift, w_out, w_ple, ln_g, ln_b):
    bsz, L, _ = x.shape
    split_points = [int(s) for s in np.cumsum(SPLIT_SIZES)[:-1]]
    for i in range(DEPTH):
        proj = x @ w_in[i]
        (a_x, a_z, b_u, b_z, c_q, c_kv, c_kr, c_z,
         gate_logits, ple_logits) = jnp.split(proj, split_points, axis=-1)
        y_a = s5_mixer(a_x, s5_lambda_re[i], s5_lambda_im[i], s5_log_dt[i], s5_b_re[i], s5_b_im[i],
                       s5_c_re[i], s5_c_im[i], s5_d[i], s5_w_glu[i], s5_b_glu[i])
        y_a = y_a.astype(x.dtype) * jax.nn.silu(a_z)
        filt_f = hyena_filters(L, hy_w1[i], hy_b1[i], hy_w2[i], hy_b2[i], hy_freq[i], hy_w3[i], hy_b3[i])
        y_b = hyena_mixer(b_u, hy_conv_w[i], hy_conv_b[i], filt_f, hy_bias[i])
        y_b = y_b.astype(x.dtype) * jax.nn.silu(b_z)
        y_c = mla_mixer(c_q, c_kv, c_kr, positions, mla_q_norm[i], mla_w_uq[i],
                        mla_kv_norm[i], mla_w_ukv[i])
        y_c = y_c.astype(x.dtype) * jax.nn.silu(c_z)
        branches = jnp.stack([y_a, y_b, y_c], axis=2)
        lifted = jnp.einsum('blnc,ncd->blnd', branches, w_lift[i])
        gates = jax.nn.sigmoid(gate_logits.reshape(bsz, L, N_BRANCH, D_MODEL))
        mixed = jnp.sum(gates * lifted, axis=2) @ w_out[i]
        ple = (p[i] @ w_ple[i]) * jax.nn.sigmoid(ple_logits)
        x = layer_norm(DEEPNORM_ALPHA * x + mixed + ple, ln_g[i], ln_b[i])
    return x
```

```python
import functools
import math

import numpy as np
import jax
import jax.numpy as jnp
from jax import lax
from jax.experimental import pallas as pl
from jax.experimental.pallas import tpu as pltpu

F32 = jnp.float32
BF16 = jnp.bfloat16

D_MODEL = 2048
PLE_DIM = 256
N_BRANCH = 3
BRANCH_W = 1024

S5_GROUP = 16
S5_GROUPS = BRANCH_W // S5_GROUP
S5_STATE = 64
S5_CHUNK = 16
S5_ROW = S5_CHUNK * S5_GROUP

HY_EMB = 33
HY_FF = 64
HY_PAD = 128
HY_DECAY_TARGET = 0.01
HY_FAST_DECAY = 0.3
HY_SLOW_DECAY = 1.5
DFT_ROWS = 64

MLA_HEADS = 8
MLA_NOPE = 128
MLA_ROPE = 64
MLA_V = 128
MLA_LORA = 512
MLA_QK_PAD = 256
ROPE_BASE = 10000.0

LN_EPS = 1e-5
RMS_EPS = 1e-6
DEPTH = 2
DEEPNORM_ALPHA = (2 * DEPTH) ** 0.25

COL_AX = 0
COL_AZ = 1024
COL_BZ = 2048
COL_CZ = 3072
COL_PLE = 4096
COL_GATE = 6144
COL_BU = 12288
COL_CQ = 15360
COL_CKV = 15872
COL_CKR = 16384
PROJ_N = 16896
PROJ_TN = 512
_SILU_TILES = (COL_AZ // PROJ_TN, COL_PLE // PROJ_TN)
_SIGM_TILES = (COL_PLE // PROJ_TN, COL_BU // PROJ_TN)

VMEM_LIMIT = 56 * 1024 * 1024


def _params(*sem):
    return pltpu.CompilerParams(dimension_semantics=sem, vmem_limit_bytes=VMEM_LIMIT)


def _proj_kernel(x_ref, w_ref, o_ref):
    j = pl.program_id(1)
    acc = jnp.dot(x_ref[...], w_ref[...], preferred_element_type=F32)
    is_silu = (j >= _SILU_TILES[0]) & (j < _SILU_TILES[1])
    is_sigm = (j >= _SIGM_TILES[0]) & (j < _SIGM_TILES[1])

    @pl.when(is_silu)
    def _():
        o_ref[...] = acc * jax.nn.sigmoid(acc)

    @pl.when(is_sigm)
    def _():
        o_ref[...] = jax.nn.sigmoid(acc)

    @pl.when(jnp.logical_not(is_silu | is_sigm))
    def _():
        o_ref[...] = acc


def _proj(xb, w_all):
    m, k = xb.shape
    tm = min(1024, m)
    return pl.pallas_call(
        _proj_kernel,
        out_shape=jax.ShapeDtypeStruct((m, PROJ_N), F32),
        grid=(m // tm, PROJ_N // PROJ_TN),
        in_specs=[pl.BlockSpec((tm, k), lambda i, j: (i, 0)),
                  pl.BlockSpec((k, PROJ_TN), lambda i, j: (0, j))],
        out_specs=pl.BlockSpec((tm, PROJ_TN), lambda i, j: (i, j)),
        compiler_params=_params("parallel", "arbitrary"),
        name="proj",
    )(xb, w_all)


def _proj_weights(w_in):
    sizes = (1024, 1024, 3072, 1024, 512, 512, 64, 1024, 6144, 2048)
    offs = np.concatenate([[0], np.cumsum(sizes)])
    a_x, a_z, b_u, b_z, c_q, c_kv, c_kr, c_z, gate, ple = [
        w_in[:, int(offs[i]):int(offs[i + 1])] for i in range(len(sizes))]
    pad = jnp.zeros((w_in.shape[0], PROJ_N - COL_CKR - MLA_ROPE), w_in.dtype)
    return jnp.concatenate([a_x, a_z, b_z, c_z, ple, gate, b_u, c_q, c_kv, c_kr, pad],
                           axis=1).astype(BF16)


def _s5_matrices(lam_re, lam_im, log_dt, b_re, b_im, c_re, c_im, d):
    hi = lax.Precision.HIGHEST
    t_n, g_n, p_n, h_n = S5_CHUNK, S5_GROUPS, S5_STATE, S5_GROUP
    lam = lax.complex(lam_re.astype(F32), lam_im.astype(F32))
    dt = jnp.exp(log_dt.astype(F32))[..., None]
    lam_dt = lam * dt
    lam_bar = jnp.exp(lam_dt)
    steps = jnp.arange(t_n + 1, dtype=F32)
    lamp = jnp.exp(lam_dt[None] * steps[:, None, None, None])
    bbar = ((lam_bar - 1.0) / lam)[..., None] * lax.complex(b_re.astype(F32), b_im.astype(F32))
    cm = lax.complex(c_re.astype(F32), c_im.astype(F32))
    kk = jnp.real(jnp.einsum('dgop,tdgp,dgpi->tdgoi', cm, lamp[:t_n], bbar, precision=hi))
    kf, kb = kk[:, 0], kk[:, 1]
    dmat = d.astype(F32).reshape(g_n, h_n)[:, :, None] * jnp.eye(h_n, dtype=F32)
    kfull = jnp.concatenate([kb[1:][::-1], (kf[0] + kb[0] + dmat)[None], kf[1:]], axis=0)
    idx = (np.arange(t_n)[None, :] - np.arange(t_n)[:, None]) + (t_n - 1)
    m_intra = kfull[idx]
    m_intra = m_intra.transpose(2, 0, 4, 1, 3).reshape(g_n, S5_ROW, S5_ROW)
    parity = jnp.asarray(np.eye(2, dtype=np.float32)[np.arange(g_n) % 2])
    sf = lamp[:t_n][::-1, 0][..., None] * bbar[0][None]
    sb = lamp[:t_n, 1][..., None] * bbar[1][None]
    parts = jnp.stack([jnp.real(sf), jnp.imag(sf), jnp.real(sb), jnp.imag(sb)])
    parts = parts.transpose(2, 1, 4, 0, 3)
    m_state = parts[:, :, :, :, None, :] * parity[:, None, None, None, :, None]
    m_state = m_state.reshape(g_n, S5_ROW, 4 * 2 * p_n)
    of = cm[0][None] * lamp[1:, 0][:, :, None, :]
    ob = cm[1][None] * lamp[1:][::-1, 1][:, :, None, :]
    outs = jnp.stack([jnp.real(of), -jnp.imag(of), jnp.real(ob), -jnp.imag(ob)])
    outs = outs.transpose(2, 0, 4, 1, 3)
    m_out = outs[:, :, None] * parity[:, None, :, None, None, None]
    m_out = m_out.reshape(g_n, 4 * 2 * p_n, S5_ROW)
    a_t = lamp[t_n].reshape(2, 1, g_n * p_n)
    a_rows = (jnp.real(a_t[0]), jnp.imag(a_t[0]), jnp.real(a_t[1]), jnp.imag(a_t[1]))
    return m_intra.astype(BF16), m_state.astype(BF16), m_out.astype(BF16), a_rows


def _s5_state_kernel(u_ref, ms_ref, fr_ref, fi_ref, br_ref, bi_ref):
    s = (jnp.dot(u_ref[0], ms_ref[0], preferred_element_type=F32)
         + jnp.dot(u_ref[1], ms_ref[1], preferred_element_type=F32))
    fr_ref[...] = s[:, 0:128]
    fi_ref[...] = s[:, 128:256]
    br_ref[...] = s[:, 256:384]
    bi_ref[...] = s[:, 384:512]


def _s5_scan_kernel(xfr, xfi, xbr, xbi, afr, afi, abr, abi, ofr, ofi, obr, obi):
    bsz, n_chunks, width = xfr.shape
    a_fr, a_fi, a_br, a_bi = afr[...], afi[...], abr[...], abi[...]
    zero = jnp.zeros((1, width), F32)

    def body(c, carry):
        cb = n_chunks - 1 - c
        new = []
        for b in range(bsz):
            s_fr, s_fi, s_br, s_bi = carry[b]
            ofr[b, pl.ds(c, 1), :] = s_fr
            ofi[b, pl.ds(c, 1), :] = s_fi
            obr[b, pl.ds(cb, 1), :] = s_br
            obi[b, pl.ds(cb, 1), :] = s_bi
            n_fr = a_fr * s_fr - a_fi * s_fi + xfr[b, pl.ds(c, 1), :]
            n_fi = a_fr * s_fi + a_fi * s_fr + xfi[b, pl.ds(c, 1), :]
            n_br = a_br * s_br - a_bi * s_bi + xbr[b, pl.ds(cb, 1), :]
            n_bi = a_br * s_bi + a_bi * s_br + xbi[b, pl.ds(cb, 1), :]
            new.append((n_fr, n_fi, n_br, n_bi))
        return tuple(new)

    lax.fori_loop(0, n_chunks, body, tuple((zero,) * 4 for _ in range(bsz)))


def _s5_out_kernel(u_ref, mi_ref, mo_ref, fr_ref, fi_ref, br_ref, bi_ref, y_ref):
    st = jnp.concatenate([fr_ref[...], fi_ref[...], br_ref[...], bi_ref[...]], axis=1).astype(BF16)
    for e in range(2):
        y = (jnp.dot(u_ref[e], mi_ref[e], preferred_element_type=F32)
             + jnp.dot(st, mo_ref[e], preferred_element_type=F32))
        y_ref[e] = jax.nn.gelu(y)


def _s5_glu_kernel(g_ref, w_ref, b_ref, z_ref, o_ref):
    g = g_ref[...]
    acc = jnp.dot(g.astype(BF16), w_ref[...], preferred_element_type=F32) + b_ref[...]
    o_ref[...] = (g * jax.nn.sigmoid(acc) * z_ref[...]).astype(o_ref.dtype)


def _s5_branch(proj, bsz, seq, mats, w_glu, b_glu):
    m_intra, m_state, m_out, a_rows = mats
    g_n, t_n, h_n, p_n = S5_GROUPS, S5_CHUNK, S5_GROUP, S5_STATE
    n_chunks = seq // t_n
    rows = bsz * n_chunks
    m = bsz * seq
    u = proj[:, COL_AX:COL_AX + BRANCH_W].reshape(bsz, n_chunks, t_n, g_n, h_n)
    u = u.transpose(3, 0, 1, 2, 4).reshape(g_n, rows, S5_ROW).astype(BF16)
    st_shape = jax.ShapeDtypeStruct((rows, g_n * p_n), F32)
    st_spec = pl.BlockSpec((rows, 128), lambda j: (0, j))
    u_spec = pl.BlockSpec((2, rows, S5_ROW), lambda j: (j, 0, 0))
    local = pl.pallas_call(
        _s5_state_kernel,
        out_shape=(st_shape,) * 4,
        grid=(g_n // 2,),
        in_specs=[u_spec, pl.BlockSpec((2, S5_ROW, 8 * p_n), lambda j: (j, 0, 0))],
        out_specs=(st_spec,) * 4,
        compiler_params=_params("parallel"),
        name="s5_state",
    )(u, m_state)
    local = [a.reshape(bsz, n_chunks, g_n * p_n) for a in local]
    cw = 1024
    x_spec = pl.BlockSpec((bsz, n_chunks, cw), lambda j: (0, 0, j))
    a_spec = pl.BlockSpec((1, cw), lambda j: (0, j))
    carried = pl.pallas_call(
        _s5_scan_kernel,
        out_shape=(jax.ShapeDtypeStruct((bsz, n_chunks, g_n * p_n), F32),) * 4,
        grid=(g_n * p_n // cw,),
        in_specs=[x_spec] * 4 + [a_spec] * 4,
        out_specs=(x_spec,) * 4,
        compiler_params=_params("parallel"),
        name="s5_scan",
    )(*local, *a_rows)
    carried = [a.reshape(rows, g_n * p_n) for a in carried]
    y = pl.pallas_call(
        _s5_out_kernel,
        out_shape=jax.ShapeDtypeStruct((g_n, rows, S5_ROW), F32),
        grid=(g_n // 2,),
        in_specs=[u_spec,
                  pl.BlockSpec((2, S5_ROW, S5_ROW), lambda j: (j, 0, 0)),
                  pl.BlockSpec((2, 8 * p_n, S5_ROW), lambda j: (j, 0, 0))] + [st_spec] * 4,
        out_specs=pl.BlockSpec((2, rows, S5_ROW), lambda j: (j, 0, 0)),
        compiler_params=_params("parallel"),
        name="s5_out",
    )(u, m_intra, m_out, *carried)
    y = y.reshape(g_n, bsz, n_chunks, t_n, h_n).transpose(1, 2, 3, 0, 4).reshape(m, BRANCH_W)
    tm = min(512, m)
    return pl.pallas_call(
        _s5_glu_kernel,
        out_shape=jax.ShapeDtypeStruct((m, BRANCH_W), BF16),
        grid=(m // tm,),
        in_specs=[pl.BlockSpec((tm, BRANCH_W), lambda i: (i, 0)),
                  pl.BlockSpec((BRANCH_W, BRANCH_W), lambda i: (0, 0)),
                  pl.BlockSpec((1, BRANCH_W), lambda i: (0, 0)),
                  pl.BlockSpec((tm, BRANCH_W), lambda i: (i, COL_AZ // BRANCH_W))],
        out_specs=pl.BlockSpec((tm, BRANCH_W), lambda i: (i, 0)),
        compiler_params=_params("parallel"),
        name="s5_glu",
    )(y, w_glu.astype(BF16), b_glu.astype(F32).reshape(1, BRANCH_W), proj)


def _dft_tables(seq):
    n = 2 * seq
    mm = np.arange(seq, dtype=np.int64)
    k1 = np.arange(seq // DFT_ROWS, dtype=np.int64)[:, None] * DFT_ROWS
    k0 = np.arange(DFT_ROWS, dtype=np.int64)[:, None]
    ang_a = 2.0 * np.pi * ((k1 * mm) % n).astype(np.float64) / n
    ang_b = 2.0 * np.pi * ((k0 * mm) % n).astype(np.float64) / n
    return tuple(jnp.asarray(t, F32) for t in (np.cos(ang_a), np.sin(ang_a), np.cos(ang_b), np.sin(ang_b)))


def _dft_gen_kernel(ac_ref, as_ref, bc_ref, bs_ref, c_ref, s1_ref, s2_ref):
    i = pl.program_id(0)
    a_c = ac_ref[pl.ds(i, 1), :]
    a_s = as_ref[pl.ds(i, 1), :]
    b_c, b_s = bc_ref[...], bs_ref[...]
    cos_t = a_c * b_c - a_s * b_s
    sin_t = a_s * b_c + a_c * b_s
    rows = lax.broadcasted_iota(jnp.int32, cos_t.shape, 0) + i * DFT_ROWS
    cols = lax.broadcasted_iota(jnp.int32, cos_t.shape, 1)
    alt_cols = jnp.where((cols & 1) == 0, 1.0, -1.0).astype(F32)
    alt_rows = jnp.where((rows & 1) == 0, 1.0, -1.0).astype(F32)
    c_ref[...] = cos_t.astype(BF16)
    s1_ref[...] = jnp.where(rows == 0, alt_cols, sin_t).astype(BF16)
    s2_ref[...] = jnp.where(cols == 0, alt_rows, sin_t).astype(BF16)


def _dft_matrices(seq):
    tabs = _dft_tables(seq)
    n_steps = seq // DFT_ROWS
    tab_spec = pl.BlockSpec(tabs[0].shape, lambda i: (0, 0))
    b_spec = pl.BlockSpec((DFT_ROWS, seq), lambda i: (0, 0))
    o_spec = pl.BlockSpec((DFT_ROWS, seq), lambda i: (i, 0))
    return pl.pallas_call(
        _dft_gen_kernel,
        out_shape=(jax.ShapeDtypeStruct((seq, seq), BF16),) * 3,
        grid=(n_steps,),
        in_specs=[tab_spec, tab_spec, b_spec, b_spec],
        out_specs=(o_spec,) * 3,
        compiler_params=_params("parallel"),
        name="dft_gen",
    )(*tabs)


def _hy_filter_kernel(feat_ref, w1_ref, b1_ref, w2_ref, b2_ref, f0_ref, f1_ref,
                      w3p_ref, w3n_ref, b3p_ref, b3n_ref, dl_ref, t_ref,
                      hs_ref, hd_ref, nyq_ref):
    hi = lax.Precision.HIGHEST
    h = jnp.sin(f0_ref[...] * (jnp.dot(feat_ref[...], w1_ref[...], precision=hi,
                                       preferred_element_type=F32) + b1_ref[...]))
    h = jnp.sin(f1_ref[...] * (jnp.dot(h, w2_ref[...], precision=hi,
                                       preferred_element_type=F32) + b2_ref[...]))
    win = jnp.exp(-t_ref[...] * jnp.abs(dl_ref[...]))
    hpos = (jnp.dot(h, w3p_ref[...], precision=hi, preferred_element_type=F32) + b3p_ref[...]) * win
    hneg = (jnp.dot(h, w3n_ref[...], precision=hi, preferred_element_type=F32) + b3n_ref[...]) * win
    rows = lax.broadcasted_iota(jnp.int32, hpos.shape, 0)
    hneg = jnp.where(rows == 0, 0.0, hneg)
    norm = (jnp.sum(jnp.abs(hpos), axis=0, keepdims=True)
            + jnp.sum(jnp.abs(hneg), axis=0, keepdims=True))
    hsum = (hpos + hneg) / norm
    hdiff = (hpos - hneg) / norm
    alt = jnp.where((rows & 1) == 0, 1.0, -1.0).astype(F32)
    nyq_ref[...] = jnp.sum(alt * hsum, axis=0, keepdims=True)
    hs_ref[...] = hsum.astype(BF16)
    hd_ref[...] = hdiff.astype(BF16)


def _hy_filter_taps(seq, w1, b1, w2, b2, freq, w3, b3):
    n_ch = 2 * BRANCH_W
    bands = (HY_EMB - 1) // 2
    t = jnp.linspace(0.0, 1.0, seq, dtype=F32)[:, None]
    w = 2.0 * math.pi * jnp.arange(seq, dtype=F32)[:, None] / seq
    f = jnp.linspace(1e-4, bands - 1, bands, dtype=F32)[None, :]
    feats = jnp.concatenate([t, jnp.cos(f * w), -jnp.sin(f * w),
                             jnp.zeros((seq, HY_PAD - HY_EMB), F32)], axis=-1)
    deltas = jnp.linspace(math.log(HY_DECAY_TARGET) / HY_SLOW_DECAY,
                          math.log(HY_DECAY_TARGET) / HY_FAST_DECAY, n_ch, dtype=F32)[None, :]

    def pad2(a, r, c):
        a = a.astype(F32)
        return jnp.pad(a, ((0, r - a.shape[0]), (0, c - a.shape[1])))

    w1p = pad2(w1, HY_PAD, HY_PAD)
    w2p = pad2(w2, HY_PAD, HY_PAD)
    b1p = pad2(b1[None], 1, HY_PAD)
    b2p = pad2(b2[None], 1, HY_PAD)
    f0p = pad2(freq[0][None], 1, HY_PAD)
    f1p = pad2(freq[1][None], 1, HY_PAD)
    w3p = pad2(w3, HY_PAD, 2 * n_ch)
    b3r = b3.astype(F32)[None]
    tn = 256
    nt = n_ch // tn
    full = lambda shape: pl.BlockSpec(shape, lambda j: (0, 0))
    return pl.pallas_call(
        _hy_filter_kernel,
        out_shape=(jax.ShapeDtypeStruct((seq, n_ch), BF16), jax.ShapeDtypeStruct((seq, n_ch), BF16),
                   jax.ShapeDtypeStruct((1, n_ch), F32)),
        grid=(nt,),
        in_specs=[full((seq, HY_PAD)), full((HY_PAD, HY_PAD)), full((1, HY_PAD)),
                  full((HY_PAD, HY_PAD)), full((1, HY_PAD)), full((1, HY_PAD)), full((1, HY_PAD)),
                  pl.BlockSpec((HY_PAD, tn), lambda j: (0, j)),
                  pl.BlockSpec((HY_PAD, tn), lambda j: (0, j + nt)),
                  pl.BlockSpec((1, tn), lambda j: (0, j)),
                  pl.BlockSpec((1, tn), lambda j: (0, j + nt)),
                  pl.BlockSpec((1, tn), lambda j: (0, j)),
                  full((seq, 1))],
        out_specs=(pl.BlockSpec((seq, tn), lambda j: (0, j)), pl.BlockSpec((seq, tn), lambda j: (0, j)),
                   pl.BlockSpec((1, tn), lambda j: (0, j))),
        compiler_params=_params("parallel"),
        name="hy_filter",
    )(feats, w1p, b1p, w2p, b2p, f0p, f1p, w3p, w3p, b3r, b3r, deltas, t)


def _hy_spectrum_kernel(c_ref, s_ref, hs_ref, hd_ref, hc_out, hso_out, *, inv_n):
    i = pl.program_id(1)
    zc = jnp.dot(c_ref[...], hs_ref[...], preferred_element_type=F32)
    zs = jnp.dot(s_ref[...], hd_ref[...], preferred_element_type=F32)
    rows = lax.broadcasted_iota(jnp.int32, zc.shape, 0) + i * zc.shape[0]
    first = rows == 0
    hc_out[...] = zc * jnp.where(first, inv_n, 2.0 * inv_n)
    hso_out[...] = jnp.where(first, 0.0, zs * (2.0 * inv_n))


def _hy_spectrum(cm, s1, hsum, hdiff):
    seq, n_ch = hsum.shape
    tm = min(512, seq)
    tn = min(512, n_ch)
    w_spec = pl.BlockSpec((tm, seq), lambda j, i: (i, 0))
    h_spec = pl.BlockSpec((seq, tn), lambda j, i: (0, j))
    o_spec = pl.BlockSpec((tm, tn), lambda j, i: (i, j))
    return pl.pallas_call(
        functools.partial(_hy_spectrum_kernel, inv_n=1.0 / (2 * seq)),
        out_shape=(jax.ShapeDtypeStruct((seq, n_ch), F32),) * 2,
        grid=(n_ch // tn, seq // tm),
        in_specs=[w_spec, w_spec, h_spec, h_spec],
        out_specs=(o_spec, o_spec),
        compiler_params=_params("parallel", "arbitrary"),
        name="hy_spectrum",
    )(cm, s1, hsum, hdiff)


def _hy_conv3_kernel(u_ref, w_ref, b_ref, o_ref, vb_ref):
    u = u_ref[...]
    n = u.shape[0]
    rows = lax.broadcasted_iota(jnp.int32, u.shape, 0)
    prev = jnp.where(rows == 0, 0.0, pltpu.roll(u, 1, 0))
    nxt = jnp.where(rows == n - 1, 0.0, pltpu.roll(u, n - 1, 0))
    w = w_ref[...]
    out = prev * w[0:1] + u * w[1:2] + nxt * w[2:3] + b_ref[...]
    o_ref[...] = out
    vb_ref[...] = out.astype(BF16)


def _hy_conv3(proj, bsz, seq, conv_w, conv_b):
    m = bsz * seq
    tn = 256
    nb = 3 * BRANCH_W // tn
    return pl.pallas_call(
        _hy_conv3_kernel,
        out_shape=(jax.ShapeDtypeStruct((m, 3 * BRANCH_W), F32),
                   jax.ShapeDtypeStruct((m, 3 * BRANCH_W), BF16)),
        grid=(bsz, nb),
        in_specs=[pl.BlockSpec((seq, tn), lambda b, j: (b, j + COL_BU // tn)),
                  pl.BlockSpec((3, tn), lambda b, j: (0, j)),
                  pl.BlockSpec((1, tn), lambda b, j: (0, j))],
        out_specs=(pl.BlockSpec((seq, tn), lambda b, j: (b, j)),
                   pl.BlockSpec((seq, tn), lambda b, j: (b, j))),
        compiler_params=_params("parallel", "parallel"),
        name="hy_conv3",
    )(proj, conv_w.astype(F32), conv_b.astype(F32).reshape(1, -1))


def _hy_fwd_kernel(c_ref, s_ref, z_ref, hc_ref, hs_ref, nyq_ref, yc_ref, ys_ref, *, inv_n):
    i = pl.program_id(2)
    z = z_ref[...]
    zc = jnp.dot(c_ref[...], z, preferred_element_type=F32)
    zs = jnp.dot(s_ref[...], z, preferred_element_type=F32)
    hc, hs = hc_ref[...], hs_ref[...]
    rows = lax.broadcasted_iota(jnp.int32, zc.shape, 0) + i * zc.shape[0]
    hd = jnp.where(rows == 0, nyq_ref[...] * inv_n, hc)
    yc_ref[...] = (zc * hc - zs * hs).astype(BF16)
    ys_ref[...] = (zc * hs + zs * hd).astype(BF16)


def _hy_fwd(cm, s1, zb, z_col, hc, hs, nyq, h_col, bsz, seq):
    m = bsz * seq
    tm = min(512, seq)
    tn = 512
    nt = BRANCH_W // tn
    mt = seq // tm
    w_spec = pl.BlockSpec((tm, seq), lambda b, j, i: (i, 0))
    h_spec = pl.BlockSpec((tm, tn), lambda b, j, i: (i, j + h_col // tn))
    o_spec = pl.BlockSpec((tm, tn), lambda b, j, i: (b * mt + i, j))
    return pl.pallas_call(
        functools.partial(_hy_fwd_kernel, inv_n=1.0 / (2 * seq)),
        out_shape=(jax.ShapeDtypeStruct((m, BRANCH_W), BF16),) * 2,
        grid=(bsz, nt, mt),
        in_specs=[w_spec, w_spec,
                  pl.BlockSpec((seq, tn), lambda b, j, i: (b, j + z_col // tn)),
                  h_spec, h_spec,
                  pl.BlockSpec((1, tn), lambda b, j, i: (0, j + h_col // tn))],
        out_specs=(o_spec, o_spec),
        compiler_params=_params("parallel", "parallel", "arbitrary"),
        name="hy_fwd",
    )(cm, s1, zb, hc, hs, nyq)


def _hy_inv_mid_kernel(c_ref, s_ref, yc_ref, ys_ref, gate_ref, z_ref, bias_ref, o_ref, ob_ref):
    conv = (jnp.dot(c_ref[...], yc_ref[...], preferred_element_type=F32)
            + jnp.dot(s_ref[...], ys_ref[...], preferred_element_type=F32))
    out = gate_ref[...] * (conv + bias_ref[...] * z_ref[...])
    o_ref[...] = out
    ob_ref[...] = out.astype(BF16)


def _hy_inv_last_kernel(c_ref, s_ref, yc_ref, ys_ref, gate_ref, z_ref, bias_ref, sz_ref, o_ref):
    conv = (jnp.dot(c_ref[...], yc_ref[...], preferred_element_type=F32)
            + jnp.dot(s_ref[...], ys_ref[...], preferred_element_type=F32))
    out = gate_ref[...] * (conv + bias_ref[...] * z_ref[...])
    o_ref[...] = (out * sz_ref[...]).astype(BF16)


def _hy_inv(cm, s2, yc, ys, gates, gate_col, zprev, z_col, bias_row, bsz, seq, silu_src=None):
    m = bsz * seq
    tm = min(512, seq)
    tn = 512
    nt = BRANCH_W // tn
    mt = seq // tm
    w_spec = pl.BlockSpec((tm, seq), lambda b, j, i: (i, 0))
    y_spec = pl.BlockSpec((seq, tn), lambda b, j, i: (b, j))
    o_spec = pl.BlockSpec((tm, tn), lambda b, j, i: (b * mt + i, j))
    in_specs = [w_spec, w_spec, y_spec, y_spec,
                pl.BlockSpec((tm, tn), lambda b, j, i: (b * mt + i, j + gate_col // tn)),
                pl.BlockSpec((tm, tn), lambda b, j, i: (b * mt + i, j + z_col // tn)),
                pl.BlockSpec((1, tn), lambda b, j, i: (0, j))]
    args = [cm, s2, yc, ys, gates, zprev, bias_row]
    if silu_src is None:
        body = _hy_inv_mid_kernel
        out_shape = (jax.ShapeDtypeStruct((m, BRANCH_W), F32), jax.ShapeDtypeStruct((m, BRANCH_W), BF16))
        out_specs = (o_spec, o_spec)
    else:
        body = _hy_inv_last_kernel
        in_specs.append(pl.BlockSpec((tm, tn), lambda b, j, i: (b * mt + i, j + COL_BZ // tn)))
        args.append(silu_src)
        out_shape = jax.ShapeDtypeStruct((m, BRANCH_W), BF16)
        out_specs = o_spec
    return pl.pallas_call(
        body, out_shape=out_shape, grid=(bsz, nt, mt), in_specs=in_specs, out_specs=out_specs,
        compiler_params=_params("parallel", "parallel", "arbitrary"),
        name="hy_inv",
    )(*args)


def _hyena_branch(proj, bsz, seq, dft, conv_w, conv_b, w1, b1, w2, b2, freq, w3, b3, bias):
    cm, s1, s2 = dft
    hsum, hdiff, nyq = _hy_filter_taps(seq, w1, b1, w2, b2, freq, w3, b3)
    hc, hs = _hy_spectrum(cm, s1, hsum, hdiff)
    uc, ucb = _hy_conv3(proj, bsz, seq, conv_w, conv_b)
    bias = bias.astype(F32)
    yc, ys = _hy_fwd(cm, s1, ucb, 0, hc, hs, nyq, 0, bsz, seq)
    z1, z1b = _hy_inv(cm, s2, yc, ys, uc, BRANCH_W, uc, 0, bias[0:1], bsz, seq)
    yc, ys = _hy_fwd(cm, s1, z1b, 0, hc, hs, nyq, BRANCH_W, bsz, seq)
    return _hy_inv(cm, s2, yc, ys, uc, 2 * BRANCH_W, z1, 0, bias[1:2], bsz, seq, silu_src=proj)


def _rope_table_kernel(pos_ref, inv_ref, cos_ref, sin_ref):
    ang = pos_ref[...] * inv_ref[...]
    lane = lax.broadcasted_iota(jnp.int32, ang.shape, 1)
    live = lane < MLA_ROPE
    cos_ref[...] = jnp.where(live, jnp.cos(ang), 0.0)
    sin_ref[...] = jnp.where(live, jnp.where(lane < MLA_ROPE // 2, -1.0, 1.0) * jnp.sin(ang), 0.0)


def _rope_tables(positions):
    m = positions.size
    half = MLA_ROPE // 2
    inv = ROPE_BASE ** (-jnp.arange(half, dtype=F32) / half)
    inv = jnp.concatenate([inv, inv, jnp.zeros((128 - MLA_ROPE,), F32)])[None]
    pos = positions.astype(F32).reshape(m, 1)
    tm = min(1024, m)
    spec = pl.BlockSpec((tm, 128), lambda i: (i, 0))
    return pl.pallas_call(
        _rope_table_kernel,
        out_shape=(jax.ShapeDtypeStruct((m, 128), F32),) * 2,
        grid=(m // tm,),
        in_specs=[pl.BlockSpec((tm, 1), lambda i: (i, 0)), pl.BlockSpec((1, 128), lambda i: (0, 0))],
        out_specs=(spec, spec),
        compiler_params=_params("parallel"),
        name="rope_table",
    )(pos, inv)


def _rope128(x, cos_t, sin_t):
    lane = lax.broadcasted_iota(jnp.int32, x.shape, 1)
    half = MLA_ROPE // 2
    partner = jnp.where(lane < half, pltpu.roll(x, 128 - half, 1), pltpu.roll(x, half, 1))
    return x * cos_t + partner * sin_t


def _rms(x, g):
    ms = jnp.mean(jnp.square(x), axis=-1, keepdims=True)
    return x * lax.rsqrt(ms + RMS_EPS) * g


def _mla_q_kernel(cq_ref, g_ref, w_ref, cos_ref, sin_ref, q_ref, *, scale):
    xn = _rms(cq_ref[...], g_ref[...]).astype(BF16)
    q = jnp.dot(xn, w_ref[...], preferred_element_type=F32) * scale
    cos_t, sin_t = cos_ref[...], sin_ref[...]
    for h in range(MLA_HEADS):
        base = h * MLA_QK_PAD
        q_ref[:, base:base + MLA_NOPE] = q[:, base:base + MLA_NOPE].astype(BF16)
        q_ref[:, base + MLA_NOPE:base + MLA_QK_PAD] = _rope128(
            q[:, base + MLA_NOPE:base + MLA_QK_PAD], cos_t, sin_t).astype(BF16)


def _mla_kv_kernel(ckv_ref, g_ref, wk_ref, wv_ref, kr_ref, cos_ref, sin_ref, k_ref, v_ref):
    xn = _rms(ckv_ref[...], g_ref[...]).astype(BF16)
    kn = jnp.dot(xn, wk_ref[...], preferred_element_type=F32)
    v_ref[...] = jnp.dot(xn, wv_ref[...], preferred_element_type=F32).astype(BF16)
    kr = _rope128(kr_ref[:, 0:128], cos_ref[...], sin_ref[...]).astype(BF16)
    for h in range(MLA_HEADS):
        base = h * MLA_QK_PAD
        k_ref[:, base:base + MLA_NOPE] = kn[:, h * MLA_NOPE:(h + 1) * MLA_NOPE].astype(BF16)
        k_ref[:, base + MLA_NOPE:base + MLA_QK_PAD] = kr


def _mla_attn_kernel(q_ref, k_ref, v_ref, z_ref, o_ref):
    s = lax.dot_general(q_ref[...], k_ref[...], (((1,), (1,)), ((), ())),
                        preferred_element_type=F32)
    p = jnp.exp(s - jnp.max(s, axis=-1, keepdims=True))
    l = jnp.sum(p, axis=-1, keepdims=True)
    o = jnp.dot(p.astype(BF16), v_ref[...], preferred_element_type=F32)
    o_ref[...] = (o / l * z_ref[...]).astype(BF16)


def _mla_branch(proj, bsz, seq, rope, q_norm_g, w_uq, kv_norm_g, w_ukv):
    m = bsz * seq
    cos_t, sin_t = rope
    dqk = MLA_NOPE + MLA_ROPE
    hq = MLA_HEADS * MLA_QK_PAD
    w_q = w_uq.reshape(MLA_LORA, MLA_HEADS, dqk)
    w_q = jnp.pad(w_q, ((0, 0), (0, 0), (0, MLA_QK_PAD - dqk))).reshape(MLA_LORA, hq).astype(BF16)
    w_kv = w_ukv.reshape(MLA_LORA, MLA_HEADS, MLA_NOPE + MLA_V)
    w_k = w_kv[:, :, :MLA_NOPE].reshape(MLA_LORA, MLA_HEADS * MLA_NOPE).astype(BF16)
    w_v = w_kv[:, :, MLA_NOPE:].reshape(MLA_LORA, MLA_HEADS * MLA_V).astype(BF16)
    tm = min(512, m)
    row = lambda shape, col=0: pl.BlockSpec(shape, lambda i: (i, col))
    full = lambda shape: pl.BlockSpec(shape, lambda i: (0, 0))
    qp = pl.pallas_call(
        functools.partial(_mla_q_kernel, scale=dqk ** -0.5),
        out_shape=jax.ShapeDtypeStruct((m, hq), BF16),
        grid=(m // tm,),
        in_specs=[row((tm, MLA_LORA), COL_CQ // MLA_LORA), full((1, MLA_LORA)), full((MLA_LORA, hq)),
                  row((tm, 128)), row((tm, 128))],
        out_specs=row((tm, hq)),
        compiler_params=_params("parallel"),
        name="mla_q",
    )(proj, q_norm_g.astype(F32).reshape(1, -1), w_q, cos_t, sin_t)
    kp, vp = pl.pallas_call(
        _mla_kv_kernel,
        out_shape=(jax.ShapeDtypeStruct((m, hq), BF16),
                   jax.ShapeDtypeStruct((m, MLA_HEADS * MLA_V), BF16)),
        grid=(m // tm,),
        in_specs=[row((tm, MLA_LORA), COL_CKV // MLA_LORA), full((1, MLA_LORA)),
                  full((MLA_LORA, MLA_HEADS * MLA_NOPE)), full((MLA_LORA, MLA_HEADS * MLA_V)),
                  row((tm, 512), COL_CKR // 512), row((tm, 128)), row((tm, 128))],
        out_specs=(row((tm, hq)), row((tm, MLA_HEADS * MLA_V))),
        compiler_params=_params("parallel"),
        name="mla_kv",
    )(proj, kv_norm_g.astype(F32).reshape(1, -1), w_k, w_v, proj, cos_t, sin_t)
    tq = min(256, seq)
    qt = seq // tq
    return pl.pallas_call(
        _mla_attn_kernel,
        out_shape=jax.ShapeDtypeStruct((m, MLA_HEADS * MLA_V), BF16),
        grid=(bsz, MLA_HEADS, qt),
        in_specs=[pl.BlockSpec((tq, MLA_QK_PAD), lambda b, h, i: (b * qt + i, h)),
                  pl.BlockSpec((seq, MLA_QK_PAD), lambda b, h, i: (b, h)),
                  pl.BlockSpec((seq, MLA_V), lambda b, h, i: (b, h)),
                  pl.BlockSpec((tq, MLA_V), lambda b, h, i: (b * qt + i, h + COL_CZ // MLA_V))],
        out_specs=pl.BlockSpec((tq, MLA_V), lambda b, h, i: (b * qt + i, h)),
        compiler_params=_params("parallel", "parallel", "arbitrary"),
        name="mla_attn",
    )(qp, kp, vp, proj)


def _lift_kernel(ya_ref, yb_ref, yc_ref, w_ref, ga_ref, gb_ref, gc_ref, o_ref):
    acc = ga_ref[...] * jnp.dot(ya_ref[...], w_ref[0], preferred_element_type=F32)
    acc += gb_ref[...] * jnp.dot(yb_ref[...], w_ref[1], preferred_element_type=F32)
    acc += gc_ref[...] * jnp.dot(yc_ref[...], w_ref[2], preferred_element_type=F32)
    o_ref[...] = acc.astype(BF16)


def _lift(ya, yb, yc, w_lift, proj):
    m = ya.shape[0]
    tm = min(512, m)
    tn = 512
    y_spec = pl.BlockSpec((tm, BRANCH_W), lambda i, j: (i, 0))
    gate = lambda n: pl.BlockSpec((tm, tn), lambda i, j: (i, j + (COL_GATE + n * D_MODEL) // tn))
    return pl.pallas_call(
        _lift_kernel,
        out_shape=jax.ShapeDtypeStruct((m, D_MODEL), BF16),
        grid=(m // tm, D_MODEL // tn),
        in_specs=[y_spec, y_spec, y_spec,
                  pl.BlockSpec((N_BRANCH, BRANCH_W, tn), lambda i, j: (0, 0, j)),
                  gate(0), gate(1), gate(2)],
        out_specs=pl.BlockSpec((tm, tn), lambda i, j: (i, j)),
        compiler_params=_params("parallel", "arbitrary"),
        name="lift",
    )(ya, yb, yc, w_lift.astype(BF16), proj, proj, proj)


def _out_kernel(mix_ref, wo_ref, p_ref, wp_ref, sp_ref, x_ref, g_ref, b_ref, o_ref, ob_ref):
    mixed = jnp.dot(mix_ref[...], wo_ref[...], preferred_element_type=F32)
    ple = jnp.dot(p_ref[...].astype(BF16), wp_ref[...], preferred_element_type=F32) * sp_ref[...]
    r = DEEPNORM_ALPHA * x_ref[...] + mixed + ple
    mu = jnp.mean(r, axis=-1, keepdims=True)
    var = jnp.mean(jnp.square(r - mu), axis=-1, keepdims=True)
    out = (r - mu) * lax.rsqrt(var + LN_EPS) * g_ref[...] + b_ref[...]
    o_ref[...] = out
    ob_ref[...] = out.astype(BF16)


def _out_norm(mix, w_out, p, w_ple, proj, x, ln_g, ln_b):
    m = mix.shape[0]
    tm = min(256, m)
    row = lambda w, col=0: pl.BlockSpec((tm, w), lambda i: (i, col))
    full = lambda shape: pl.BlockSpec(shape, lambda i: (0, 0))
    return pl.pallas_call(
        _out_kernel,
        out_shape=(jax.ShapeDtypeStruct((m, D_MODEL), F32), jax.ShapeDtypeStruct((m, D_MODEL), BF16)),
        grid=(m // tm,),
        in_specs=[row(D_MODEL), full((D_MODEL, D_MODEL)), row(PLE_DIM), full((PLE_DIM, D_MODEL)),
                  row(D_MODEL, COL_PLE // D_MODEL), row(D_MODEL), full((1, D_MODEL)), full((1, D_MODEL))],
        out_specs=(row(D_MODEL), row(D_MODEL)),
        compiler_params=_params("parallel"),
        name="out_norm",
    )(mix, w_out.astype(BF16), p, w_ple.astype(BF16), proj, x,
      ln_g.astype(F32).reshape(1, -1), ln_b.astype(F32).reshape(1, -1))


def kernel(x, p, positions, w_in, s5_lambda_re, s5_lambda_im, s5_log_dt, s5_b_re, s5_b_im, s5_c_re, s5_c_im, s5_d, s5_w_glu, s5_b_glu, hy_conv_w, hy_conv_b, hy_w1, hy_b1, hy_w2, hy_b2, hy_freq, hy_w3, hy_b3, hy_bias, mla_q_norm, mla_w_uq, mla_kv_norm, mla_w_ukv, w_lift, w_out, w_ple, ln_g, ln_b):
    bsz, seq, _ = x.shape
    m = bsz * seq
    depth = w_in.shape[0]
    dft = _dft_matrices(seq)
    rope = _rope_tables(positions)
    xf = x.reshape(m, D_MODEL).astype(F32)
    xb = xf.astype(BF16)
    for i in range(depth):
        proj = _proj(xb, _proj_weights(w_in[i]))
        mats = _s5_matrices(s5_lambda_re[i], s5_lambda_im[i], s5_log_dt[i], s5_b_re[i], s5_b_im[i],
                            s5_c_re[i], s5_c_im[i], s5_d[i])
        y_a = _s5_branch(proj, bsz, seq, mats, s5_w_glu[i], s5_b_glu[i])
        y_b = _hyena_branch(proj, bsz, seq, dft, hy_conv_w[i], hy_conv_b[i], hy_w1[i], hy_b1[i],
                            hy_w2[i], hy_b2[i], hy_freq[i], hy_w3[i], hy_b3[i], hy_bias[i])
        y_c = _mla_branch(proj, bsz, seq, rope, mla_q_norm[i], mla_w_uq[i], mla_kv_norm[i], mla_w_ukv[i])
        mix = _lift(y_a, y_b, y_c, w_lift[i], proj)
        xf, xb = _out_norm(mix, w_out[i], p[i].reshape(m, PLE_DIM), w_ple[i], proj, xf, ln_g[i], ln_b[i])
    return xf.reshape(bsz, seq, D_MODEL).astype(x.dtype)
```

```python
import functools
import math

import numpy as np
import jax
import jax.numpy as jnp
from jax import lax
from jax.experimental import pallas as pl
from jax.experimental.pallas import tpu as pltpu

F32 = jnp.float32
BF16 = jnp.bfloat16

D_MODEL = 2048
PLE_DIM = 256
N_BRANCH = 3
BRANCH_W = 1024

S5_GROUP = 16
S5_GROUPS = BRANCH_W // S5_GROUP
S5_STATE = 64
S5_CHUNK = 16
S5_ROW = S5_CHUNK * S5_GROUP

HY_EMB = 33
HY_FF = 64
HY_PAD = 128
HY_DECAY_TARGET = 0.01
HY_FAST_DECAY = 0.3
HY_SLOW_DECAY = 1.5
DFT_ROWS = 64

MLA_HEADS = 8
MLA_NOPE = 128
MLA_ROPE = 64
MLA_V = 128
MLA_LORA = 512
MLA_QK_PAD = 256
ROPE_BASE = 10000.0

LN_EPS = 1e-5
RMS_EPS = 1e-6
DEPTH = 2
DEEPNORM_ALPHA = (2 * DEPTH) ** 0.25

COL_AX = 0
COL_AZ = 1024
COL_BU = 2048
COL_BZ = 5120
COL_CQ = 6144
COL_CKV = 6656
PROJ_A_N = 7168
COL_PLE = 0
COL_GATE = 2048
COL_CZ = 8192
COL_CKR = 9216
PROJ_B_N = 9728
PROJ_TN = 512
W_IN_CKR = 7168
W_IN_CZ = 7232
W_IN_GATE = 8256
W_IN_PLE = 14400
W_IN_END = 16448

VMEM_LIMIT = 56 * 1024 * 1024


def _params(*sem):
    return pltpu.CompilerParams(dimension_semantics=sem, vmem_limit_bytes=VMEM_LIMIT)


def _in_tiles(j, ranges):
    hit = None
    for lo, hi in ranges:
        cur = (j >= lo // PROJ_TN) & (j < hi // PROJ_TN)
        hit = cur if hit is None else (hit | cur)
    return hit


def _proj_kernel(x_ref, w_ref, o_ref, xb_ref, *, silu_cols, sigm_cols):
    j = pl.program_id(1)

    @pl.when(j == 0)
    def _():
        xb_ref[...] = x_ref[...].astype(BF16)

    acc = jnp.dot(xb_ref[...], w_ref[...].astype(BF16), preferred_element_type=F32)
    is_silu = _in_tiles(j, silu_cols)
    plain = jnp.logical_not(is_silu)

    @pl.when(is_silu)
    def _():
        o_ref[...] = acc * jax.nn.sigmoid(acc)

    if sigm_cols:
        is_sigm = _in_tiles(j, sigm_cols)
        plain = jnp.logical_not(is_silu | is_sigm)

        @pl.when(is_sigm)
        def _():
            o_ref[...] = jax.nn.sigmoid(acc)

    @pl.when(plain)
    def _():
        o_ref[...] = acc


def _proj(x, w, n_out, silu_cols, sigm_cols, name):
    m, k = x.shape
    tm = min(1024, m)
    return pl.pallas_call(
        functools.partial(_proj_kernel, silu_cols=silu_cols, sigm_cols=sigm_cols),
        out_shape=jax.ShapeDtypeStruct((m, n_out), F32),
        grid=(m // tm, n_out // PROJ_TN),
        in_specs=[pl.BlockSpec((tm, k), lambda i, j: (i, 0)),
                  pl.BlockSpec((k, PROJ_TN), lambda i, j: (0, j))],
        out_specs=pl.BlockSpec((tm, PROJ_TN), lambda i, j: (i, j)),
        scratch_shapes=[pltpu.VMEM((tm, k), BF16)],
        compiler_params=_params("parallel", "arbitrary"),
        name=name,
    )(x, w)


def _proj_both(x, w_in):
    proj_a = _proj(x, w_in, PROJ_A_N, ((COL_AZ, COL_BU), (COL_BZ, COL_CQ)), (), "proj_a")
    pad = jnp.zeros((w_in.shape[0], PROJ_B_N - COL_CKR - MLA_ROPE), w_in.dtype)
    w_b = jnp.concatenate([w_in[:, W_IN_PLE:W_IN_END], w_in[:, W_IN_GATE:W_IN_PLE],
                           w_in[:, W_IN_CZ:W_IN_GATE], w_in[:, W_IN_CKR:W_IN_CZ], pad],
                          axis=1).astype(BF16)
    proj_b = _proj(x, w_b, PROJ_B_N, ((COL_CZ, COL_CKR),), ((COL_PLE, COL_CZ),), "proj_b")
    return proj_a, proj_b


def _s5_matrices(lam_re, lam_im, log_dt, b_re, b_im, c_re, c_im, d):
    hi = lax.Precision.HIGHEST
    t_n, g_n, p_n, h_n = S5_CHUNK, S5_GROUPS, S5_STATE, S5_GROUP
    lam = lax.complex(lam_re.astype(F32), lam_im.astype(F32))
    dt = jnp.exp(log_dt.astype(F32))[..., None]
    lam_dt = lam * dt
    lam_bar = jnp.exp(lam_dt)
    steps = jnp.arange(t_n + 1, dtype=F32)
    lamp = jnp.exp(lam_dt[None] * steps[:, None, None, None])
    bbar = ((lam_bar - 1.0) / lam)[..., None] * lax.complex(b_re.astype(F32), b_im.astype(F32))
    cm = lax.complex(c_re.astype(F32), c_im.astype(F32))
    kk = jnp.real(jnp.einsum('dgop,tdgp,dgpi->tdgoi', cm, lamp[:t_n], bbar, precision=hi))
    kf, kb = kk[:, 0], kk[:, 1]
    dmat = d.astype(F32).reshape(g_n, h_n)[:, :, None] * jnp.eye(h_n, dtype=F32)
    kfull = jnp.concatenate([kb[1:][::-1], (kf[0] + kb[0] + dmat)[None], kf[1:]], axis=0)
    idx = (np.arange(t_n)[None, :] - np.arange(t_n)[:, None]) + (t_n - 1)
    m_intra = kfull[idx]
    m_intra = m_intra.transpose(2, 0, 4, 1, 3).reshape(g_n, S5_ROW, S5_ROW)
    parity = jnp.asarray(np.eye(2, dtype=np.float32)[np.arange(g_n) % 2])
    sf = lamp[:t_n][::-1, 0][..., None] * bbar[0][None]
    sb = lamp[:t_n, 1][..., None] * bbar[1][None]
    parts = jnp.stack([jnp.real(sf), jnp.imag(sf), jnp.real(sb), jnp.imag(sb)])
    parts = parts.transpose(2, 1, 4, 0, 3)
    m_state = parts[:, :, :, :, None, :] * parity[:, None, None, None, :, None]
    m_state = m_state.reshape(g_n, S5_ROW, 4 * 2 * p_n)
    of = cm[0][None] * lamp[1:, 0][:, :, None, :]
    ob = cm[1][None] * lamp[1:][::-1, 1][:, :, None, :]
    outs = jnp.stack([jnp.real(of), -jnp.imag(of), jnp.real(ob), -jnp.imag(ob)])
    outs = outs.transpose(2, 0, 4, 1, 3)
    m_out = outs[:, :, None] * parity[:, None, :, None, None, None]
    m_out = m_out.reshape(g_n, 4 * 2 * p_n, S5_ROW)
    a_t = lamp[t_n].reshape(2, 1, g_n * p_n)
    a_rows = (jnp.real(a_t[0]), jnp.imag(a_t[0]), jnp.real(a_t[1]), jnp.imag(a_t[1]))
    return m_intra.astype(BF16), m_state.astype(BF16), m_out.astype(BF16), a_rows


def _s5_state_kernel(u_ref, ms_ref, fr_ref, fi_ref, br_ref, bi_ref):
    s = (jnp.dot(u_ref[0], ms_ref[0], preferred_element_type=F32)
         + jnp.dot(u_ref[1], ms_ref[1], preferred_element_type=F32))
    fr_ref[...] = s[:, 0:128]
    fi_ref[...] = s[:, 128:256]
    br_ref[...] = s[:, 256:384]
    bi_ref[...] = s[:, 384:512]


def _s5_scan_kernel(xfr, xfi, xbr, xbi, afr, afi, abr, abi, ofr, ofi, obr, obi):
    bsz, n_chunks, width = xfr.shape
    a_fr, a_fi, a_br, a_bi = afr[...], afi[...], abr[...], abi[...]
    zero = jnp.zeros((1, width), F32)

    def body(c, carry):
        cb = n_chunks - 1 - c
        new = []
        for b in range(bsz):
            s_fr, s_fi, s_br, s_bi = carry[b]
            ofr[b, pl.ds(c, 1), :] = s_fr
            ofi[b, pl.ds(c, 1), :] = s_fi
            obr[b, pl.ds(cb, 1), :] = s_br
            obi[b, pl.ds(cb, 1), :] = s_bi
            n_fr = a_fr * s_fr - a_fi * s_fi + xfr[b, pl.ds(c, 1), :]
            n_fi = a_fr * s_fi + a_fi * s_fr + xfi[b, pl.ds(c, 1), :]
            n_br = a_br * s_br - a_bi * s_bi + xbr[b, pl.ds(cb, 1), :]
            n_bi = a_br * s_bi + a_bi * s_br + xbi[b, pl.ds(cb, 1), :]
            new.append((n_fr, n_fi, n_br, n_bi))
        return tuple(new)

    lax.fori_loop(0, n_chunks, body, tuple((zero,) * 4 for _ in range(bsz)))


def _s5_out_kernel(u_ref, mi_ref, mo_ref, fr_ref, fi_ref, br_ref, bi_ref, y_ref):
    st = jnp.concatenate([fr_ref[...], fi_ref[...], br_ref[...], bi_ref[...]], axis=1).astype(BF16)
    for e in range(2):
        y = (jnp.dot(u_ref[e], mi_ref[e], preferred_element_type=F32)
             + jnp.dot(st, mo_ref[e], preferred_element_type=F32))
        y_ref[e] = jax.nn.gelu(y)


def _s5_glu_kernel(g_ref, w_ref, b_ref, z_ref, o_ref):
    g = g_ref[...]
    acc = jnp.dot(g.astype(BF16), w_ref[...], preferred_element_type=F32) + b_ref[...]
    o_ref[...] = (g * jax.nn.sigmoid(acc) * z_ref[...]).astype(o_ref.dtype)


def _s5_branch(proj, bsz, seq, mats, w_glu, b_glu):
    m_intra, m_state, m_out, a_rows = mats
    g_n, t_n, h_n, p_n = S5_GROUPS, S5_CHUNK, S5_GROUP, S5_STATE
    n_chunks = seq // t_n
    rows = bsz * n_chunks
    m = bsz * seq
    u = proj[:, COL_AX:COL_AX + BRANCH_W].reshape(bsz, n_chunks, t_n, g_n, h_n)
    u = u.transpose(3, 0, 1, 2, 4).reshape(g_n, rows, S5_ROW).astype(BF16)
    st_shape = jax.ShapeDtypeStruct((rows, g_n * p_n), F32)
    st_spec = pl.BlockSpec((rows, 128), lambda j: (0, j))
    u_spec = pl.BlockSpec((2, rows, S5_ROW), lambda j: (j, 0, 0))
    local = pl.pallas_call(
        _s5_state_kernel,
        out_shape=(st_shape,) * 4,
        grid=(g_n // 2,),
        in_specs=[u_spec, pl.BlockSpec((2, S5_ROW, 8 * p_n), lambda j: (j, 0, 0))],
        out_specs=(st_spec,) * 4,
        compiler_params=_params("parallel"),
        name="s5_state",
    )(u, m_state)
    local = [a.reshape(bsz, n_chunks, g_n * p_n) for a in local]
    cw = 1024
    x_spec = pl.BlockSpec((bsz, n_chunks, cw), lambda j: (0, 0, j))
    a_spec = pl.BlockSpec((1, cw), lambda j: (0, j))
    carried = pl.pallas_call(
        _s5_scan_kernel,
        out_shape=(jax.ShapeDtypeStruct((bsz, n_chunks, g_n * p_n), F32),) * 4,
        grid=(g_n * p_n // cw,),
        in_specs=[x_spec] * 4 + [a_spec] * 4,
        out_specs=(x_spec,) * 4,
        compiler_params=_params("parallel"),
        name="s5_scan",
    )(*local, *a_rows)
    carried = [a.reshape(rows, g_n * p_n) for a in carried]
    y = pl.pallas_call(
        _s5_out_kernel,
        out_shape=jax.ShapeDtypeStruct((g_n, rows, S5_ROW), F32),
        grid=(g_n // 2,),
        in_specs=[u_spec,
                  pl.BlockSpec((2, S5_ROW, S5_ROW), lambda j: (j, 0, 0)),
                  pl.BlockSpec((2, 8 * p_n, S5_ROW), lambda j: (j, 0, 0))] + [st_spec] * 4,
        out_specs=pl.BlockSpec((2, rows, S5_ROW), lambda j: (j, 0, 0)),
        compiler_params=_params("parallel"),
        name="s5_out",
    )(u, m_intra, m_out, *carried)
    y = y.reshape(g_n, bsz, n_chunks, t_n, h_n).transpose(1, 2, 3, 0, 4).reshape(m, BRANCH_W)
    tm = min(512, m)
    return pl.pallas_call(
        _s5_glu_kernel,
        out_shape=jax.ShapeDtypeStruct((m, BRANCH_W), BF16),
        grid=(m // tm,),
        in_specs=[pl.BlockSpec((tm, BRANCH_W), lambda i: (i, 0)),
                  pl.BlockSpec((BRANCH_W, BRANCH_W), lambda i: (0, 0)),
                  pl.BlockSpec((1, BRANCH_W), lambda i: (0, 0)),
                  pl.BlockSpec((tm, BRANCH_W), lambda i: (i, COL_AZ // BRANCH_W))],
        out_specs=pl.BlockSpec((tm, BRANCH_W), lambda i: (i, 0)),
        compiler_params=_params("parallel"),
        name="s5_glu",
    )(y, w_glu.astype(BF16), b_glu.astype(F32).reshape(1, BRANCH_W), proj)


def _dft_tables(seq):
    n = 2 * seq
    mm = np.arange(seq, dtype=np.int64)
    k1 = np.arange(seq // DFT_ROWS, dtype=np.int64)[:, None] * DFT_ROWS
    k0 = np.arange(DFT_ROWS, dtype=np.int64)[:, None]
    ang_a = 2.0 * np.pi * ((k1 * mm) % n).astype(np.float64) / n
    ang_b = 2.0 * np.pi * ((k0 * mm) % n).astype(np.float64) / n
    return tuple(jnp.asarray(t, F32) for t in (np.cos(ang_a), np.sin(ang_a), np.cos(ang_b), np.sin(ang_b)))


def _dft_gen_kernel(ac_ref, as_ref, bc_ref, bs_ref, c_ref, s1_ref, s2_ref):
    i = pl.program_id(0)
    a_c = ac_ref[pl.ds(i, 1), :]
    a_s = as_ref[pl.ds(i, 1), :]
    b_c, b_s = bc_ref[...], bs_ref[...]
    cos_t = a_c * b_c - a_s * b_s
    sin_t = a_s * b_c + a_c * b_s
    rows = lax.broadcasted_iota(jnp.int32, cos_t.shape, 0) + i * DFT_ROWS
    cols = lax.broadcasted_iota(jnp.int32, cos_t.shape, 1)
    alt_cols = jnp.where((cols & 1) == 0, 1.0, -1.0).astype(F32)
    alt_rows = jnp.where((rows & 1) == 0, 1.0, -1.0).astype(F32)
    c_ref[...] = cos_t.astype(BF16)
    s1_ref[...] = jnp.where(rows == 0, alt_cols, sin_t).astype(BF16)
    s2_ref[...] = jnp.where(cols == 0, alt_rows, sin_t).astype(BF16)


def _dft_matrices(seq):
    tabs = _dft_tables(seq)
    n_steps = seq // DFT_ROWS
    tab_spec = pl.BlockSpec(tabs[0].shape, lambda i: (0, 0))
    b_spec = pl.BlockSpec((DFT_ROWS, seq), lambda i: (0, 0))
    o_spec = pl.BlockSpec((DFT_ROWS, seq), lambda i: (i, 0))
    return pl.pallas_call(
        _dft_gen_kernel,
        out_shape=(jax.ShapeDtypeStruct((seq, seq), BF16),) * 3,
        grid=(n_steps,),
        in_specs=[tab_spec, tab_spec, b_spec, b_spec],
        out_specs=(o_spec,) * 3,
        compiler_params=_params("parallel"),
        name="dft_gen",
    )(*tabs)


def _hy_filter_kernel(feat_ref, w1_ref, b1_ref, w2_ref, b2_ref, f0_ref, f1_ref,
                      w3p_ref, w3n_ref, b3p_ref, b3n_ref, dl_ref, t_ref,
                      hs_ref, hd_ref, nyq_ref, h_ref):
    hi = lax.Precision.HIGHEST

    @pl.when(pl.program_id(0) == 0)
    def _():
        h1 = jnp.sin(f0_ref[...] * (jnp.dot(feat_ref[...], w1_ref[...], precision=hi,
                                            preferred_element_type=F32) + b1_ref[...]))
        h_ref[...] = jnp.sin(f1_ref[...] * (jnp.dot(h1, w2_ref[...], precision=hi,
                                                    preferred_element_type=F32) + b2_ref[...]))

    h = h_ref[...]
    win = jnp.exp(-t_ref[...] * jnp.abs(dl_ref[...]))
    hpos = (jnp.dot(h, w3p_ref[...], precision=hi, preferred_element_type=F32) + b3p_ref[...]) * win
    hneg = (jnp.dot(h, w3n_ref[...], precision=hi, preferred_element_type=F32) + b3n_ref[...]) * win
    rows = lax.broadcasted_iota(jnp.int32, hpos.shape, 0)
    hneg = jnp.where(rows == 0, 0.0, hneg)
    norm = (jnp.sum(jnp.abs(hpos), axis=0, keepdims=True)
            + jnp.sum(jnp.abs(hneg), axis=0, keepdims=True))
    hsum = (hpos + hneg) / norm
    hdiff = (hpos - hneg) / norm
    alt = jnp.where((rows & 1) == 0, 1.0, -1.0).astype(F32)
    nyq_ref[...] = jnp.sum(alt * hsum, axis=0, keepdims=True)
    hs_ref[...] = hsum.astype(BF16)
    hd_ref[...] = hdiff.astype(BF16)


def _hy_filter_taps(seq, w1, b1, w2, b2, freq, w3, b3):
    n_ch = 2 * BRANCH_W
    bands = (HY_EMB - 1) // 2
    t = jnp.linspace(0.0, 1.0, seq, dtype=F32)[:, None]
    w = 2.0 * math.pi * jnp.arange(seq, dtype=F32)[:, None] / seq
    f = jnp.linspace(1e-4, bands - 1, bands, dtype=F32)[None, :]
    feats = jnp.concatenate([t, jnp.cos(f * w), -jnp.sin(f * w),
                             jnp.zeros((seq, HY_PAD - HY_EMB), F32)], axis=-1)
    deltas = jnp.linspace(math.log(HY_DECAY_TARGET) / HY_SLOW_DECAY,
                          math.log(HY_DECAY_TARGET) / HY_FAST_DECAY, n_ch, dtype=F32)[None, :]

    def pad2(a, r, c):
        a = a.astype(F32)
        return jnp.pad(a, ((0, r - a.shape[0]), (0, c - a.shape[1])))

    w1p = pad2(w1, HY_PAD, HY_PAD)
    w2p = pad2(w2, HY_PAD, HY_PAD)
    b1p = pad2(b1[None], 1, HY_PAD)
    b2p = pad2(b2[None], 1, HY_PAD)
    f0p = pad2(freq[0][None], 1, HY_PAD)
    f1p = pad2(freq[1][None], 1, HY_PAD)
    w3p = pad2(w3, HY_PAD, 2 * n_ch)
    b3r = b3.astype(F32)[None]
    tn = 256
    nt = n_ch // tn
    full = lambda shape: pl.BlockSpec(shape, lambda j: (0, 0))
    return pl.pallas_call(
        _hy_filter_kernel,
        out_shape=(jax.ShapeDtypeStruct((seq, n_ch), BF16), jax.ShapeDtypeStruct((seq, n_ch), BF16),
                   jax.ShapeDtypeStruct((1, n_ch), F32)),
        grid=(nt,),
        in_specs=[full((seq, HY_PAD)), full((HY_PAD, HY_PAD)), full((1, HY_PAD)),
                  full((HY_PAD, HY_PAD)), full((1, HY_PAD)), full((1, HY_PAD)), full((1, HY_PAD)),
                  pl.BlockSpec((HY_PAD, tn), lambda j: (0, j)),
                  pl.BlockSpec((HY_PAD, tn), lambda j: (0, j + nt)),
                  pl.BlockSpec((1, tn), lambda j: (0, j)),
                  pl.BlockSpec((1, tn), lambda j: (0, j + nt)),
                  pl.BlockSpec((1, tn), lambda j: (0, j)),
                  full((seq, 1))],
        out_specs=(pl.BlockSpec((seq, tn), lambda j: (0, j)), pl.BlockSpec((seq, tn), lambda j: (0, j)),
                   pl.BlockSpec((1, tn), lambda j: (0, j))),
        scratch_shapes=[pltpu.VMEM((seq, HY_PAD), F32)],
        compiler_params=_params("arbitrary"),
        name="hy_filter",
    )(feats, w1p, b1p, w2p, b2p, f0p, f1p, w3p, w3p, b3r, b3r, deltas, t)


def _hy_spectrum_kernel(c_ref, s_ref, hs_ref, hd_ref, hc_out, hso_out, *, inv_n):
    i = pl.program_id(1)
    zc = jnp.dot(c_ref[...], hs_ref[...], preferred_element_type=F32)
    zs = jnp.dot(s_ref[...], hd_ref[...], preferred_element_type=F32)
    rows = lax.broadcasted_iota(jnp.int32, zc.shape, 0) + i * zc.shape[0]
    first = rows == 0
    hc_out[...] = zc * jnp.where(first, inv_n, 2.0 * inv_n)
    hso_out[...] = jnp.where(first, 0.0, zs * (2.0 * inv_n))


def _hy_spectrum(cm, s1, hsum, hdiff):
    seq, n_ch = hsum.shape
    tm = min(512, seq)
    tn = min(512, n_ch)
    w_spec = pl.BlockSpec((tm, seq), lambda j, i: (i, 0))
    h_spec = pl.BlockSpec((seq, tn), lambda j, i: (0, j))
    o_spec = pl.BlockSpec((tm, tn), lambda j, i: (i, j))
    return pl.pallas_call(
        functools.partial(_hy_spectrum_kernel, inv_n=1.0 / (2 * seq)),
        out_shape=(jax.ShapeDtypeStruct((seq, n_ch), F32),) * 2,
        grid=(n_ch // tn, seq // tm),
        in_specs=[w_spec, w_spec, h_spec, h_spec],
        out_specs=(o_spec, o_spec),
        compiler_params=_params("parallel", "arbitrary"),
        name="hy_spectrum",
    )(cm, s1, hsum, hdiff)


def _hy_conv3_kernel(u_ref, w_ref, b_ref, o_ref, vb_ref):
    u = u_ref[...]
    n = u.shape[0]
    rows = lax.broadcasted_iota(jnp.int32, u.shape, 0)
    prev = jnp.where(rows == 0, 0.0, pltpu.roll(u, 1, 0))
    nxt = jnp.where(rows == n - 1, 0.0, pltpu.roll(u, n - 1, 0))
    w = w_ref[...]
    out = prev * w[0:1] + u * w[1:2] + nxt * w[2:3] + b_ref[...]
    o_ref[...] = out
    vb_ref[...] = out.astype(BF16)


def _hy_conv3(proj, bsz, seq, conv_w, conv_b):
    m = bsz * seq
    tn = 256
    nb = 3 * BRANCH_W // tn
    return pl.pallas_call(
        _hy_conv3_kernel,
        out_shape=(jax.ShapeDtypeStruct((m, 3 * BRANCH_W), F32),
                   jax.ShapeDtypeStruct((m, 3 * BRANCH_W), BF16)),
        grid=(bsz, nb),
        in_specs=[pl.BlockSpec((seq, tn), lambda b, j: (b, j + COL_BU // tn)),
                  pl.BlockSpec((3, tn), lambda b, j: (0, j)),
                  pl.BlockSpec((1, tn), lambda b, j: (0, j))],
        out_specs=(pl.BlockSpec((seq, tn), lambda b, j: (b, j)),
                   pl.BlockSpec((seq, tn), lambda b, j: (b, j))),
        compiler_params=_params("parallel", "parallel"),
        name="hy_conv3",
    )(proj, conv_w.astype(F32), conv_b.astype(F32).reshape(1, -1))


def _hy_fwd_kernel(c_ref, s_ref, z_ref, hc_ref, hs_ref, nyq_ref, yc_ref, ys_ref, *, inv_n):
    i = pl.program_id(2)
    z = z_ref[...]
    zc = jnp.dot(c_ref[...], z, preferred_element_type=F32)
    zs = jnp.dot(s_ref[...], z, preferred_element_type=F32)
    hc, hs = hc_ref[...], hs_ref[...]
    rows = lax.broadcasted_iota(jnp.int32, zc.shape, 0) + i * zc.shape[0]
    hd = jnp.where(rows == 0, nyq_ref[...] * inv_n, hc)
    yc_ref[...] = (zc * hc - zs * hs).astype(BF16)
    ys_ref[...] = (zc * hs + zs * hd).astype(BF16)


def _hy_fwd(cm, s1, zb, z_col, hc, hs, nyq, h_col, bsz, seq):
    m = bsz * seq
    tm = min(512, seq)
    tn = 512
    nt = BRANCH_W // tn
    mt = seq // tm
    w_spec = pl.BlockSpec((tm, seq), lambda b, j, i: (i, 0))
    h_spec = pl.BlockSpec((tm, tn), lambda b, j, i: (i, j + h_col // tn))
    o_spec = pl.BlockSpec((tm, tn), lambda b, j, i: (b * mt + i, j))
    return pl.pallas_call(
        functools.partial(_hy_fwd_kernel, inv_n=1.0 / (2 * seq)),
        out_shape=(jax.ShapeDtypeStruct((m, BRANCH_W), BF16),) * 2,
        grid=(bsz, nt, mt),
        in_specs=[w_spec, w_spec,
                  pl.BlockSpec((seq, tn), lambda b, j, i: (b, j + z_col // tn)),
                  h_spec, h_spec,
                  pl.BlockSpec((1, tn), lambda b, j, i: (0, j + h_col // tn))],
        out_specs=(o_spec, o_spec),
        compiler_params=_params("parallel", "parallel", "arbitrary"),
        name="hy_fwd",
    )(cm, s1, zb, hc, hs, nyq)


def _hy_inv_mid_kernel(c_ref, s_ref, yc_ref, ys_ref, gate_ref, z_ref, bias_ref, o_ref, ob_ref):
    conv = (jnp.dot(c_ref[...], yc_ref[...], preferred_element_type=F32)
            + jnp.dot(s_ref[...], ys_ref[...], preferred_element_type=F32))
    out = gate_ref[...] * (conv + bias_ref[...] * z_ref[...])
    o_ref[...] = out
    ob_ref[...] = out.astype(BF16)


def _hy_inv_last_kernel(c_ref, s_ref, yc_ref, ys_ref, gate_ref, z_ref, bias_ref, sz_ref, o_ref):
    conv = (jnp.dot(c_ref[...], yc_ref[...], preferred_element_type=F32)
            + jnp.dot(s_ref[...], ys_ref[...], preferred_element_type=F32))
    out = gate_ref[...] * (conv + bias_ref[...] * z_ref[...])
    o_ref[...] = (out * sz_ref[...]).astype(BF16)


def _hy_inv(cm, s2, yc, ys, gates, gate_col, zprev, z_col, bias_row, bsz, seq, silu_src=None):
    m = bsz * seq
    tm = min(512, seq)
    tn = 512
    nt = BRANCH_W // tn
    mt = seq // tm
    w_spec = pl.BlockSpec((tm, seq), lambda b, j, i: (i, 0))
    y_spec = pl.BlockSpec((seq, tn), lambda b, j, i: (b, j))
    o_spec = pl.BlockSpec((tm, tn), lambda b, j, i: (b * mt + i, j))
    in_specs = [w_spec, w_spec, y_spec, y_spec,
                pl.BlockSpec((tm, tn), lambda b, j, i: (b * mt + i, j + gate_col // tn)),
                pl.BlockSpec((tm, tn), lambda b, j, i: (b * mt + i, j + z_col // tn)),
                pl.BlockSpec((1, tn), lambda b, j, i: (0, j))]
    args = [cm, s2, yc, ys, gates, zprev, bias_row]
    if silu_src is None:
        body = _hy_inv_mid_kernel
        out_shape = (jax.ShapeDtypeStruct((m, BRANCH_W), F32), jax.ShapeDtypeStruct((m, BRANCH_W), BF16))
        out_specs = (o_spec, o_spec)
    else:
        body = _hy_inv_last_kernel
        in_specs.append(pl.BlockSpec((tm, tn), lambda b, j, i: (b * mt + i, j + COL_BZ // tn)))
        args.append(silu_src)
        out_shape = jax.ShapeDtypeStruct((m, BRANCH_W), BF16)
        out_specs = o_spec
    return pl.pallas_call(
        body, out_shape=out_shape, grid=(bsz, nt, mt), in_specs=in_specs, out_specs=out_specs,
        compiler_params=_params("parallel", "parallel", "arbitrary"),
        name="hy_inv",
    )(*args)


def _hyena_branch(proj, bsz, seq, dft, conv_w, conv_b, w1, b1, w2, b2, freq, w3, b3, bias):
    cm, s1, s2 = dft
    hsum, hdiff, nyq = _hy_filter_taps(seq, w1, b1, w2, b2, freq, w3, b3)
    hc, hs = _hy_spectrum(cm, s1, hsum, hdiff)
    uc, ucb = _hy_conv3(proj, bsz, seq, conv_w, conv_b)
    bias = bias.astype(F32)
    yc, ys = _hy_fwd(cm, s1, ucb, 0, hc, hs, nyq, 0, bsz, seq)
    z1, z1b = _hy_inv(cm, s2, yc, ys, uc, BRANCH_W, uc, 0, bias[0:1], bsz, seq)
    yc, ys = _hy_fwd(cm, s1, z1b, 0, hc, hs, nyq, BRANCH_W, bsz, seq)
    return _hy_inv(cm, s2, yc, ys, uc, 2 * BRANCH_W, z1, 0, bias[1:2], bsz, seq, silu_src=proj)


def _rope_table_kernel(pos_ref, inv_ref, cos_ref, sin_ref):
    ang = pos_ref[...] * inv_ref[...]
    lane = lax.broadcasted_iota(jnp.int32, ang.shape, 1)
    live = lane < MLA_ROPE
    cos_ref[...] = jnp.where(live, jnp.cos(ang), 0.0)
    sin_ref[...] = jnp.where(live, jnp.where(lane < MLA_ROPE // 2, -1.0, 1.0) * jnp.sin(ang), 0.0)


def _rope_tables(positions):
    m = positions.size
    half = MLA_ROPE // 2
    inv = ROPE_BASE ** (-jnp.arange(half, dtype=F32) / half)
    inv = jnp.concatenate([inv, inv, jnp.zeros((128 - MLA_ROPE,), F32)])[None]
    pos = positions.astype(F32).reshape(m, 1)
    tm = min(1024, m)
    spec = pl.BlockSpec((tm, 128), lambda i: (i, 0))
    return pl.pallas_call(
        _rope_table_kernel,
        out_shape=(jax.ShapeDtypeStruct((m, 128), F32),) * 2,
        grid=(m // tm,),
        in_specs=[pl.BlockSpec((tm, 1), lambda i: (i, 0)), pl.BlockSpec((1, 128), lambda i: (0, 0))],
        out_specs=(spec, spec),
        compiler_params=_params("parallel"),
        name="rope_table",
    )(pos, inv)


def _rope128(x, cos_t, sin_t):
    lane = lax.broadcasted_iota(jnp.int32, x.shape, 1)
    half = MLA_ROPE // 2
    partner = jnp.where(lane < half, pltpu.roll(x, 128 - half, 1), pltpu.roll(x, half, 1))
    return x * cos_t + partner * sin_t


def _rms(x, g):
    ms = jnp.mean(jnp.square(x), axis=-1, keepdims=True)
    return x * lax.rsqrt(ms + RMS_EPS) * g


def _mla_q_kernel(cq_ref, g_ref, w_ref, cos_ref, sin_ref, q_ref, *, scale):
    xn = _rms(cq_ref[...], g_ref[...]).astype(BF16)
    q = jnp.dot(xn, w_ref[...], preferred_element_type=F32) * scale
    cos_t, sin_t = cos_ref[...], sin_ref[...]
    for h in range(MLA_HEADS):
        base = h * MLA_QK_PAD
        q_ref[:, base:base + MLA_NOPE] = q[:, base:base + MLA_NOPE].astype(BF16)
        q_ref[:, base + MLA_NOPE:base + MLA_QK_PAD] = _rope128(
            q[:, base + MLA_NOPE:base + MLA_QK_PAD], cos_t, sin_t).astype(BF16)


def _mla_kv_kernel(ckv_ref, g_ref, wk_ref, wv_ref, kr_ref, cos_ref, sin_ref, k_ref, v_ref):
    xn = _rms(ckv_ref[...], g_ref[...]).astype(BF16)
    kn = jnp.dot(xn, wk_ref[...], preferred_element_type=F32)
    v_ref[...] = jnp.dot(xn, wv_ref[...], preferred_element_type=F32).astype(BF16)
    kr = _rope128(kr_ref[:, 0:128], cos_ref[...], sin_ref[...]).astype(BF16)
    for h in range(MLA_HEADS):
        base = h * MLA_QK_PAD
        k_ref[:, base:base + MLA_NOPE] = kn[:, h * MLA_NOPE:(h + 1) * MLA_NOPE].astype(BF16)
        k_ref[:, base + MLA_NOPE:base + MLA_QK_PAD] = kr


def _mla_attn_kernel(q_ref, k_ref, v_ref, z_ref, o_ref):
    s = lax.dot_general(q_ref[...], k_ref[...], (((1,), (1,)), ((), ())),
                        preferred_element_type=F32)
    p = jnp.exp(s - jnp.max(s, axis=-1, keepdims=True))
    l = jnp.sum(p, axis=-1, keepdims=True)
    o = jnp.dot(p.astype(BF16), v_ref[...], preferred_element_type=F32)
    o_ref[...] = (o / l * z_ref[...]).astype(BF16)


def _mla_branch(proj, proj_b, bsz, seq, rope, q_norm_g, w_uq, kv_norm_g, w_ukv):
    m = bsz * seq
    cos_t, sin_t = rope
    dqk = MLA_NOPE + MLA_ROPE
    hq = MLA_HEADS * MLA_QK_PAD
    w_q = w_uq.reshape(MLA_LORA, MLA_HEADS, dqk)
    w_q = jnp.pad(w_q, ((0, 0), (0, 0), (0, MLA_QK_PAD - dqk))).reshape(MLA_LORA, hq).astype(BF16)
    w_kv = w_ukv.reshape(MLA_LORA, MLA_HEADS, MLA_NOPE + MLA_V)
    w_k = w_kv[:, :, :MLA_NOPE].reshape(MLA_LORA, MLA_HEADS * MLA_NOPE).astype(BF16)
    w_v = w_kv[:, :, MLA_NOPE:].reshape(MLA_LORA, MLA_HEADS * MLA_V).astype(BF16)
    tm = min(512, m)
    row = lambda shape, col=0: pl.BlockSpec(shape, lambda i: (i, col))
    full = lambda shape: pl.BlockSpec(shape, lambda i: (0, 0))
    qp = pl.pallas_call(
        functools.partial(_mla_q_kernel, scale=dqk ** -0.5),
        out_shape=jax.ShapeDtypeStruct((m, hq), BF16),
        grid=(m // tm,),
        in_specs=[row((tm, MLA_LORA), COL_CQ // MLA_LORA), full((1, MLA_LORA)), full((MLA_LORA, hq)),
                  row((tm, 128)), row((tm, 128))],
        out_specs=row((tm, hq)),
        compiler_params=_params("parallel"),
        name="mla_q",
    )(proj, q_norm_g.astype(F32).reshape(1, -1), w_q, cos_t, sin_t)
    kp, vp = pl.pallas_call(
        _mla_kv_kernel,
        out_shape=(jax.ShapeDtypeStruct((m, hq), BF16),
                   jax.ShapeDtypeStruct((m, MLA_HEADS * MLA_V), BF16)),
        grid=(m // tm,),
        in_specs=[row((tm, MLA_LORA), COL_CKV // MLA_LORA), full((1, MLA_LORA)),
                  full((MLA_LORA, MLA_HEADS * MLA_NOPE)), full((MLA_LORA, MLA_HEADS * MLA_V)),
                  row((tm, 512), COL_CKR // 512), row((tm, 128)), row((tm, 128))],
        out_specs=(row((tm, hq)), row((tm, MLA_HEADS * MLA_V))),
        compiler_params=_params("parallel"),
        name="mla_kv",
    )(proj, kv_norm_g.astype(F32).reshape(1, -1), w_k, w_v, proj_b, cos_t, sin_t)
    tq = min(256, seq)
    qt = seq // tq
    return pl.pallas_call(
        _mla_attn_kernel,
        out_shape=jax.ShapeDtypeStruct((m, MLA_HEADS * MLA_V), BF16),
        grid=(bsz, MLA_HEADS, qt),
        in_specs=[pl.BlockSpec((tq, MLA_QK_PAD), lambda b, h, i: (b * qt + i, h)),
                  pl.BlockSpec((seq, MLA_QK_PAD), lambda b, h, i: (b, h)),
                  pl.BlockSpec((seq, MLA_V), lambda b, h, i: (b, h)),
                  pl.BlockSpec((tq, MLA_V), lambda b, h, i: (b * qt + i, h + COL_CZ // MLA_V))],
        out_specs=pl.BlockSpec((tq, MLA_V), lambda b, h, i: (b * qt + i, h)),
        compiler_params=_params("parallel", "parallel", "arbitrary"),
        name="mla_attn",
    )(qp, kp, vp, proj_b)


def _lift_kernel(ya_ref, yb_ref, yc_ref, w_ref, ga_ref, gb_ref, gc_ref, o_ref):
    acc = ga_ref[...] * jnp.dot(ya_ref[...], w_ref[0], preferred_element_type=F32)
    acc += gb_ref[...] * jnp.dot(yb_ref[...], w_ref[1], preferred_element_type=F32)
    acc += gc_ref[...] * jnp.dot(yc_ref[...], w_ref[2], preferred_element_type=F32)
    o_ref[...] = acc.astype(BF16)


def _lift(ya, yb, yc, w_lift, proj):
    m = ya.shape[0]
    tm = min(512, m)
    tn = 512
    y_spec = pl.BlockSpec((tm, BRANCH_W), lambda i, j: (i, 0))
    gate = lambda n: pl.BlockSpec((tm, tn), lambda i, j: (i, j + (COL_GATE + n * D_MODEL) // tn))
    return pl.pallas_call(
        _lift_kernel,
        out_shape=jax.ShapeDtypeStruct((m, D_MODEL), BF16),
        grid=(m // tm, D_MODEL // tn),
        in_specs=[y_spec, y_spec, y_spec,
                  pl.BlockSpec((N_BRANCH, BRANCH_W, tn), lambda i, j: (0, 0, j)),
                  gate(0), gate(1), gate(2)],
        out_specs=pl.BlockSpec((tm, tn), lambda i, j: (i, j)),
        compiler_params=_params("parallel", "arbitrary"),
        name="lift",
    )(ya, yb, yc, w_lift.astype(BF16), proj, proj, proj)


def _out_kernel(mix_ref, wo_ref, p_ref, wp_ref, sp_ref, x_ref, g_ref, b_ref, o_ref):
    mixed = jnp.dot(mix_ref[...], wo_ref[...], preferred_element_type=F32)
    ple = jnp.dot(p_ref[...].astype(BF16), wp_ref[...], preferred_element_type=F32) * sp_ref[...]
    r = DEEPNORM_ALPHA * x_ref[...] + mixed + ple
    mu = jnp.mean(r, axis=-1, keepdims=True)
    var = jnp.mean(jnp.square(r - mu), axis=-1, keepdims=True)
    o_ref[...] = (r - mu) * lax.rsqrt(var + LN_EPS) * g_ref[...] + b_ref[...]


def _out_norm(mix, w_out, p, w_ple, proj, x, ln_g, ln_b):
    m = mix.shape[0]
    tm = min(256, m)
    row = lambda w, col=0: pl.BlockSpec((tm, w), lambda i: (i, col))
    full = lambda shape: pl.BlockSpec(shape, lambda i: (0, 0))
    return pl.pallas_call(
        _out_kernel,
        out_shape=jax.ShapeDtypeStruct((m, D_MODEL), F32),
        grid=(m // tm,),
        in_specs=[row(D_MODEL), full((D_MODEL, D_MODEL)), row(PLE_DIM), full((PLE_DIM, D_MODEL)),
                  row(D_MODEL, COL_PLE // D_MODEL), row(D_MODEL), full((1, D_MODEL)), full((1, D_MODEL))],
        out_specs=row(D_MODEL),
        compiler_params=_params("parallel"),
        name="out_norm",
    )(mix, w_out.astype(BF16), p, w_ple.astype(BF16), proj, x,
      ln_g.astype(F32).reshape(1, -1), ln_b.astype(F32).reshape(1, -1))


def kernel(x, p, positions, w_in, s5_lambda_re, s5_lambda_im, s5_log_dt, s5_b_re, s5_b_im, s5_c_re, s5_c_im, s5_d, s5_w_glu, s5_b_glu, hy_conv_w, hy_conv_b, hy_w1, hy_b1, hy_w2, hy_b2, hy_freq, hy_w3, hy_b3, hy_bias, mla_q_norm, mla_w_uq, mla_kv_norm, mla_w_ukv, w_lift, w_out, w_ple, ln_g, ln_b):
    bsz, seq, _ = x.shape
    m = bsz * seq
    depth = w_in.shape[0]
    dft = _dft_matrices(seq)
    rope = _rope_tables(positions)
    xf = x.reshape(m, D_MODEL).astype(F32)
    for i in range(depth):
        proj, proj_b = _proj_both(xf, w_in[i])
        mats = _s5_matrices(s5_lambda_re[i], s5_lambda_im[i], s5_log_dt[i], s5_b_re[i], s5_b_im[i],
                            s5_c_re[i], s5_c_im[i], s5_d[i])
        y_a = _s5_branch(proj, bsz, seq, mats, s5_w_glu[i], s5_b_glu[i])
        y_b = _hyena_branch(proj, bsz, seq, dft, hy_conv_w[i], hy_conv_b[i], hy_w1[i], hy_b1[i],
                            hy_w2[i], hy_b2[i], hy_freq[i], hy_w3[i], hy_b3[i], hy_bias[i])
        y_c = _mla_branch(proj, proj_b, bsz, seq, rope, mla_q_norm[i], mla_w_uq[i], mla_kv_norm[i], mla_w_ukv[i])
        mix = _lift(y_a, y_b, y_c, w_lift[i], proj_b)
        xf = _out_norm(mix, w_out[i], p[i].reshape(m, PLE_DIM), w_ple[i], proj_b, xf, ln_g[i], ln_b[i])
    return xf.reshape(bsz, seq, D_MODEL).astype(x.dtype)
```

```python
import functools
import math

import numpy as np
import jax
import jax.numpy as jnp
from jax import lax
from jax.experimental import pallas as pl
from jax.experimental.pallas import tpu as pltpu

F32 = jnp.float32
BF16 = jnp.bfloat16

D_MODEL = 2048
PLE_DIM = 256
N_BRANCH = 3
BRANCH_W = 1024

S5_GROUP = 16
S5_GROUPS = BRANCH_W // S5_GROUP
S5_STATE = 64
S5_CHUNK = 16
S5_ROW = S5_CHUNK * S5_GROUP

HY_EMB = 33
HY_FF = 64
HY_PAD = 128
HY_DECAY_TARGET = 0.01
HY_FAST_DECAY = 0.3
HY_SLOW_DECAY = 1.5
DFT_ROWS = 64

MLA_HEADS = 8
MLA_NOPE = 128
MLA_ROPE = 64
MLA_V = 128
MLA_LORA = 512
MLA_QK_PAD = 256
ROPE_BASE = 10000.0

LN_EPS = 1e-5
RMS_EPS = 1e-6
DEPTH = 2
DEEPNORM_ALPHA = (2 * DEPTH) ** 0.25

COL_AX = 0
COL_AZ = 1024
COL_BU = 2048
COL_BZ = 5120
COL_CQ = 6144
COL_CKV = 6656
PROJ_A_N = 7168
COL_PLE = 0
COL_GATE = 2048
COL_CZ = 8192
COL_CKR = 9216
PROJ_B_N = 9728
PROJ_TN = 512
W_IN_CKR = 7168
W_IN_CZ = 7232
W_IN_GATE = 8256
W_IN_PLE = 14400
W_IN_END = 16448

VMEM_LIMIT = 56 * 1024 * 1024


def _params(*sem):
    return pltpu.CompilerParams(dimension_semantics=sem, vmem_limit_bytes=VMEM_LIMIT)


def _in_tiles(j, ranges):
    hit = None
    for lo, hi in ranges:
        cur = (j >= lo // PROJ_TN) & (j < hi // PROJ_TN)
        hit = cur if hit is None else (hit | cur)
    return hit


def _proj_kernel(x_ref, w_ref, o_ref, xb_ref, *, silu_cols, sigm_cols):
    j = pl.program_id(1)

    @pl.when(j == 0)
    def _():
        xb_ref[...] = x_ref[...].astype(BF16)

    acc = jnp.dot(xb_ref[...], w_ref[...].astype(BF16), preferred_element_type=F32)
    is_silu = _in_tiles(j, silu_cols)
    plain = jnp.logical_not(is_silu)

    @pl.when(is_silu)
    def _():
        o_ref[...] = acc * jax.nn.sigmoid(acc)

    if sigm_cols:
        is_sigm = _in_tiles(j, sigm_cols)
        plain = jnp.logical_not(is_silu | is_sigm)

        @pl.when(is_sigm)
        def _():
            o_ref[...] = jax.nn.sigmoid(acc)

    @pl.when(plain)
    def _():
        o_ref[...] = acc


def _proj(x, w, n_out, silu_cols, sigm_cols, name):
    m, k = x.shape
    tm = min(1024, m)
    return pl.pallas_call(
        functools.partial(_proj_kernel, silu_cols=silu_cols, sigm_cols=sigm_cols),
        out_shape=jax.ShapeDtypeStruct((m, n_out), F32),
        grid=(m // tm, n_out // PROJ_TN),
        in_specs=[pl.BlockSpec((tm, k), lambda i, j: (i, 0)),
                  pl.BlockSpec((k, PROJ_TN), lambda i, j: (0, j))],
        out_specs=pl.BlockSpec((tm, PROJ_TN), lambda i, j: (i, j)),
        scratch_shapes=[pltpu.VMEM((tm, k), BF16)],
        compiler_params=_params("parallel", "arbitrary"),
        name=name,
    )(x, w)


def _proj_both(x, w_in):
    proj_a = _proj(x, w_in, PROJ_A_N, ((COL_AZ, COL_BU), (COL_BZ, COL_CQ)), (), "proj_a")
    pad = jnp.zeros((w_in.shape[0], PROJ_B_N - COL_CKR - MLA_ROPE), w_in.dtype)
    w_b = jnp.concatenate([w_in[:, W_IN_PLE:W_IN_END], w_in[:, W_IN_GATE:W_IN_PLE],
                           w_in[:, W_IN_CZ:W_IN_GATE], w_in[:, W_IN_CKR:W_IN_CZ], pad],
                          axis=1).astype(BF16)
    proj_b = _proj(x, w_b, PROJ_B_N, ((COL_CZ, COL_CKR),), ((COL_PLE, COL_CZ),), "proj_b")
    return proj_a, proj_b


S5_LANES = 8 * S5_STATE
S5_SLAB = 8
S5_RELAYOUT_ROWS = 32


def _s5_tile_lanes(a):
    return jnp.concatenate([a[:, 0]] * 4 + [a[:, 1]] * 4, axis=-1)


def _s5_tiled_params(lam_re, lam_im, log_dt, b_re, b_im, c_re, c_im, d):
    f = lambda a: a.astype(F32)
    depth = lam_re.shape[0]
    ldt = jnp.broadcast_to(f(log_dt)[..., None, None], lam_re.shape[:3] + (1, S5_STATE))
    return (_s5_tile_lanes(f(lam_re)[:, :, :, None, :]), _s5_tile_lanes(f(lam_im)[:, :, :, None, :]),
            _s5_tile_lanes(ldt),
            _s5_tile_lanes(jnp.swapaxes(f(b_re), -1, -2)), _s5_tile_lanes(jnp.swapaxes(f(b_im), -1, -2)),
            _s5_tile_lanes(f(c_re)), _s5_tile_lanes(f(c_im)),
            f(d).reshape(depth, S5_GROUPS, S5_GROUP, 1))


def _s5_mats_kernel(lr_ref, li_ref, ldt_ref, br_ref, bi_ref, cr_ref, ci_ref, d_ref,
                    ms_ref, mi_ref, mo_ref, ar_ref, ai_ref):
    t_n, h_n = S5_CHUNK, S5_GROUP
    hi = lax.Precision.HIGHEST
    nt = (((1,), (1,)), ((), ()))
    blk = lax.broadcasted_iota(jnp.int32, (1, S5_LANES), 1) // S5_STATE
    is_im = (blk // 2) % 2 == 1
    is_fwd = blk < 4
    steps = lax.broadcasted_iota(jnp.int32, (24, S5_LANES), 0).astype(F32)
    lane_k = lax.broadcasted_iota(jnp.int32, (h_n, S5_ROW), 1)
    sub_k = lax.broadcasted_iota(jnp.int32, (h_n, S5_ROW), 0)

    def per_group(gi, carry):
        lr, li = lr_ref[gi], li_ref[gi]
        dt = jnp.exp(ldt_ref[gi])
        zr, zi = lr * dt, li * dt
        mag = jnp.exp(steps * zr)
        tr, ti = mag * jnp.cos(steps * zi), mag * jnp.sin(steps * zi)
        lbr, lbi = tr[1:2], ti[1:2]
        n2 = lr * lr + li * li
        qr = ((lbr - 1.0) * lr + lbi * li) / n2
        qi = (lbi * lr - (lbr - 1.0) * li) / n2
        br, bi = br_ref[gi], bi_ref[gi]
        bbr, bbi = qr * br - qi * bi, qr * bi + qi * br
        y1, y2 = jnp.where(is_im, bbi, bbr), jnp.where(is_im, bbr, bbi)
        cr, ci = cr_ref[gi], ci_ref[gi]

        def pick(tab, t_fwd, t_bwd):
            return jnp.where(is_fwd, tab[t_fwd:t_fwd + 1], tab[t_bwd:t_bwd + 1])

        def c_times(p_r, p_i):
            return jnp.where(is_im, -(cr * p_i + ci * p_r), cr * p_r - ci * p_i)

        q_rows = []
        for t in range(t_n):
            a_r, a_i = pick(tr, t_n - 1 - t, t), pick(ti, t_n - 1 - t, t)
            rows = pl.ds(t * h_n, h_n)
            ms_ref[gi, rows, :] = (a_r * y1 + jnp.where(is_im, a_i, -a_i) * y2).astype(BF16)
            mo_ref[gi, rows, :] = c_times(pick(tr, t + 1, t_n - t), pick(ti, t + 1, t_n - t)).astype(BF16)
            q_rows.append(c_times(pick(tr, t, t_n - 1 - t), pick(ti, t, t_n - 1 - t)))
        q = jnp.concatenate(q_rows, axis=0)
        half = S5_LANES // 2
        kf = 0.5 * lax.dot_general(y1[:, :half], q[:, :half], nt, precision=hi, preferred_element_type=F32)
        kb = 0.5 * lax.dot_general(y1[:, half:], q[:, half:], nt, precision=hi, preferred_element_type=F32)
        kf = kf + jnp.where(lane_k == sub_k, d_ref[gi], 0.0)
        for t in range(t_n):
            fwd = kf if t == 0 else jnp.where(lane_k >= h_n * t, pltpu.roll(kf, h_n * t, 1), 0.0)
            sh = h_n * (t_n - 1 - t)
            bwd = kb if sh == 0 else jnp.where(lane_k < S5_ROW - sh, pltpu.roll(kb, S5_ROW - sh, 1), 0.0)
            mi_ref[gi, pl.ds(t * h_n, h_n), :] = (fwd + bwd).astype(BF16)
        ar_ref[gi] = tr[t_n:t_n + 1]
        ai_ref[gi] = ti[t_n:t_n + 1]
        return carry

    lax.fori_loop(0, lr_ref.shape[0], per_group, 0)


def _s5_mats(tiled):
    g_n = S5_GROUPS
    gb = S5_SLAB
    spec = lambda r, w: pl.BlockSpec((gb, r, w), lambda j: (j, 0, 0))
    in_rows = (1, 1, 1, S5_GROUP, S5_GROUP, S5_GROUP, S5_GROUP)
    return pl.pallas_call(
        _s5_mats_kernel,
        out_shape=(jax.ShapeDtypeStruct((g_n, S5_ROW, S5_LANES), BF16),
                   jax.ShapeDtypeStruct((g_n, S5_ROW, S5_ROW), BF16),
                   jax.ShapeDtypeStruct((g_n, S5_ROW, S5_LANES), BF16),
                   jax.ShapeDtypeStruct((g_n, 1, S5_LANES), F32),
                   jax.ShapeDtypeStruct((g_n, 1, S5_LANES), F32)),
        grid=(g_n // gb,),
        in_specs=[spec(r, S5_LANES) for r in in_rows] + [spec(S5_GROUP, 1)],
        out_specs=(spec(S5_ROW, S5_LANES), spec(S5_ROW, S5_ROW), spec(S5_ROW, S5_LANES),
                   spec(1, S5_LANES), spec(1, S5_LANES)),
        compiler_params=_params("parallel"),
        name="s5_mats",
    )(*tiled)


def _s5_main_kernel(x_ref, ms_ref, mi_ref, mo_ref, ar_ref, ai_ref, y_ref,
                    u_ref, sl_ref, st_ref, yg_ref, *, bsz):
    rows = x_ref.shape[0]
    n_chunks = rows // bsz
    rc = S5_RELAYOUT_ROWS
    seg = lax.broadcasted_iota(jnp.int32, (rc, 128), 1) // S5_GROUP
    slot = (lax.broadcasted_iota(jnp.int32, (1, S5_LANES), 1) // S5_STATE) % 2
    slot128 = slot[:, :128]

    def relayout_in(r, carry):
        r0 = pl.multiple_of(r * rc, rc)
        for th in range(2):
            src = [x_ref[pl.ds(r0, rc), th * 8 + t8, :] for t8 in range(8)]
            for gi in range(S5_SLAB):
                out = src[0] if gi == 0 else pltpu.roll(src[0], ((0 - gi) % 8) * S5_GROUP, 1)
                for t8 in range(1, 8):
                    k = (t8 - gi) % 8
                    piece = src[t8] if k == 0 else pltpu.roll(src[t8], k * S5_GROUP, 1)
                    out = jnp.where(seg == t8, piece, out)
                u_ref[gi, pl.ds(r0, rc), th * 128:(th + 1) * 128] = out.astype(BF16)
        return carry

    lax.fori_loop(0, rows // rc, relayout_in, 0)

    a_parts = [[], [], [], []]
    for jp in range(S5_SLAB // 2):
        g0, g1 = 2 * jp, 2 * jp + 1
        r0 = jnp.dot(u_ref[g0], ms_ref[g0], preferred_element_type=F32)
        r1 = jnp.dot(u_ref[g1], ms_ref[g1], preferred_element_type=F32)
        loc = jnp.where(slot == 0, r0, r1)
        for k in range(4):
            sl_ref[k, :, jp * 128:(jp + 1) * 128] = loc[:, k * 128:(k + 1) * 128]
        for k, (ref, off) in enumerate(((ar_ref, 0), (ai_ref, 0), (ar_ref, 256), (ai_ref, 256))):
            a_parts[k].append(jnp.where(slot128 == 0, ref[g0][:, off:off + 128], ref[g1][:, off:off + 128]))
    a_fr, a_fi, a_br, a_bi = [jnp.concatenate(parts, axis=1) for parts in a_parts]

    zero = jnp.zeros((1, a_fr.shape[1]), F32)

    def scan(c, carry):
        cb = n_chunks - 1 - c
        new = []
        for b in range(bsz):
            s_fr, s_fi, s_br, s_bi = carry[b]
            rf, rb = b * n_chunks + c, b * n_chunks + cb
            st_ref[0, pl.ds(rf, 1), :] = s_fr
            st_ref[1, pl.ds(rf, 1), :] = s_fi
            st_ref[2, pl.ds(rb, 1), :] = s_br
            st_ref[3, pl.ds(rb, 1), :] = s_bi
            n_fr = a_fr * s_fr - a_fi * s_fi + sl_ref[0, pl.ds(rf, 1), :]
            n_fi = a_fr * s_fi + a_fi * s_fr + sl_ref[1, pl.ds(rf, 1), :]
            n_br = a_br * s_br - a_bi * s_bi + sl_ref[2, pl.ds(rb, 1), :]
            n_bi = a_br * s_bi + a_bi * s_br + sl_ref[3, pl.ds(rb, 1), :]
            new.append((n_fr, n_fi, n_br, n_bi))
        return tuple(new)

    lax.fori_loop(0, n_chunks, scan, tuple((zero,) * 4 for _ in range(bsz)))

    nt = (((1,), (1,)), ((), ()))
    for jp in range(S5_SLAB // 2):
        st = jnp.concatenate([st_ref[k, :, jp * 128:(jp + 1) * 128] for k in range(4)], axis=1)
        for e in range(2):
            g = 2 * jp + e
            st_g = jnp.where(slot == e, st, 0.0).astype(BF16)
            y = (jnp.dot(u_ref[g], mi_ref[g], preferred_element_type=F32)
                 + lax.dot_general(st_g, mo_ref[g], nt, preferred_element_type=F32))
            yg_ref[g] = jax.nn.gelu(y)

    def relayout_out(r, carry):
        r0 = pl.multiple_of(r * rc, rc)
        for th in range(2):
            src = [yg_ref[gi, pl.ds(r0, rc), th * 128:(th + 1) * 128] for gi in range(S5_SLAB)]
            for t8 in range(8):
                out = src[0] if t8 == 0 else pltpu.roll(src[0], ((0 - t8) % 8) * S5_GROUP, 1)
                for gi in range(1, S5_SLAB):
                    k = (gi - t8) % 8
                    piece = src[gi] if k == 0 else pltpu.roll(src[gi], k * S5_GROUP, 1)
                    out = jnp.where(seg == gi, piece, out)
                y_ref[pl.ds(r0, rc), th * 8 + t8, :] = out
        return carry

    lax.fori_loop(0, rows // rc, relayout_out, 0)


def _s5_glu_kernel(g_ref, w_ref, b_ref, z_ref, o_ref):
    g = g_ref[...]
    acc = jnp.dot(g.astype(BF16), w_ref[...], preferred_element_type=F32) + b_ref[...]
    o_ref[...] = (g * jax.nn.sigmoid(acc) * z_ref[...]).astype(o_ref.dtype)


def _s5_branch(proj, bsz, seq, tiled, w_glu, b_glu):
    m_state, m_intra, m_out, a_re, a_im = _s5_mats(tiled)
    t_n = S5_CHUNK
    rows = bsz * (seq // t_n)
    m = bsz * seq
    gb = S5_SLAB
    x3 = proj.reshape(rows, t_n, proj.shape[1])
    mat = lambda r, w: pl.BlockSpec((gb, r, w), lambda s: (s, 0, 0))
    io_spec = pl.BlockSpec((rows, t_n, 128), lambda s: (0, 0, s + COL_AX // 128))
    y = pl.pallas_call(
        functools.partial(_s5_main_kernel, bsz=bsz),
        out_shape=jax.ShapeDtypeStruct((rows, t_n, BRANCH_W), F32),
        grid=(S5_GROUPS // gb,),
        in_specs=[io_spec, mat(S5_ROW, S5_LANES), mat(S5_ROW, S5_ROW), mat(S5_ROW, S5_LANES),
                  mat(1, S5_LANES), mat(1, S5_LANES)],
        out_specs=pl.BlockSpec((rows, t_n, 128), lambda s: (0, 0, s)),
        scratch_shapes=[pltpu.VMEM((gb, rows, S5_ROW), BF16),
                        pltpu.VMEM((4, rows, S5_LANES), F32),
                        pltpu.VMEM((4, rows, S5_LANES), F32),
                        pltpu.VMEM((gb, rows, S5_ROW), F32)],
        compiler_params=_params("parallel"),
        name="s5_main",
    )(x3, m_state, m_intra, m_out, a_re, a_im)
    y = y.reshape(m, BRANCH_W)
    tm = min(512, m)
    return pl.pallas_call(
        _s5_glu_kernel,
        out_shape=jax.ShapeDtypeStruct((m, BRANCH_W), BF16),
        grid=(m // tm,),
        in_specs=[pl.BlockSpec((tm, BRANCH_W), lambda i: (i, 0)),
                  pl.BlockSpec((BRANCH_W, BRANCH_W), lambda i: (0, 0)),
                  pl.BlockSpec((1, BRANCH_W), lambda i: (0, 0)),
                  pl.BlockSpec((tm, BRANCH_W), lambda i: (i, COL_AZ // BRANCH_W))],
        out_specs=pl.BlockSpec((tm, BRANCH_W), lambda i: (i, 0)),
        compiler_params=_params("parallel"),
        name="s5_glu",
    )(y, w_glu.astype(BF16), b_glu.astype(F32).reshape(1, BRANCH_W), proj)


def _dft_tables(seq):
    n = 2 * seq
    mm = np.arange(seq, dtype=np.int64)
    k1 = np.arange(seq // DFT_ROWS, dtype=np.int64)[:, None] * DFT_ROWS
    k0 = np.arange(DFT_ROWS, dtype=np.int64)[:, None]
    ang_a = 2.0 * np.pi * ((k1 * mm) % n).astype(np.float64) / n
    ang_b = 2.0 * np.pi * ((k0 * mm) % n).astype(np.float64) / n
    return tuple(jnp.asarray(t, F32) for t in (np.cos(ang_a), np.sin(ang_a), np.cos(ang_b), np.sin(ang_b)))


def _dft_gen_kernel(ac_ref, as_ref, bc_ref, bs_ref, c_ref, s1_ref, s2_ref):
    i = pl.program_id(0)
    a_c = ac_ref[pl.ds(i, 1), :]
    a_s = as_ref[pl.ds(i, 1), :]
    b_c, b_s = bc_ref[...], bs_ref[...]
    cos_t = a_c * b_c - a_s * b_s
    sin_t = a_s * b_c + a_c * b_s
    rows = lax.broadcasted_iota(jnp.int32, cos_t.shape, 0) + i * DFT_ROWS
    cols = lax.broadcasted_iota(jnp.int32, cos_t.shape, 1)
    alt_cols = jnp.where((cols & 1) == 0, 1.0, -1.0).astype(F32)
    alt_rows = jnp.where((rows & 1) == 0, 1.0, -1.0).astype(F32)
    c_ref[...] = cos_t.astype(BF16)
    s1_ref[...] = jnp.where(rows == 0, alt_cols, sin_t).astype(BF16)
    s2_ref[...] = jnp.where(cols == 0, alt_rows, sin_t).astype(BF16)


def _dft_matrices(seq):
    tabs = _dft_tables(seq)
    n_steps = seq // DFT_ROWS
    tab_spec = pl.BlockSpec(tabs[0].shape, lambda i: (0, 0))
    b_spec = pl.BlockSpec((DFT_ROWS, seq), lambda i: (0, 0))
    o_spec = pl.BlockSpec((DFT_ROWS, seq), lambda i: (i, 0))
    return pl.pallas_call(
        _dft_gen_kernel,
        out_shape=(jax.ShapeDtypeStruct((seq, seq), BF16),) * 3,
        grid=(n_steps,),
        in_specs=[tab_spec, tab_spec, b_spec, b_spec],
        out_specs=(o_spec,) * 3,
        compiler_params=_params("parallel"),
        name="dft_gen",
    )(*tabs)


def _hy_filter_kernel(feat_ref, w1_ref, b1_ref, w2_ref, b2_ref, f0_ref, f1_ref,
                      w3p_ref, w3n_ref, b3p_ref, b3n_ref, dl_ref, t_ref,
                      hs_ref, hd_ref, nyq_ref, h_ref):
    hi = lax.Precision.HIGHEST

    @pl.when(pl.program_id(0) == 0)
    def _():
        h1 = jnp.sin(f0_ref[...] * (jnp.dot(feat_ref[...], w1_ref[...], precision=hi,
                                            preferred_element_type=F32) + b1_ref[...]))
        h_ref[...] = jnp.sin(f1_ref[...] * (jnp.dot(h1, w2_ref[...], precision=hi,
                                                    preferred_element_type=F32) + b2_ref[...]))

    h = h_ref[...]
    win = jnp.exp(-t_ref[...] * jnp.abs(dl_ref[...]))
    hpos = (jnp.dot(h, w3p_ref[...], precision=hi, preferred_element_type=F32) + b3p_ref[...]) * win
    hneg = (jnp.dot(h, w3n_ref[...], precision=hi, preferred_element_type=F32) + b3n_ref[...]) * win
    rows = lax.broadcasted_iota(jnp.int32, hpos.shape, 0)
    hneg = jnp.where(rows == 0, 0.0, hneg)
    norm = (jnp.sum(jnp.abs(hpos), axis=0, keepdims=True)
            + jnp.sum(jnp.abs(hneg), axis=0, keepdims=True))
    hsum = (hpos + hneg) / norm
    hdiff = (hpos - hneg) / norm
    alt = jnp.where((rows & 1) == 0, 1.0, -1.0).astype(F32)
    nyq_ref[...] = jnp.sum(alt * hsum, axis=0, keepdims=True)
    hs_ref[...] = hsum.astype(BF16)
    hd_ref[...] = hdiff.astype(BF16)


def _hy_filter_taps(seq, w1, b1, w2, b2, freq, w3, b3):
    n_ch = 2 * BRANCH_W
    bands = (HY_EMB - 1) // 2
    t = jnp.linspace(0.0, 1.0, seq, dtype=F32)[:, None]
    w = 2.0 * math.pi * jnp.arange(seq, dtype=F32)[:, None] / seq
    f = jnp.linspace(1e-4, bands - 1, bands, dtype=F32)[None, :]
    feats = jnp.concatenate([t, jnp.cos(f * w), -jnp.sin(f * w),
                             jnp.zeros((seq, HY_PAD - HY_EMB), F32)], axis=-1)
    deltas = jnp.linspace(math.log(HY_DECAY_TARGET) / HY_SLOW_DECAY,
                          math.log(HY_DECAY_TARGET) / HY_FAST_DECAY, n_ch, dtype=F32)[None, :]

    def pad2(a, r, c):
        a = a.astype(F32)
        return jnp.pad(a, ((0, r - a.shape[0]), (0, c - a.shape[1])))

    w1p = pad2(w1, HY_PAD, HY_PAD)
    w2p = pad2(w2, HY_PAD, HY_PAD)
    b1p = pad2(b1[None], 1, HY_PAD)
    b2p = pad2(b2[None], 1, HY_PAD)
    f0p = pad2(freq[0][None], 1, HY_PAD)
    f1p = pad2(freq[1][None], 1, HY_PAD)
    w3p = pad2(w3, HY_PAD, 2 * n_ch)
    b3r = b3.astype(F32)[None]
    tn = 256
    nt = n_ch // tn
    full = lambda shape: pl.BlockSpec(shape, lambda j: (0, 0))
    return pl.pallas_call(
        _hy_filter_kernel,
        out_shape=(jax.ShapeDtypeStruct((seq, n_ch), BF16), jax.ShapeDtypeStruct((seq, n_ch), BF16),
                   jax.ShapeDtypeStruct((1, n_ch), F32)),
        grid=(nt,),
        in_specs=[full((seq, HY_PAD)), full((HY_PAD, HY_PAD)), full((1, HY_PAD)),
                  full((HY_PAD, HY_PAD)), full((1, HY_PAD)), full((1, HY_PAD)), full((1, HY_PAD)),
                  pl.BlockSpec((HY_PAD, tn), lambda j: (0, j)),
                  pl.BlockSpec((HY_PAD, tn), lambda j: (0, j + nt)),
                  pl.BlockSpec((1, tn), lambda j: (0, j)),
                  pl.BlockSpec((1, tn), lambda j: (0, j + nt)),
                  pl.BlockSpec((1, tn), lambda j: (0, j)),
                  full((seq, 1))],
        out_specs=(pl.BlockSpec((seq, tn), lambda j: (0, j)), pl.BlockSpec((seq, tn), lambda j: (0, j)),
                   pl.BlockSpec((1, tn), lambda j: (0, j))),
        scratch_shapes=[pltpu.VMEM((seq, HY_PAD), F32)],
        compiler_params=_params("arbitrary"),
        name="hy_filter",
    )(feats, w1p, b1p, w2p, b2p, f0p, f1p, w3p, w3p, b3r, b3r, deltas, t)


def _hy_spectrum_kernel(c_ref, s_ref, hs_ref, hd_ref, hc_out, hso_out, *, inv_n):
    i = pl.program_id(1)
    zc = jnp.dot(c_ref[...], hs_ref[...], preferred_element_type=F32)
    zs = jnp.dot(s_ref[...], hd_ref[...], preferred_element_type=F32)
    rows = lax.broadcasted_iota(jnp.int32, zc.shape, 0) + i * zc.shape[0]
    first = rows == 0
    hc_out[...] = zc * jnp.where(first, inv_n, 2.0 * inv_n)
    hso_out[...] = jnp.where(first, 0.0, zs * (2.0 * inv_n))


def _hy_spectrum(cm, s1, hsum, hdiff):
    seq, n_ch = hsum.shape
    tm = min(512, seq)
    tn = min(512, n_ch)
    w_spec = pl.BlockSpec((tm, seq), lambda j, i: (i, 0))
    h_spec = pl.BlockSpec((seq, tn), lambda j, i: (0, j))
    o_spec = pl.BlockSpec((tm, tn), lambda j, i: (i, j))
    return pl.pallas_call(
        functools.partial(_hy_spectrum_kernel, inv_n=1.0 / (2 * seq)),
        out_shape=(jax.ShapeDtypeStruct((seq, n_ch), F32),) * 2,
        grid=(n_ch // tn, seq // tm),
        in_specs=[w_spec, w_spec, h_spec, h_spec],
        out_specs=(o_spec, o_spec),
        compiler_params=_params("parallel", "arbitrary"),
        name="hy_spectrum",
    )(cm, s1, hsum, hdiff)


def _hy_conv3_kernel(u_ref, w_ref, b_ref, o_ref, vb_ref):
    u = u_ref[...]
    n = u.shape[0]
    rows = lax.broadcasted_iota(jnp.int32, u.shape, 0)
    prev = jnp.where(rows == 0, 0.0, pltpu.roll(u, 1, 0))
    nxt = jnp.where(rows == n - 1, 0.0, pltpu.roll(u, n - 1, 0))
    w = w_ref[...]
    out = prev * w[0:1] + u * w[1:2] + nxt * w[2:3] + b_ref[...]
    o_ref[...] = out
    vb_ref[...] = out.astype(BF16)


def _hy_conv3(proj, bsz, seq, conv_w, conv_b):
    m = bsz * seq
    tn = 256
    nb = 3 * BRANCH_W // tn
    return pl.pallas_call(
        _hy_conv3_kernel,
        out_shape=(jax.ShapeDtypeStruct((m, 3 * BRANCH_W), F32),
                   jax.ShapeDtypeStruct((m, 3 * BRANCH_W), BF16)),
        grid=(bsz, nb),
        in_specs=[pl.BlockSpec((seq, tn), lambda b, j: (b, j + COL_BU // tn)),
                  pl.BlockSpec((3, tn), lambda b, j: (0, j)),
                  pl.BlockSpec((1, tn), lambda b, j: (0, j))],
        out_specs=(pl.BlockSpec((seq, tn), lambda b, j: (b, j)),
                   pl.BlockSpec((seq, tn), lambda b, j: (b, j))),
        compiler_params=_params("parallel", "parallel"),
        name="hy_conv3",
    )(proj, conv_w.astype(F32), conv_b.astype(F32).reshape(1, -1))


def _hy_fwd_kernel(c_ref, s_ref, z_ref, hc_ref, hs_ref, nyq_ref, yc_ref, ys_ref, *, inv_n):
    i = pl.program_id(2)
    z = z_ref[...]
    zc = jnp.dot(c_ref[...], z, preferred_element_type=F32)
    zs = jnp.dot(s_ref[...], z, preferred_element_type=F32)
    hc, hs = hc_ref[...], hs_ref[...]
    rows = lax.broadcasted_iota(jnp.int32, zc.shape, 0) + i * zc.shape[0]
    hd = jnp.where(rows == 0, nyq_ref[...] * inv_n, hc)
    yc_ref[...] = (zc * hc - zs * hs).astype(BF16)
    ys_ref[...] = (zc * hs + zs * hd).astype(BF16)


def _hy_fwd(cm, s1, zb, z_col, hc, hs, nyq, h_col, bsz, seq):
    m = bsz * seq
    tm = min(512, seq)
    tn = 512
    nt = BRANCH_W // tn
    mt = seq // tm
    w_spec = pl.BlockSpec((tm, seq), lambda b, j, i: (i, 0))
    h_spec = pl.BlockSpec((tm, tn), lambda b, j, i: (i, j + h_col // tn))
    o_spec = pl.BlockSpec((tm, tn), lambda b, j, i: (b * mt + i, j))
    return pl.pallas_call(
        functools.partial(_hy_fwd_kernel, inv_n=1.0 / (2 * seq)),
        out_shape=(jax.ShapeDtypeStruct((m, BRANCH_W), BF16),) * 2,
        grid=(bsz, nt, mt),
        in_specs=[w_spec, w_spec,
                  pl.BlockSpec((seq, tn), lambda b, j, i: (b, j + z_col // tn)),
                  h_spec, h_spec,
                  pl.BlockSpec((1, tn), lambda b, j, i: (0, j + h_col // tn))],
        out_specs=(o_spec, o_spec),
        compiler_params=_params("parallel", "parallel", "arbitrary"),
        name="hy_fwd",
    )(cm, s1, zb, hc, hs, nyq)


def _hy_inv_mid_kernel(c_ref, s_ref, yc_ref, ys_ref, gate_ref, z_ref, bias_ref, o_ref, ob_ref):
    conv = (jnp.dot(c_ref[...], yc_ref[...], preferred_element_type=F32)
            + jnp.dot(s_ref[...], ys_ref[...], preferred_element_type=F32))
    out = gate_ref[...] * (conv + bias_ref[...] * z_ref[...])
    o_ref[...] = out
    ob_ref[...] = out.astype(BF16)


def _hy_inv_last_kernel(c_ref, s_ref, yc_ref, ys_ref, gate_ref, z_ref, bias_ref, sz_ref, o_ref):
    conv = (jnp.dot(c_ref[...], yc_ref[...], preferred_element_type=F32)
            + jnp.dot(s_ref[...], ys_ref[...], preferred_element_type=F32))
    out = gate_ref[...] * (conv + bias_ref[...] * z_ref[...])
    o_ref[...] = (out * sz_ref[...]).astype(BF16)


def _hy_inv(cm, s2, yc, ys, gates, gate_col, zprev, z_col, bias_row, bsz, seq, silu_src=None):
    m = bsz * seq
    tm = min(512, seq)
    tn = 512
    nt = BRANCH_W // tn
    mt = seq // tm
    w_spec = pl.BlockSpec((tm, seq), lambda b, j, i: (i, 0))
    y_spec = pl.BlockSpec((seq, tn), lambda b, j, i: (b, j))
    o_spec = pl.BlockSpec((tm, tn), lambda b, j, i: (b * mt + i, j))
    in_specs = [w_spec, w_spec, y_spec, y_spec,
                pl.BlockSpec((tm, tn), lambda b, j, i: (b * mt + i, j + gate_col // tn)),
                pl.BlockSpec((tm, tn), lambda b, j, i: (b * mt + i, j + z_col // tn)),
                pl.BlockSpec((1, tn), lambda b, j, i: (0, j))]
    args = [cm, s2, yc, ys, gates, zprev, bias_row]
    if silu_src is None:
        body = _hy_inv_mid_kernel
        out_shape = (jax.ShapeDtypeStruct((m, BRANCH_W), F32), jax.ShapeDtypeStruct((m, BRANCH_W), BF16))
        out_specs = (o_spec, o_spec)
    else:
        body = _hy_inv_last_kernel
        in_specs.append(pl.BlockSpec((tm, tn), lambda b, j, i: (b * mt + i, j + COL_BZ // tn)))
        args.append(silu_src)
        out_shape = jax.ShapeDtypeStruct((m, BRANCH_W), BF16)
        out_specs = o_spec
    return pl.pallas_call(
        body, out_shape=out_shape, grid=(bsz, nt, mt), in_specs=in_specs, out_specs=out_specs,
        compiler_params=_params("parallel", "parallel", "arbitrary"),
        name="hy_inv",
    )(*args)


def _hyena_branch(proj, bsz, seq, dft, conv_w, conv_b, w1, b1, w2, b2, freq, w3, b3, bias):
    cm, s1, s2 = dft
    hsum, hdiff, nyq = _hy_filter_taps(seq, w1, b1, w2, b2, freq, w3, b3)
    hc, hs = _hy_spectrum(cm, s1, hsum, hdiff)
    uc, ucb = _hy_conv3(proj, bsz, seq, conv_w, conv_b)
    bias = bias.astype(F32)
    yc, ys = _hy_fwd(cm, s1, ucb, 0, hc, hs, nyq, 0, bsz, seq)
    z1, z1b = _hy_inv(cm, s2, yc, ys, uc, BRANCH_W, uc, 0, bias[0:1], bsz, seq)
    yc, ys = _hy_fwd(cm, s1, z1b, 0, hc, hs, nyq, BRANCH_W, bsz, seq)
    return _hy_inv(cm, s2, yc, ys, uc, 2 * BRANCH_W, z1, 0, bias[1:2], bsz, seq, silu_src=proj)


def _rope_table_kernel(pos_ref, inv_ref, cos_ref, sin_ref):
    ang = pos_ref[...] * inv_ref[...]
    lane = lax.broadcasted_iota(jnp.int32, ang.shape, 1)
    live = lane < MLA_ROPE
    cos_ref[...] = jnp.where(live, jnp.cos(ang), 0.0)
    sin_ref[...] = jnp.where(live, jnp.where(lane < MLA_ROPE // 2, -1.0, 1.0) * jnp.sin(ang), 0.0)


def _rope_tables(positions):
    m = positions.size
    half = MLA_ROPE // 2
    inv = ROPE_BASE ** (-jnp.arange(half, dtype=F32) / half)
    inv = jnp.concatenate([inv, inv, jnp.zeros((128 - MLA_ROPE,), F32)])[None]
    pos = positions.astype(F32).reshape(m, 1)
    tm = min(1024, m)
    spec = pl.BlockSpec((tm, 128), lambda i: (i, 0))
    return pl.pallas_call(
        _rope_table_kernel,
        out_shape=(jax.ShapeDtypeStruct((m, 128), F32),) * 2,
        grid=(m // tm,),
        in_specs=[pl.BlockSpec((tm, 1), lambda i: (i, 0)), pl.BlockSpec((1, 128), lambda i: (0, 0))],
        out_specs=(spec, spec),
        compiler_params=_params("parallel"),
        name="rope_table",
    )(pos, inv)


def _rope128(x, cos_t, sin_t):
    lane = lax.broadcasted_iota(jnp.int32, x.shape, 1)
    half = MLA_ROPE // 2
    partner = jnp.where(lane < half, pltpu.roll(x, 128 - half, 1), pltpu.roll(x, half, 1))
    return x * cos_t + partner * sin_t


def _rms(x, g):
    ms = jnp.mean(jnp.square(x), axis=-1, keepdims=True)
    return x * lax.rsqrt(ms + RMS_EPS) * g


def _mla_q_kernel(cq_ref, g_ref, w_ref, cos_ref, sin_ref, q_ref, *, scale):
    xn = _rms(cq_ref[...], g_ref[...]).astype(BF16)
    q = jnp.dot(xn, w_ref[...], preferred_element_type=F32) * scale
    cos_t, sin_t = cos_ref[...], sin_ref[...]
    for h in range(MLA_HEADS):
        base = h * MLA_QK_PAD
        q_ref[:, base:base + MLA_NOPE] = q[:, base:base + MLA_NOPE].astype(BF16)
        q_ref[:, base + MLA_NOPE:base + MLA_QK_PAD] = _rope128(
            q[:, base + MLA_NOPE:base + MLA_QK_PAD], cos_t, sin_t).astype(BF16)


def _mla_kv_kernel(ckv_ref, g_ref, wk_ref, wv_ref, kr_ref, cos_ref, sin_ref, k_ref, v_ref):
    xn = _rms(ckv_ref[...], g_ref[...]).astype(BF16)
    kn = jnp.dot(xn, wk_ref[...], preferred_element_type=F32)
    v_ref[...] = jnp.dot(xn, wv_ref[...], preferred_element_type=F32).astype(BF16)
    kr = _rope128(kr_ref[:, 0:128], cos_ref[...], sin_ref[...]).astype(BF16)
    for h in range(MLA_HEADS):
        base = h * MLA_QK_PAD
        k_ref[:, base:base + MLA_NOPE] = kn[:, h * MLA_NOPE:(h + 1) * MLA_NOPE].astype(BF16)
        k_ref[:, base + MLA_NOPE:base + MLA_QK_PAD] = kr


def _mla_attn_kernel(q_ref, k_ref, v_ref, z_ref, o_ref):
    s = lax.dot_general(q_ref[...], k_ref[...], (((1,), (1,)), ((), ())),
                        preferred_element_type=F32)
    p = jnp.exp(s - jnp.max(s, axis=-1, keepdims=True))
    l = jnp.sum(p, axis=-1, keepdims=True)
    o = jnp.dot(p.astype(BF16), v_ref[...], preferred_element_type=F32)
    o_ref[...] = (o / l * z_ref[...]).astype(BF16)


def _mla_branch(proj, proj_b, bsz, seq, rope, q_norm_g, w_uq, kv_norm_g, w_ukv):
    m = bsz * seq
    cos_t, sin_t = rope
    dqk = MLA_NOPE + MLA_ROPE
    hq = MLA_HEADS * MLA_QK_PAD
    w_q = w_uq.reshape(MLA_LORA, MLA_HEADS, dqk)
    w_q = jnp.pad(w_q, ((0, 0), (0, 0), (0, MLA_QK_PAD - dqk))).reshape(MLA_LORA, hq).astype(BF16)
    w_kv = w_ukv.reshape(MLA_LORA, MLA_HEADS, MLA_NOPE + MLA_V)
    w_k = w_kv[:, :, :MLA_NOPE].reshape(MLA_LORA, MLA_HEADS * MLA_NOPE).astype(BF16)
    w_v = w_kv[:, :, MLA_NOPE:].reshape(MLA_LORA, MLA_HEADS * MLA_V).astype(BF16)
    tm = min(512, m)
    row = lambda shape, col=0: pl.BlockSpec(shape, lambda i: (i, col))
    full = lambda shape: pl.BlockSpec(shape, lambda i: (0, 0))
    qp = pl.pallas_call(
        functools.partial(_mla_q_kernel, scale=dqk ** -0.5),
        out_shape=jax.ShapeDtypeStruct((m, hq), BF16),
        grid=(m // tm,),
        in_specs=[row((tm, MLA_LORA), COL_CQ // MLA_LORA), full((1, MLA_LORA)), full((MLA_LORA, hq)),
                  row((tm, 128)), row((tm, 128))],
        out_specs=row((tm, hq)),
        compiler_params=_params("parallel"),
        name="mla_q",
    )(proj, q_norm_g.astype(F32).reshape(1, -1), w_q, cos_t, sin_t)
    kp, vp = pl.pallas_call(
        _mla_kv_kernel,
        out_shape=(jax.ShapeDtypeStruct((m, hq), BF16),
                   jax.ShapeDtypeStruct((m, MLA_HEADS * MLA_V), BF16)),
        grid=(m // tm,),
        in_specs=[row((tm, MLA_LORA), COL_CKV // MLA_LORA), full((1, MLA_LORA)),
                  full((MLA_LORA, MLA_HEADS * MLA_NOPE)), full((MLA_LORA, MLA_HEADS * MLA_V)),
                  row((tm, 512), COL_CKR // 512), row((tm, 128)), row((tm, 128))],
        out_specs=(row((tm, hq)), row((tm, MLA_HEADS * MLA_V))),
        compiler_params=_params("parallel"),
        name="mla_kv",
    )(proj, kv_norm_g.astype(F32).reshape(1, -1), w_k, w_v, proj_b, cos_t, sin_t)
    tq = min(256, seq)
    qt = seq // tq
    return pl.pallas_call(
        _mla_attn_kernel,
        out_shape=jax.ShapeDtypeStruct((m, MLA_HEADS * MLA_V), BF16),
        grid=(bsz, MLA_HEADS, qt),
        in_specs=[pl.BlockSpec((tq, MLA_QK_PAD), lambda b, h, i: (b * qt + i, h)),
                  pl.BlockSpec((seq, MLA_QK_PAD), lambda b, h, i: (b, h)),
                  pl.BlockSpec((seq, MLA_V), lambda b, h, i: (b, h)),
                  pl.BlockSpec((tq, MLA_V), lambda b, h, i: (b * qt + i, h + COL_CZ // MLA_V))],
        out_specs=pl.BlockSpec((tq, MLA_V), lambda b, h, i: (b * qt + i, h)),
        compiler_params=_params("parallel", "parallel", "arbitrary"),
        name="mla_attn",
    )(qp, kp, vp, proj_b)


def _lift_kernel(ya_ref, yb_ref, yc_ref, w_ref, ga_ref, gb_ref, gc_ref, o_ref):
    acc = ga_ref[...] * jnp.dot(ya_ref[...], w_ref[0], preferred_element_type=F32)
    acc += gb_ref[...] * jnp.dot(yb_ref[...], w_ref[1], preferred_element_type=F32)
    acc += gc_ref[...] * jnp.dot(yc_ref[...], w_ref[2], preferred_element_type=F32)
    o_ref[...] = acc.astype(BF16)


def _lift(ya, yb, yc, w_lift, proj):
    m = ya.shape[0]
    tm = min(512, m)
    tn = 512
    y_spec = pl.BlockSpec((tm, BRANCH_W), lambda i, j: (i, 0))
    gate = lambda n: pl.BlockSpec((tm, tn), lambda i, j: (i, j + (COL_GATE + n * D_MODEL) // tn))
    return pl.pallas_call(
        _lift_kernel,
        out_shape=jax.ShapeDtypeStruct((m, D_MODEL), BF16),
        grid=(m // tm, D_MODEL // tn),
        in_specs=[y_spec, y_spec, y_spec,
                  pl.BlockSpec((N_BRANCH, BRANCH_W, tn), lambda i, j: (0, 0, j)),
                  gate(0), gate(1), gate(2)],
        out_specs=pl.BlockSpec((tm, tn), lambda i, j: (i, j)),
        compiler_params=_params("parallel", "arbitrary"),
        name="lift",
    )(ya, yb, yc, w_lift.astype(BF16), proj, proj, proj)


def _out_kernel(mix_ref, wo_ref, p_ref, wp_ref, sp_ref, x_ref, g_ref, b_ref, o_ref):
    mixed = jnp.dot(mix_ref[...], wo_ref[...], preferred_element_type=F32)
    ple = jnp.dot(p_ref[...].astype(BF16), wp_ref[...], preferred_element_type=F32) * sp_ref[...]
    r = DEEPNORM_ALPHA * x_ref[...] + mixed + ple
    mu = jnp.mean(r, axis=-1, keepdims=True)
    var = jnp.mean(jnp.square(r - mu), axis=-1, keepdims=True)
    o_ref[...] = (r - mu) * lax.rsqrt(var + LN_EPS) * g_ref[...] + b_ref[...]


def _out_norm(mix, w_out, p, w_ple, proj, x, ln_g, ln_b):
    m = mix.shape[0]
    tm = min(256, m)
    row = lambda w, col=0: pl.BlockSpec((tm, w), lambda i: (i, col))
    full = lambda shape: pl.BlockSpec(shape, lambda i: (0, 0))
    return pl.pallas_call(
        _out_kernel,
        out_shape=jax.ShapeDtypeStruct((m, D_MODEL), F32),
        grid=(m // tm,),
        in_specs=[row(D_MODEL), full((D_MODEL, D_MODEL)), row(PLE_DIM), full((PLE_DIM, D_MODEL)),
                  row(D_MODEL, COL_PLE // D_MODEL), row(D_MODEL), full((1, D_MODEL)), full((1, D_MODEL))],
        out_specs=row(D_MODEL),
        compiler_params=_params("parallel"),
        name="out_norm",
    )(mix, w_out.astype(BF16), p, w_ple.astype(BF16), proj, x,
      ln_g.astype(F32).reshape(1, -1), ln_b.astype(F32).reshape(1, -1))


def kernel(x, p, positions, w_in, s5_lambda_re, s5_lambda_im, s5_log_dt, s5_b_re, s5_b_im, s5_c_re, s5_c_im, s5_d, s5_w_glu, s5_b_glu, hy_conv_w, hy_conv_b, hy_w1, hy_b1, hy_w2, hy_b2, hy_freq, hy_w3, hy_b3, hy_bias, mla_q_norm, mla_w_uq, mla_kv_norm, mla_w_ukv, w_lift, w_out, w_ple, ln_g, ln_b):
    bsz, seq, _ = x.shape
    m = bsz * seq
    depth = w_in.shape[0]
    dft = _dft_matrices(seq)
    rope = _rope_tables(positions)
    s5_tiled = _s5_tiled_params(s5_lambda_re, s5_lambda_im, s5_log_dt, s5_b_re, s5_b_im,
                                s5_c_re, s5_c_im, s5_d)
    xf = x.reshape(m, D_MODEL).astype(F32)
    for i in range(depth):
        proj, proj_b = _proj_both(xf, w_in[i])
        y_a = _s5_branch(proj, bsz, seq, [a[i] for a in s5_tiled], s5_w_glu[i], s5_b_glu[i])
        y_b = _hyena_branch(proj, bsz, seq, dft, hy_conv_w[i], hy_conv_b[i], hy_w1[i], hy_b1[i],
                            hy_w2[i], hy_b2[i], hy_freq[i], hy_w3[i], hy_b3[i], hy_bias[i])
        y_c = _mla_branch(proj, proj_b, bsz, seq, rope, mla_q_norm[i], mla_w_uq[i], mla_kv_norm[i], mla_w_ukv[i])
        mix = _lift(y_a, y_b, y_c, w_lift[i], proj_b)
        xf = _out_norm(mix, w_out[i], p[i].reshape(m, PLE_DIM), w_ple[i], proj_b, xf, ln_g[i], ln_b[i])
    return xf.reshape(bsz, seq, D_MODEL).astype(x.dtype)
```

```python
import functools
import math

import numpy as np
import jax
import jax.numpy as jnp
from jax import lax
from jax.experimental import pallas as pl
from jax.experimental.pallas import tpu as pltpu

F32 = jnp.float32
BF16 = jnp.bfloat16

D_MODEL = 2048
PLE_DIM = 256
N_BRANCH = 3
BRANCH_W = 1024

S5_GROUP = 16
S5_GROUPS = BRANCH_W // S5_GROUP
S5_STATE = 64
S5_CHUNK = 16
S5_ROW = S5_CHUNK * S5_GROUP

HY_EMB = 33
HY_FF = 64
HY_PAD = 128
HY_DECAY_TARGET = 0.01
HY_FAST_DECAY = 0.3
HY_SLOW_DECAY = 1.5
DFT_ROWS = 64

MLA_HEADS = 8
MLA_NOPE = 128
MLA_ROPE = 64
MLA_V = 128
MLA_LORA = 512
MLA_QK_PAD = 256
ROPE_BASE = 10000.0

LN_EPS = 1e-5
RMS_EPS = 1e-6
DEPTH = 2
DEEPNORM_ALPHA = (2 * DEPTH) ** 0.25

COL_AX = 0
COL_AZ = 1024
COL_BU = 2048
COL_BZ = 5120
COL_CQ = 6144
COL_CKV = 6656
PROJ_A_N = 7168
COL_PLE = 0
COL_GATE = 2048
COL_CZ = 8192
COL_CKR = 9216
PROJ_B_N = 9728
PROJ_TN = 512
W_IN_CKR = 7168
W_IN_CZ = 7232
W_IN_GATE = 8256
W_IN_PLE = 14400

VMEM_LIMIT = 56 * 1024 * 1024


def _params(*sem):
    return pltpu.CompilerParams(dimension_semantics=sem, vmem_limit_bytes=VMEM_LIMIT)


def _sigmoid(x):
    return 0.5 * jnp.tanh(0.5 * x) + 0.5


def _in_tiles(j, ranges):
    hit = None
    for lo, hi in ranges:
        cur = (j >= lo // PROJ_TN) & (j < hi // PROJ_TN)
        hit = cur if hit is None else (hit | cur)
    return hit


def _proj_kernel(x_ref, w_ref, o_ref, xb_ref, *, silu_cols, sigm_cols):
    j = pl.program_id(1)

    @pl.when(j == 0)
    def _():
        xb_ref[...] = x_ref[...].astype(BF16)

    acc = jnp.dot(xb_ref[...], w_ref[...].astype(BF16), preferred_element_type=F32)
    is_silu = _in_tiles(j, silu_cols)
    plain = jnp.logical_not(is_silu)

    @pl.when(is_silu)
    def _():
        o_ref[...] = acc * _sigmoid(acc)

    if sigm_cols:
        is_sigm = _in_tiles(j, sigm_cols)
        plain = jnp.logical_not(is_silu | is_sigm)

        @pl.when(is_sigm)
        def _():
            o_ref[...] = _sigmoid(acc)

    @pl.when(plain)
    def _():
        o_ref[...] = acc


def _proj(x, w, n_out, silu_cols, sigm_cols, name):
    m, k = x.shape
    tm = min(1024, m)
    return pl.pallas_call(
        functools.partial(_proj_kernel, silu_cols=silu_cols, sigm_cols=sigm_cols),
        out_shape=jax.ShapeDtypeStruct((m, n_out), F32),
        grid=(m // tm, n_out // PROJ_TN),
        in_specs=[pl.BlockSpec((tm, k), lambda i, j: (i, 0)),
                  pl.BlockSpec((k, PROJ_TN), lambda i, j: (0, j))],
        out_specs=pl.BlockSpec((tm, PROJ_TN), lambda i, j: (i, j)),
        scratch_shapes=[pltpu.VMEM((tm, k), BF16)],
        compiler_params=_params("parallel", "arbitrary"),
        name=name,
    )(x, w)


def _repack_src_tile(j):
    t = lambda col: col // PROJ_TN
    return jnp.where(j < t(COL_GATE), j + t(W_IN_PLE),
                     jnp.where(j < t(COL_CZ), j - t(COL_GATE) + t(W_IN_GATE),
                               jnp.where(j < t(COL_CKR), j - t(COL_CZ) + t(W_IN_CZ), t(W_IN_CKR))))


def _repack_kernel(a_ref, b_ref, o_ref):
    is_last = pl.program_id(0) == pl.num_programs(0) - 1
    n_sub = PROJ_TN // 128
    src = [a_ref[:, c * 128:(c + 1) * 128] for c in range(n_sub)] + [b_ref[...]]
    lane = lax.broadcasted_iota(jnp.int32, src[0].shape, 1)
    low = lane < MLA_ROPE
    rolled = [pltpu.roll(t, 128 - MLA_ROPE, 1) for t in src]
    for c in range(n_sub):
        shifted = jnp.where(low, rolled[c], rolled[c + 1])
        plain = jnp.where(low, src[0], 0.0) if c == 0 else jnp.zeros_like(shifted)
        o_ref[:, c * 128:(c + 1) * 128] = jnp.where(is_last, plain, shifted).astype(BF16)


def _repack_b(w_in):
    k = w_in.shape[0]
    sub = PROJ_TN // 128
    return pl.pallas_call(
        _repack_kernel,
        out_shape=jax.ShapeDtypeStruct((k, PROJ_B_N), BF16),
        grid=(PROJ_B_N // PROJ_TN,),
        in_specs=[pl.BlockSpec((k, PROJ_TN), lambda j: (0, _repack_src_tile(j))),
                  pl.BlockSpec((k, 128), lambda j: (0, (_repack_src_tile(j) + 1) * sub))],
        out_specs=pl.BlockSpec((k, PROJ_TN), lambda j: (0, j)),
        compiler_params=_params("parallel"),
        name="repack_b",
    )(w_in, w_in)


def _proj_both(x, w_in):
    proj_a = _proj(x, w_in, PROJ_A_N, ((COL_AZ, COL_BU), (COL_BZ, COL_CQ)), (), "proj_a")
    proj_b = _proj(x, _repack_b(w_in), PROJ_B_N, ((COL_CZ, COL_CKR),), ((COL_PLE, COL_CZ),), "proj_b")
    return proj_a, proj_b


S5_LANES = 8 * S5_STATE
S5_SLAB = 8
S5_RELAYOUT_ROWS = 32


def _s5_tile_lanes(a):
    return jnp.concatenate([a[:, 0]] * 4 + [a[:, 1]] * 4, axis=-1)


def _s5_tiled_params(lam_re, lam_im, log_dt, b_re, b_im, c_re, c_im, d):
    f = lambda a: a.astype(F32)
    depth = lam_re.shape[0]
    ldt = jnp.broadcast_to(f(log_dt)[..., None, None], lam_re.shape[:3] + (1, S5_STATE))
    return (_s5_tile_lanes(f(lam_re)[:, :, :, None, :]), _s5_tile_lanes(f(lam_im)[:, :, :, None, :]),
            _s5_tile_lanes(ldt),
            _s5_tile_lanes(jnp.swapaxes(f(b_re), -1, -2)), _s5_tile_lanes(jnp.swapaxes(f(b_im), -1, -2)),
            _s5_tile_lanes(f(c_re)), _s5_tile_lanes(f(c_im)),
            f(d).reshape(depth, S5_GROUPS, S5_GROUP, 1))


def _s5_mats_kernel(lr_ref, li_ref, ldt_ref, br_ref, bi_ref, cr_ref, ci_ref, d_ref,
                    ms_ref, mi_ref, mo_ref, ar_ref, ai_ref):
    t_n, h_n = S5_CHUNK, S5_GROUP
    hi = lax.Precision.HIGHEST
    nt = (((1,), (1,)), ((), ()))
    blk = lax.broadcasted_iota(jnp.int32, (1, S5_LANES), 1) // S5_STATE
    is_im = (blk // 2) % 2 == 1
    is_fwd = blk < 4
    steps = lax.broadcasted_iota(jnp.int32, (24, S5_LANES), 0).astype(F32)
    lane_k = lax.broadcasted_iota(jnp.int32, (h_n, S5_ROW), 1)
    sub_k = lax.broadcasted_iota(jnp.int32, (h_n, S5_ROW), 0)

    def per_group(gi, carry):
        lr, li = lr_ref[gi], li_ref[gi]
        dt = jnp.exp(ldt_ref[gi])
        zr, zi = lr * dt, li * dt
        mag = jnp.exp(steps * zr)
        tr, ti = mag * jnp.cos(steps * zi), mag * jnp.sin(steps * zi)
        lbr, lbi = tr[1:2], ti[1:2]
        n2 = lr * lr + li * li
        qr = ((lbr - 1.0) * lr + lbi * li) / n2
        qi = (lbi * lr - (lbr - 1.0) * li) / n2
        br, bi = br_ref[gi], bi_ref[gi]
        bbr, bbi = qr * br - qi * bi, qr * bi + qi * br
        y1, y2 = jnp.where(is_im, bbi, bbr), jnp.where(is_im, bbr, bbi)
        cr, ci = cr_ref[gi], ci_ref[gi]

        def pick(tab, t_fwd, t_bwd):
            return jnp.where(is_fwd, tab[t_fwd:t_fwd + 1], tab[t_bwd:t_bwd + 1])

        def c_times(p_r, p_i):
            return jnp.where(is_im, -(cr * p_i + ci * p_r), cr * p_r - ci * p_i)

        q_rows = []
        for t in range(t_n):
            a_r, a_i = pick(tr, t_n - 1 - t, t), pick(ti, t_n - 1 - t, t)
            rows = pl.ds(t * h_n, h_n)
            ms_ref[gi, rows, :] = (a_r * y1 + jnp.where(is_im, a_i, -a_i) * y2).astype(BF16)
            mo_ref[gi, rows, :] = c_times(pick(tr, t + 1, t_n - t), pick(ti, t + 1, t_n - t)).astype(BF16)
            q_rows.append(c_times(pick(tr, t, t_n - 1 - t), pick(ti, t, t_n - 1 - t)))
        q = jnp.concatenate(q_rows, axis=0)
        half = S5_LANES // 2
        kf = 0.5 * lax.dot_general(y1[:, :half], q[:, :half], nt, precision=hi, preferred_element_type=F32)
        kb = 0.5 * lax.dot_general(y1[:, half:], q[:, half:], nt, precision=hi, preferred_element_type=F32)
        kf = kf + jnp.where(lane_k == sub_k, d_ref[gi], 0.0)
        for t in range(t_n):
            fwd = kf if t == 0 else jnp.where(lane_k >= h_n * t, pltpu.roll(kf, h_n * t, 1), 0.0)
            sh = h_n * (t_n - 1 - t)
            bwd = kb if sh == 0 else jnp.where(lane_k < S5_ROW - sh, pltpu.roll(kb, S5_ROW - sh, 1), 0.0)
            mi_ref[gi, pl.ds(t * h_n, h_n), :] = (fwd + bwd).astype(BF16)
        ar_ref[gi] = tr[t_n:t_n + 1]
        ai_ref[gi] = ti[t_n:t_n + 1]
        return carry

    lax.fori_loop(0, lr_ref.shape[0], per_group, 0)


def _s5_mats(tiled):
    g_n = S5_GROUPS
    gb = S5_SLAB
    spec = lambda r, w: pl.BlockSpec((gb, r, w), lambda j: (j, 0, 0))
    in_rows = (1, 1, 1, S5_GROUP, S5_GROUP, S5_GROUP, S5_GROUP)
    return pl.pallas_call(
        _s5_mats_kernel,
        out_shape=(jax.ShapeDtypeStruct((g_n, S5_ROW, S5_LANES), BF16),
                   jax.ShapeDtypeStruct((g_n, S5_ROW, S5_ROW), BF16),
                   jax.ShapeDtypeStruct((g_n, S5_ROW, S5_LANES), BF16),
                   jax.ShapeDtypeStruct((g_n, 1, S5_LANES), F32),
                   jax.ShapeDtypeStruct((g_n, 1, S5_LANES), F32)),
        grid=(g_n // gb,),
        in_specs=[spec(r, S5_LANES) for r in in_rows] + [spec(S5_GROUP, 1)],
        out_specs=(spec(S5_ROW, S5_LANES), spec(S5_ROW, S5_ROW), spec(S5_ROW, S5_LANES),
                   spec(1, S5_LANES), spec(1, S5_LANES)),
        compiler_params=_params("parallel"),
        name="s5_mats",
    )(*tiled)


def _seg_transpose(vs, seg):
    vs = list(vs)
    for s in (4, 2, 1):
        keep = (seg & s) == 0
        for i in range(8):
            if i & s:
                continue
            a, b = vs[i], vs[i + s]
            vs[i] = jnp.where(keep, a, pltpu.roll(b, s * S5_GROUP, 1))
            vs[i + s] = jnp.where(keep, pltpu.roll(a, 128 - s * S5_GROUP, 1), b)
    return vs


def _s5_main_kernel(x_ref, ms_ref, mi_ref, mo_ref, ar_ref, ai_ref, y_ref,
                    u_ref, sl_ref, st_ref, yg_ref, *, bsz):
    rows = x_ref.shape[0]
    n_chunks = rows // bsz
    rc = S5_RELAYOUT_ROWS
    seg = lax.broadcasted_iota(jnp.int32, (rc, 128), 1) // S5_GROUP
    slot = (lax.broadcasted_iota(jnp.int32, (1, S5_LANES), 1) // S5_STATE) % 2
    slot128 = slot[:, :128]

    def relayout_in(r, carry):
        r0 = pl.multiple_of(r * rc, rc)
        for th in range(2):
            src = [x_ref[pl.ds(r0, rc), th * 8 + t8, :] for t8 in range(8)]
            for gi, out in enumerate(_seg_transpose(src, seg)):
                u_ref[gi, pl.ds(r0, rc), th * 128:(th + 1) * 128] = out.astype(BF16)
        return carry

    lax.fori_loop(0, rows // rc, relayout_in, 0)

    a_parts = [[], [], [], []]
    for jp in range(S5_SLAB // 2):
        g0, g1 = 2 * jp, 2 * jp + 1
        r0 = jnp.dot(u_ref[g0], ms_ref[g0], preferred_element_type=F32)
        r1 = jnp.dot(u_ref[g1], ms_ref[g1], preferred_element_type=F32)
        loc = jnp.where(slot == 0, r0, r1)
        for k in range(4):
            sl_ref[k, :, jp * 128:(jp + 1) * 128] = loc[:, k * 128:(k + 1) * 128]
        for k, (ref, off) in enumerate(((ar_ref, 0), (ai_ref, 0), (ar_ref, 256), (ai_ref, 256))):
            a_parts[k].append(jnp.where(slot128 == 0, ref[g0][:, off:off + 128], ref[g1][:, off:off + 128]))
    a_fr, a_fi, a_br, a_bi = [jnp.concatenate(parts, axis=1) for parts in a_parts]

    zero = jnp.zeros((1, a_fr.shape[1]), F32)

    def scan(c, carry):
        cb = n_chunks - 1 - c
        new = []
        for b in range(bsz):
            s_fr, s_fi, s_br, s_bi = carry[b]
            rf, rb = b * n_chunks + c, b * n_chunks + cb
            st_ref[0, pl.ds(rf, 1), :] = s_fr
            st_ref[1, pl.ds(rf, 1), :] = s_fi
            st_ref[2, pl.ds(rb, 1), :] = s_br
            st_ref[3, pl.ds(rb, 1), :] = s_bi
            n_fr = a_fr * s_fr - a_fi * s_fi + sl_ref[0, pl.ds(rf, 1), :]
            n_fi = a_fr * s_fi + a_fi * s_fr + sl_ref[1, pl.ds(rf, 1), :]
            n_br = a_br * s_br - a_bi * s_bi + sl_ref[2, pl.ds(rb, 1), :]
            n_bi = a_br * s_bi + a_bi * s_br + sl_ref[3, pl.ds(rb, 1), :]
            new.append((n_fr, n_fi, n_br, n_bi))
        return tuple(new)

    lax.fori_loop(0, n_chunks, scan, tuple((zero,) * 4 for _ in range(bsz)))

    nt = (((1,), (1,)), ((), ()))
    for jp in range(S5_SLAB // 2):
        st = jnp.concatenate([st_ref[k, :, jp * 128:(jp + 1) * 128] for k in range(4)], axis=1)
        for e in range(2):
            g = 2 * jp + e
            st_g = jnp.where(slot == e, st, 0.0).astype(BF16)
            y = (jnp.dot(u_ref[g], mi_ref[g], preferred_element_type=F32)
                 + lax.dot_general(st_g, mo_ref[g], nt, preferred_element_type=F32))
            yg_ref[g] = jax.nn.gelu(y)

    def relayout_out(r, carry):
        r0 = pl.multiple_of(r * rc, rc)
        for th in range(2):
            src = [yg_ref[gi, pl.ds(r0, rc), th * 128:(th + 1) * 128] for gi in range(S5_SLAB)]
            for t8, out in enumerate(_seg_transpose(src, seg)):
                y_ref[pl.ds(r0, rc), th * 8 + t8, :] = out
        return carry

    lax.fori_loop(0, rows // rc, relayout_out, 0)


def _s5_glu_kernel(g_ref, w_ref, b_ref, z_ref, o_ref):
    g = g_ref[...]
    acc = jnp.dot(g.astype(BF16), w_ref[...], preferred_element_type=F32) + b_ref[...]
    o_ref[...] = (g * _sigmoid(acc) * z_ref[...]).astype(o_ref.dtype)


def _s5_branch(proj, bsz, seq, tiled, w_glu, b_glu):
    m_state, m_intra, m_out, a_re, a_im = _s5_mats(tiled)
    t_n = S5_CHUNK
    rows = bsz * (seq // t_n)
    m = bsz * seq
    gb = S5_SLAB
    x3 = proj.reshape(rows, t_n, proj.shape[1])
    mat = lambda r, w: pl.BlockSpec((gb, r, w), lambda s: (s, 0, 0))
    io_spec = pl.BlockSpec((rows, t_n, 128), lambda s: (0, 0, s + COL_AX // 128))
    y = pl.pallas_call(
        functools.partial(_s5_main_kernel, bsz=bsz),
        out_shape=jax.ShapeDtypeStruct((rows, t_n, BRANCH_W), F32),
        grid=(S5_GROUPS // gb,),
        in_specs=[io_spec, mat(S5_ROW, S5_LANES), mat(S5_ROW, S5_ROW), mat(S5_ROW, S5_LANES),
                  mat(1, S5_LANES), mat(1, S5_LANES)],
        out_specs=pl.BlockSpec((rows, t_n, 128), lambda s: (0, 0, s)),
        scratch_shapes=[pltpu.VMEM((gb, rows, S5_ROW), BF16),
                        pltpu.VMEM((4, rows, S5_LANES), F32),
                        pltpu.VMEM((4, rows, S5_LANES), F32),
                        pltpu.VMEM((gb, rows, S5_ROW), F32)],
        compiler_params=_params("parallel"),
        name="s5_main",
    )(x3, m_state, m_intra, m_out, a_re, a_im)
    y = y.reshape(m, BRANCH_W)
    tm = min(512, m)
    return pl.pallas_call(
        _s5_glu_kernel,
        out_shape=jax.ShapeDtypeStruct((m, BRANCH_W), BF16),
        grid=(m // tm,),
        in_specs=[pl.BlockSpec((tm, BRANCH_W), lambda i: (i, 0)),
                  pl.BlockSpec((BRANCH_W, BRANCH_W), lambda i: (0, 0)),
                  pl.BlockSpec((1, BRANCH_W), lambda i: (0, 0)),
                  pl.BlockSpec((tm, BRANCH_W), lambda i: (i, COL_AZ // BRANCH_W))],
        out_specs=pl.BlockSpec((tm, BRANCH_W), lambda i: (i, 0)),
        compiler_params=_params("parallel"),
        name="s5_glu",
    )(y, w_glu.astype(BF16), b_glu.astype(F32).reshape(1, BRANCH_W), proj)


def _dft_tables(seq):
    n = 2 * seq
    mm = np.arange(seq, dtype=np.int64)
    k1 = np.arange(seq // DFT_ROWS, dtype=np.int64)[:, None] * DFT_ROWS
    k0 = np.arange(DFT_ROWS, dtype=np.int64)[:, None]
    ang_a = 2.0 * np.pi * ((k1 * mm) % n).astype(np.float64) / n
    ang_b = 2.0 * np.pi * ((k0 * mm) % n).astype(np.float64) / n
    return tuple(jnp.asarray(t, F32) for t in (np.cos(ang_a), np.sin(ang_a), np.cos(ang_b), np.sin(ang_b)))


def _dft_gen_kernel(ac_ref, as_ref, bc_ref, bs_ref, c_ref, s1_ref, s2_ref):
    i = pl.program_id(0)
    a_c = ac_ref[pl.ds(i, 1), :]
    a_s = as_ref[pl.ds(i, 1), :]
    b_c, b_s = bc_ref[...], bs_ref[...]
    cos_t = a_c * b_c - a_s * b_s
    sin_t = a_s * b_c + a_c * b_s
    rows = lax.broadcasted_iota(jnp.int32, cos_t.shape, 0) + i * DFT_ROWS
    cols = lax.broadcasted_iota(jnp.int32, cos_t.shape, 1)
    alt_cols = jnp.where((cols & 1) == 0, 1.0, -1.0).astype(F32)
    alt_rows = jnp.where((rows & 1) == 0, 1.0, -1.0).astype(F32)
    c_ref[...] = cos_t.astype(BF16)
    s1_ref[...] = jnp.where(rows == 0, alt_cols, sin_t).astype(BF16)
    s2_ref[...] = jnp.where(cols == 0, alt_rows, sin_t).astype(BF16)


def _dft_matrices(seq):
    tabs = _dft_tables(seq)
    n_steps = seq // DFT_ROWS
    tab_spec = pl.BlockSpec(tabs[0].shape, lambda i: (0, 0))
    b_spec = pl.BlockSpec((DFT_ROWS, seq), lambda i: (0, 0))
    o_spec = pl.BlockSpec((DFT_ROWS, seq), lambda i: (i, 0))
    return pl.pallas_call(
        _dft_gen_kernel,
        out_shape=(jax.ShapeDtypeStruct((seq, seq), BF16),) * 3,
        grid=(n_steps,),
        in_specs=[tab_spec, tab_spec, b_spec, b_spec],
        out_specs=(o_spec,) * 3,
        compiler_params=_params("parallel"),
        name="dft_gen",
    )(*tabs)


def _hy_filter_kernel(feat_ref, w1_ref, b1_ref, w2_ref, b2_ref, f0_ref, f1_ref,
                      w3p_ref, w3n_ref, b3p_ref, b3n_ref, dl_ref, t_ref,
                      hs_ref, hd_ref, nyq_ref, h_ref):
    hi = lax.Precision.HIGHEST

    @pl.when(pl.program_id(0) == 0)
    def _():
        h1 = jnp.sin(f0_ref[...] * (jnp.dot(feat_ref[...], w1_ref[...], precision=hi,
                                            preferred_element_type=F32) + b1_ref[...]))
        h_ref[...] = jnp.sin(f1_ref[...] * (jnp.dot(h1, w2_ref[...], precision=hi,
                                                    preferred_element_type=F32) + b2_ref[...]))

    h = h_ref[...]
    win = jnp.exp(-t_ref[...] * jnp.abs(dl_ref[...]))
    hpos = (jnp.dot(h, w3p_ref[...], precision=hi, preferred_element_type=F32) + b3p_ref[...]) * win
    hneg = (jnp.dot(h, w3n_ref[...], precision=hi, preferred_element_type=F32) + b3n_ref[...]) * win
    rows = lax.broadcasted_iota(jnp.int32, hpos.shape, 0)
    hneg = jnp.where(rows == 0, 0.0, hneg)
    norm = (jnp.sum(jnp.abs(hpos), axis=0, keepdims=True)
            + jnp.sum(jnp.abs(hneg), axis=0, keepdims=True))
    hsum = (hpos + hneg) / norm
    hdiff = (hpos - hneg) / norm
    alt = jnp.where((rows & 1) == 0, 1.0, -1.0).astype(F32)
    nyq_ref[...] = jnp.sum(alt * hsum, axis=0, keepdims=True)
    hs_ref[...] = hsum.astype(BF16)
    hd_ref[...] = hdiff.astype(BF16)


def _hy_filter_taps(seq, w1, b1, w2, b2, freq, w3, b3):
    n_ch = 2 * BRANCH_W
    bands = (HY_EMB - 1) // 2
    t = jnp.linspace(0.0, 1.0, seq, dtype=F32)[:, None]
    w = 2.0 * math.pi * jnp.arange(seq, dtype=F32)[:, None] / seq
    f = jnp.linspace(1e-4, bands - 1, bands, dtype=F32)[None, :]
    feats = jnp.concatenate([t, jnp.cos(f * w), -jnp.sin(f * w),
                             jnp.zeros((seq, HY_PAD - HY_EMB), F32)], axis=-1)
    deltas = jnp.linspace(math.log(HY_DECAY_TARGET) / HY_SLOW_DECAY,
                          math.log(HY_DECAY_TARGET) / HY_FAST_DECAY, n_ch, dtype=F32)[None, :]

    def pad2(a, r, c):
        a = a.astype(F32)
        return jnp.pad(a, ((0, r - a.shape[0]), (0, c - a.shape[1])))

    w1p = pad2(w1, HY_PAD, HY_PAD)
    w2p = pad2(w2, HY_PAD, HY_PAD)
    b1p = pad2(b1[None], 1, HY_PAD)
    b2p = pad2(b2[None], 1, HY_PAD)
    f0p = pad2(freq[0][None], 1, HY_PAD)
    f1p = pad2(freq[1][None], 1, HY_PAD)
    w3p = pad2(w3, HY_PAD, 2 * n_ch)
    b3r = b3.astype(F32)[None]
    tn = 256
    nt = n_ch // tn
    full = lambda shape: pl.BlockSpec(shape, lambda j: (0, 0))
    return pl.pallas_call(
        _hy_filter_kernel,
        out_shape=(jax.ShapeDtypeStruct((seq, n_ch), BF16), jax.ShapeDtypeStruct((seq, n_ch), BF16),
                   jax.ShapeDtypeStruct((1, n_ch), F32)),
        grid=(nt,),
        in_specs=[full((seq, HY_PAD)), full((HY_PAD, HY_PAD)), full((1, HY_PAD)),
                  full((HY_PAD, HY_PAD)), full((1, HY_PAD)), full((1, HY_PAD)), full((1, HY_PAD)),
                  pl.BlockSpec((HY_PAD, tn), lambda j: (0, j)),
                  pl.BlockSpec((HY_PAD, tn), lambda j: (0, j + nt)),
                  pl.BlockSpec((1, tn), lambda j: (0, j)),
                  pl.BlockSpec((1, tn), lambda j: (0, j + nt)),
                  pl.BlockSpec((1, tn), lambda j: (0, j)),
                  full((seq, 1))],
        out_specs=(pl.BlockSpec((seq, tn), lambda j: (0, j)), pl.BlockSpec((seq, tn), lambda j: (0, j)),
                   pl.BlockSpec((1, tn), lambda j: (0, j))),
        scratch_shapes=[pltpu.VMEM((seq, HY_PAD), F32)],
        compiler_params=_params("arbitrary"),
        name="hy_filter",
    )(feats, w1p, b1p, w2p, b2p, f0p, f1p, w3p, w3p, b3r, b3r, deltas, t)


def _hy_spectrum_kernel(c_ref, s_ref, hs_ref, hd_ref, hc_out, hso_out, *, inv_n):
    i = pl.program_id(1)
    zc = jnp.dot(c_ref[...], hs_ref[...], preferred_element_type=F32)
    zs = jnp.dot(s_ref[...], hd_ref[...], preferred_element_type=F32)
    rows = lax.broadcasted_iota(jnp.int32, zc.shape, 0) + i * zc.shape[0]
    first = rows == 0
    hc_out[...] = zc * jnp.where(first, inv_n, 2.0 * inv_n)
    hso_out[...] = jnp.where(first, 0.0, zs * (2.0 * inv_n))


def _hy_spectrum(cm, s1, hsum, hdiff):
    seq, n_ch = hsum.shape
    tm = min(512, seq)
    tn = min(512, n_ch)
    w_spec = pl.BlockSpec((tm, seq), lambda j, i: (i, 0))
    h_spec = pl.BlockSpec((seq, tn), lambda j, i: (0, j))
    o_spec = pl.BlockSpec((tm, tn), lambda j, i: (i, j))
    return pl.pallas_call(
        functools.partial(_hy_spectrum_kernel, inv_n=1.0 / (2 * seq)),
        out_shape=(jax.ShapeDtypeStruct((seq, n_ch), F32),) * 2,
        grid=(n_ch // tn, seq // tm),
        in_specs=[w_spec, w_spec, h_spec, h_spec],
        out_specs=(o_spec, o_spec),
        compiler_params=_params("parallel", "arbitrary"),
        name="hy_spectrum",
    )(cm, s1, hsum, hdiff)


def _hy_conv3_kernel(u_ref, w_ref, b_ref, o_ref, vb_ref):
    u = u_ref[...]
    n = u.shape[0]
    rows = lax.broadcasted_iota(jnp.int32, u.shape, 0)
    prev = jnp.where(rows == 0, 0.0, pltpu.roll(u, 1, 0))
    nxt = jnp.where(rows == n - 1, 0.0, pltpu.roll(u, n - 1, 0))
    w = w_ref[...]
    out = prev * w[0:1] + u * w[1:2] + nxt * w[2:3] + b_ref[...]
    o_ref[...] = out
    vb_ref[...] = out.astype(BF16)


def _hy_conv3(proj, bsz, seq, conv_w, conv_b):
    m = bsz * seq
    tn = 256
    nb = 3 * BRANCH_W // tn
    return pl.pallas_call(
        _hy_conv3_kernel,
        out_shape=(jax.ShapeDtypeStruct((m, 3 * BRANCH_W), F32),
                   jax.ShapeDtypeStruct((m, 3 * BRANCH_W), BF16)),
        grid=(bsz, nb),
        in_specs=[pl.BlockSpec((seq, tn), lambda b, j: (b, j + COL_BU // tn)),
                  pl.BlockSpec((3, tn), lambda b, j: (0, j)),
                  pl.BlockSpec((1, tn), lambda b, j: (0, j))],
        out_specs=(pl.BlockSpec((seq, tn), lambda b, j: (b, j)),
                   pl.BlockSpec((seq, tn), lambda b, j: (b, j))),
        compiler_params=_params("parallel", "parallel"),
        name="hy_conv3",
    )(proj, conv_w.astype(F32), conv_b.astype(F32).reshape(1, -1))


def _hy_fwd_kernel(c_ref, s_ref, z_ref, hc_ref, hs_ref, nyq_ref, yc_ref, ys_ref, *, inv_n):
    i = pl.program_id(2)
    z = z_ref[...]
    zc = jnp.dot(c_ref[...], z, preferred_element_type=F32)
    zs = jnp.dot(s_ref[...], z, preferred_element_type=F32)
    hc, hs = hc_ref[...], hs_ref[...]
    rows = lax.broadcasted_iota(jnp.int32, zc.shape, 0) + i * zc.shape[0]
    hd = jnp.where(rows == 0, nyq_ref[...] * inv_n, hc)
    yc_ref[...] = (zc * hc - zs * hs).astype(BF16)
    ys_ref[...] = (zc * hs + zs * hd).astype(BF16)


def _hy_fwd(cm, s1, zb, z_col, hc, hs, nyq, h_col, bsz, seq):
    m = bsz * seq
    tm = min(512, seq)
    tn = 512
    nt = BRANCH_W // tn
    mt = seq // tm
    w_spec = pl.BlockSpec((tm, seq), lambda b, j, i: (i, 0))
    h_spec = pl.BlockSpec((tm, tn), lambda b, j, i: (i, j + h_col // tn))
    o_spec = pl.BlockSpec((tm, tn), lambda b, j, i: (b * mt + i, j))
    return pl.pallas_call(
        functools.partial(_hy_fwd_kernel, inv_n=1.0 / (2 * seq)),
        out_shape=(jax.ShapeDtypeStruct((m, BRANCH_W), BF16),) * 2,
        grid=(bsz, nt, mt),
        in_specs=[w_spec, w_spec,
                  pl.BlockSpec((seq, tn), lambda b, j, i: (b, j + z_col // tn)),
                  h_spec, h_spec,
                  pl.BlockSpec((1, tn), lambda b, j, i: (0, j + h_col // tn))],
        out_specs=(o_spec, o_spec),
        compiler_params=_params("parallel", "parallel", "arbitrary"),
        name="hy_fwd",
    )(cm, s1, zb, hc, hs, nyq)


def _hy_inv_mid_kernel(c_ref, s_ref, yc_ref, ys_ref, gate_ref, z_ref, bias_ref, o_ref, ob_ref):
    conv = (jnp.dot(c_ref[...], yc_ref[...], preferred_element_type=F32)
            + jnp.dot(s_ref[...], ys_ref[...], preferred_element_type=F32))
    out = gate_ref[...] * (conv + bias_ref[...] * z_ref[...])
    o_ref[...] = out
    ob_ref[...] = out.astype(BF16)


def _hy_inv_last_kernel(c_ref, s_ref, yc_ref, ys_ref, gate_ref, z_ref, bias_ref, sz_ref, o_ref):
    conv = (jnp.dot(c_ref[...], yc_ref[...], preferred_element_type=F32)
            + jnp.dot(s_ref[...], ys_ref[...], preferred_element_type=F32))
    out = gate_ref[...] * (conv + bias_ref[...] * z_ref[...])
    o_ref[...] = (out * sz_ref[...]).astype(BF16)


def _hy_inv(cm, s2, yc, ys, gates, gate_col, zprev, z_col, bias_row, bsz, seq, silu_src=None):
    m = bsz * seq
    tm = min(512, seq)
    tn = 512
    nt = BRANCH_W // tn
    mt = seq // tm
    w_spec = pl.BlockSpec((tm, seq), lambda b, j, i: (i, 0))
    y_spec = pl.BlockSpec((seq, tn), lambda b, j, i: (b, j))
    o_spec = pl.BlockSpec((tm, tn), lambda b, j, i: (b * mt + i, j))
    in_specs = [w_spec, w_spec, y_spec, y_spec,
                pl.BlockSpec((tm, tn), lambda b, j, i: (b * mt + i, j + gate_col // tn)),
                pl.BlockSpec((tm, tn), lambda b, j, i: (b * mt + i, j + z_col // tn)),
                pl.BlockSpec((1, tn), lambda b, j, i: (0, j))]
    args = [cm, s2, yc, ys, gates, zprev, bias_row]
    if silu_src is None:
        body = _hy_inv_mid_kernel
        out_shape = (jax.ShapeDtypeStruct((m, BRANCH_W), F32), jax.ShapeDtypeStruct((m, BRANCH_W), BF16))
        out_specs = (o_spec, o_spec)
    else:
        body = _hy_inv_last_kernel
        in_specs.append(pl.BlockSpec((tm, tn), lambda b, j, i: (b * mt + i, j + COL_BZ // tn)))
        args.append(silu_src)
        out_shape = jax.ShapeDtypeStruct((m, BRANCH_W), BF16)
        out_specs = o_spec
    return pl.pallas_call(
        body, out_shape=out_shape, grid=(bsz, nt, mt), in_specs=in_specs, out_specs=out_specs,
        compiler_params=_params("parallel", "parallel", "arbitrary"),
        name="hy_inv",
    )(*args)


def _hyena_branch(proj, bsz, seq, dft, conv_w, conv_b, w1, b1, w2, b2, freq, w3, b3, bias):
    cm, s1, s2 = dft
    hsum, hdiff, nyq = _hy_filter_taps(seq, w1, b1, w2, b2, freq, w3, b3)
    hc, hs = _hy_spectrum(cm, s1, hsum, hdiff)
    uc, ucb = _hy_conv3(proj, bsz, seq, conv_w, conv_b)
    bias = bias.astype(F32)
    yc, ys = _hy_fwd(cm, s1, ucb, 0, hc, hs, nyq, 0, bsz, seq)
    z1, z1b = _hy_inv(cm, s2, yc, ys, uc, BRANCH_W, uc, 0, bias[0:1], bsz, seq)
    yc, ys = _hy_fwd(cm, s1, z1b, 0, hc, hs, nyq, BRANCH_W, bsz, seq)
    return _hy_inv(cm, s2, yc, ys, uc, 2 * BRANCH_W, z1, 0, bias[1:2], bsz, seq, silu_src=proj)


def _rope_table_kernel(pos_ref, inv_ref, cos_ref, sin_ref):
    ang = pos_ref[...] * inv_ref[...]
    lane = lax.broadcasted_iota(jnp.int32, ang.shape, 1)
    live = lane < MLA_ROPE
    cos_ref[...] = jnp.where(live, jnp.cos(ang), 0.0)
    sin_ref[...] = jnp.where(live, jnp.where(lane < MLA_ROPE // 2, -1.0, 1.0) * jnp.sin(ang), 0.0)


def _rope_tables(positions):
    m = positions.size
    half = MLA_ROPE // 2
    inv = ROPE_BASE ** (-jnp.arange(half, dtype=F32) / half)
    inv = jnp.concatenate([inv, inv, jnp.zeros((128 - MLA_ROPE,), F32)])[None]
    pos = positions.astype(F32).reshape(m, 1)
    tm = min(1024, m)
    spec = pl.BlockSpec((tm, 128), lambda i: (i, 0))
    return pl.pallas_call(
        _rope_table_kernel,
        out_shape=(jax.ShapeDtypeStruct((m, 128), F32),) * 2,
        grid=(m // tm,),
        in_specs=[pl.BlockSpec((tm, 1), lambda i: (i, 0)), pl.BlockSpec((1, 128), lambda i: (0, 0))],
        out_specs=(spec, spec),
        compiler_params=_params("parallel"),
        name="rope_table",
    )(pos, inv)


def _rope128(x, cos_t, sin_t):
    lane = lax.broadcasted_iota(jnp.int32, x.shape, 1)
    half = MLA_ROPE // 2
    partner = jnp.where(lane < half, pltpu.roll(x, 128 - half, 1), pltpu.roll(x, half, 1))
    return x * cos_t + partner * sin_t


def _rms(x, g):
    ms = jnp.mean(jnp.square(x), axis=-1, keepdims=True)
    return x * lax.rsqrt(ms + RMS_EPS) * g


def _mla_q_kernel(cq_ref, g_ref, w_ref, cos_ref, sin_ref, q_ref, *, scale):
    xn = _rms(cq_ref[...], g_ref[...]).astype(BF16)
    q = jnp.dot(xn, w_ref[...], preferred_element_type=F32) * scale
    cos_t, sin_t = cos_ref[...], sin_ref[...]
    for h in range(MLA_HEADS):
        base = h * MLA_QK_PAD
        q_ref[:, base:base + MLA_NOPE] = q[:, base:base + MLA_NOPE].astype(BF16)
        q_ref[:, base + MLA_NOPE:base + MLA_QK_PAD] = _rope128(
            q[:, base + MLA_NOPE:base + MLA_QK_PAD], cos_t, sin_t).astype(BF16)


def _mla_kv_kernel(ckv_ref, g_ref, wk_ref, wv_ref, kr_ref, cos_ref, sin_ref, k_ref, v_ref):
    xn = _rms(ckv_ref[...], g_ref[...]).astype(BF16)
    kn = jnp.dot(xn, wk_ref[...], preferred_element_type=F32)
    v_ref[...] = jnp.dot(xn, wv_ref[...], preferred_element_type=F32).astype(BF16)
    kr = _rope128(kr_ref[:, 0:128], cos_ref[...], sin_ref[...]).astype(BF16)
    for h in range(MLA_HEADS):
        base = h * MLA_QK_PAD
        k_ref[:, base:base + MLA_NOPE] = kn[:, h * MLA_NOPE:(h + 1) * MLA_NOPE].astype(BF16)
        k_ref[:, base + MLA_NOPE:base + MLA_QK_PAD] = kr


def _mla_attn_kernel(q_ref, k_ref, v_ref, z_ref, o_ref):
    k, v = k_ref[...], v_ref[...]
    half = q_ref.shape[0] // 2
    for r in range(2):
        rows = pl.ds(r * half, half)
        s = lax.dot_general(q_ref[rows, :], k, (((1,), (1,)), ((), ())),
                            preferred_element_type=F32)
        p = jnp.exp2(s - jnp.max(s, axis=-1, keepdims=True))
        l = jnp.sum(p, axis=-1, keepdims=True)
        o = jnp.dot(p.astype(BF16), v, preferred_element_type=F32)
        o_ref[rows, :] = (o / l * z_ref[rows, :]).astype(BF16)


def _mla_branch(proj, proj_b, bsz, seq, rope, q_norm_g, w_uq, kv_norm_g, w_ukv):
    m = bsz * seq
    cos_t, sin_t = rope
    dqk = MLA_NOPE + MLA_ROPE
    hq = MLA_HEADS * MLA_QK_PAD
    w_q = w_uq.reshape(MLA_LORA, MLA_HEADS, dqk)
    w_q = jnp.pad(w_q, ((0, 0), (0, 0), (0, MLA_QK_PAD - dqk))).reshape(MLA_LORA, hq).astype(BF16)
    w_kv = w_ukv.reshape(MLA_LORA, MLA_HEADS, MLA_NOPE + MLA_V)
    w_k = w_kv[:, :, :MLA_NOPE].reshape(MLA_LORA, MLA_HEADS * MLA_NOPE).astype(BF16)
    w_v = w_kv[:, :, MLA_NOPE:].reshape(MLA_LORA, MLA_HEADS * MLA_V).astype(BF16)
    tm = min(512, m)
    row = lambda shape, col=0: pl.BlockSpec(shape, lambda i: (i, col))
    full = lambda shape: pl.BlockSpec(shape, lambda i: (0, 0))
    qp = pl.pallas_call(
        functools.partial(_mla_q_kernel, scale=dqk ** -0.5 * math.log2(math.e)),
        out_shape=jax.ShapeDtypeStruct((m, hq), BF16),
        grid=(m // tm,),
        in_specs=[row((tm, MLA_LORA), COL_CQ // MLA_LORA), full((1, MLA_LORA)), full((MLA_LORA, hq)),
                  row((tm, 128)), row((tm, 128))],
        out_specs=row((tm, hq)),
        compiler_params=_params("parallel"),
        name="mla_q",
    )(proj, q_norm_g.astype(F32).reshape(1, -1), w_q, cos_t, sin_t)
    kp, vp = pl.pallas_call(
        _mla_kv_kernel,
        out_shape=(jax.ShapeDtypeStruct((m, hq), BF16),
                   jax.ShapeDtypeStruct((m, MLA_HEADS * MLA_V), BF16)),
        grid=(m // tm,),
        in_specs=[row((tm, MLA_LORA), COL_CKV // MLA_LORA), full((1, MLA_LORA)),
                  full((MLA_LORA, MLA_HEADS * MLA_NOPE)), full((MLA_LORA, MLA_HEADS * MLA_V)),
                  row((tm, 512), COL_CKR // 512), row((tm, 128)), row((tm, 128))],
        out_specs=(row((tm, hq)), row((tm, MLA_HEADS * MLA_V))),
        compiler_params=_params("parallel"),
        name="mla_kv",
    )(proj, kv_norm_g.astype(F32).reshape(1, -1), w_k, w_v, proj_b, cos_t, sin_t)
    tq = min(512, seq)
    qt = seq // tq
    return pl.pallas_call(
        _mla_attn_kernel,
        out_shape=jax.ShapeDtypeStruct((m, MLA_HEADS * MLA_V), BF16),
        grid=(bsz, MLA_HEADS, qt),
        in_specs=[pl.BlockSpec((tq, MLA_QK_PAD), lambda b, h, i: (b * qt + i, h)),
                  pl.BlockSpec((seq, MLA_QK_PAD), lambda b, h, i: (b, h)),
                  pl.BlockSpec((seq, MLA_V), lambda b, h, i: (b, h)),
                  pl.BlockSpec((tq, MLA_V), lambda b, h, i: (b * qt + i, h + COL_CZ // MLA_V))],
        out_specs=pl.BlockSpec((tq, MLA_V), lambda b, h, i: (b * qt + i, h)),
        compiler_params=_params("parallel", "parallel", "arbitrary"),
        name="mla_attn",
    )(qp, kp, vp, proj_b)


def _lift_kernel(ya_ref, yb_ref, yc_ref, w_ref, ga_ref, gb_ref, gc_ref, o_ref):
    acc = ga_ref[...] * jnp.dot(ya_ref[...], w_ref[0], preferred_element_type=F32)
    acc += gb_ref[...] * jnp.dot(yb_ref[...], w_ref[1], preferred_element_type=F32)
    acc += gc_ref[...] * jnp.dot(yc_ref[...], w_ref[2], preferred_element_type=F32)
    o_ref[...] = acc.astype(BF16)


def _lift(ya, yb, yc, w_lift, proj):
    m = ya.shape[0]
    tm = min(512, m)
    tn = 512
    y_spec = pl.BlockSpec((tm, BRANCH_W), lambda i, j: (i, 0))
    gate = lambda n: pl.BlockSpec((tm, tn), lambda i, j: (i, j + (COL_GATE + n * D_MODEL) // tn))
    return pl.pallas_call(
        _lift_kernel,
        out_shape=jax.ShapeDtypeStruct((m, D_MODEL), BF16),
        grid=(m // tm, D_MODEL // tn),
        in_specs=[y_spec, y_spec, y_spec,
                  pl.BlockSpec((N_BRANCH, BRANCH_W, tn), lambda i, j: (0, 0, j)),
                  gate(0), gate(1), gate(2)],
        out_specs=pl.BlockSpec((tm, tn), lambda i, j: (i, j)),
        compiler_params=_params("parallel", "arbitrary"),
        name="lift",
    )(ya, yb, yc, w_lift.astype(BF16), proj, proj, proj)


def _out_kernel(mix_ref, wo_ref, p_ref, wp_ref, sp_ref, x_ref, g_ref, b_ref, o_ref):
    mixed = jnp.dot(mix_ref[...], wo_ref[...], preferred_element_type=F32)
    ple = jnp.dot(p_ref[...].astype(BF16), wp_ref[...], preferred_element_type=F32) * sp_ref[...]
    r = DEEPNORM_ALPHA * x_ref[...] + mixed + ple
    mu = jnp.mean(r, axis=-1, keepdims=True)
    var = jnp.mean(jnp.square(r - mu), axis=-1, keepdims=True)
    o_ref[...] = (r - mu) * lax.rsqrt(var + LN_EPS) * g_ref[...] + b_ref[...]


def _out_norm(mix, w_out, p, w_ple, proj, x, ln_g, ln_b):
    m = mix.shape[0]
    tm = min(256, m)
    row = lambda w, col=0: pl.BlockSpec((tm, w), lambda i: (i, col))
    full = lambda shape: pl.BlockSpec(shape, lambda i: (0, 0))
    return pl.pallas_call(
        _out_kernel,
        out_shape=jax.ShapeDtypeStruct((m, D_MODEL), F32),
        grid=(m // tm,),
        in_specs=[row(D_MODEL), full((D_MODEL, D_MODEL)), row(PLE_DIM), full((PLE_DIM, D_MODEL)),
                  row(D_MODEL, COL_PLE // D_MODEL), row(D_MODEL), full((1, D_MODEL)), full((1, D_MODEL))],
        out_specs=row(D_MODEL),
        compiler_params=_params("parallel"),
        name="out_norm",
    )(mix, w_out.astype(BF16), p, w_ple.astype(BF16), proj, x,
      ln_g.astype(F32).reshape(1, -1), ln_b.astype(F32).reshape(1, -1))


def kernel(x, p, positions, w_in, s5_lambda_re, s5_lambda_im, s5_log_dt, s5_b_re, s5_b_im, s5_c_re, s5_c_im, s5_d, s5_w_glu, s5_b_glu, hy_conv_w, hy_conv_b, hy_w1, hy_b1, hy_w2, hy_b2, hy_freq, hy_w3, hy_b3, hy_bias, mla_q_norm, mla_w_uq, mla_kv_norm, mla_w_ukv, w_lift, w_out, w_ple, ln_g, ln_b):
    bsz, seq, _ = x.shape
    m = bsz * seq
    depth = w_in.shape[0]
    dft = _dft_matrices(seq)
    rope = _rope_tables(positions)
    s5_tiled = _s5_tiled_params(s5_lambda_re, s5_lambda_im, s5_log_dt, s5_b_re, s5_b_im,
                                s5_c_re, s5_c_im, s5_d)
    xf = x.reshape(m, D_MODEL).astype(F32)
    for i in range(depth):
        proj, proj_b = _proj_both(xf, w_in[i])
        y_a = _s5_branch(proj, bsz, seq, [a[i] for a in s5_tiled], s5_w_glu[i], s5_b_glu[i])
        y_b = _hyena_branch(proj, bsz, seq, dft, hy_conv_w[i], hy_conv_b[i], hy_w1[i], hy_b1[i],
                            hy_w2[i], hy_b2[i], hy_freq[i], hy_w3[i], hy_b3[i], hy_bias[i])
        y_c = _mla_branch(proj, proj_b, bsz, seq, rope, mla_q_norm[i], mla_w_uq[i], mla_kv_norm[i], mla_w_ukv[i])
        mix = _lift(y_a, y_b, y_c, w_lift[i], proj_b)
        xf = _out_norm(mix, w_out[i], p[i].reshape(m, PLE_DIM), w_ple[i], proj_b, xf, ln_g[i], ln_b[i])
    return xf.reshape(bsz, seq, D_MODEL).astype(x.dtype)
```

```python
import functools
import math

import numpy as np
import jax
import jax.numpy as jnp
from jax import lax
from jax.experimental import pallas as pl
from jax.experimental.pallas import tpu as pltpu

F32 = jnp.float32
BF16 = jnp.bfloat16

D_MODEL = 2048
PLE_DIM = 256
N_BRANCH = 3
BRANCH_W = 1024

S5_GROUP = 16
S5_GROUPS = BRANCH_W // S5_GROUP
S5_STATE = 64
S5_CHUNK = 16
S5_ROW = S5_CHUNK * S5_GROUP

HY_EMB = 33
HY_FF = 64
HY_PAD = 128
HY_DECAY_TARGET = 0.01
HY_FAST_DECAY = 0.3
HY_SLOW_DECAY = 1.5
DFT_ROWS = 64

MLA_HEADS = 8
MLA_NOPE = 128
MLA_ROPE = 64
MLA_V = 128
MLA_LORA = 512
MLA_QK_PAD = 256
ROPE_BASE = 10000.0

LN_EPS = 1e-5
RMS_EPS = 1e-6
DEPTH = 2
DEEPNORM_ALPHA = (2 * DEPTH) ** 0.25

COL_AX = 0
COL_AZ = 1024
COL_BU = 2048
COL_BZ = 5120
COL_CQ = 6144
COL_CKV = 6656
PROJ_A_N = 7168
COL_PLE = 0
COL_GATE = 2048
COL_CZ = 8192
COL_CKR = 9216
PROJ_B_N = 9728
PROJ_TN = 512
W_IN_CKR = 7168
W_IN_CZ = 7232
W_IN_GATE = 8256
W_IN_PLE = 14400

VMEM_LIMIT = 56 * 1024 * 1024


def _params(*sem):
    return pltpu.CompilerParams(dimension_semantics=sem, vmem_limit_bytes=VMEM_LIMIT)


def _sigmoid(x):
    return 0.5 * jnp.tanh(0.5 * x) + 0.5


def _in_tiles(j, ranges):
    hit = None
    for lo, hi in ranges:
        cur = (j >= lo // PROJ_TN) & (j < hi // PROJ_TN)
        hit = cur if hit is None else (hit | cur)
    return hit


def _proj_kernel(x_ref, w_ref, o_ref, xb_ref, *, silu_cols, sigm_cols):
    j = pl.program_id(1)

    @pl.when(j == 0)
    def _():
        xb_ref[...] = x_ref[...].astype(BF16)

    acc = jnp.dot(xb_ref[...], w_ref[...].astype(BF16), preferred_element_type=F32)
    is_silu = _in_tiles(j, silu_cols)
    plain = jnp.logical_not(is_silu)

    @pl.when(is_silu)
    def _():
        o_ref[...] = acc * _sigmoid(acc)

    if sigm_cols:
        is_sigm = _in_tiles(j, sigm_cols)
        plain = jnp.logical_not(is_silu | is_sigm)

        @pl.when(is_sigm)
        def _():
            o_ref[...] = _sigmoid(acc)

    @pl.when(plain)
    def _():
        o_ref[...] = acc


def _proj(x, w, layer, n_out, silu_cols, sigm_cols, name):
    m, k = x.shape
    tm = min(1024, m)
    return pl.pallas_call(
        functools.partial(_proj_kernel, silu_cols=silu_cols, sigm_cols=sigm_cols),
        out_shape=jax.ShapeDtypeStruct((m, n_out), F32),
        grid=(m // tm, n_out // PROJ_TN),
        in_specs=[pl.BlockSpec((tm, k), lambda i, j: (i, 0)),
                  pl.BlockSpec((None, k, PROJ_TN), lambda i, j: (layer, 0, j))],
        out_specs=pl.BlockSpec((tm, PROJ_TN), lambda i, j: (i, j)),
        scratch_shapes=[pltpu.VMEM((tm, k), BF16)],
        compiler_params=_params("parallel", "arbitrary"),
        name=name,
    )(x, w)


def _repack_src_tile(j):
    t = lambda col: col // PROJ_TN
    return jnp.where(j < t(COL_GATE), j + t(W_IN_PLE),
                     jnp.where(j < t(COL_CZ), j - t(COL_GATE) + t(W_IN_GATE),
                               jnp.where(j < t(COL_CKR), j - t(COL_CZ) + t(W_IN_CZ), t(W_IN_CKR))))


def _repack_kernel(a_ref, b_ref, o_ref):
    is_last = pl.program_id(0) == pl.num_programs(0) - 1
    n_sub = PROJ_TN // 128
    src = [a_ref[:, c * 128:(c + 1) * 128] for c in range(n_sub)] + [b_ref[...]]
    lane = lax.broadcasted_iota(jnp.int32, src[0].shape, 1)
    low = lane < MLA_ROPE
    rolled = [pltpu.roll(t, 128 - MLA_ROPE, 1) for t in src]
    for c in range(n_sub):
        shifted = jnp.where(low, rolled[c], rolled[c + 1])
        plain = jnp.where(low, src[0], 0.0) if c == 0 else jnp.zeros_like(shifted)
        o_ref[:, c * 128:(c + 1) * 128] = jnp.where(is_last, plain, shifted).astype(BF16)


def _repack_b(w_in, layer):
    k = w_in.shape[1]
    sub = PROJ_TN // 128
    return pl.pallas_call(
        _repack_kernel,
        out_shape=jax.ShapeDtypeStruct((1, k, PROJ_B_N), BF16),
        grid=(PROJ_B_N // PROJ_TN,),
        in_specs=[pl.BlockSpec((None, k, PROJ_TN), lambda j: (layer, 0, _repack_src_tile(j))),
                  pl.BlockSpec((None, k, 128), lambda j: (layer, 0, (_repack_src_tile(j) + 1) * sub))],
        out_specs=pl.BlockSpec((None, k, PROJ_TN), lambda j: (0, 0, j)),
        compiler_params=_params("parallel"),
        name="repack_b",
    )(w_in, w_in)


def _proj_both(x, w_in, layer):
    proj_a = _proj(x, w_in, layer, PROJ_A_N, ((COL_AZ, COL_BU), (COL_BZ, COL_CQ)), (), "proj_a")
    proj_b = _proj(x, _repack_b(w_in, layer), 0, PROJ_B_N, ((COL_CZ, COL_CKR),), ((COL_PLE, COL_CZ),),
                   "proj_b")
    return proj_a, proj_b


S5_LANES = 8 * S5_STATE
S5_SLAB = 8
S5_RELAYOUT_ROWS = 32


def _s5_tile_lanes(a):
    return jnp.concatenate([a[:, 0]] * 4 + [a[:, 1]] * 4, axis=-1)


def _s5_tiled_params(lam_re, lam_im, log_dt, b_re, b_im, c_re, c_im, d):
    f = lambda a: a.astype(F32)
    depth = lam_re.shape[0]
    ldt = jnp.broadcast_to(f(log_dt)[..., None, None], lam_re.shape[:3] + (1, S5_STATE))
    return (_s5_tile_lanes(f(lam_re)[:, :, :, None, :]), _s5_tile_lanes(f(lam_im)[:, :, :, None, :]),
            _s5_tile_lanes(ldt),
            _s5_tile_lanes(jnp.swapaxes(f(b_re), -1, -2)), _s5_tile_lanes(jnp.swapaxes(f(b_im), -1, -2)),
            _s5_tile_lanes(f(c_re)), _s5_tile_lanes(f(c_im)),
            f(d).reshape(depth, S5_GROUPS, S5_GROUP, 1))


def _s5_mats_kernel(lr_ref, li_ref, ldt_ref, br_ref, bi_ref, cr_ref, ci_ref, d_ref,
                    ms_ref, mi_ref, mo_ref, ar_ref, ai_ref):
    t_n, h_n = S5_CHUNK, S5_GROUP
    hi = lax.Precision.HIGHEST
    nt = (((1,), (1,)), ((), ()))
    blk = lax.broadcasted_iota(jnp.int32, (1, S5_LANES), 1) // S5_STATE
    is_im = (blk // 2) % 2 == 1
    is_fwd = blk < 4
    steps = lax.broadcasted_iota(jnp.int32, (24, S5_LANES), 0).astype(F32)
    lane_k = lax.broadcasted_iota(jnp.int32, (h_n, S5_ROW), 1)
    sub_k = lax.broadcasted_iota(jnp.int32, (h_n, S5_ROW), 0)

    def per_group(gi, carry):
        lr, li = lr_ref[gi], li_ref[gi]
        dt = jnp.exp(ldt_ref[gi])
        zr, zi = lr * dt, li * dt
        mag = jnp.exp(steps * zr)
        tr, ti = mag * jnp.cos(steps * zi), mag * jnp.sin(steps * zi)
        lbr, lbi = tr[1:2], ti[1:2]
        n2 = lr * lr + li * li
        qr = ((lbr - 1.0) * lr + lbi * li) / n2
        qi = (lbi * lr - (lbr - 1.0) * li) / n2
        br, bi = br_ref[gi], bi_ref[gi]
        bbr, bbi = qr * br - qi * bi, qr * bi + qi * br
        y1, y2 = jnp.where(is_im, bbi, bbr), jnp.where(is_im, bbr, bbi)
        cr, ci = cr_ref[gi], ci_ref[gi]

        def pick(tab, t_fwd, t_bwd):
            return jnp.where(is_fwd, tab[t_fwd:t_fwd + 1], tab[t_bwd:t_bwd + 1])

        def c_times(p_r, p_i):
            return jnp.where(is_im, -(cr * p_i + ci * p_r), cr * p_r - ci * p_i)

        q_rows = []
        for t in range(t_n):
            a_r, a_i = pick(tr, t_n - 1 - t, t), pick(ti, t_n - 1 - t, t)
            rows = pl.ds(t * h_n, h_n)
            ms_ref[gi, rows, :] = (a_r * y1 + jnp.where(is_im, a_i, -a_i) * y2).astype(BF16)
            mo_ref[gi, rows, :] = c_times(pick(tr, t + 1, t_n - t), pick(ti, t + 1, t_n - t)).astype(BF16)
            q_rows.append(c_times(pick(tr, t, t_n - 1 - t), pick(ti, t, t_n - 1 - t)))
        q = jnp.concatenate(q_rows, axis=0)
        half = S5_LANES // 2
        kf = 0.5 * lax.dot_general(y1[:, :half], q[:, :half], nt, precision=hi, preferred_element_type=F32)
        kb = 0.5 * lax.dot_general(y1[:, half:], q[:, half:], nt, precision=hi, preferred_element_type=F32)
        kf = kf + jnp.where(lane_k == sub_k, d_ref[gi], 0.0)
        for t in range(t_n):
            fwd = kf if t == 0 else jnp.where(lane_k >= h_n * t, pltpu.roll(kf, h_n * t, 1), 0.0)
            sh = h_n * (t_n - 1 - t)
            bwd = kb if sh == 0 else jnp.where(lane_k < S5_ROW - sh, pltpu.roll(kb, S5_ROW - sh, 1), 0.0)
            mi_ref[gi, pl.ds(t * h_n, h_n), :] = (fwd + bwd).astype(BF16)
        ar_ref[gi] = tr[t_n:t_n + 1]
        ai_ref[gi] = ti[t_n:t_n + 1]
        return carry

    lax.fori_loop(0, lr_ref.shape[0], per_group, 0)


def _s5_mats(tiled):
    g_n = S5_GROUPS
    gb = S5_SLAB
    spec = lambda r, w: pl.BlockSpec((gb, r, w), lambda j: (j, 0, 0))
    in_rows = (1, 1, 1, S5_GROUP, S5_GROUP, S5_GROUP, S5_GROUP)
    return pl.pallas_call(
        _s5_mats_kernel,
        out_shape=(jax.ShapeDtypeStruct((g_n, S5_ROW, S5_LANES), BF16),
                   jax.ShapeDtypeStruct((g_n, S5_ROW, S5_ROW), BF16),
                   jax.ShapeDtypeStruct((g_n, S5_ROW, S5_LANES), BF16),
                   jax.ShapeDtypeStruct((g_n, 1, S5_LANES), F32),
                   jax.ShapeDtypeStruct((g_n, 1, S5_LANES), F32)),
        grid=(g_n // gb,),
        in_specs=[spec(r, S5_LANES) for r in in_rows] + [spec(S5_GROUP, 1)],
        out_specs=(spec(S5_ROW, S5_LANES), spec(S5_ROW, S5_ROW), spec(S5_ROW, S5_LANES),
                   spec(1, S5_LANES), spec(1, S5_LANES)),
        compiler_params=_params("parallel"),
        name="s5_mats",
    )(*tiled)


def _seg_transpose(vs, seg):
    vs = list(vs)
    for s in (4, 2, 1):
        keep = (seg & s) == 0
        for i in range(8):
            if i & s:
                continue
            a, b = vs[i], vs[i + s]
            vs[i] = jnp.where(keep, a, pltpu.roll(b, s * S5_GROUP, 1))
            vs[i + s] = jnp.where(keep, pltpu.roll(a, 128 - s * S5_GROUP, 1), b)
    return vs


def _s5_main_kernel(x_ref, ms_ref, mi_ref, mo_ref, ar_ref, ai_ref, y_ref,
                    u_ref, sl_ref, st_ref, yg_ref, *, bsz):
    rows = x_ref.shape[0]
    n_chunks = rows // bsz
    rc = S5_RELAYOUT_ROWS
    seg = lax.broadcasted_iota(jnp.int32, (rc, 128), 1) // S5_GROUP
    slot = (lax.broadcasted_iota(jnp.int32, (1, S5_LANES), 1) // S5_STATE) % 2
    slot128 = slot[:, :128]

    def relayout_in(r, carry):
        r0 = pl.multiple_of(r * rc, rc)
        for th in range(2):
            src = [x_ref[pl.ds(r0, rc), th * 8 + t8, :] for t8 in range(8)]
            for gi, out in enumerate(_seg_transpose(src, seg)):
                u_ref[gi, pl.ds(r0, rc), th * 128:(th + 1) * 128] = out.astype(BF16)
        return carry

    lax.fori_loop(0, rows // rc, relayout_in, 0)

    a_parts = [[], [], [], []]
    for jp in range(S5_SLAB // 2):
        g0, g1 = 2 * jp, 2 * jp + 1
        r0 = jnp.dot(u_ref[g0], ms_ref[g0], preferred_element_type=F32)
        r1 = jnp.dot(u_ref[g1], ms_ref[g1], preferred_element_type=F32)
        loc = jnp.where(slot == 0, r0, r1)
        for k in range(4):
            sl_ref[k, :, jp * 128:(jp + 1) * 128] = loc[:, k * 128:(k + 1) * 128]
        for k, (ref, off) in enumerate(((ar_ref, 0), (ai_ref, 0), (ar_ref, 256), (ai_ref, 256))):
            a_parts[k].append(jnp.where(slot128 == 0, ref[g0][:, off:off + 128], ref[g1][:, off:off + 128]))
    a_fr, a_fi, a_br, a_bi = [jnp.concatenate(parts, axis=1) for parts in a_parts]

    zero = jnp.zeros((1, a_fr.shape[1]), F32)

    def scan(c, carry):
        cb = n_chunks - 1 - c
        new = []
        for b in range(bsz):
            s_fr, s_fi, s_br, s_bi = carry[b]
            rf, rb = b * n_chunks + c, b * n_chunks + cb
            st_ref[0, pl.ds(rf, 1), :] = s_fr
            st_ref[1, pl.ds(rf, 1), :] = s_fi
            st_ref[2, pl.ds(rb, 1), :] = s_br
            st_ref[3, pl.ds(rb, 1), :] = s_bi
            n_fr = a_fr * s_fr - a_fi * s_fi + sl_ref[0, pl.ds(rf, 1), :]
            n_fi = a_fr * s_fi + a_fi * s_fr + sl_ref[1, pl.ds(rf, 1), :]
            n_br = a_br * s_br - a_bi * s_bi + sl_ref[2, pl.ds(rb, 1), :]
            n_bi = a_br * s_bi + a_bi * s_br + sl_ref[3, pl.ds(rb, 1), :]
            new.append((n_fr, n_fi, n_br, n_bi))
        return tuple(new)

    lax.fori_loop(0, n_chunks, scan, tuple((zero,) * 4 for _ in range(bsz)))

    nt = (((1,), (1,)), ((), ()))
    for jp in range(S5_SLAB // 2):
        st = jnp.concatenate([st_ref[k, :, jp * 128:(jp + 1) * 128] for k in range(4)], axis=1)
        for e in range(2):
            g = 2 * jp + e
            st_g = jnp.where(slot == e, st, 0.0).astype(BF16)
            y = (jnp.dot(u_ref[g], mi_ref[g], preferred_element_type=F32)
                 + lax.dot_general(st_g, mo_ref[g], nt, preferred_element_type=F32))
            yg_ref[g] = jax.nn.gelu(y)

    def relayout_out(r, carry):
        r0 = pl.multiple_of(r * rc, rc)
        for th in range(2):
            src = [yg_ref[gi, pl.ds(r0, rc), th * 128:(th + 1) * 128] for gi in range(S5_SLAB)]
            for t8, out in enumerate(_seg_transpose(src, seg)):
                y_ref[pl.ds(r0, rc), th * 8 + t8, :] = out
        return carry

    lax.fori_loop(0, rows // rc, relayout_out, 0)


def _s5_glu_kernel(g_ref, w_ref, b_ref, z_ref, o_ref):
    g = g_ref[...]
    acc = jnp.dot(g.astype(BF16), w_ref[...], preferred_element_type=F32) + b_ref[...]
    o_ref[...] = (g * _sigmoid(acc) * z_ref[...]).astype(o_ref.dtype)


def _s5_branch(proj, bsz, seq, tiled, w_glu, b_glu):
    m_state, m_intra, m_out, a_re, a_im = _s5_mats(tiled)
    t_n = S5_CHUNK
    rows = bsz * (seq // t_n)
    m = bsz * seq
    gb = S5_SLAB
    x3 = proj.reshape(rows, t_n, proj.shape[1])
    mat = lambda r, w: pl.BlockSpec((gb, r, w), lambda s: (s, 0, 0))
    io_spec = pl.BlockSpec((rows, t_n, 128), lambda s: (0, 0, s + COL_AX // 128))
    y = pl.pallas_call(
        functools.partial(_s5_main_kernel, bsz=bsz),
        out_shape=jax.ShapeDtypeStruct((rows, t_n, BRANCH_W), F32),
        grid=(S5_GROUPS // gb,),
        in_specs=[io_spec, mat(S5_ROW, S5_LANES), mat(S5_ROW, S5_ROW), mat(S5_ROW, S5_LANES),
                  mat(1, S5_LANES), mat(1, S5_LANES)],
        out_specs=pl.BlockSpec((rows, t_n, 128), lambda s: (0, 0, s)),
        scratch_shapes=[pltpu.VMEM((gb, rows, S5_ROW), BF16),
                        pltpu.VMEM((4, rows, S5_LANES), F32),
                        pltpu.VMEM((4, rows, S5_LANES), F32),
                        pltpu.VMEM((gb, rows, S5_ROW), F32)],
        compiler_params=_params("parallel"),
        name="s5_main",
    )(x3, m_state, m_intra, m_out, a_re, a_im)
    y = y.reshape(m, BRANCH_W)
    tm = min(512, m)
    return pl.pallas_call(
        _s5_glu_kernel,
        out_shape=jax.ShapeDtypeStruct((m, BRANCH_W), BF16),
        grid=(m // tm,),
        in_specs=[pl.BlockSpec((tm, BRANCH_W), lambda i: (i, 0)),
                  pl.BlockSpec((BRANCH_W, BRANCH_W), lambda i: (0, 0)),
                  pl.BlockSpec((1, BRANCH_W), lambda i: (0, 0)),
                  pl.BlockSpec((tm, BRANCH_W), lambda i: (i, COL_AZ // BRANCH_W))],
        out_specs=pl.BlockSpec((tm, BRANCH_W), lambda i: (i, 0)),
        compiler_params=_params("parallel"),
        name="s5_glu",
    )(y, w_glu.astype(BF16), b_glu.astype(F32).reshape(1, BRANCH_W), proj)


def _dft_tables(seq):
    n = 2 * seq
    mm = np.arange(seq, dtype=np.int64)
    k1 = np.arange(seq // DFT_ROWS, dtype=np.int64)[:, None] * DFT_ROWS
    k0 = np.arange(DFT_ROWS, dtype=np.int64)[:, None]
    ang_a = 2.0 * np.pi * ((k1 * mm) % n).astype(np.float64) / n
    ang_b = 2.0 * np.pi * ((k0 * mm) % n).astype(np.float64) / n
    return tuple(jnp.asarray(t, F32) for t in (np.cos(ang_a), np.sin(ang_a), np.cos(ang_b), np.sin(ang_b)))


def _dft_gen_kernel(ac_ref, as_ref, bc_ref, bs_ref, c_ref, s1_ref, s2_ref):
    i = pl.program_id(0)
    a_c = ac_ref[pl.ds(i, 1), :]
    a_s = as_ref[pl.ds(i, 1), :]
    b_c, b_s = bc_ref[...], bs_ref[...]
    cos_t = a_c * b_c - a_s * b_s
    sin_t = a_s * b_c + a_c * b_s
    rows = lax.broadcasted_iota(jnp.int32, cos_t.shape, 0) + i * DFT_ROWS
    cols = lax.broadcasted_iota(jnp.int32, cos_t.shape, 1)
    alt_cols = jnp.where((cols & 1) == 0, 1.0, -1.0).astype(F32)
    alt_rows = jnp.where((rows & 1) == 0, 1.0, -1.0).astype(F32)
    c_ref[...] = cos_t.astype(BF16)
    s1_ref[...] = jnp.where(rows == 0, alt_cols, sin_t).astype(BF16)
    s2_ref[...] = jnp.where(cols == 0, alt_rows, sin_t).astype(BF16)


def _dft_matrices(seq):
    tabs = _dft_tables(seq)
    n_steps = seq // DFT_ROWS
    tab_spec = pl.BlockSpec(tabs[0].shape, lambda i: (0, 0))
    b_spec = pl.BlockSpec((DFT_ROWS, seq), lambda i: (0, 0))
    o_spec = pl.BlockSpec((DFT_ROWS, seq), lambda i: (i, 0))
    return pl.pallas_call(
        _dft_gen_kernel,
        out_shape=(jax.ShapeDtypeStruct((seq, seq), BF16),) * 3,
        grid=(n_steps,),
        in_specs=[tab_spec, tab_spec, b_spec, b_spec],
        out_specs=(o_spec,) * 3,
        compiler_params=_params("parallel"),
        name="dft_gen",
    )(*tabs)


def _hy_filter_kernel(feat_ref, w1_ref, b1_ref, w2_ref, b2_ref, f0_ref, f1_ref,
                      w3p_ref, w3n_ref, b3p_ref, b3n_ref, dl_ref, t_ref,
                      hs_ref, hd_ref, hp_ref, hn_ref, r0_ref, h_ref, *, inv_n):
    hi = lax.Precision.HIGHEST

    @pl.when(pl.program_id(0) == 0)
    def _():
        h1 = jnp.sin(f0_ref[...] * (jnp.dot(feat_ref[...], w1_ref[...], precision=hi,
                                            preferred_element_type=F32) + b1_ref[...]))
        h_ref[...] = jnp.sin(f1_ref[...] * (jnp.dot(h1, w2_ref[...], precision=hi,
                                                    preferred_element_type=F32) + b2_ref[...]))

    h = h_ref[...].astype(BF16)
    win = jnp.exp(-t_ref[...] * jnp.abs(dl_ref[...]))
    hpos = (jnp.dot(h, w3p_ref[...].astype(BF16), preferred_element_type=F32) + b3p_ref[...]) * win
    hneg = (jnp.dot(h, w3n_ref[...].astype(BF16), preferred_element_type=F32) + b3n_ref[...]) * win
    rows = lax.broadcasted_iota(jnp.int32, hpos.shape, 0)
    hneg = jnp.where(rows == 0, 0.0, hneg)
    norm = (jnp.sum(jnp.abs(hpos), axis=0, keepdims=True)
            + jnp.sum(jnp.abs(hneg), axis=0, keepdims=True))
    hpos = hpos / norm
    hneg = hneg / norm
    hsum = hpos + hneg
    hdiff = hpos - hneg
    even = (rows & 1) == 0
    alt2 = jnp.where(((rows >> 1) & 1) == 0, 1.0, -1.0).astype(F32)
    col_sum = lambda a: jnp.sum(a, axis=0, keepdims=True)
    a0 = 2.0 * col_sum(jnp.where(even, hsum, 0.0))
    d0 = 2.0 * col_sum(jnp.where(even, 0.0, hsum))
    hr2 = 2.0 * col_sum(jnp.where(even, alt2 * hsum, 0.0))
    hi2 = -2.0 * col_sum(jnp.where(even, 0.0, alt2 * hdiff))
    r0_ref[...] = jnp.concatenate([a0, d0, hr2, hi2, jnp.zeros((4, a0.shape[1]), F32)], axis=0) * inv_n
    hs_ref[...] = hsum.astype(BF16)
    hd_ref[...] = hdiff.astype(BF16)
    hp_ref[...] = hpos.astype(BF16)
    hn_ref[...] = hneg.astype(BF16)


def _hy_filter_taps(seq, w1, b1, w2, b2, freq, w3, b3):
    n_ch = 2 * BRANCH_W
    bands = (HY_EMB - 1) // 2
    t = jnp.linspace(0.0, 1.0, seq, dtype=F32)[:, None]
    w = 2.0 * math.pi * jnp.arange(seq, dtype=F32)[:, None] / seq
    f = jnp.linspace(1e-4, bands - 1, bands, dtype=F32)[None, :]
    feats = jnp.concatenate([t, jnp.cos(f * w), -jnp.sin(f * w),
                             jnp.zeros((seq, HY_PAD - HY_EMB), F32)], axis=-1)
    deltas = jnp.linspace(math.log(HY_DECAY_TARGET) / HY_SLOW_DECAY,
                          math.log(HY_DECAY_TARGET) / HY_FAST_DECAY, n_ch, dtype=F32)[None, :]

    def pad2(a, r, c):
        a = a.astype(F32)
        return jnp.pad(a, ((0, r - a.shape[0]), (0, c - a.shape[1])))

    w1p = pad2(w1, HY_PAD, HY_PAD)
    w2p = pad2(w2, HY_PAD, HY_PAD)
    b1p = pad2(b1[None], 1, HY_PAD)
    b2p = pad2(b2[None], 1, HY_PAD)
    f0p = pad2(freq[0][None], 1, HY_PAD)
    f1p = pad2(freq[1][None], 1, HY_PAD)
    w3p = pad2(w3, HY_PAD, 2 * n_ch)
    b3r = b3.astype(F32)[None]
    tn = 256
    nt = n_ch // tn
    full = lambda shape: pl.BlockSpec(shape, lambda j: (0, 0))
    tap_spec = pl.BlockSpec((seq, tn), lambda j: (0, j))
    return pl.pallas_call(
        functools.partial(_hy_filter_kernel, inv_n=1.0 / (2 * seq)),
        out_shape=(jax.ShapeDtypeStruct((seq, n_ch), BF16),) * 4 + (jax.ShapeDtypeStruct((8, n_ch), F32),),
        grid=(nt,),
        in_specs=[full((seq, HY_PAD)), full((HY_PAD, HY_PAD)), full((1, HY_PAD)),
                  full((HY_PAD, HY_PAD)), full((1, HY_PAD)), full((1, HY_PAD)), full((1, HY_PAD)),
                  pl.BlockSpec((HY_PAD, tn), lambda j: (0, j)),
                  pl.BlockSpec((HY_PAD, tn), lambda j: (0, j + nt)),
                  pl.BlockSpec((1, tn), lambda j: (0, j)),
                  pl.BlockSpec((1, tn), lambda j: (0, j + nt)),
                  pl.BlockSpec((1, tn), lambda j: (0, j)),
                  full((seq, 1))],
        out_specs=(tap_spec,) * 4 + (pl.BlockSpec((8, tn), lambda j: (0, j)),),
        scratch_shapes=[pltpu.VMEM((seq, HY_PAD), F32)],
        compiler_params=_params("arbitrary"),
        name="hy_filter",
    )(feats, w1p, b1p, w2p, b2p, f0p, f1p, w3p, w3p, b3r, b3r, deltas, t)


def _hy_spectrum_kernel(c_ref, s_ref, hse_ref, hde_ref, hpo_ref, hno_ref,
                        ac_ref, as_ref, bc_ref, bs_ref, gc_ref, gs_ref, *, n_half):
    i = pl.program_id(1)
    dot = lambda w, h: jnp.dot(w[...], h[...], preferred_element_type=F32)
    hec, hes = dot(c_ref, hse_ref), dot(s_ref, hde_ref)
    upc, ups = dot(c_ref, hpo_ref), dot(s_ref, hpo_ref)
    umc, ums = dot(c_ref, hno_ref), dot(s_ref, hno_ref)
    tm = hec.shape[0]
    k = (lax.broadcasted_iota(jnp.int32, (tm, 128), 0) + i * tm).astype(F32)
    psi = k * (math.pi / n_half)
    reps = hec.shape[1] // 128
    cp = jnp.concatenate([jnp.cos(psi)] * reps, axis=1)
    sp = jnp.concatenate([jnp.sin(psi)] * reps, axis=1)
    w = 1.0 / n_half
    ac_ref[...] = w * hec
    as_ref[...] = w * hes
    bc_ref[...] = w * (cp * upc - sp * ups + umc)
    bs_ref[...] = w * (cp * ups + sp * upc - ums)
    gc_ref[...] = w * (upc + cp * umc - sp * ums)
    gs_ref[...] = w * (ups - cp * ums - sp * umc)


def _hy_spectrum(cm, s1, taps):
    hsum2, hdiff2, hpos2, hneg2 = taps
    half, n_ch = hsum2.shape[0], hsum2.shape[1] // 2
    tm = min(512, half)
    tn = 512
    odd = n_ch // tn
    w_spec = pl.BlockSpec((tm, half), lambda j, i: (i, 0))
    even_spec = pl.BlockSpec((half, tn), lambda j, i: (0, j))
    odd_spec = pl.BlockSpec((half, tn), lambda j, i: (0, j + odd))
    o_spec = pl.BlockSpec((tm, tn), lambda j, i: (i, j))
    return pl.pallas_call(
        functools.partial(_hy_spectrum_kernel, n_half=half),
        out_shape=(jax.ShapeDtypeStruct((half, n_ch), F32),) * 6,
        grid=(n_ch // tn, half // tm),
        in_specs=[w_spec, w_spec, even_spec, even_spec, odd_spec, odd_spec],
        out_specs=(o_spec,) * 6,
        compiler_params=_params("parallel", "arbitrary"),
        name="hy_spectrum",
    )(cm, s1, hsum2, hdiff2, hpos2, hneg2)


def _hy_conv3_kernel(u_ref, w_ref, b_ref, o_ref, vb_ref):
    u = u_ref[...]
    n = u.shape[0]
    rows = lax.broadcasted_iota(jnp.int32, u.shape, 0)
    prev = jnp.where(rows == 0, 0.0, pltpu.roll(u, 1, 0))
    nxt = jnp.where(rows == n - 1, 0.0, pltpu.roll(u, n - 1, 0))
    w = w_ref[...]
    out = prev * w[0:1] + u * w[1:2] + nxt * w[2:3] + b_ref[...]
    o_ref[...] = out
    vb_ref[...] = out.astype(BF16)


def _hy_conv3(proj, bsz, seq, conv_w, conv_b):
    m = bsz * seq
    tn = 256
    nb = 3 * BRANCH_W // tn
    return pl.pallas_call(
        _hy_conv3_kernel,
        out_shape=(jax.ShapeDtypeStruct((m, 3 * BRANCH_W), F32),
                   jax.ShapeDtypeStruct((m, 3 * BRANCH_W), BF16)),
        grid=(bsz, nb),
        in_specs=[pl.BlockSpec((seq, tn), lambda b, j: (b, j + COL_BU // tn)),
                  pl.BlockSpec((3, tn), lambda b, j: (0, j)),
                  pl.BlockSpec((1, tn), lambda b, j: (0, j))],
        out_specs=(pl.BlockSpec((seq, tn), lambda b, j: (b, j)),
                   pl.BlockSpec((seq, tn), lambda b, j: (b, j))),
        compiler_params=_params("parallel", "parallel"),
        name="hy_conv3",
    )(proj, conv_w.astype(F32), conv_b.astype(F32).reshape(1, -1))


def _hy_fwd_kernel(c_ref, s_ref, ze_ref, zo_ref, ac_ref, as_ref, bc_ref, bs_ref, gc_ref, gs_ref, r0_ref,
                   pc_ref, ps_ref, qc_ref, qs_ref):
    dot = lambda w, z: jnp.dot(w[...], z[...], preferred_element_type=F32)
    ec, es = dot(c_ref, ze_ref), dot(s_ref, ze_ref)
    oc, os_ = dot(c_ref, zo_ref), dot(s_ref, zo_ref)
    a_c, a_s = ac_ref[...], as_ref[...]
    b_c, b_s = bc_ref[...], bs_ref[...]
    g_c, g_s = gc_ref[...], gs_ref[...]
    outs = (ec * a_c - es * a_s + oc * b_c - os_ * b_s,
            ec * a_s + es * a_c + oc * b_s + os_ * b_c,
            ec * g_c - es * g_s + oc * a_c - os_ * a_s,
            ec * g_s + es * g_c + oc * a_s + os_ * a_c)
    r0 = r0_ref[...]
    a0, d0, hr2, hi2 = r0[0:1], r0[1:2], r0[2:3], r0[3:4]
    e0, eh, o0, oh = ec[0:1], es[0:1], oc[0:1], os_[0:1]
    first = (e0 * a0 + o0 * d0, eh * hr2 + oh * hi2, e0 * d0 + o0 * a0, oh * hr2 - eh * hi2)
    top = 16
    is_row0 = (lax.broadcasted_iota(jnp.int32, (top, ec.shape[1]), 0) == 0) & (pl.program_id(2) == 0)
    for ref, val, row0 in zip((pc_ref, ps_ref, qc_ref, qs_ref), outs, first):
        ref[...] = val.astype(BF16)
        ref[0:top, :] = jnp.where(is_row0, row0, val[0:top]).astype(BF16)


HY_TN = 512


def _hy_fwd(cm, s1, z_even, z_odd, tables, r0, h_col, bsz, half):
    tm = min(512, half)
    tn = HY_TN
    mt = half // tm
    w_spec = pl.BlockSpec((tm, half), lambda b, j, i: (i, 0))
    h_spec = pl.BlockSpec((tm, tn), lambda b, j, i: (i, j + h_col // tn))
    o_spec = pl.BlockSpec((tm, tn), lambda b, j, i: (b * mt + i, j))
    return pl.pallas_call(
        _hy_fwd_kernel,
        out_shape=(jax.ShapeDtypeStruct((bsz * half, BRANCH_W), BF16),) * 4,
        grid=(bsz, BRANCH_W // tn, mt),
        in_specs=[w_spec, w_spec, z_even[1], z_odd[1]] + [h_spec] * 6
                 + [pl.BlockSpec((8, tn), lambda b, j, i: (0, j + h_col // tn))],
        out_specs=(o_spec,) * 4,
        compiler_params=_params("parallel", "parallel", "arbitrary"),
        name="hy_fwd",
    )(cm, s1, z_even[0], z_odd[0], *tables, r0)


def _hy_inv_convs(c_ref, s_ref, pc_ref, ps_ref, qc_ref, qs_ref):
    dot = lambda w, y: jnp.dot(w[...], y[...], preferred_element_type=F32)
    return dot(c_ref, pc_ref) + dot(s_ref, ps_ref), dot(c_ref, qc_ref) + dot(s_ref, qs_ref)


def _hy_inv_mid_kernel(c_ref, s_ref, pc_ref, ps_ref, qc_ref, qs_ref, ge_ref, go_ref, ze_ref, zo_ref,
                       bias_ref, o_ref, ob_ref):
    convs = _hy_inv_convs(c_ref, s_ref, pc_ref, ps_ref, qc_ref, qs_ref)
    for par, (conv, g_ref, z_ref) in enumerate(zip(convs, (ge_ref, go_ref), (ze_ref, zo_ref))):
        out = g_ref[...] * (conv + bias_ref[...] * z_ref[...])
        o_ref[par] = out
        ob_ref[par] = out.astype(BF16)


def _hy_inv_last_kernel(c_ref, s_ref, pc_ref, ps_ref, qc_ref, qs_ref, ge_ref, go_ref, ze_ref, zo_ref,
                        bias_ref, se_ref, so_ref, o_ref):
    convs = _hy_inv_convs(c_ref, s_ref, pc_ref, ps_ref, qc_ref, qs_ref)
    for par, (conv, g_ref, z_ref, sz_ref) in enumerate(zip(convs, (ge_ref, go_ref), (ze_ref, zo_ref),
                                                           (se_ref, so_ref))):
        out = g_ref[...] * (conv + bias_ref[...] * z_ref[...])
        o_ref[par] = (out * sz_ref[...]).astype(BF16)


def _hy_inv(cm, s2, spectra, gates, zprev, bias_row, bsz, half, silu=None):
    tm = min(512, half)
    tn = HY_TN
    mt = half // tm
    w_spec = pl.BlockSpec((tm, half), lambda b, j, i: (i, 0))
    y_spec = pl.BlockSpec((half, tn), lambda b, j, i: (b, j))
    o_spec = pl.BlockSpec((2, tm, tn), lambda b, j, i: (0, b * mt + i, j))
    pairs = list(gates) + list(zprev)
    in_specs = [w_spec, w_spec] + [y_spec] * 4 + [s for _, s in pairs] + [pl.BlockSpec((1, tn), lambda b, j, i: (0, j))]
    args = [cm, s2, *spectra] + [a for a, _ in pairs] + [bias_row]
    shape = lambda dt: jax.ShapeDtypeStruct((2, bsz * half, BRANCH_W), dt)
    if silu is None:
        body, out_shape, out_specs = _hy_inv_mid_kernel, (shape(F32), shape(BF16)), (o_spec, o_spec)
    else:
        body, out_shape, out_specs = _hy_inv_last_kernel, shape(BF16), o_spec
        in_specs += [s for _, s in silu]
        args += [a for a, _ in silu]
    return pl.pallas_call(
        body, out_shape=out_shape, grid=(bsz, BRANCH_W // tn, mt), in_specs=in_specs, out_specs=out_specs,
        compiler_params=_params("parallel", "parallel", "arbitrary"),
        name="hy_inv",
    )(*args)


def _hyena_branch(proj, bsz, seq, dft, conv_w, conv_b, w1, b1, w2, b2, freq, w3, b3, bias):
    cm, s1, s2 = dft
    half = seq // 2
    m2 = bsz * half
    tm = min(512, half)
    tn = HY_TN
    mt = half // tm
    w3c = 3 * BRANCH_W
    full2 = lambda col: pl.BlockSpec((half, tn), lambda b, j, i: (b, j + col // tn))
    full3 = lambda par: pl.BlockSpec((None, half, tn), lambda b, j, i: (par, b, j))
    row2 = lambda col: pl.BlockSpec((tm, tn), lambda b, j, i: (b * mt + i, j + col // tn))
    row3 = lambda par: pl.BlockSpec((None, tm, tn), lambda b, j, i: (par, b * mt + i, j))
    *taps, r0 = _hy_filter_taps(seq, w1, b1, w2, b2, freq, w3, b3)
    tables = _hy_spectrum(cm, s1, [a.reshape(half, -1) for a in taps])
    uc, ucb = _hy_conv3(proj, bsz, seq, conv_w, conv_b)
    uc2, ucb2 = uc.reshape(m2, 2 * w3c), ucb.reshape(m2, 2 * w3c)
    proj2 = proj.reshape(m2, 2 * PROJ_A_N)
    bias = bias.astype(F32)
    spectra = _hy_fwd(cm, s1, (ucb2, full2(0)), (ucb2, full2(w3c)), tables, r0, 0, bsz, half)
    z1, z1b = _hy_inv(cm, s2, spectra,
                      gates=((uc2, row2(BRANCH_W)), (uc2, row2(w3c + BRANCH_W))),
                      zprev=((uc2, row2(0)), (uc2, row2(w3c))), bias_row=bias[0:1], bsz=bsz, half=half)
    spectra = _hy_fwd(cm, s1, (z1b, full3(0)), (z1b, full3(1)), tables, r0, BRANCH_W, bsz, half)
    return _hy_inv(cm, s2, spectra,
                   gates=((uc2, row2(2 * BRANCH_W)), (uc2, row2(w3c + 2 * BRANCH_W))),
                   zprev=((z1, row3(0)), (z1, row3(1))), bias_row=bias[1:2], bsz=bsz, half=half,
                   silu=((proj2, row2(COL_BZ)), (proj2, row2(PROJ_A_N + COL_BZ))))


def _rope_table_kernel(pos_ref, inv_ref, cos_ref, sin_ref):
    ang = pos_ref[...] * inv_ref[...]
    lane = lax.broadcasted_iota(jnp.int32, ang.shape, 1)
    live = lane < MLA_ROPE
    cos_ref[...] = jnp.where(live, jnp.cos(ang), 0.0)
    sin_ref[...] = jnp.where(live, jnp.where(lane < MLA_ROPE // 2, -1.0, 1.0) * jnp.sin(ang), 0.0)


def _rope_tables(positions):
    m = positions.size
    half = MLA_ROPE // 2
    inv = ROPE_BASE ** (-jnp.arange(half, dtype=F32) / half)
    inv = jnp.concatenate([inv, inv, jnp.zeros((128 - MLA_ROPE,), F32)])[None]
    pos = positions.astype(F32).reshape(m, 1)
    tm = min(1024, m)
    spec = pl.BlockSpec((tm, 128), lambda i: (i, 0))
    return pl.pallas_call(
        _rope_table_kernel,
        out_shape=(jax.ShapeDtypeStruct((m, 128), F32),) * 2,
        grid=(m // tm,),
        in_specs=[pl.BlockSpec((tm, 1), lambda i: (i, 0)), pl.BlockSpec((1, 128), lambda i: (0, 0))],
        out_specs=(spec, spec),
        compiler_params=_params("parallel"),
        name="rope_table",
    )(pos, inv)


def _rope128(x, cos_t, sin_t):
    lane = lax.broadcasted_iota(jnp.int32, x.shape, 1)
    half = MLA_ROPE // 2
    partner = jnp.where(lane < half, pltpu.roll(x, 128 - half, 1), pltpu.roll(x, half, 1))
    return x * cos_t + partner * sin_t


def _rms(x, g):
    ms = jnp.mean(jnp.square(x), axis=-1, keepdims=True)
    return x * lax.rsqrt(ms + RMS_EPS) * g


def _mla_q_kernel(cq_ref, g_ref, w_ref, cos_ref, sin_ref, q_ref, *, scale):
    xn = _rms(cq_ref[...], g_ref[...]).astype(BF16)
    q = jnp.dot(xn, w_ref[...], preferred_element_type=F32) * scale
    cos_t, sin_t = cos_ref[...], sin_ref[...]
    for h in range(MLA_HEADS):
        base = h * MLA_QK_PAD
        q_ref[:, base:base + MLA_NOPE] = q[:, base:base + MLA_NOPE].astype(BF16)
        q_ref[:, base + MLA_NOPE:base + MLA_QK_PAD] = _rope128(
            q[:, base + MLA_NOPE:base + MLA_QK_PAD], cos_t, sin_t).astype(BF16)


def _mla_kv_kernel(ckv_ref, g_ref, wk_ref, wv_ref, kr_ref, cos_ref, sin_ref, k_ref, v_ref):
    xn = _rms(ckv_ref[...], g_ref[...]).astype(BF16)
    kn = jnp.dot(xn, wk_ref[...], preferred_element_type=F32)
    v_ref[...] = jnp.dot(xn, wv_ref[...], preferred_element_type=F32).astype(BF16)
    kr = _rope128(kr_ref[:, 0:128], cos_ref[...], sin_ref[...]).astype(BF16)
    for h in range(MLA_HEADS):
        base = h * MLA_QK_PAD
        k_ref[:, base:base + MLA_NOPE] = kn[:, h * MLA_NOPE:(h + 1) * MLA_NOPE].astype(BF16)
        k_ref[:, base + MLA_NOPE:base + MLA_QK_PAD] = kr


def _mla_attn_kernel(q_ref, k_ref, v_ref, z_ref, o_ref):
    k, v = k_ref[...], v_ref[...]
    half = q_ref.shape[0] // 2
    for r in range(2):
        rows = pl.ds(r * half, half)
        s = lax.dot_general(q_ref[rows, :], k, (((1,), (1,)), ((), ())),
                            preferred_element_type=F32)
        p = jnp.exp2(s - jnp.max(s, axis=-1, keepdims=True))
        l = jnp.sum(p, axis=-1, keepdims=True)
        o = jnp.dot(p.astype(BF16), v, preferred_element_type=F32)
        o_ref[rows, :] = (o / l * z_ref[rows, :]).astype(BF16)


def _mla_branch(proj, proj_b, bsz, seq, rope, q_norm_g, w_uq, kv_norm_g, w_ukv):
    m = bsz * seq
    cos_t, sin_t = rope
    dqk = MLA_NOPE + MLA_ROPE
    hq = MLA_HEADS * MLA_QK_PAD
    w_q = w_uq.reshape(MLA_LORA, MLA_HEADS, dqk)
    w_q = jnp.pad(w_q, ((0, 0), (0, 0), (0, MLA_QK_PAD - dqk))).reshape(MLA_LORA, hq).astype(BF16)
    w_kv = w_ukv.reshape(MLA_LORA, MLA_HEADS, MLA_NOPE + MLA_V)
    w_k = w_kv[:, :, :MLA_NOPE].reshape(MLA_LORA, MLA_HEADS * MLA_NOPE).astype(BF16)
    w_v = w_kv[:, :, MLA_NOPE:].reshape(MLA_LORA, MLA_HEADS * MLA_V).astype(BF16)
    tm = min(512, m)
    row = lambda shape, col=0: pl.BlockSpec(shape, lambda i: (i, col))
    full = lambda shape: pl.BlockSpec(shape, lambda i: (0, 0))
    qp = pl.pallas_call(
        functools.partial(_mla_q_kernel, scale=dqk ** -0.5 * math.log2(math.e)),
        out_shape=jax.ShapeDtypeStruct((m, hq), BF16),
        grid=(m // tm,),
        in_specs=[row((tm, MLA_LORA), COL_CQ // MLA_LORA), full((1, MLA_LORA)), full((MLA_LORA, hq)),
                  row((tm, 128)), row((tm, 128))],
        out_specs=row((tm, hq)),
        compiler_params=_params("parallel"),
        name="mla_q",
    )(proj, q_norm_g.astype(F32).reshape(1, -1), w_q, cos_t, sin_t)
    kp, vp = pl.pallas_call(
        _mla_kv_kernel,
        out_shape=(jax.ShapeDtypeStruct((m, hq), BF16),
                   jax.ShapeDtypeStruct((m, MLA_HEADS * MLA_V), BF16)),
        grid=(m // tm,),
        in_specs=[row((tm, MLA_LORA), COL_CKV // MLA_LORA), full((1, MLA_LORA)),
                  full((MLA_LORA, MLA_HEADS * MLA_NOPE)), full((MLA_LORA, MLA_HEADS * MLA_V)),
                  row((tm, 512), COL_CKR // 512), row((tm, 128)), row((tm, 128))],
        out_specs=(row((tm, hq)), row((tm, MLA_HEADS * MLA_V))),
        compiler_params=_params("parallel"),
        name="mla_kv",
    )(proj, kv_norm_g.astype(F32).reshape(1, -1), w_k, w_v, proj_b, cos_t, sin_t)
    tq = min(512, seq)
    qt = seq // tq
    return pl.pallas_call(
        _mla_attn_kernel,
        out_shape=jax.ShapeDtypeStruct((m, MLA_HEADS * MLA_V), BF16),
        grid=(bsz, MLA_HEADS, qt),
        in_specs=[pl.BlockSpec((tq, MLA_QK_PAD), lambda b, h, i: (b * qt + i, h)),
                  pl.BlockSpec((seq, MLA_QK_PAD), lambda b, h, i: (b, h)),
                  pl.BlockSpec((seq, MLA_V), lambda b, h, i: (b, h)),
                  pl.BlockSpec((tq, MLA_V), lambda b, h, i: (b * qt + i, h + COL_CZ // MLA_V))],
        out_specs=pl.BlockSpec((tq, MLA_V), lambda b, h, i: (b * qt + i, h)),
        compiler_params=_params("parallel", "parallel", "arbitrary"),
        name="mla_attn",
    )(qp, kp, vp, proj_b)


def _lift_kernel(ya_ref, yb_ref, yc_ref, w_ref, ga_ref, gb_ref, gc_ref, o_ref, wb_ref):
    @pl.when((pl.program_id(1) == 0) & (pl.program_id(2) == 0))
    def _():
        wb_ref[...] = w_ref[...].astype(BF16)

    acc = ga_ref[...] * jnp.dot(ya_ref[...], wb_ref[0], preferred_element_type=F32)
    acc += gb_ref[...] * jnp.dot(yb_ref[...], wb_ref[1], preferred_element_type=F32)
    acc += gc_ref[...] * jnp.dot(yc_ref[...], wb_ref[2], preferred_element_type=F32)
    o_ref[...] = acc.astype(BF16)


def _lift(ya, yb, yc, w_lift, layer, proj_b):
    m2 = ya.shape[0] // 2
    tm = min(512, m2)
    tn = 512
    y_spec = pl.BlockSpec((tm, BRANCH_W), lambda j, par, i: (i, par))
    gate = lambda n: pl.BlockSpec(
        (tm, tn), lambda j, par, i: (i, j + par * (PROJ_B_N // tn) + (COL_GATE + n * D_MODEL) // tn))
    gates2 = proj_b.reshape(m2, 2 * PROJ_B_N)
    mix = pl.pallas_call(
        _lift_kernel,
        out_shape=jax.ShapeDtypeStruct((m2, 2 * D_MODEL), BF16),
        grid=(D_MODEL // tn, 2, m2 // tm),
        in_specs=[y_spec, pl.BlockSpec((None, tm, BRANCH_W), lambda j, par, i: (par, i, 0)), y_spec,
                  pl.BlockSpec((None, N_BRANCH, BRANCH_W, tn), lambda j, par, i: (layer, 0, 0, j)),
                  gate(0), gate(1), gate(2)],
        out_specs=pl.BlockSpec((tm, tn), lambda j, par, i: (i, j + par * (D_MODEL // tn))),
        scratch_shapes=[pltpu.VMEM((N_BRANCH, BRANCH_W, tn), BF16)],
        compiler_params=_params("parallel", "arbitrary", "arbitrary"),
        name="lift",
    )(ya.reshape(m2, 2 * BRANCH_W), yb, yc.reshape(m2, 2 * BRANCH_W), w_lift, gates2, gates2, gates2)
    return mix.reshape(2 * m2, D_MODEL)


def _out_kernel(mix_ref, wo_ref, p_ref, wp_ref, sp_ref, x_ref, g_ref, b_ref, o_ref):
    mixed = jnp.dot(mix_ref[...], wo_ref[...], preferred_element_type=F32)
    ple = jnp.dot(p_ref[...].astype(BF16), wp_ref[...], preferred_element_type=F32) * sp_ref[...]
    r = DEEPNORM_ALPHA * x_ref[...] + mixed + ple
    mu = jnp.mean(r, axis=-1, keepdims=True)
    var = jnp.mean(jnp.square(r - mu), axis=-1, keepdims=True)
    o_ref[...] = (r - mu) * lax.rsqrt(var + LN_EPS) * g_ref[...] + b_ref[...]


def _out_norm(mix, w_out, p, w_ple, proj, x, ln_g, ln_b):
    m = mix.shape[0]
    tm = min(256, m)
    row = lambda w, col=0: pl.BlockSpec((tm, w), lambda i: (i, col))
    full = lambda shape: pl.BlockSpec(shape, lambda i: (0, 0))
    return pl.pallas_call(
        _out_kernel,
        out_shape=jax.ShapeDtypeStruct((m, D_MODEL), F32),
        grid=(m // tm,),
        in_specs=[row(D_MODEL), full((D_MODEL, D_MODEL)), row(PLE_DIM), full((PLE_DIM, D_MODEL)),
                  row(D_MODEL, COL_PLE // D_MODEL), row(D_MODEL), full((1, D_MODEL)), full((1, D_MODEL))],
        out_specs=row(D_MODEL),
        compiler_params=_params("parallel"),
        name="out_norm",
    )(mix, w_out.astype(BF16), p, w_ple.astype(BF16), proj, x,
      ln_g.astype(F32).reshape(1, -1), ln_b.astype(F32).reshape(1, -1))


def kernel(x, p, positions, w_in, s5_lambda_re, s5_lambda_im, s5_log_dt, s5_b_re, s5_b_im, s5_c_re, s5_c_im, s5_d, s5_w_glu, s5_b_glu, hy_conv_w, hy_conv_b, hy_w1, hy_b1, hy_w2, hy_b2, hy_freq, hy_w3, hy_b3, hy_bias, mla_q_norm, mla_w_uq, mla_kv_norm, mla_w_ukv, w_lift, w_out, w_ple, ln_g, ln_b):
    bsz, seq, _ = x.shape
    m = bsz * seq
    depth = w_in.shape[0]
    dft = _dft_matrices(seq // 2)
    rope = _rope_tables(positions)
    s5_tiled = _s5_tiled_params(s5_lambda_re, s5_lambda_im, s5_log_dt, s5_b_re, s5_b_im,
                                s5_c_re, s5_c_im, s5_d)
    xf = x.reshape(m, D_MODEL).astype(F32)
    for i in range(depth):
        proj, proj_b = _proj_both(xf, w_in, i)
        y_a = _s5_branch(proj, bsz, seq, [a[i] for a in s5_tiled], s5_w_glu[i], s5_b_glu[i])
        y_b = _hyena_branch(proj, bsz, seq, dft, hy_conv_w[i], hy_conv_b[i], hy_w1[i], hy_b1[i],
                            hy_w2[i], hy_b2[i], hy_freq[i], hy_w3[i], hy_b3[i], hy_bias[i])
        y_c = _mla_branch(proj, proj_b, bsz, seq, rope, mla_q_norm[i], mla_w_uq[i], mla_kv_norm[i], mla_w_ukv[i])
        mix = _lift(y_a, y_b, y_c, w_lift, i, proj_b)
        xf = _out_norm(mix, w_out[i], p[i].reshape(m, PLE_DIM), w_ple[i], proj_b, xf, ln_g[i], ln_b[i])
    return xf.reshape(bsz, seq, D_MODEL).astype(x.dtype)
```

```python
import functools
import math

import numpy as np
import jax
import jax.numpy as jnp
from jax import lax
from jax.experimental import pallas as pl
from jax.experimental.pallas import tpu as pltpu

F32 = jnp.float32
BF16 = jnp.bfloat16

D_MODEL = 2048
PLE_DIM = 256
N_BRANCH = 3
BRANCH_W = 1024

S5_GROUP = 16
S5_GROUPS = BRANCH_W // S5_GROUP
S5_STATE = 64
S5_CHUNK = 16
S5_ROW = S5_CHUNK * S5_GROUP

HY_EMB = 33
HY_FF = 64
HY_PAD = 128
HY_DECAY_TARGET = 0.01
HY_FAST_DECAY = 0.3
HY_SLOW_DECAY = 1.5
DFT_ROWS = 64

MLA_HEADS = 8
MLA_NOPE = 128
MLA_ROPE = 64
MLA_V = 128
MLA_LORA = 512
MLA_QK_PAD = 256
ROPE_BASE = 10000.0

LN_EPS = 1e-5
RMS_EPS = 1e-6
DEPTH = 2
DEEPNORM_ALPHA = (2 * DEPTH) ** 0.25

COL_AX = 0
COL_AZ = 1024
COL_BU = 2048
COL_BZ = 5120
COL_CQ = 6144
COL_CKV = 6656
PROJ_A_N = 7168
COL_PLE = 0
COL_GATE = 2048
COL_CZ = 8192
COL_CKR = 9216
PROJ_B_N = 9728
PROJ_TN = 512
W_IN_CKR = 7168
W_IN_CZ = 7232
W_IN_GATE = 8256
W_IN_PLE = 14400

VMEM_LIMIT = 56 * 1024 * 1024


def _params(*sem):
    return pltpu.CompilerParams(dimension_semantics=sem, vmem_limit_bytes=VMEM_LIMIT)


def _sigmoid(x):
    return 0.5 * jnp.tanh(0.5 * x) + 0.5


def _in_tiles(j, ranges):
    hit = None
    for lo, hi in ranges:
        cur = (j >= lo // PROJ_TN) & (j < hi // PROJ_TN)
        hit = cur if hit is None else (hit | cur)
    return hit


def _proj_kernel(x_ref, w_ref, o_ref, xb_ref, *, silu_cols, sigm_cols):
    j = pl.program_id(1)

    @pl.when(j == 0)
    def _():
        xb_ref[...] = x_ref[...].astype(BF16)

    acc = jnp.dot(xb_ref[...], w_ref[...].astype(BF16), preferred_element_type=F32)
    is_silu = _in_tiles(j, silu_cols)
    plain = jnp.logical_not(is_silu)

    @pl.when(is_silu)
    def _():
        o_ref[...] = acc * _sigmoid(acc)

    if sigm_cols:
        is_sigm = _in_tiles(j, sigm_cols)
        plain = jnp.logical_not(is_silu | is_sigm)

        @pl.when(is_sigm)
        def _():
            o_ref[...] = _sigmoid(acc)

    @pl.when(plain)
    def _():
        o_ref[...] = acc


def _proj(x, w, layer, n_out, silu_cols, sigm_cols, name):
    m, k = x.shape
    tm = min(1024, m)
    return pl.pallas_call(
        functools.partial(_proj_kernel, silu_cols=silu_cols, sigm_cols=sigm_cols),
        out_shape=jax.ShapeDtypeStruct((m, n_out), F32),
        grid=(m // tm, n_out // PROJ_TN),
        in_specs=[pl.BlockSpec((tm, k), lambda i, j: (i, 0)),
                  pl.BlockSpec((None, k, PROJ_TN), lambda i, j: (layer, 0, j))],
        out_specs=pl.BlockSpec((tm, PROJ_TN), lambda i, j: (i, j)),
        scratch_shapes=[pltpu.VMEM((tm, k), BF16)],
        compiler_params=_params("parallel", "arbitrary"),
        name=name,
    )(x, w)


def _repack_src_tile(j):
    t = lambda col: col // PROJ_TN
    return jnp.where(j < t(COL_GATE), j + t(W_IN_PLE),
                     jnp.where(j < t(COL_CZ), j - t(COL_GATE) + t(W_IN_GATE),
                               jnp.where(j < t(COL_CKR), j - t(COL_CZ) + t(W_IN_CZ), t(W_IN_CKR))))


def _repack_kernel(a_ref, b_ref, o_ref):
    is_last = pl.program_id(0) == pl.num_programs(0) - 1
    n_sub = PROJ_TN // 128
    src = [a_ref[:, c * 128:(c + 1) * 128] for c in range(n_sub)] + [b_ref[...]]
    lane = lax.broadcasted_iota(jnp.int32, src[0].shape, 1)
    low = lane < MLA_ROPE
    rolled = [pltpu.roll(t, 128 - MLA_ROPE, 1) for t in src]
    for c in range(n_sub):
        shifted = jnp.where(low, rolled[c], rolled[c + 1])
        plain = jnp.where(low, src[0], 0.0) if c == 0 else jnp.zeros_like(shifted)
        o_ref[:, c * 128:(c + 1) * 128] = jnp.where(is_last, plain, shifted).astype(BF16)


def _repack_b(w_in, layer):
    k = w_in.shape[1]
    sub = PROJ_TN // 128
    return pl.pallas_call(
        _repack_kernel,
        out_shape=jax.ShapeDtypeStruct((1, k, PROJ_B_N), BF16),
        grid=(PROJ_B_N // PROJ_TN,),
        in_specs=[pl.BlockSpec((None, k, PROJ_TN), lambda j: (layer, 0, _repack_src_tile(j))),
                  pl.BlockSpec((None, k, 128), lambda j: (layer, 0, (_repack_src_tile(j) + 1) * sub))],
        out_specs=pl.BlockSpec((None, k, PROJ_TN), lambda j: (0, 0, j)),
        compiler_params=_params("parallel"),
        name="repack_b",
    )(w_in, w_in)


def _proj_both(x, w_in, layer):
    proj_a = _proj(x, w_in, layer, PROJ_A_N, ((COL_AZ, COL_BU), (COL_BZ, COL_CQ)), (), "proj_a")
    proj_b = _proj(x, _repack_b(w_in, layer), 0, PROJ_B_N, ((COL_CZ, COL_CKR),), ((COL_PLE, COL_CZ),),
                   "proj_b")
    return proj_a, proj_b


S5_LANES = 8 * S5_STATE
S5_SLAB = 8
S5_RELAYOUT_ROWS = 32


def _s5_tile_lanes(a):
    return jnp.concatenate([a[:, 0]] * 4 + [a[:, 1]] * 4, axis=-1)


def _s5_tiled_params(lam_re, lam_im, log_dt, b_re, b_im, c_re, c_im, d):
    f = lambda a: a.astype(F32)
    depth = lam_re.shape[0]
    ldt = jnp.broadcast_to(f(log_dt)[..., None, None], lam_re.shape[:3] + (1, S5_STATE))
    return (_s5_tile_lanes(f(lam_re)[:, :, :, None, :]), _s5_tile_lanes(f(lam_im)[:, :, :, None, :]),
            _s5_tile_lanes(ldt),
            _s5_tile_lanes(jnp.swapaxes(f(b_re), -1, -2)), _s5_tile_lanes(jnp.swapaxes(f(b_im), -1, -2)),
            _s5_tile_lanes(f(c_re)), _s5_tile_lanes(f(c_im)),
            f(d).reshape(depth, S5_GROUPS, S5_GROUP, 1))


def _s5_mats_kernel(lr_ref, li_ref, ldt_ref, br_ref, bi_ref, cr_ref, ci_ref, d_ref,
                    ms_ref, mi_ref, mo_ref, ar_ref, ai_ref):
    t_n, h_n = S5_CHUNK, S5_GROUP
    hi = lax.Precision.HIGHEST
    nt = (((1,), (1,)), ((), ()))
    blk = lax.broadcasted_iota(jnp.int32, (1, S5_LANES), 1) // S5_STATE
    is_im = (blk // 2) % 2 == 1
    is_fwd = blk < 4
    steps = lax.broadcasted_iota(jnp.int32, (24, S5_LANES), 0).astype(F32)
    lane_k = lax.broadcasted_iota(jnp.int32, (h_n, S5_ROW), 1)
    sub_k = lax.broadcasted_iota(jnp.int32, (h_n, S5_ROW), 0)

    def per_group(gi, carry):
        lr, li = lr_ref[gi], li_ref[gi]
        dt = jnp.exp(ldt_ref[gi])
        zr, zi = lr * dt, li * dt
        mag = jnp.exp(steps * zr)
        tr, ti = mag * jnp.cos(steps * zi), mag * jnp.sin(steps * zi)
        lbr, lbi = tr[1:2], ti[1:2]
        n2 = lr * lr + li * li
        qr = ((lbr - 1.0) * lr + lbi * li) / n2
        qi = (lbi * lr - (lbr - 1.0) * li) / n2
        br, bi = br_ref[gi], bi_ref[gi]
        bbr, bbi = qr * br - qi * bi, qr * bi + qi * br
        y1, y2 = jnp.where(is_im, bbi, bbr), jnp.where(is_im, bbr, bbi)
        cr, ci = cr_ref[gi], ci_ref[gi]

        def pick(tab, t_fwd, t_bwd):
            return jnp.where(is_fwd, tab[t_fwd:t_fwd + 1], tab[t_bwd:t_bwd + 1])

        def c_times(p_r, p_i):
            return jnp.where(is_im, -(cr * p_i + ci * p_r), cr * p_r - ci * p_i)

        q_rows = []
        for t in range(t_n):
            a_r, a_i = pick(tr, t_n - 1 - t, t), pick(ti, t_n - 1 - t, t)
            rows = pl.ds(t * h_n, h_n)
            ms_ref[gi, rows, :] = (a_r * y1 + jnp.where(is_im, a_i, -a_i) * y2).astype(BF16)
            mo_ref[gi, rows, :] = c_times(pick(tr, t + 1, t_n - t), pick(ti, t + 1, t_n - t)).astype(BF16)
            q_rows.append(c_times(pick(tr, t, t_n - 1 - t), pick(ti, t, t_n - 1 - t)))
        q = jnp.concatenate(q_rows, axis=0)
        half = S5_LANES // 2
        kf = 0.5 * lax.dot_general(y1[:, :half], q[:, :half], nt, precision=hi, preferred_element_type=F32)
        kb = 0.5 * lax.dot_general(y1[:, half:], q[:, half:], nt, precision=hi, preferred_element_type=F32)
        kf = kf + jnp.where(lane_k == sub_k, d_ref[gi], 0.0)
        for t in range(t_n):
            fwd = kf if t == 0 else jnp.where(lane_k >= h_n * t, pltpu.roll(kf, h_n * t, 1), 0.0)
            sh = h_n * (t_n - 1 - t)
            bwd = kb if sh == 0 else jnp.where(lane_k < S5_ROW - sh, pltpu.roll(kb, S5_ROW - sh, 1), 0.0)
            mi_ref[gi, pl.ds(t * h_n, h_n), :] = (fwd + bwd).astype(BF16)
        ar_ref[gi] = tr[t_n:t_n + 1]
        ai_ref[gi] = ti[t_n:t_n + 1]
        return carry

    lax.fori_loop(0, lr_ref.shape[0], per_group, 0)


def _s5_mats(tiled):
    g_n = S5_GROUPS
    gb = S5_SLAB
    spec = lambda r, w: pl.BlockSpec((gb, r, w), lambda j: (j, 0, 0))
    in_rows = (1, 1, 1, S5_GROUP, S5_GROUP, S5_GROUP, S5_GROUP)
    return pl.pallas_call(
        _s5_mats_kernel,
        out_shape=(jax.ShapeDtypeStruct((g_n, S5_ROW, S5_LANES), BF16),
                   jax.ShapeDtypeStruct((g_n, S5_ROW, S5_ROW), BF16),
                   jax.ShapeDtypeStruct((g_n, S5_ROW, S5_LANES), BF16),
                   jax.ShapeDtypeStruct((g_n, 1, S5_LANES), F32),
                   jax.ShapeDtypeStruct((g_n, 1, S5_LANES), F32)),
        grid=(g_n // gb,),
        in_specs=[spec(r, S5_LANES) for r in in_rows] + [spec(S5_GROUP, 1)],
        out_specs=(spec(S5_ROW, S5_LANES), spec(S5_ROW, S5_ROW), spec(S5_ROW, S5_LANES),
                   spec(1, S5_LANES), spec(1, S5_LANES)),
        compiler_params=_params("parallel"),
        name="s5_mats",
    )(*tiled)


def _seg_transpose(vs, seg):
    vs = list(vs)
    for s in (4, 2, 1):
        keep = (seg & s) == 0
        for i in range(8):
            if i & s:
                continue
            a, b = vs[i], vs[i + s]
            vs[i] = jnp.where(keep, a, pltpu.roll(b, s * S5_GROUP, 1))
            vs[i + s] = jnp.where(keep, pltpu.roll(a, 128 - s * S5_GROUP, 1), b)
    return vs


def _s5_main_kernel(x_ref, ms_ref, mi_ref, mo_ref, ar_ref, ai_ref, y_ref,
                    u_ref, sl_ref, st_ref, yg_ref, *, bsz):
    rows = x_ref.shape[0]
    n_chunks = rows // bsz
    rc = S5_RELAYOUT_ROWS
    seg = lax.broadcasted_iota(jnp.int32, (rc, 128), 1) // S5_GROUP
    slot = (lax.broadcasted_iota(jnp.int32, (1, S5_LANES), 1) // S5_STATE) % 2
    slot128 = slot[:, :128]

    def relayout_in(r, carry):
        r0 = pl.multiple_of(r * rc, rc)
        for th in range(2):
            src = [x_ref[pl.ds(r0, rc), th * 8 + t8, :] for t8 in range(8)]
            for gi, out in enumerate(_seg_transpose(src, seg)):
                u_ref[gi, pl.ds(r0, rc), th * 128:(th + 1) * 128] = out.astype(BF16)
        return carry

    lax.fori_loop(0, rows // rc, relayout_in, 0)

    a_parts = [[], [], [], []]
    for jp in range(S5_SLAB // 2):
        g0, g1 = 2 * jp, 2 * jp + 1
        r0 = jnp.dot(u_ref[g0], ms_ref[g0], preferred_element_type=F32)
        r1 = jnp.dot(u_ref[g1], ms_ref[g1], preferred_element_type=F32)
        loc = jnp.where(slot == 0, r0, r1)
        for k in range(4):
            sl_ref[k, :, jp * 128:(jp + 1) * 128] = loc[:, k * 128:(k + 1) * 128]
        for k, (ref, off) in enumerate(((ar_ref, 0), (ai_ref, 0), (ar_ref, 256), (ai_ref, 256))):
            a_parts[k].append(jnp.where(slot128 == 0, ref[g0][:, off:off + 128], ref[g1][:, off:off + 128]))
    a_fr, a_fi, a_br, a_bi = [jnp.concatenate(parts, axis=1) for parts in a_parts]

    zero = jnp.zeros((1, a_fr.shape[1]), F32)

    def scan(c, carry):
        cb = n_chunks - 1 - c
        new = []
        for b in range(bsz):
            s_fr, s_fi, s_br, s_bi = carry[b]
            rf, rb = b * n_chunks + c, b * n_chunks + cb
            st_ref[0, pl.ds(rf, 1), :] = s_fr
            st_ref[1, pl.ds(rf, 1), :] = s_fi
            st_ref[2, pl.ds(rb, 1), :] = s_br
            st_ref[3, pl.ds(rb, 1), :] = s_bi
            n_fr = a_fr * s_fr - a_fi * s_fi + sl_ref[0, pl.ds(rf, 1), :]
            n_fi = a_fr * s_fi + a_fi * s_fr + sl_ref[1, pl.ds(rf, 1), :]
            n_br = a_br * s_br - a_bi * s_bi + sl_ref[2, pl.ds(rb, 1), :]
            n_bi = a_br * s_bi + a_bi * s_br + sl_ref[3, pl.ds(rb, 1), :]
            new.append((n_fr, n_fi, n_br, n_bi))
        return tuple(new)

    lax.fori_loop(0, n_chunks, scan, tuple((zero,) * 4 for _ in range(bsz)))

    nt = (((1,), (1,)), ((), ()))
    for jp in range(S5_SLAB // 2):
        st = jnp.concatenate([st_ref[k, :, jp * 128:(jp + 1) * 128] for k in range(4)], axis=1)
        for e in range(2):
            g = 2 * jp + e
            st_g = jnp.where(slot == e, st, 0.0).astype(BF16)
            y = (jnp.dot(u_ref[g], mi_ref[g], preferred_element_type=F32)
                 + lax.dot_general(st_g, mo_ref[g], nt, preferred_element_type=F32))
            yg_ref[g] = jax.nn.gelu(y)

    def relayout_out(r, carry):
        r0 = pl.multiple_of(r * rc, rc)
        for th in range(2):
            src = [yg_ref[gi, pl.ds(r0, rc), th * 128:(th + 1) * 128] for gi in range(S5_SLAB)]
            for t8, out in enumerate(_seg_transpose(src, seg)):
                y_ref[pl.ds(r0, rc), th * 8 + t8, :] = out
        return carry

    lax.fori_loop(0, rows // rc, relayout_out, 0)


def _s5_glu_kernel(g_ref, w_ref, b_ref, z_ref, o_ref):
    g = g_ref[...]
    acc = jnp.dot(g.astype(BF16), w_ref[...], preferred_element_type=F32) + b_ref[...]
    o_ref[...] = (g * _sigmoid(acc) * z_ref[...]).astype(o_ref.dtype)


def _s5_branch(proj, bsz, seq, tiled, w_glu, b_glu):
    m_state, m_intra, m_out, a_re, a_im = _s5_mats(tiled)
    t_n = S5_CHUNK
    rows = bsz * (seq // t_n)
    m = bsz * seq
    gb = S5_SLAB
    x3 = proj.reshape(rows, t_n, proj.shape[1])
    mat = lambda r, w: pl.BlockSpec((gb, r, w), lambda s: (s, 0, 0))
    io_spec = pl.BlockSpec((rows, t_n, 128), lambda s: (0, 0, s + COL_AX // 128))
    y = pl.pallas_call(
        functools.partial(_s5_main_kernel, bsz=bsz),
        out_shape=jax.ShapeDtypeStruct((rows, t_n, BRANCH_W), F32),
        grid=(S5_GROUPS // gb,),
        in_specs=[io_spec, mat(S5_ROW, S5_LANES), mat(S5_ROW, S5_ROW), mat(S5_ROW, S5_LANES),
                  mat(1, S5_LANES), mat(1, S5_LANES)],
        out_specs=pl.BlockSpec((rows, t_n, 128), lambda s: (0, 0, s)),
        scratch_shapes=[pltpu.VMEM((gb, rows, S5_ROW), BF16),
                        pltpu.VMEM((4, rows, S5_LANES), F32),
                        pltpu.VMEM((4, rows, S5_LANES), F32),
                        pltpu.VMEM((gb, rows, S5_ROW), F32)],
        compiler_params=_params("parallel"),
        name="s5_main",
    )(x3, m_state, m_intra, m_out, a_re, a_im)
    y = y.reshape(m, BRANCH_W)
    tm = min(512, m)
    return pl.pallas_call(
        _s5_glu_kernel,
        out_shape=jax.ShapeDtypeStruct((m, BRANCH_W), BF16),
        grid=(m // tm,),
        in_specs=[pl.BlockSpec((tm, BRANCH_W), lambda i: (i, 0)),
                  pl.BlockSpec((BRANCH_W, BRANCH_W), lambda i: (0, 0)),
                  pl.BlockSpec((1, BRANCH_W), lambda i: (0, 0)),
                  pl.BlockSpec((tm, BRANCH_W), lambda i: (i, COL_AZ // BRANCH_W))],
        out_specs=pl.BlockSpec((tm, BRANCH_W), lambda i: (i, 0)),
        compiler_params=_params("parallel"),
        name="s5_glu",
    )(y, w_glu.astype(BF16), b_glu.astype(F32).reshape(1, BRANCH_W), proj)


def _dft_tables(seq):
    n = 2 * seq
    mm = np.arange(seq, dtype=np.int64)
    k1 = np.arange(seq // DFT_ROWS, dtype=np.int64)[:, None] * DFT_ROWS
    k0 = np.arange(DFT_ROWS, dtype=np.int64)[:, None]
    ang_a = 2.0 * np.pi * ((k1 * mm) % n).astype(np.float64) / n
    ang_b = 2.0 * np.pi * ((k0 * mm) % n).astype(np.float64) / n
    return tuple(jnp.asarray(t, F32) for t in (np.cos(ang_a), np.sin(ang_a), np.cos(ang_b), np.sin(ang_b)))


def _dft_gen_kernel(ac_ref, as_ref, bc_ref, bs_ref, c_ref, s1_ref, s2_ref):
    i = pl.program_id(0)
    a_c = ac_ref[pl.ds(i, 1), :]
    a_s = as_ref[pl.ds(i, 1), :]
    b_c, b_s = bc_ref[...], bs_ref[...]
    cos_t = a_c * b_c - a_s * b_s
    sin_t = a_s * b_c + a_c * b_s
    rows = lax.broadcasted_iota(jnp.int32, cos_t.shape, 0) + i * DFT_ROWS
    cols = lax.broadcasted_iota(jnp.int32, cos_t.shape, 1)
    alt_cols = jnp.where((cols & 1) == 0, 1.0, -1.0).astype(F32)
    alt_rows = jnp.where((rows & 1) == 0, 1.0, -1.0).astype(F32)
    c_ref[...] = cos_t.astype(BF16)
    s1_ref[...] = jnp.where(rows == 0, alt_cols, sin_t).astype(BF16)
    s2_ref[...] = jnp.where(cols == 0, alt_rows, sin_t).astype(BF16)


def _dft_matrices(seq):
    tabs = _dft_tables(seq)
    n_steps = seq // DFT_ROWS
    tab_spec = pl.BlockSpec(tabs[0].shape, lambda i: (0, 0))
    b_spec = pl.BlockSpec((DFT_ROWS, seq), lambda i: (0, 0))
    o_spec = pl.BlockSpec((DFT_ROWS, seq), lambda i: (i, 0))
    return pl.pallas_call(
        _dft_gen_kernel,
        out_shape=(jax.ShapeDtypeStruct((seq, seq), BF16),) * 3,
        grid=(n_steps,),
        in_specs=[tab_spec, tab_spec, b_spec, b_spec],
        out_specs=(o_spec,) * 3,
        compiler_params=_params("parallel"),
        name="dft_gen",
    )(*tabs)


def _hy_filter_kernel(feat_ref, w1_ref, b1_ref, w2_ref, b2_ref, f0_ref, f1_ref,
                      w3p_ref, w3n_ref, b3p_ref, b3n_ref, dl_ref, t_ref,
                      hs_ref, hd_ref, hp_ref, hn_ref, r0_ref, h_ref, split_ref, *, inv_n):
    hi = lax.Precision.HIGHEST

    @pl.when(pl.program_id(0) == 0)
    def _():
        h1 = jnp.sin(f0_ref[...] * (jnp.dot(feat_ref[...], w1_ref[...], precision=hi,
                                            preferred_element_type=F32) + b1_ref[...]))
        h_ref[...] = jnp.sin(f1_ref[...] * (jnp.dot(h1, w2_ref[...], precision=hi,
                                                    preferred_element_type=F32) + b2_ref[...]))

    h = h_ref[...].astype(BF16)
    win = jnp.exp(-t_ref[...] * jnp.abs(dl_ref[...]))
    hpos = (jnp.dot(h, w3p_ref[...].astype(BF16), preferred_element_type=F32) + b3p_ref[...]) * win
    hneg = (jnp.dot(h, w3n_ref[...].astype(BF16), preferred_element_type=F32) + b3n_ref[...]) * win
    rows = lax.broadcasted_iota(jnp.int32, hpos.shape, 0)
    hneg = jnp.where(rows == 0, 0.0, hneg)
    norm = (jnp.sum(jnp.abs(hpos), axis=0, keepdims=True)
            + jnp.sum(jnp.abs(hneg), axis=0, keepdims=True))
    hpos = hpos / norm
    hneg = hneg / norm
    hsum = hpos + hneg
    hdiff = hpos - hneg
    even = (rows & 1) == 0
    alt2 = jnp.where(((rows >> 1) & 1) == 0, 1.0, -1.0).astype(F32)
    col_sum = lambda a: jnp.sum(a, axis=0, keepdims=True)
    a0 = 2.0 * col_sum(jnp.where(even, hsum, 0.0))
    d0 = 2.0 * col_sum(jnp.where(even, 0.0, hsum))
    hr2 = 2.0 * col_sum(jnp.where(even, alt2 * hsum, 0.0))
    hi2 = -2.0 * col_sum(jnp.where(even, 0.0, alt2 * hdiff))
    r0_ref[...] = jnp.concatenate([a0, d0, hr2, hi2, jnp.zeros((4, a0.shape[1]), F32)], axis=0) * inv_n
    def lags(x, par):
        _stage_rows(split_ref, x)
        return _rows_of_parity(split_ref, par).astype(BF16)

    hs_ref[...] = lags(hsum, 0)
    hd_ref[...] = lags(hdiff, 0)
    hp_ref[...] = lags(hpos, 1)
    hn_ref[...] = lags(hneg, 1)


def _hy_filter_taps(seq, w1, b1, w2, b2, freq, w3, b3):
    n_ch = 2 * BRANCH_W
    bands = (HY_EMB - 1) // 2
    t = jnp.linspace(0.0, 1.0, seq, dtype=F32)[:, None]
    w = 2.0 * math.pi * jnp.arange(seq, dtype=F32)[:, None] / seq
    f = jnp.linspace(1e-4, bands - 1, bands, dtype=F32)[None, :]
    feats = jnp.concatenate([t, jnp.cos(f * w), -jnp.sin(f * w),
                             jnp.zeros((seq, HY_PAD - HY_EMB), F32)], axis=-1)
    deltas = jnp.linspace(math.log(HY_DECAY_TARGET) / HY_SLOW_DECAY,
                          math.log(HY_DECAY_TARGET) / HY_FAST_DECAY, n_ch, dtype=F32)[None, :]

    def pad2(a, r, c):
        a = a.astype(F32)
        return jnp.pad(a, ((0, r - a.shape[0]), (0, c - a.shape[1])))

    w1p = pad2(w1, HY_PAD, HY_PAD)
    w2p = pad2(w2, HY_PAD, HY_PAD)
    b1p = pad2(b1[None], 1, HY_PAD)
    b2p = pad2(b2[None], 1, HY_PAD)
    f0p = pad2(freq[0][None], 1, HY_PAD)
    f1p = pad2(freq[1][None], 1, HY_PAD)
    w3p = pad2(w3, HY_PAD, 2 * n_ch)
    b3r = b3.astype(F32)[None]
    tn = 256
    nt = n_ch // tn
    full = lambda shape: pl.BlockSpec(shape, lambda j: (0, 0))
    tap_spec = pl.BlockSpec((seq // 2, tn), lambda j: (0, j))
    return pl.pallas_call(
        functools.partial(_hy_filter_kernel, inv_n=1.0 / (2 * seq)),
        out_shape=(jax.ShapeDtypeStruct((seq // 2, n_ch), BF16),) * 4
                  + (jax.ShapeDtypeStruct((8, n_ch), F32),),
        grid=(nt,),
        in_specs=[full((seq, HY_PAD)), full((HY_PAD, HY_PAD)), full((1, HY_PAD)),
                  full((HY_PAD, HY_PAD)), full((1, HY_PAD)), full((1, HY_PAD)), full((1, HY_PAD)),
                  pl.BlockSpec((HY_PAD, tn), lambda j: (0, j)),
                  pl.BlockSpec((HY_PAD, tn), lambda j: (0, j + nt)),
                  pl.BlockSpec((1, tn), lambda j: (0, j)),
                  pl.BlockSpec((1, tn), lambda j: (0, j + nt)),
                  pl.BlockSpec((1, tn), lambda j: (0, j)),
                  full((seq, 1))],
        out_specs=(tap_spec,) * 4 + (pl.BlockSpec((8, tn), lambda j: (0, j)),),
        scratch_shapes=[pltpu.VMEM((seq, HY_PAD), F32), pltpu.VMEM((tn // 128, seq, 128), F32)],
        compiler_params=_params("arbitrary"),
        name="hy_filter",
    )(feats, w1p, b1p, w2p, b2p, f0p, f1p, w3p, w3p, b3r, b3r, deltas, t)


def _hy_spectrum_kernel(c_ref, s_ref, hse_ref, hde_ref, hpo_ref, hno_ref,
                        ac_ref, as_ref, bc_ref, bs_ref, gc_ref, gs_ref, *, n_half):
    i = pl.program_id(1)
    dot = lambda w, h: jnp.dot(w[...], h[...], preferred_element_type=F32)
    hec, hes = dot(c_ref, hse_ref), dot(s_ref, hde_ref)
    upc, ups = dot(c_ref, hpo_ref), dot(s_ref, hpo_ref)
    umc, ums = dot(c_ref, hno_ref), dot(s_ref, hno_ref)
    tm = hec.shape[0]
    k = (lax.broadcasted_iota(jnp.int32, (tm, 128), 0) + i * tm).astype(F32)
    psi = k * (math.pi / n_half)
    reps = hec.shape[1] // 128
    cp = jnp.concatenate([jnp.cos(psi)] * reps, axis=1)
    sp = jnp.concatenate([jnp.sin(psi)] * reps, axis=1)
    w = 1.0 / n_half
    ac_ref[...] = w * hec
    as_ref[...] = w * hes
    bc_ref[...] = w * (cp * upc - sp * ups + umc)
    bs_ref[...] = w * (cp * ups + sp * upc - ums)
    gc_ref[...] = w * (upc + cp * umc - sp * ums)
    gs_ref[...] = w * (ups - cp * ums - sp * umc)


def _hy_spectrum(cm, s1, taps):
    half, n_ch = taps[0].shape
    tm = min(512, half)
    tn = 512
    w_spec = pl.BlockSpec((tm, half), lambda j, i: (i, 0))
    tap_spec = pl.BlockSpec((half, tn), lambda j, i: (0, j))
    o_spec = pl.BlockSpec((tm, tn), lambda j, i: (i, j))
    return pl.pallas_call(
        functools.partial(_hy_spectrum_kernel, n_half=half),
        out_shape=(jax.ShapeDtypeStruct((half, n_ch), F32),) * 6,
        grid=(n_ch // tn, half // tm),
        in_specs=[w_spec, w_spec] + [tap_spec] * 4,
        out_specs=(o_spec,) * 6,
        compiler_params=_params("parallel", "arbitrary"),
        name="hy_spectrum",
    )(cm, s1, *taps)


def _stage_rows(scr_ref, x):
    for c in range(x.shape[1] // 128):
        scr_ref[c] = x[:, c * 128:(c + 1) * 128]


def _rows_of_parity(scr_ref, par):
    n = scr_ref.shape[1] // 2
    return jnp.concatenate([scr_ref[c, pl.ds(par, n, stride=2), :] for c in range(scr_ref.shape[0])], axis=1)


def _hy_conv3_kernel(u_ref, w_ref, b_ref, o_ref, vb_ref, s_ref):
    u = u_ref[...]
    n = u.shape[0]
    rows = lax.broadcasted_iota(jnp.int32, u.shape, 0)
    prev = jnp.where(rows == 0, 0.0, pltpu.roll(u, 1, 0))
    nxt = jnp.where(rows == n - 1, 0.0, pltpu.roll(u, n - 1, 0))
    w = w_ref[...]
    _stage_rows(s_ref, prev * w[0:1] + u * w[1:2] + nxt * w[2:3] + b_ref[...])
    for par in range(2):
        part = _rows_of_parity(s_ref, par)
        o_ref[par] = part
        vb_ref[par] = part.astype(BF16)


def _hy_conv3(proj, bsz, seq, conv_w, conv_b):
    half = seq // 2
    tn = 256
    nb = 3 * BRANCH_W // tn
    o_spec = pl.BlockSpec((2, half, tn), lambda b, j: (0, b, j))
    return pl.pallas_call(
        _hy_conv3_kernel,
        out_shape=(jax.ShapeDtypeStruct((2, bsz * half, 3 * BRANCH_W), F32),
                   jax.ShapeDtypeStruct((2, bsz * half, 3 * BRANCH_W), BF16)),
        grid=(bsz, nb),
        in_specs=[pl.BlockSpec((seq, tn), lambda b, j: (b, j + COL_BU // tn)),
                  pl.BlockSpec((3, tn), lambda b, j: (0, j)),
                  pl.BlockSpec((1, tn), lambda b, j: (0, j))],
        out_specs=(o_spec, o_spec),
        scratch_shapes=[pltpu.VMEM((tn // 128, seq, 128), F32)],
        compiler_params=_params("parallel", "parallel"),
        name="hy_conv3",
    )(proj, conv_w.astype(F32), conv_b.astype(F32).reshape(1, -1))


def _hy_fwd_kernel(c_ref, s_ref, ze_ref, zo_ref, ac_ref, as_ref, bc_ref, bs_ref, gc_ref, gs_ref, r0_ref,
                   pc_ref, ps_ref, qc_ref, qs_ref):
    dot = lambda w, z: jnp.dot(w[...], z[...], preferred_element_type=F32)
    ec, es = dot(c_ref, ze_ref), dot(s_ref, ze_ref)
    oc, os_ = dot(c_ref, zo_ref), dot(s_ref, zo_ref)
    a_c, a_s = ac_ref[...], as_ref[...]
    b_c, b_s = bc_ref[...], bs_ref[...]
    g_c, g_s = gc_ref[...], gs_ref[...]
    outs = (ec * a_c - es * a_s + oc * b_c - os_ * b_s,
            ec * a_s + es * a_c + oc * b_s + os_ * b_c,
            ec * g_c - es * g_s + oc * a_c - os_ * a_s,
            ec * g_s + es * g_c + oc * a_s + os_ * a_c)
    r0 = r0_ref[...]
    a0, d0, hr2, hi2 = r0[0:1], r0[1:2], r0[2:3], r0[3:4]
    e0, eh, o0, oh = ec[0:1], es[0:1], oc[0:1], os_[0:1]
    first = (e0 * a0 + o0 * d0, eh * hr2 + oh * hi2, e0 * d0 + o0 * a0, oh * hr2 - eh * hi2)
    top = 16
    is_row0 = (lax.broadcasted_iota(jnp.int32, (top, ec.shape[1]), 0) == 0) & (pl.program_id(2) == 0)
    for ref, val, row0 in zip((pc_ref, ps_ref, qc_ref, qs_ref), outs, first):
        ref[...] = val.astype(BF16)
        ref[0:top, :] = jnp.where(is_row0, row0, val[0:top]).astype(BF16)


HY_TN = 512


def _hy_fwd(cm, s1, z_even, z_odd, tables, r0, h_col, bsz, half):
    tm = min(512, half)
    tn = HY_TN
    mt = half // tm
    w_spec = pl.BlockSpec((tm, half), lambda b, j, i: (i, 0))
    h_spec = pl.BlockSpec((tm, tn), lambda b, j, i: (i, j + h_col // tn))
    o_spec = pl.BlockSpec((tm, tn), lambda b, j, i: (b * mt + i, j))
    return pl.pallas_call(
        _hy_fwd_kernel,
        out_shape=(jax.ShapeDtypeStruct((bsz * half, BRANCH_W), BF16),) * 4,
        grid=(bsz, BRANCH_W // tn, mt),
        in_specs=[w_spec, w_spec, z_even[1], z_odd[1]] + [h_spec] * 6
                 + [pl.BlockSpec((8, tn), lambda b, j, i: (0, j + h_col // tn))],
        out_specs=(o_spec,) * 4,
        compiler_params=_params("parallel", "parallel", "arbitrary"),
        name="hy_fwd",
    )(cm, s1, z_even[0], z_odd[0], *tables, r0)


def _hy_inv_convs(c_ref, s_ref, pc_ref, ps_ref, qc_ref, qs_ref):
    dot = lambda w, y: jnp.dot(w[...], y[...], preferred_element_type=F32)
    return dot(c_ref, pc_ref) + dot(s_ref, ps_ref), dot(c_ref, qc_ref) + dot(s_ref, qs_ref)


def _hy_inv_mid_kernel(c_ref, s_ref, pc_ref, ps_ref, qc_ref, qs_ref, ge_ref, go_ref, ze_ref, zo_ref,
                       bias_ref, o_ref, ob_ref):
    convs = _hy_inv_convs(c_ref, s_ref, pc_ref, ps_ref, qc_ref, qs_ref)
    for par, (conv, g_ref, z_ref) in enumerate(zip(convs, (ge_ref, go_ref), (ze_ref, zo_ref))):
        out = g_ref[...] * (conv + bias_ref[...] * z_ref[...])
        o_ref[par] = out
        ob_ref[par] = out.astype(BF16)


def _hy_inv_last_kernel(c_ref, s_ref, pc_ref, ps_ref, qc_ref, qs_ref, ge_ref, go_ref, ze_ref, zo_ref,
                        bias_ref, sz_ref, o_ref, mix_ref):
    convs = _hy_inv_convs(c_ref, s_ref, pc_ref, ps_ref, qc_ref, qs_ref)
    tm = ge_ref.shape[0]
    _stage_rows(mix_ref, sz_ref[...])
    outs = [g_ref[...] * (conv + bias_ref[...] * z_ref[...]) * _rows_of_parity(mix_ref, par)
            for par, (conv, g_ref, z_ref) in enumerate(zip(convs, (ge_ref, go_ref), (ze_ref, zo_ref)))]
    for par, out in enumerate(outs):
        for c in range(mix_ref.shape[0]):
            mix_ref[c, pl.ds(par, tm, stride=2), :] = out[:, c * 128:(c + 1) * 128]
    for c in range(mix_ref.shape[0]):
        o_ref[:, c * 128:(c + 1) * 128] = mix_ref[c]


def _hy_inv(cm, s2, spectra, gates, zprev, bias_row, bsz, half, silu=None):
    tm = min(512, half)
    tn = HY_TN
    mt = half // tm
    w_spec = pl.BlockSpec((tm, half), lambda b, j, i: (i, 0))
    y_spec = pl.BlockSpec((half, tn), lambda b, j, i: (b, j))
    pairs = list(gates) + list(zprev)
    in_specs = [w_spec, w_spec] + [y_spec] * 4 + [s for _, s in pairs] + [pl.BlockSpec((1, tn), lambda b, j, i: (0, j))]
    args = [cm, s2, *spectra] + [a for a, _ in pairs] + [bias_row]
    if silu is None:
        o_spec = pl.BlockSpec((2, tm, tn), lambda b, j, i: (0, b * mt + i, j))
        shape = lambda dt: jax.ShapeDtypeStruct((2, bsz * half, BRANCH_W), dt)
        body, out_shape, out_specs = _hy_inv_mid_kernel, (shape(F32), shape(BF16)), (o_spec, o_spec)
        scratch = []
    else:
        silu_arr, silu_col = silu
        body = _hy_inv_last_kernel
        out_shape = jax.ShapeDtypeStruct((2 * bsz * half, BRANCH_W), F32)
        out_specs = pl.BlockSpec((2 * tm, tn), lambda b, j, i: (b * mt + i, j))
        in_specs.append(pl.BlockSpec((2 * tm, tn), lambda b, j, i: (b * mt + i, j + silu_col // tn)))
        args.append(silu_arr)
        scratch = [pltpu.VMEM((tn // 128, 2 * tm, 128), F32)]
    return pl.pallas_call(
        body, out_shape=out_shape, grid=(bsz, BRANCH_W // tn, mt), in_specs=in_specs, out_specs=out_specs,
        scratch_shapes=scratch, compiler_params=_params("parallel", "parallel", "arbitrary"),
        name="hy_inv",
    )(*args)


def _hyena_branch(proj, bsz, seq, dft, conv_w, conv_b, w1, b1, w2, b2, freq, w3, b3, bias):
    cm, s1, s2 = dft
    half = seq // 2
    tm = min(512, half)
    tn = HY_TN
    mt = half // tm
    full = lambda par, col=0: pl.BlockSpec((None, half, tn), lambda b, j, i: (par, b, j + col // tn))
    row = lambda par, col=0: pl.BlockSpec((None, tm, tn), lambda b, j, i: (par, b * mt + i, j + col // tn))
    both = lambda arr, spec, col=0: ((arr, spec(0, col)), (arr, spec(1, col)))
    *taps, r0 = _hy_filter_taps(seq, w1, b1, w2, b2, freq, w3, b3)
    tables = _hy_spectrum(cm, s1, taps)
    uc, ucb = _hy_conv3(proj, bsz, seq, conv_w, conv_b)
    bias = bias.astype(F32)
    spectra = _hy_fwd(cm, s1, *both(ucb, full), tables, r0, 0, bsz, half)
    z1, z1b = _hy_inv(cm, s2, spectra, gates=both(uc, row, BRANCH_W), zprev=both(uc, row),
                      bias_row=bias[0:1], bsz=bsz, half=half)
    spectra = _hy_fwd(cm, s1, *both(z1b, full), tables, r0, BRANCH_W, bsz, half)
    return _hy_inv(cm, s2, spectra, gates=both(uc, row, 2 * BRANCH_W), zprev=both(z1, row),
                   bias_row=bias[1:2], bsz=bsz, half=half, silu=(proj, COL_BZ))


def _rope_table_kernel(pos_ref, inv_ref, cos_ref, sin_ref):
    ang = pos_ref[...] * inv_ref[...]
    lane = lax.broadcasted_iota(jnp.int32, ang.shape, 1)
    live = lane < MLA_ROPE
    cos_ref[...] = jnp.where(live, jnp.cos(ang), 0.0)
    sin_ref[...] = jnp.where(live, jnp.where(lane < MLA_ROPE // 2, -1.0, 1.0) * jnp.sin(ang), 0.0)


def _rope_tables(positions):
    m = positions.size
    half = MLA_ROPE // 2
    inv = ROPE_BASE ** (-jnp.arange(half, dtype=F32) / half)
    inv = jnp.concatenate([inv, inv, jnp.zeros((128 - MLA_ROPE,), F32)])[None]
    pos = positions.astype(F32).reshape(m, 1)
    tm = min(1024, m)
    spec = pl.BlockSpec((tm, 128), lambda i: (i, 0))
    return pl.pallas_call(
        _rope_table_kernel,
        out_shape=(jax.ShapeDtypeStruct((m, 128), F32),) * 2,
        grid=(m // tm,),
        in_specs=[pl.BlockSpec((tm, 1), lambda i: (i, 0)), pl.BlockSpec((1, 128), lambda i: (0, 0))],
        out_specs=(spec, spec),
        compiler_params=_params("parallel"),
        name="rope_table",
    )(pos, inv)


def _rope128(x, cos_t, sin_t):
    lane = lax.broadcasted_iota(jnp.int32, x.shape, 1)
    half = MLA_ROPE // 2
    partner = jnp.where(lane < half, pltpu.roll(x, 128 - half, 1), pltpu.roll(x, half, 1))
    return x * cos_t + partner * sin_t


def _rms(x, g):
    ms = jnp.mean(jnp.square(x), axis=-1, keepdims=True)
    return x * lax.rsqrt(ms + RMS_EPS) * g


def _mla_q_kernel(cq_ref, g_ref, w_ref, cos_ref, sin_ref, q_ref, *, scale):
    xn = _rms(cq_ref[...], g_ref[...]).astype(BF16)
    q = jnp.dot(xn, w_ref[...], preferred_element_type=F32) * scale
    cos_t, sin_t = cos_ref[...], sin_ref[...]
    for h in range(MLA_HEADS):
        base = h * MLA_QK_PAD
        q_ref[:, base:base + MLA_NOPE] = q[:, base:base + MLA_NOPE].astype(BF16)
        q_ref[:, base + MLA_NOPE:base + MLA_QK_PAD] = _rope128(
            q[:, base + MLA_NOPE:base + MLA_QK_PAD], cos_t, sin_t).astype(BF16)


def _mla_kv_kernel(ckv_ref, g_ref, wk_ref, wv_ref, kr_ref, cos_ref, sin_ref, k_ref, v_ref):
    xn = _rms(ckv_ref[...], g_ref[...]).astype(BF16)
    kn = jnp.dot(xn, wk_ref[...], preferred_element_type=F32)
    v_ref[...] = jnp.dot(xn, wv_ref[...], preferred_element_type=F32).astype(BF16)
    kr = _rope128(kr_ref[:, 0:128], cos_ref[...], sin_ref[...]).astype(BF16)
    for h in range(MLA_HEADS):
        base = h * MLA_QK_PAD
        k_ref[:, base:base + MLA_NOPE] = kn[:, h * MLA_NOPE:(h + 1) * MLA_NOPE].astype(BF16)
        k_ref[:, base + MLA_NOPE:base + MLA_QK_PAD] = kr


def _mla_attn_kernel(q_ref, k_ref, v_ref, z_ref, o_ref):
    k, v = k_ref[...], v_ref[...]
    half = q_ref.shape[0] // 2
    for r in range(2):
        rows = pl.ds(r * half, half)
        s = lax.dot_general(q_ref[rows, :], k, (((1,), (1,)), ((), ())),
                            preferred_element_type=F32)
        p = jnp.exp2(s - jnp.max(s, axis=-1, keepdims=True))
        l = jnp.sum(p, axis=-1, keepdims=True)
        o = jnp.dot(p.astype(BF16), v, preferred_element_type=F32)
        o_ref[rows, :] = (o / l * z_ref[rows, :]).astype(BF16)


def _mla_branch(proj, proj_b, bsz, seq, rope, q_norm_g, w_uq, kv_norm_g, w_ukv):
    m = bsz * seq
    cos_t, sin_t = rope
    dqk = MLA_NOPE + MLA_ROPE
    hq = MLA_HEADS * MLA_QK_PAD
    w_q = w_uq.reshape(MLA_LORA, MLA_HEADS, dqk)
    w_q = jnp.pad(w_q, ((0, 0), (0, 0), (0, MLA_QK_PAD - dqk))).reshape(MLA_LORA, hq).astype(BF16)
    w_kv = w_ukv.reshape(MLA_LORA, MLA_HEADS, MLA_NOPE + MLA_V)
    w_k = w_kv[:, :, :MLA_NOPE].reshape(MLA_LORA, MLA_HEADS * MLA_NOPE).astype(BF16)
    w_v = w_kv[:, :, MLA_NOPE:].reshape(MLA_LORA, MLA_HEADS * MLA_V).astype(BF16)
    tm = min(512, m)
    row = lambda shape, col=0: pl.BlockSpec(shape, lambda i: (i, col))
    full = lambda shape: pl.BlockSpec(shape, lambda i: (0, 0))
    qp = pl.pallas_call(
        functools.partial(_mla_q_kernel, scale=dqk ** -0.5 * math.log2(math.e)),
        out_shape=jax.ShapeDtypeStruct((m, hq), BF16),
        grid=(m // tm,),
        in_specs=[row((tm, MLA_LORA), COL_CQ // MLA_LORA), full((1, MLA_LORA)), full((MLA_LORA, hq)),
                  row((tm, 128)), row((tm, 128))],
        out_specs=row((tm, hq)),
        compiler_params=_params("parallel"),
        name="mla_q",
    )(proj, q_norm_g.astype(F32).reshape(1, -1), w_q, cos_t, sin_t)
    kp, vp = pl.pallas_call(
        _mla_kv_kernel,
        out_shape=(jax.ShapeDtypeStruct((m, hq), BF16),
                   jax.ShapeDtypeStruct((m, MLA_HEADS * MLA_V), BF16)),
        grid=(m // tm,),
        in_specs=[row((tm, MLA_LORA), COL_CKV // MLA_LORA), full((1, MLA_LORA)),
                  full((MLA_LORA, MLA_HEADS * MLA_NOPE)), full((MLA_LORA, MLA_HEADS * MLA_V)),
                  row((tm, 512), COL_CKR // 512), row((tm, 128)), row((tm, 128))],
        out_specs=(row((tm, hq)), row((tm, MLA_HEADS * MLA_V))),
        compiler_params=_params("parallel"),
        name="mla_kv",
    )(proj, kv_norm_g.astype(F32).reshape(1, -1), w_k, w_v, proj_b, cos_t, sin_t)
    tq = min(512, seq)
    qt = seq // tq
    return pl.pallas_call(
        _mla_attn_kernel,
        out_shape=jax.ShapeDtypeStruct((m, MLA_HEADS * MLA_V), BF16),
        grid=(bsz, MLA_HEADS, qt),
        in_specs=[pl.BlockSpec((tq, MLA_QK_PAD), lambda b, h, i: (b * qt + i, h)),
                  pl.BlockSpec((seq, MLA_QK_PAD), lambda b, h, i: (b, h)),
                  pl.BlockSpec((seq, MLA_V), lambda b, h, i: (b, h)),
                  pl.BlockSpec((tq, MLA_V), lambda b, h, i: (b * qt + i, h + COL_CZ // MLA_V))],
        out_specs=pl.BlockSpec((tq, MLA_V), lambda b, h, i: (b * qt + i, h)),
        compiler_params=_params("parallel", "parallel", "arbitrary"),
        name="mla_attn",
    )(qp, kp, vp, proj_b)


def _lift_kernel(ya_ref, yb_ref, yc_ref, w_ref, ga_ref, gb_ref, gc_ref, o_ref, wb_ref):
    @pl.when(pl.program_id(1) == 0)
    def _():
        wb_ref[...] = w_ref[...].astype(BF16)

    acc = ga_ref[...] * jnp.dot(ya_ref[...], wb_ref[0], preferred_element_type=F32)
    acc += gb_ref[...] * jnp.dot(yb_ref[...].astype(BF16), wb_ref[1], preferred_element_type=F32)
    acc += gc_ref[...] * jnp.dot(yc_ref[...], wb_ref[2], preferred_element_type=F32)
    o_ref[...] = acc.astype(BF16)


def _lift(ya, yb, yc, w_lift, layer, proj_b):
    m = ya.shape[0]
    tm = min(512, m)
    tn = 512
    y_spec = pl.BlockSpec((tm, BRANCH_W), lambda j, i: (i, 0))
    gate = lambda n: pl.BlockSpec((tm, tn), lambda j, i: (i, j + (COL_GATE + n * D_MODEL) // tn))
    return pl.pallas_call(
        _lift_kernel,
        out_shape=jax.ShapeDtypeStruct((m, D_MODEL), BF16),
        grid=(D_MODEL // tn, m // tm),
        in_specs=[y_spec, y_spec, y_spec,
                  pl.BlockSpec((None, N_BRANCH, BRANCH_W, tn), lambda j, i: (layer, 0, 0, j)),
                  gate(0), gate(1), gate(2)],
        out_specs=pl.BlockSpec((tm, tn), lambda j, i: (i, j)),
        scratch_shapes=[pltpu.VMEM((N_BRANCH, BRANCH_W, tn), BF16)],
        compiler_params=_params("parallel", "arbitrary"),
        name="lift",
    )(ya, yb, yc, w_lift, proj_b, proj_b, proj_b)


def _out_kernel(mix_ref, wo_ref, p_ref, wp_ref, sp_ref, x_ref, g_ref, b_ref, o_ref):
    mixed = jnp.dot(mix_ref[...], wo_ref[...], preferred_element_type=F32)
    ple = jnp.dot(p_ref[...].astype(BF16), wp_ref[...], preferred_element_type=F32) * sp_ref[...]
    r = DEEPNORM_ALPHA * x_ref[...] + mixed + ple
    mu = jnp.mean(r, axis=-1, keepdims=True)
    var = jnp.mean(jnp.square(r - mu), axis=-1, keepdims=True)
    o_ref[...] = (r - mu) * lax.rsqrt(var + LN_EPS) * g_ref[...] + b_ref[...]


def _out_norm(mix, w_out, p, w_ple, proj, x, ln_g, ln_b):
    m = mix.shape[0]
    tm = min(256, m)
    row = lambda w, col=0: pl.BlockSpec((tm, w), lambda i: (i, col))
    full = lambda shape: pl.BlockSpec(shape, lambda i: (0, 0))
    return pl.pallas_call(
        _out_kernel,
        out_shape=jax.ShapeDtypeStruct((m, D_MODEL), F32),
        grid=(m // tm,),
        in_specs=[row(D_MODEL), full((D_MODEL, D_MODEL)), row(PLE_DIM), full((PLE_DIM, D_MODEL)),
                  row(D_MODEL, COL_PLE // D_MODEL), row(D_MODEL), full((1, D_MODEL)), full((1, D_MODEL))],
        out_specs=row(D_MODEL),
        compiler_params=_params("parallel"),
        name="out_norm",
    )(mix, w_out.astype(BF16), p, w_ple.astype(BF16), proj, x,
      ln_g.astype(F32).reshape(1, -1), ln_b.astype(F32).reshape(1, -1))


def kernel(x, p, positions, w_in, s5_lambda_re, s5_lambda_im, s5_log_dt, s5_b_re, s5_b_im, s5_c_re, s5_c_im, s5_d, s5_w_glu, s5_b_glu, hy_conv_w, hy_conv_b, hy_w1, hy_b1, hy_w2, hy_b2, hy_freq, hy_w3, hy_b3, hy_bias, mla_q_norm, mla_w_uq, mla_kv_norm, mla_w_ukv, w_lift, w_out, w_ple, ln_g, ln_b):
    bsz, seq, _ = x.shape
    m = bsz * seq
    depth = w_in.shape[0]
    dft = _dft_matrices(seq // 2)
    rope = _rope_tables(positions)
    s5_tiled = _s5_tiled_params(s5_lambda_re, s5_lambda_im, s5_log_dt, s5_b_re, s5_b_im,
                                s5_c_re, s5_c_im, s5_d)
    xf = x.reshape(m, D_MODEL).astype(F32)
    for i in range(depth):
        proj, proj_b = _proj_both(xf, w_in, i)
        y_a = _s5_branch(proj, bsz, seq, [a[i] for a in s5_tiled], s5_w_glu[i], s5_b_glu[i])
        y_b = _hyena_branch(proj, bsz, seq, dft, hy_conv_w[i], hy_conv_b[i], hy_w1[i], hy_b1[i],
                            hy_w2[i], hy_b2[i], hy_freq[i], hy_w3[i], hy_b3[i], hy_bias[i])
        y_c = _mla_branch(proj, proj_b, bsz, seq, rope, mla_q_norm[i], mla_w_uq[i], mla_kv_norm[i], mla_w_ukv[i])
        mix = _lift(y_a, y_b, y_c, w_lift, i, proj_b)
        xf = _out_norm(mix, w_out[i], p[i].reshape(m, PLE_DIM), w_ple[i], proj_b, xf, ln_g[i], ln_b[i])
    return xf.reshape(bsz, seq, D_MODEL).astype(x.dtype)
```

```python
import functools
import math

import numpy as np
import jax
import jax.numpy as jnp
from jax import lax
from jax.experimental import pallas as pl
from jax.experimental.pallas import tpu as pltpu

F32 = jnp.float32
BF16 = jnp.bfloat16

D_MODEL = 2048
PLE_DIM = 256
N_BRANCH = 3
BRANCH_W = 1024

S5_GROUP = 16
S5_GROUPS = BRANCH_W // S5_GROUP
S5_STATE = 64
S5_CHUNK = 16
S5_ROW = S5_CHUNK * S5_GROUP

HY_EMB = 33
HY_FF = 64
HY_PAD = 128
HY_DECAY_TARGET = 0.01
HY_FAST_DECAY = 0.3
HY_SLOW_DECAY = 1.5
DFT_ROWS = 64

MLA_HEADS = 8
MLA_NOPE = 128
MLA_ROPE = 64
MLA_V = 128
MLA_LORA = 512
MLA_QK_PAD = 256
ROPE_BASE = 10000.0

LN_EPS = 1e-5
RMS_EPS = 1e-6
DEPTH = 2
DEEPNORM_ALPHA = (2 * DEPTH) ** 0.25

COL_AX = 0
COL_AZ = 1024
COL_BU = 2048
COL_BZ = 5120
COL_CQ = 6144
COL_CKV = 6656
PROJ_A_N = 7168
COL_PLE = 0
COL_GATE = 2048
COL_CZ = 8192
COL_CKR = 9216
PROJ_B_N = 9728
PROJ_TN = 512
W_IN_CKR = 7168
W_IN_CZ = 7232
W_IN_GATE = 8256
W_IN_PLE = 14400

VMEM_LIMIT = 56 * 1024 * 1024


def _params(*sem):
    return pltpu.CompilerParams(dimension_semantics=sem, vmem_limit_bytes=VMEM_LIMIT)


def _sigmoid(x):
    return 0.5 * jnp.tanh(0.5 * x) + 0.5


def _in_tiles(j, ranges):
    hit = None
    for lo, hi in ranges:
        cur = (j >= lo // PROJ_TN) & (j < hi // PROJ_TN)
        hit = cur if hit is None else (hit | cur)
    return hit


def _proj_kernel(x_ref, w_ref, o_ref, xb_ref, *, silu_cols, sigm_cols):
    j = pl.program_id(1)

    @pl.when(j == 0)
    def _():
        xb_ref[...] = x_ref[...].astype(BF16)

    acc = lax.dot_general(xb_ref[...], w_ref[...].astype(BF16), (((1,), (1,)), ((), ())),
                          preferred_element_type=F32)
    is_silu = _in_tiles(j, silu_cols)
    plain = jnp.logical_not(is_silu)

    @pl.when(is_silu)
    def _():
        o_ref[...] = acc * _sigmoid(acc)

    if sigm_cols:
        is_sigm = _in_tiles(j, sigm_cols)
        plain = jnp.logical_not(is_silu | is_sigm)

        @pl.when(is_sigm)
        def _():
            o_ref[...] = _sigmoid(acc)

    @pl.when(plain)
    def _():
        o_ref[...] = acc


def _proj(x, w, layer, n_out, silu_cols, sigm_cols, name):
    m, k = x.shape
    tm = min(1024, m)
    return pl.pallas_call(
        functools.partial(_proj_kernel, silu_cols=silu_cols, sigm_cols=sigm_cols),
        out_shape=jax.ShapeDtypeStruct((m, n_out), F32),
        grid=(m // tm, n_out // PROJ_TN),
        in_specs=[pl.BlockSpec((tm, k), lambda i, j: (i, 0)),
                  pl.BlockSpec((None, PROJ_TN, k), lambda i, j: (layer, j, 0))],
        out_specs=pl.BlockSpec((tm, PROJ_TN), lambda i, j: (i, j)),
        scratch_shapes=[pltpu.VMEM((tm, k), BF16)],
        compiler_params=_params("parallel", "arbitrary"),
        name=name,
    )(x, w)


def _repack_src_tile(j):
    t = lambda col: col // PROJ_TN
    return jnp.where(j < t(COL_GATE), j + t(W_IN_PLE),
                     jnp.where(j < t(COL_CZ), j - t(COL_GATE) + t(W_IN_GATE),
                               jnp.where(j < t(COL_CKR), j - t(COL_CZ) + t(W_IN_CZ), t(W_IN_CKR))))


def _repack_kernel(a_ref, b_ref, o_ref):
    is_last = pl.program_id(0) == pl.num_programs(0) - 1
    a = a_ref[...]
    shifted = jnp.concatenate([a[MLA_ROPE:], b_ref[...]], axis=0)
    plain = jnp.concatenate([a[:MLA_ROPE], jnp.zeros_like(a[MLA_ROPE:])], axis=0)
    o_ref[...] = jnp.where(is_last, plain, shifted).astype(BF16)


def _repack_b(w_t, layer):
    k = w_t.shape[2]
    sub = PROJ_TN // MLA_ROPE
    return pl.pallas_call(
        _repack_kernel,
        out_shape=jax.ShapeDtypeStruct((1, PROJ_B_N, k), BF16),
        grid=(PROJ_B_N // PROJ_TN,),
        in_specs=[pl.BlockSpec((None, PROJ_TN, k), lambda j: (layer, _repack_src_tile(j), 0)),
                  pl.BlockSpec((None, MLA_ROPE, k), lambda j: (layer, (_repack_src_tile(j) + 1) * sub, 0))],
        out_specs=pl.BlockSpec((None, PROJ_TN, k), lambda j: (0, j, 0)),
        compiler_params=_params("parallel"),
        name="repack_b",
    )(w_t, w_t)


def _proj_both(x, w_t, layer):
    proj_a = _proj(x, w_t, layer, PROJ_A_N, ((COL_AZ, COL_BU), (COL_BZ, COL_CQ)), (), "proj_a")
    proj_b = _proj(x, _repack_b(w_t, layer), 0, PROJ_B_N, ((COL_CZ, COL_CKR),), ((COL_PLE, COL_CZ),),
                   "proj_b")
    return proj_a, proj_b


S5_LANES = 8 * S5_STATE
S5_SLAB = 8
S5_RELAYOUT_ROWS = 32


def _s5_tile_lanes(a):
    return jnp.concatenate([a[:, 0]] * 4 + [a[:, 1]] * 4, axis=-1)


def _s5_tiled_params(lam_re, lam_im, log_dt, b_re, b_im, c_re, c_im, d):
    f = lambda a: a.astype(F32)
    depth = lam_re.shape[0]
    ldt = jnp.broadcast_to(f(log_dt)[..., None, None], lam_re.shape[:3] + (1, S5_STATE))
    return (_s5_tile_lanes(f(lam_re)[:, :, :, None, :]), _s5_tile_lanes(f(lam_im)[:, :, :, None, :]),
            _s5_tile_lanes(ldt),
            _s5_tile_lanes(jnp.swapaxes(f(b_re), -1, -2)), _s5_tile_lanes(jnp.swapaxes(f(b_im), -1, -2)),
            _s5_tile_lanes(f(c_re)), _s5_tile_lanes(f(c_im)),
            f(d).reshape(depth, S5_GROUPS, S5_GROUP, 1))


def _s5_mats_kernel(lr_ref, li_ref, ldt_ref, br_ref, bi_ref, cr_ref, ci_ref, d_ref,
                    ms_ref, mi_ref, mo_ref, ar_ref, ai_ref):
    t_n, h_n = S5_CHUNK, S5_GROUP
    hi = lax.Precision.HIGHEST
    nt = (((1,), (1,)), ((), ()))
    blk = lax.broadcasted_iota(jnp.int32, (1, S5_LANES), 1) // S5_STATE
    is_im = (blk // 2) % 2 == 1
    is_fwd = blk < 4
    steps = lax.broadcasted_iota(jnp.int32, (24, S5_LANES), 0).astype(F32)
    lane_k = lax.broadcasted_iota(jnp.int32, (h_n, S5_ROW), 1)
    sub_k = lax.broadcasted_iota(jnp.int32, (h_n, S5_ROW), 0)

    def per_group(gi, carry):
        lr, li = lr_ref[gi], li_ref[gi]
        dt = jnp.exp(ldt_ref[gi])
        zr, zi = lr * dt, li * dt
        mag = jnp.exp(steps * zr)
        tr, ti = mag * jnp.cos(steps * zi), mag * jnp.sin(steps * zi)
        lbr, lbi = tr[1:2], ti[1:2]
        n2 = lr * lr + li * li
        qr = ((lbr - 1.0) * lr + lbi * li) / n2
        qi = (lbi * lr - (lbr - 1.0) * li) / n2
        br, bi = br_ref[gi], bi_ref[gi]
        bbr, bbi = qr * br - qi * bi, qr * bi + qi * br
        y1, y2 = jnp.where(is_im, bbi, bbr), jnp.where(is_im, bbr, bbi)
        cr, ci = cr_ref[gi], ci_ref[gi]

        def pick(tab, t_fwd, t_bwd):
            return jnp.where(is_fwd, tab[t_fwd:t_fwd + 1], tab[t_bwd:t_bwd + 1])

        def c_times(p_r, p_i):
            return jnp.where(is_im, -(cr * p_i + ci * p_r), cr * p_r - ci * p_i)

        q_rows = []
        for t in range(t_n):
            a_r, a_i = pick(tr, t_n - 1 - t, t), pick(ti, t_n - 1 - t, t)
            rows = pl.ds(t * h_n, h_n)
            ms_ref[gi, rows, :] = (a_r * y1 + jnp.where(is_im, a_i, -a_i) * y2).astype(BF16)
            mo_ref[gi, rows, :] = c_times(pick(tr, t + 1, t_n - t), pick(ti, t + 1, t_n - t)).astype(BF16)
            q_rows.append(c_times(pick(tr, t, t_n - 1 - t), pick(ti, t, t_n - 1 - t)))
        q = jnp.concatenate(q_rows, axis=0)
        half = S5_LANES // 2
        kf = 0.5 * lax.dot_general(y1[:, :half], q[:, :half], nt, precision=hi, preferred_element_type=F32)
        kb = 0.5 * lax.dot_general(y1[:, half:], q[:, half:], nt, precision=hi, preferred_element_type=F32)
        kf = kf + jnp.where(lane_k == sub_k, d_ref[gi], 0.0)
        for t in range(t_n):
            fwd = kf if t == 0 else jnp.where(lane_k >= h_n * t, pltpu.roll(kf, h_n * t, 1), 0.0)
            sh = h_n * (t_n - 1 - t)
            bwd = kb if sh == 0 else jnp.where(lane_k < S5_ROW - sh, pltpu.roll(kb, S5_ROW - sh, 1), 0.0)
            mi_ref[gi, pl.ds(t * h_n, h_n), :] = (fwd + bwd).astype(BF16)
        ar_ref[gi] = tr[t_n:t_n + 1]
        ai_ref[gi] = ti[t_n:t_n + 1]
        return carry

    lax.fori_loop(0, lr_ref.shape[0], per_group, 0)


def _s5_mats(tiled):
    g_n = S5_GROUPS
    gb = S5_SLAB
    spec = lambda r, w: pl.BlockSpec((gb, r, w), lambda j: (j, 0, 0))
    in_rows = (1, 1, 1, S5_GROUP, S5_GROUP, S5_GROUP, S5_GROUP)
    return pl.pallas_call(
        _s5_mats_kernel,
        out_shape=(jax.ShapeDtypeStruct((g_n, S5_ROW, S5_LANES), BF16),
                   jax.ShapeDtypeStruct((g_n, S5_ROW, S5_ROW), BF16),
                   jax.ShapeDtypeStruct((g_n, S5_ROW, S5_LANES), BF16),
                   jax.ShapeDtypeStruct((g_n, 1, S5_LANES), F32),
                   jax.ShapeDtypeStruct((g_n, 1, S5_LANES), F32)),
        grid=(g_n // gb,),
        in_specs=[spec(r, S5_LANES) for r in in_rows] + [spec(S5_GROUP, 1)],
        out_specs=(spec(S5_ROW, S5_LANES), spec(S5_ROW, S5_ROW), spec(S5_ROW, S5_LANES),
                   spec(1, S5_LANES), spec(1, S5_LANES)),
        compiler_params=_params("parallel"),
        name="s5_mats",
    )(*tiled)


def _seg_transpose(vs, seg):
    vs = list(vs)
    for s in (4, 2, 1):
        keep = (seg & s) == 0
        for i in range(8):
            if i & s:
                continue
            a, b = vs[i], vs[i + s]
            vs[i] = jnp.where(keep, a, pltpu.roll(b, s * S5_GROUP, 1))
            vs[i + s] = jnp.where(keep, pltpu.roll(a, 128 - s * S5_GROUP, 1), b)
    return vs


def _s5_main_kernel(x_ref, ms_ref, mi_ref, mo_ref, ar_ref, ai_ref, y_ref,
                    u_ref, sl_ref, st_ref, yg_ref, *, bsz):
    rows = x_ref.shape[0]
    n_chunks = rows // bsz
    rc = S5_RELAYOUT_ROWS
    seg = lax.broadcasted_iota(jnp.int32, (rc, 128), 1) // S5_GROUP
    slot = (lax.broadcasted_iota(jnp.int32, (1, S5_LANES), 1) // S5_STATE) % 2
    slot128 = slot[:, :128]

    def relayout_in(r, carry):
        r0 = pl.multiple_of(r * rc, rc)
        for th in range(2):
            src = [x_ref[pl.ds(r0, rc), th * 8 + t8, :] for t8 in range(8)]
            for gi, out in enumerate(_seg_transpose(src, seg)):
                u_ref[gi, pl.ds(r0, rc), th * 128:(th + 1) * 128] = out.astype(BF16)
        return carry

    lax.fori_loop(0, rows // rc, relayout_in, 0)

    a_parts = [[], [], [], []]
    for jp in range(S5_SLAB // 2):
        g0, g1 = 2 * jp, 2 * jp + 1
        r0 = jnp.dot(u_ref[g0], ms_ref[g0], preferred_element_type=F32)
        r1 = jnp.dot(u_ref[g1], ms_ref[g1], preferred_element_type=F32)
        loc = jnp.where(slot == 0, r0, r1)
        for k in range(4):
            sl_ref[k, :, jp * 128:(jp + 1) * 128] = loc[:, k * 128:(k + 1) * 128]
        for k, (ref, off) in enumerate(((ar_ref, 0), (ai_ref, 0), (ar_ref, 256), (ai_ref, 256))):
            a_parts[k].append(jnp.where(slot128 == 0, ref[g0][:, off:off + 128], ref[g1][:, off:off + 128]))
    a_fr, a_fi, a_br, a_bi = [jnp.concatenate(parts, axis=1) for parts in a_parts]

    zero = jnp.zeros((1, a_fr.shape[1]), F32)

    def scan(c, carry):
        cb = n_chunks - 1 - c
        new = []
        for b in range(bsz):
            s_fr, s_fi, s_br, s_bi = carry[b]
            rf, rb = b * n_chunks + c, b * n_chunks + cb
            st_ref[0, pl.ds(rf, 1), :] = s_fr
            st_ref[1, pl.ds(rf, 1), :] = s_fi
            st_ref[2, pl.ds(rb, 1), :] = s_br
            st_ref[3, pl.ds(rb, 1), :] = s_bi
            n_fr = a_fr * s_fr - a_fi * s_fi + sl_ref[0, pl.ds(rf, 1), :]
            n_fi = a_fr * s_fi + a_fi * s_fr + sl_ref[1, pl.ds(rf, 1), :]
            n_br = a_br * s_br - a_bi * s_bi + sl_ref[2, pl.ds(rb, 1), :]
            n_bi = a_br * s_bi + a_bi * s_br + sl_ref[3, pl.ds(rb, 1), :]
            new.append((n_fr, n_fi, n_br, n_bi))
        return tuple(new)

    lax.fori_loop(0, n_chunks, scan, tuple((zero,) * 4 for _ in range(bsz)))

    nt = (((1,), (1,)), ((), ()))
    for jp in range(S5_SLAB // 2):
        st = jnp.concatenate([st_ref[k, :, jp * 128:(jp + 1) * 128] for k in range(4)], axis=1)
        for e in range(2):
            g = 2 * jp + e
            st_g = jnp.where(slot == e, st, 0.0).astype(BF16)
            y = (jnp.dot(u_ref[g], mi_ref[g], preferred_element_type=F32)
                 + lax.dot_general(st_g, mo_ref[g], nt, preferred_element_type=F32))
            yg_ref[g] = jax.nn.gelu(y)

    def relayout_out(r, carry):
        r0 = pl.multiple_of(r * rc, rc)
        for th in range(2):
            src = [yg_ref[gi, pl.ds(r0, rc), th * 128:(th + 1) * 128] for gi in range(S5_SLAB)]
            for t8, out in enumerate(_seg_transpose(src, seg)):
                y_ref[pl.ds(r0, rc), th * 8 + t8, :] = out
        return carry

    lax.fori_loop(0, rows // rc, relayout_out, 0)


def _s5_glu_kernel(g_ref, w_ref, b_ref, z_ref, o_ref):
    g = g_ref[...]
    acc = jnp.dot(g.astype(BF16), w_ref[...], preferred_element_type=F32) + b_ref[...]
    o_ref[...] = (g * _sigmoid(acc) * z_ref[...]).astype(o_ref.dtype)


def _s5_branch(proj, bsz, seq, tiled, w_glu, b_glu):
    m_state, m_intra, m_out, a_re, a_im = _s5_mats(tiled)
    t_n = S5_CHUNK
    rows = bsz * (seq // t_n)
    m = bsz * seq
    gb = S5_SLAB
    x3 = proj.reshape(rows, t_n, proj.shape[1])
    mat = lambda r, w: pl.BlockSpec((gb, r, w), lambda s: (s, 0, 0))
    io_spec = pl.BlockSpec((rows, t_n, 128), lambda s: (0, 0, s + COL_AX // 128))
    y = pl.pallas_call(
        functools.partial(_s5_main_kernel, bsz=bsz),
        out_shape=jax.ShapeDtypeStruct((rows, t_n, BRANCH_W), F32),
        grid=(S5_GROUPS // gb,),
        in_specs=[io_spec, mat(S5_ROW, S5_LANES), mat(S5_ROW, S5_ROW), mat(S5_ROW, S5_LANES),
                  mat(1, S5_LANES), mat(1, S5_LANES)],
        out_specs=pl.BlockSpec((rows, t_n, 128), lambda s: (0, 0, s)),
        scratch_shapes=[pltpu.VMEM((gb, rows, S5_ROW), BF16),
                        pltpu.VMEM((4, rows, S5_LANES), F32),
                        pltpu.VMEM((4, rows, S5_LANES), F32),
                        pltpu.VMEM((gb, rows, S5_ROW), F32)],
        compiler_params=_params("parallel"),
        name="s5_main",
    )(x3, m_state, m_intra, m_out, a_re, a_im)
    y = y.reshape(m, BRANCH_W)
    tm = min(512, m)
    return pl.pallas_call(
        _s5_glu_kernel,
        out_shape=jax.ShapeDtypeStruct((m, BRANCH_W), BF16),
        grid=(m // tm,),
        in_specs=[pl.BlockSpec((tm, BRANCH_W), lambda i: (i, 0)),
                  pl.BlockSpec((BRANCH_W, BRANCH_W), lambda i: (0, 0)),
                  pl.BlockSpec((1, BRANCH_W), lambda i: (0, 0)),
                  pl.BlockSpec((tm, BRANCH_W), lambda i: (i, COL_AZ // BRANCH_W))],
        out_specs=pl.BlockSpec((tm, BRANCH_W), lambda i: (i, 0)),
        compiler_params=_params("parallel"),
        name="s5_glu",
    )(y, w_glu.astype(BF16), b_glu.astype(F32).reshape(1, BRANCH_W), proj)


def _dft_tables(seq):
    n = 2 * seq
    mm = np.arange(seq, dtype=np.int64)
    k1 = np.arange(seq // DFT_ROWS, dtype=np.int64)[:, None] * DFT_ROWS
    k0 = np.arange(DFT_ROWS, dtype=np.int64)[:, None]
    ang_a = 2.0 * np.pi * ((k1 * mm) % n).astype(np.float64) / n
    ang_b = 2.0 * np.pi * ((k0 * mm) % n).astype(np.float64) / n
    return tuple(jnp.asarray(t, F32) for t in (np.cos(ang_a), np.sin(ang_a), np.cos(ang_b), np.sin(ang_b)))


def _dft_gen_kernel(ac_ref, as_ref, bc_ref, bs_ref, c_ref, s1_ref, s2_ref):
    i = pl.program_id(0)
    a_c = ac_ref[pl.ds(i, 1), :]
    a_s = as_ref[pl.ds(i, 1), :]
    b_c, b_s = bc_ref[...], bs_ref[...]
    cos_t = a_c * b_c - a_s * b_s
    sin_t = a_s * b_c + a_c * b_s
    rows = lax.broadcasted_iota(jnp.int32, cos_t.shape, 0) + i * DFT_ROWS
    cols = lax.broadcasted_iota(jnp.int32, cos_t.shape, 1)
    alt_cols = jnp.where((cols & 1) == 0, 1.0, -1.0).astype(F32)
    alt_rows = jnp.where((rows & 1) == 0, 1.0, -1.0).astype(F32)
    c_ref[...] = cos_t.astype(BF16)
    s1_ref[...] = jnp.where(rows == 0, alt_cols, sin_t).astype(BF16)
    s2_ref[...] = jnp.where(cols == 0, alt_rows, sin_t).astype(BF16)


def _dft_matrices(seq):
    tabs = _dft_tables(seq)
    n_steps = seq // DFT_ROWS
    tab_spec = pl.BlockSpec(tabs[0].shape, lambda i: (0, 0))
    b_spec = pl.BlockSpec((DFT_ROWS, seq), lambda i: (0, 0))
    o_spec = pl.BlockSpec((DFT_ROWS, seq), lambda i: (i, 0))
    return pl.pallas_call(
        _dft_gen_kernel,
        out_shape=(jax.ShapeDtypeStruct((seq, seq), BF16),) * 3,
        grid=(n_steps,),
        in_specs=[tab_spec, tab_spec, b_spec, b_spec],
        out_specs=(o_spec,) * 3,
        compiler_params=_params("parallel"),
        name="dft_gen",
    )(*tabs)


def _hy_filter_kernel(feat_ref, w1_ref, b1_ref, w2_ref, b2_ref, f0_ref, f1_ref,
                      w3p_ref, w3n_ref, b3p_ref, b3n_ref, dl_ref, t_ref,
                      hs_ref, hd_ref, hp_ref, hn_ref, r0_ref, h_ref, split_ref, *, inv_n):
    hi = lax.Precision.HIGHEST

    @pl.when(pl.program_id(0) == 0)
    def _():
        h1 = jnp.sin(f0_ref[...] * (jnp.dot(feat_ref[...], w1_ref[...], precision=hi,
                                            preferred_element_type=F32) + b1_ref[...]))
        h_ref[...] = jnp.sin(f1_ref[...] * (jnp.dot(h1, w2_ref[...], precision=hi,
                                                    preferred_element_type=F32) + b2_ref[...]))

    h = h_ref[...].astype(BF16)
    win = jnp.exp(-t_ref[...] * jnp.abs(dl_ref[...]))
    hpos = (jnp.dot(h, w3p_ref[...].astype(BF16), preferred_element_type=F32) + b3p_ref[...]) * win
    hneg = (jnp.dot(h, w3n_ref[...].astype(BF16), preferred_element_type=F32) + b3n_ref[...]) * win
    rows = lax.broadcasted_iota(jnp.int32, hpos.shape, 0)
    hneg = jnp.where(rows == 0, 0.0, hneg)
    norm = (jnp.sum(jnp.abs(hpos), axis=0, keepdims=True)
            + jnp.sum(jnp.abs(hneg), axis=0, keepdims=True))
    hpos = hpos / norm
    hneg = hneg / norm
    hsum = hpos + hneg
    hdiff = hpos - hneg
    even = (rows & 1) == 0
    alt2 = jnp.where(((rows >> 1) & 1) == 0, 1.0, -1.0).astype(F32)
    col_sum = lambda a: jnp.sum(a, axis=0, keepdims=True)
    a0 = 2.0 * col_sum(jnp.where(even, hsum, 0.0))
    d0 = 2.0 * col_sum(jnp.where(even, 0.0, hsum))
    hr2 = 2.0 * col_sum(jnp.where(even, alt2 * hsum, 0.0))
    hi2 = -2.0 * col_sum(jnp.where(even, 0.0, alt2 * hdiff))
    r0_ref[...] = jnp.concatenate([a0, d0, hr2, hi2, jnp.zeros((4, a0.shape[1]), F32)], axis=0) * inv_n
    def lags(x, par):
        _stage_rows(split_ref, x)
        return _rows_of_parity(split_ref, par).astype(BF16)

    hs_ref[...] = lags(hsum, 0)
    hd_ref[...] = lags(hdiff, 0)
    hp_ref[...] = lags(hpos, 1)
    hn_ref[...] = lags(hneg, 1)


def _hy_filter_taps(seq, w1, b1, w2, b2, freq, w3, b3):
    n_ch = 2 * BRANCH_W
    bands = (HY_EMB - 1) // 2
    t = jnp.linspace(0.0, 1.0, seq, dtype=F32)[:, None]
    w = 2.0 * math.pi * jnp.arange(seq, dtype=F32)[:, None] / seq
    f = jnp.linspace(1e-4, bands - 1, bands, dtype=F32)[None, :]
    feats = jnp.concatenate([t, jnp.cos(f * w), -jnp.sin(f * w),
                             jnp.zeros((seq, HY_PAD - HY_EMB), F32)], axis=-1)
    deltas = jnp.linspace(math.log(HY_DECAY_TARGET) / HY_SLOW_DECAY,
                          math.log(HY_DECAY_TARGET) / HY_FAST_DECAY, n_ch, dtype=F32)[None, :]

    def pad2(a, r, c):
        a = a.astype(F32)
        return jnp.pad(a, ((0, r - a.shape[0]), (0, c - a.shape[1])))

    w1p = pad2(w1, HY_PAD, HY_PAD)
    w2p = pad2(w2, HY_PAD, HY_PAD)
    b1p = pad2(b1[None], 1, HY_PAD)
    b2p = pad2(b2[None], 1, HY_PAD)
    f0p = pad2(freq[0][None], 1, HY_PAD)
    f1p = pad2(freq[1][None], 1, HY_PAD)
    w3p = pad2(w3, HY_PAD, 2 * n_ch)
    b3r = b3.astype(F32)[None]
    tn = 256
    nt = n_ch // tn
    full = lambda shape: pl.BlockSpec(shape, lambda j: (0, 0))
    tap_spec = pl.BlockSpec((seq // 2, tn), lambda j: (0, j))
    return pl.pallas_call(
        functools.partial(_hy_filter_kernel, inv_n=1.0 / (2 * seq)),
        out_shape=(jax.ShapeDtypeStruct((seq // 2, n_ch), BF16),) * 4
                  + (jax.ShapeDtypeStruct((8, n_ch), F32),),
        grid=(nt,),
        in_specs=[full((seq, HY_PAD)), full((HY_PAD, HY_PAD)), full((1, HY_PAD)),
                  full((HY_PAD, HY_PAD)), full((1, HY_PAD)), full((1, HY_PAD)), full((1, HY_PAD)),
                  pl.BlockSpec((HY_PAD, tn), lambda j: (0, j)),
                  pl.BlockSpec((HY_PAD, tn), lambda j: (0, j + nt)),
                  pl.BlockSpec((1, tn), lambda j: (0, j)),
                  pl.BlockSpec((1, tn), lambda j: (0, j + nt)),
                  pl.BlockSpec((1, tn), lambda j: (0, j)),
                  full((seq, 1))],
        out_specs=(tap_spec,) * 4 + (pl.BlockSpec((8, tn), lambda j: (0, j)),),
        scratch_shapes=[pltpu.VMEM((seq, HY_PAD), F32), pltpu.VMEM((tn // 128, seq, 128), F32)],
        compiler_params=_params("arbitrary"),
        name="hy_filter",
    )(feats, w1p, b1p, w2p, b2p, f0p, f1p, w3p, w3p, b3r, b3r, deltas, t)


def _hy_spectrum_kernel(c_ref, s_ref, hse_ref, hde_ref, hpo_ref, hno_ref,
                        ac_ref, as_ref, bc_ref, bs_ref, gc_ref, gs_ref, *, n_half):
    i = pl.program_id(1)
    dot = lambda w, h: jnp.dot(w[...], h[...], preferred_element_type=F32)
    hec, hes = dot(c_ref, hse_ref), dot(s_ref, hde_ref)
    upc, ups = dot(c_ref, hpo_ref), dot(s_ref, hpo_ref)
    umc, ums = dot(c_ref, hno_ref), dot(s_ref, hno_ref)
    tm = hec.shape[0]
    k = (lax.broadcasted_iota(jnp.int32, (tm, 128), 0) + i * tm).astype(F32)
    psi = k * (math.pi / n_half)
    reps = hec.shape[1] // 128
    cp = jnp.concatenate([jnp.cos(psi)] * reps, axis=1)
    sp = jnp.concatenate([jnp.sin(psi)] * reps, axis=1)
    w = 1.0 / n_half
    ac_ref[...] = w * hec
    as_ref[...] = w * hes
    bc_ref[...] = w * (cp * upc - sp * ups + umc)
    bs_ref[...] = w * (cp * ups + sp * upc - ums)
    gc_ref[...] = w * (upc + cp * umc - sp * ums)
    gs_ref[...] = w * (ups - cp * ums - sp * umc)


def _hy_spectrum(cm, s1, taps):
    half, n_ch = taps[0].shape
    tm = min(512, half)
    tn = 512
    w_spec = pl.BlockSpec((tm, half), lambda j, i: (i, 0))
    tap_spec = pl.BlockSpec((half, tn), lambda j, i: (0, j))
    o_spec = pl.BlockSpec((tm, tn), lambda j, i: (i, j))
    return pl.pallas_call(
        functools.partial(_hy_spectrum_kernel, n_half=half),
        out_shape=(jax.ShapeDtypeStruct((half, n_ch), F32),) * 6,
        grid=(n_ch // tn, half // tm),
        in_specs=[w_spec, w_spec] + [tap_spec] * 4,
        out_specs=(o_spec,) * 6,
        compiler_params=_params("parallel", "arbitrary"),
        name="hy_spectrum",
    )(cm, s1, *taps)


def _stage_rows(scr_ref, x):
    for c in range(x.shape[1] // 128):
        scr_ref[c] = x[:, c * 128:(c + 1) * 128]


def _rows_of_parity(scr_ref, par):
    n = scr_ref.shape[1] // 2
    return jnp.concatenate([scr_ref[c, pl.ds(par, n, stride=2), :] for c in range(scr_ref.shape[0])], axis=1)


def _hy_conv3_kernel(u_ref, w_ref, b_ref, o_ref, vb_ref, s_ref):
    u = u_ref[...]
    n = u.shape[0]
    rows = lax.broadcasted_iota(jnp.int32, u.shape, 0)
    prev = jnp.where(rows == 0, 0.0, pltpu.roll(u, 1, 0))
    nxt = jnp.where(rows == n - 1, 0.0, pltpu.roll(u, n - 1, 0))
    w = w_ref[...]
    _stage_rows(s_ref, prev * w[0:1] + u * w[1:2] + nxt * w[2:3] + b_ref[...])
    for par in range(2):
        part = _rows_of_parity(s_ref, par)
        o_ref[par] = part
        vb_ref[par] = part.astype(BF16)


def _hy_conv3(proj, bsz, seq, conv_w, conv_b):
    half = seq // 2
    tn = 256
    nb = 3 * BRANCH_W // tn
    o_spec = pl.BlockSpec((2, half, tn), lambda b, j: (0, b, j))
    return pl.pallas_call(
        _hy_conv3_kernel,
        out_shape=(jax.ShapeDtypeStruct((2, bsz * half, 3 * BRANCH_W), F32),
                   jax.ShapeDtypeStruct((2, bsz * half, 3 * BRANCH_W), BF16)),
        grid=(bsz, nb),
        in_specs=[pl.BlockSpec((seq, tn), lambda b, j: (b, j + COL_BU // tn)),
                  pl.BlockSpec((3, tn), lambda b, j: (0, j)),
                  pl.BlockSpec((1, tn), lambda b, j: (0, j))],
        out_specs=(o_spec, o_spec),
        scratch_shapes=[pltpu.VMEM((tn // 128, seq, 128), F32)],
        compiler_params=_params("parallel", "parallel"),
        name="hy_conv3",
    )(proj, conv_w.astype(F32), conv_b.astype(F32).reshape(1, -1))


def _hy_fwd_kernel(c_ref, s_ref, ze_ref, zo_ref, ac_ref, as_ref, bc_ref, bs_ref, gc_ref, gs_ref, r0_ref,
                   pc_ref, ps_ref, qc_ref, qs_ref):
    dot = lambda w, z: jnp.dot(w[...], z[...], preferred_element_type=F32)
    ec, es = dot(c_ref, ze_ref), dot(s_ref, ze_ref)
    oc, os_ = dot(c_ref, zo_ref), dot(s_ref, zo_ref)
    a_c, a_s = ac_ref[...], as_ref[...]
    b_c, b_s = bc_ref[...], bs_ref[...]
    g_c, g_s = gc_ref[...], gs_ref[...]
    outs = (ec * a_c - es * a_s + oc * b_c - os_ * b_s,
            ec * a_s + es * a_c + oc * b_s + os_ * b_c,
            ec * g_c - es * g_s + oc * a_c - os_ * a_s,
            ec * g_s + es * g_c + oc * a_s + os_ * a_c)
    r0 = r0_ref[...]
    a0, d0, hr2, hi2 = r0[0:1], r0[1:2], r0[2:3], r0[3:4]
    e0, eh, o0, oh = ec[0:1], es[0:1], oc[0:1], os_[0:1]
    first = (e0 * a0 + o0 * d0, eh * hr2 + oh * hi2, e0 * d0 + o0 * a0, oh * hr2 - eh * hi2)
    top = 16
    is_row0 = (lax.broadcasted_iota(jnp.int32, (top, ec.shape[1]), 0) == 0) & (pl.program_id(2) == 0)
    for ref, val, row0 in zip((pc_ref, ps_ref, qc_ref, qs_ref), outs, first):
        ref[...] = val.astype(BF16)
        ref[0:top, :] = jnp.where(is_row0, row0, val[0:top]).astype(BF16)


HY_TN = 512


def _hy_fwd(cm, s1, z_even, z_odd, tables, r0, h_col, bsz, half):
    tm = min(512, half)
    tn = HY_TN
    mt = half // tm
    w_spec = pl.BlockSpec((tm, half), lambda b, j, i: (i, 0))
    h_spec = pl.BlockSpec((tm, tn), lambda b, j, i: (i, j + h_col // tn))
    o_spec = pl.BlockSpec((tm, tn), lambda b, j, i: (b * mt + i, j))
    return pl.pallas_call(
        _hy_fwd_kernel,
        out_shape=(jax.ShapeDtypeStruct((bsz * half, BRANCH_W), BF16),) * 4,
        grid=(bsz, BRANCH_W // tn, mt),
        in_specs=[w_spec, w_spec, z_even[1], z_odd[1]] + [h_spec] * 6
                 + [pl.BlockSpec((8, tn), lambda b, j, i: (0, j + h_col // tn))],
        out_specs=(o_spec,) * 4,
        compiler_params=_params("parallel", "parallel", "arbitrary"),
        name="hy_fwd",
    )(cm, s1, z_even[0], z_odd[0], *tables, r0)


def _hy_inv_convs(c_ref, s_ref, pc_ref, ps_ref, qc_ref, qs_ref):
    dot = lambda w, y: jnp.dot(w[...], y[...], preferred_element_type=F32)
    return dot(c_ref, pc_ref) + dot(s_ref, ps_ref), dot(c_ref, qc_ref) + dot(s_ref, qs_ref)


def _hy_inv_mid_kernel(c_ref, s_ref, pc_ref, ps_ref, qc_ref, qs_ref, ge_ref, go_ref, ze_ref, zo_ref,
                       bias_ref, o_ref, ob_ref):
    convs = _hy_inv_convs(c_ref, s_ref, pc_ref, ps_ref, qc_ref, qs_ref)
    for par, (conv, g_ref, z_ref) in enumerate(zip(convs, (ge_ref, go_ref), (ze_ref, zo_ref))):
        out = g_ref[...] * (conv + bias_ref[...] * z_ref[...])
        o_ref[par] = out
        ob_ref[par] = out.astype(BF16)


def _hy_inv_last_kernel(c_ref, s_ref, pc_ref, ps_ref, qc_ref, qs_ref, ge_ref, go_ref, ze_ref, zo_ref,
                        bias_ref, sz_ref, o_ref, mix_ref):
    convs = _hy_inv_convs(c_ref, s_ref, pc_ref, ps_ref, qc_ref, qs_ref)
    tm = ge_ref.shape[0]
    _stage_rows(mix_ref, sz_ref[...])
    outs = [g_ref[...] * (conv + bias_ref[...] * z_ref[...]) * _rows_of_parity(mix_ref, par)
            for par, (conv, g_ref, z_ref) in enumerate(zip(convs, (ge_ref, go_ref), (ze_ref, zo_ref)))]
    for par, out in enumerate(outs):
        for c in range(mix_ref.shape[0]):
            mix_ref[c, pl.ds(par, tm, stride=2), :] = out[:, c * 128:(c + 1) * 128]
    for c in range(mix_ref.shape[0]):
        o_ref[:, c * 128:(c + 1) * 128] = mix_ref[c]


def _hy_inv(cm, s2, spectra, gates, zprev, bias_row, bsz, half, silu=None):
    tm = min(512, half)
    tn = HY_TN
    mt = half // tm
    w_spec = pl.BlockSpec((tm, half), lambda b, j, i: (i, 0))
    y_spec = pl.BlockSpec((half, tn), lambda b, j, i: (b, j))
    pairs = list(gates) + list(zprev)
    in_specs = [w_spec, w_spec] + [y_spec] * 4 + [s for _, s in pairs] + [pl.BlockSpec((1, tn), lambda b, j, i: (0, j))]
    args = [cm, s2, *spectra] + [a for a, _ in pairs] + [bias_row]
    if silu is None:
        o_spec = pl.BlockSpec((2, tm, tn), lambda b, j, i: (0, b * mt + i, j))
        shape = lambda dt: jax.ShapeDtypeStruct((2, bsz * half, BRANCH_W), dt)
        body, out_shape, out_specs = _hy_inv_mid_kernel, (shape(F32), shape(BF16)), (o_spec, o_spec)
        scratch = []
    else:
        silu_arr, silu_col = silu
        body = _hy_inv_last_kernel
        out_shape = jax.ShapeDtypeStruct((2 * bsz * half, BRANCH_W), F32)
        out_specs = pl.BlockSpec((2 * tm, tn), lambda b, j, i: (b * mt + i, j))
        in_specs.append(pl.BlockSpec((2 * tm, tn), lambda b, j, i: (b * mt + i, j + silu_col // tn)))
        args.append(silu_arr)
        scratch = [pltpu.VMEM((tn // 128, 2 * tm, 128), F32)]
    return pl.pallas_call(
        body, out_shape=out_shape, grid=(bsz, BRANCH_W // tn, mt), in_specs=in_specs, out_specs=out_specs,
        scratch_shapes=scratch, compiler_params=_params("parallel", "parallel", "arbitrary"),
        name="hy_inv",
    )(*args)


def _hyena_branch(proj, bsz, seq, dft, conv_w, conv_b, w1, b1, w2, b2, freq, w3, b3, bias):
    cm, s1, s2 = dft
    half = seq // 2
    tm = min(512, half)
    tn = HY_TN
    mt = half // tm
    full = lambda par, col=0: pl.BlockSpec((None, half, tn), lambda b, j, i: (par, b, j + col // tn))
    row = lambda par, col=0: pl.BlockSpec((None, tm, tn), lambda b, j, i: (par, b * mt + i, j + col // tn))
    both = lambda arr, spec, col=0: ((arr, spec(0, col)), (arr, spec(1, col)))
    *taps, r0 = _hy_filter_taps(seq, w1, b1, w2, b2, freq, w3, b3)
    tables = _hy_spectrum(cm, s1, taps)
    uc, ucb = _hy_conv3(proj, bsz, seq, conv_w, conv_b)
    bias = bias.astype(F32)
    spectra = _hy_fwd(cm, s1, *both(ucb, full), tables, r0, 0, bsz, half)
    z1, z1b = _hy_inv(cm, s2, spectra, gates=both(uc, row, BRANCH_W), zprev=both(uc, row),
                      bias_row=bias[0:1], bsz=bsz, half=half)
    spectra = _hy_fwd(cm, s1, *both(z1b, full), tables, r0, BRANCH_W, bsz, half)
    return _hy_inv(cm, s2, spectra, gates=both(uc, row, 2 * BRANCH_W), zprev=both(z1, row),
                   bias_row=bias[1:2], bsz=bsz, half=half, silu=(proj, COL_BZ))


def _rope_table_kernel(pos_ref, inv_ref, cos_ref, sin_ref):
    ang = pos_ref[...] * inv_ref[...]
    lane = lax.broadcasted_iota(jnp.int32, ang.shape, 1)
    live = lane < MLA_ROPE
    cos_ref[...] = jnp.where(live, jnp.cos(ang), 0.0)
    sin_ref[...] = jnp.where(live, jnp.where(lane < MLA_ROPE // 2, -1.0, 1.0) * jnp.sin(ang), 0.0)


def _rope_tables(positions):
    m = positions.size
    half = MLA_ROPE // 2
    inv = ROPE_BASE ** (-jnp.arange(half, dtype=F32) / half)
    inv = jnp.concatenate([inv, inv, jnp.zeros((128 - MLA_ROPE,), F32)])[None]
    pos = positions.astype(F32).reshape(m, 1)
    tm = min(1024, m)
    spec = pl.BlockSpec((tm, 128), lambda i: (i, 0))
    return pl.pallas_call(
        _rope_table_kernel,
        out_shape=(jax.ShapeDtypeStruct((m, 128), F32),) * 2,
        grid=(m // tm,),
        in_specs=[pl.BlockSpec((tm, 1), lambda i: (i, 0)), pl.BlockSpec((1, 128), lambda i: (0, 0))],
        out_specs=(spec, spec),
        compiler_params=_params("parallel"),
        name="rope_table",
    )(pos, inv)


def _rope128(x, cos_t, sin_t):
    lane = lax.broadcasted_iota(jnp.int32, x.shape, 1)
    half = MLA_ROPE // 2
    partner = jnp.where(lane < half, pltpu.roll(x, 128 - half, 1), pltpu.roll(x, half, 1))
    return x * cos_t + partner * sin_t


def _rms(x, g):
    ms = jnp.mean(jnp.square(x), axis=-1, keepdims=True)
    return x * lax.rsqrt(ms + RMS_EPS) * g


def _mla_q_kernel(cq_ref, g_ref, w_ref, cos_ref, sin_ref, q_ref, *, scale):
    xn = _rms(cq_ref[...], g_ref[...]).astype(BF16)
    q = jnp.dot(xn, w_ref[...], preferred_element_type=F32) * scale
    cos_t, sin_t = cos_ref[...], sin_ref[...]
    for h in range(MLA_HEADS):
        base = h * MLA_QK_PAD
        q_ref[:, base:base + MLA_NOPE] = q[:, base:base + MLA_NOPE].astype(BF16)
        q_ref[:, base + MLA_NOPE:base + MLA_QK_PAD] = _rope128(
            q[:, base + MLA_NOPE:base + MLA_QK_PAD], cos_t, sin_t).astype(BF16)


def _mla_kv_kernel(ckv_ref, g_ref, wk_ref, wv_ref, kr_ref, cos_ref, sin_ref, k_ref, v_ref):
    xn = _rms(ckv_ref[...], g_ref[...]).astype(BF16)
    kn = jnp.dot(xn, wk_ref[...], preferred_element_type=F32)
    v_ref[...] = jnp.dot(xn, wv_ref[...], preferred_element_type=F32).astype(BF16)
    kr = _rope128(kr_ref[:, 0:128], cos_ref[...], sin_ref[...]).astype(BF16)
    for h in range(MLA_HEADS):
        base = h * MLA_QK_PAD
        k_ref[:, base:base + MLA_NOPE] = kn[:, h * MLA_NOPE:(h + 1) * MLA_NOPE].astype(BF16)
        k_ref[:, base + MLA_NOPE:base + MLA_QK_PAD] = kr


def _mla_attn_kernel(q_ref, k_ref, v_ref, z_ref, o_ref):
    k, v = k_ref[...], v_ref[...]
    half = q_ref.shape[0] // 2
    for r in range(2):
        rows = pl.ds(r * half, half)
        s = lax.dot_general(q_ref[rows, :], k, (((1,), (1,)), ((), ())),
                            preferred_element_type=F32)
        p = jnp.exp2(s - jnp.max(s, axis=-1, keepdims=True))
        l = jnp.sum(p, axis=-1, keepdims=True)
        o = jnp.dot(p.astype(BF16), v, preferred_element_type=F32)
        o_ref[rows, :] = (o / l * z_ref[rows, :]).astype(BF16)


def _mla_branch(proj, proj_b, bsz, seq, rope, q_norm_g, w_uq, kv_norm_g, w_ukv):
    m = bsz * seq
    cos_t, sin_t = rope
    dqk = MLA_NOPE + MLA_ROPE
    hq = MLA_HEADS * MLA_QK_PAD
    w_q = w_uq.reshape(MLA_LORA, MLA_HEADS, dqk)
    w_q = jnp.pad(w_q, ((0, 0), (0, 0), (0, MLA_QK_PAD - dqk))).reshape(MLA_LORA, hq).astype(BF16)
    w_kv = w_ukv.reshape(MLA_LORA, MLA_HEADS, MLA_NOPE + MLA_V)
    w_k = w_kv[:, :, :MLA_NOPE].reshape(MLA_LORA, MLA_HEADS * MLA_NOPE).astype(BF16)
    w_v = w_kv[:, :, MLA_NOPE:].reshape(MLA_LORA, MLA_HEADS * MLA_V).astype(BF16)
    tm = min(512, m)
    row = lambda shape, col=0: pl.BlockSpec(shape, lambda i: (i, col))
    full = lambda shape: pl.BlockSpec(shape, lambda i: (0, 0))
    qp = pl.pallas_call(
        functools.partial(_mla_q_kernel, scale=dqk ** -0.5 * math.log2(math.e)),
        out_shape=jax.ShapeDtypeStruct((m, hq), BF16),
        grid=(m // tm,),
        in_specs=[row((tm, MLA_LORA), COL_CQ // MLA_LORA), full((1, MLA_LORA)), full((MLA_LORA, hq)),
                  row((tm, 128)), row((tm, 128))],
        out_specs=row((tm, hq)),
        compiler_params=_params("parallel"),
        name="mla_q",
    )(proj, q_norm_g.astype(F32).reshape(1, -1), w_q, cos_t, sin_t)
    kp, vp = pl.pallas_call(
        _mla_kv_kernel,
        out_shape=(jax.ShapeDtypeStruct((m, hq), BF16),
                   jax.ShapeDtypeStruct((m, MLA_HEADS * MLA_V), BF16)),
        grid=(m // tm,),
        in_specs=[row((tm, MLA_LORA), COL_CKV // MLA_LORA), full((1, MLA_LORA)),
                  full((MLA_LORA, MLA_HEADS * MLA_NOPE)), full((MLA_LORA, MLA_HEADS * MLA_V)),
                  row((tm, 512), COL_CKR // 512), row((tm, 128)), row((tm, 128))],
        out_specs=(row((tm, hq)), row((tm, MLA_HEADS * MLA_V))),
        compiler_params=_params("parallel"),
        name="mla_kv",
    )(proj, kv_norm_g.astype(F32).reshape(1, -1), w_k, w_v, proj_b, cos_t, sin_t)
    tq = min(512, seq)
    qt = seq // tq
    return pl.pallas_call(
        _mla_attn_kernel,
        out_shape=jax.ShapeDtypeStruct((m, MLA_HEADS * MLA_V), BF16),
        grid=(bsz, MLA_HEADS, qt),
        in_specs=[pl.BlockSpec((tq, MLA_QK_PAD), lambda b, h, i: (b * qt + i, h)),
                  pl.BlockSpec((seq, MLA_QK_PAD), lambda b, h, i: (b, h)),
                  pl.BlockSpec((seq, MLA_V), lambda b, h, i: (b, h)),
                  pl.BlockSpec((tq, MLA_V), lambda b, h, i: (b * qt + i, h + COL_CZ // MLA_V))],
        out_specs=pl.BlockSpec((tq, MLA_V), lambda b, h, i: (b * qt + i, h)),
        compiler_params=_params("parallel", "parallel", "arbitrary"),
        name="mla_attn",
    )(qp, kp, vp, proj_b)


def _lift_kernel(ya_ref, yb_ref, yc_ref, w_ref, ga_ref, gb_ref, gc_ref, o_ref, wb_ref):
    @pl.when(pl.program_id(1) == 0)
    def _():
        wb_ref[...] = w_ref[...].astype(BF16)

    acc = ga_ref[...] * jnp.dot(ya_ref[...], wb_ref[0], preferred_element_type=F32)
    acc += gb_ref[...] * jnp.dot(yb_ref[...].astype(BF16), wb_ref[1], preferred_element_type=F32)
    acc += gc_ref[...] * jnp.dot(yc_ref[...], wb_ref[2], preferred_element_type=F32)
    o_ref[...] = acc.astype(BF16)


def _lift(ya, yb, yc, w_lift, layer, proj_b):
    m = ya.shape[0]
    tm = min(1024, m)
    tn = 512
    y_spec = pl.BlockSpec((tm, BRANCH_W), lambda j, i: (i, 0))
    gate = lambda n: pl.BlockSpec((tm, tn), lambda j, i: (i, j + (COL_GATE + n * D_MODEL) // tn))
    return pl.pallas_call(
        _lift_kernel,
        out_shape=jax.ShapeDtypeStruct((m, D_MODEL), BF16),
        grid=(D_MODEL // tn, m // tm),
        in_specs=[y_spec, y_spec, y_spec,
                  pl.BlockSpec((None, N_BRANCH, BRANCH_W, tn), lambda j, i: (layer, 0, 0, j)),
                  gate(0), gate(1), gate(2)],
        out_specs=pl.BlockSpec((tm, tn), lambda j, i: (i, j)),
        scratch_shapes=[pltpu.VMEM((N_BRANCH, BRANCH_W, tn), BF16)],
        compiler_params=_params("parallel", "arbitrary"),
        name="lift",
    )(ya, yb, yc, w_lift, proj_b, proj_b, proj_b)


def _out_kernel(mix_ref, wo_ref, p_ref, wp_ref, sp_ref, x_ref, g_ref, b_ref, o_ref):
    mixed = jnp.dot(mix_ref[...], wo_ref[...], preferred_element_type=F32)
    ple = jnp.dot(p_ref[...].astype(BF16), wp_ref[...], preferred_element_type=F32) * sp_ref[...]
    r = DEEPNORM_ALPHA * x_ref[...] + mixed + ple
    mu = jnp.mean(r, axis=-1, keepdims=True)
    var = jnp.mean(jnp.square(r - mu), axis=-1, keepdims=True)
    o_ref[...] = (r - mu) * lax.rsqrt(var + LN_EPS) * g_ref[...] + b_ref[...]


def _out_norm(mix, w_out, p, w_ple, proj, x, ln_g, ln_b):
    m = mix.shape[0]
    tm = min(512, m)
    row = lambda w, col=0: pl.BlockSpec((tm, w), lambda i: (i, col))
    full = lambda shape: pl.BlockSpec(shape, lambda i: (0, 0))
    return pl.pallas_call(
        _out_kernel,
        out_shape=jax.ShapeDtypeStruct((m, D_MODEL), F32),
        grid=(m // tm,),
        in_specs=[row(D_MODEL), full((D_MODEL, D_MODEL)), row(PLE_DIM), full((PLE_DIM, D_MODEL)),
                  row(D_MODEL, COL_PLE // D_MODEL), row(D_MODEL), full((1, D_MODEL)), full((1, D_MODEL))],
        out_specs=row(D_MODEL),
        compiler_params=_params("parallel"),
        name="out_norm",
    )(mix, w_out.astype(BF16), p, w_ple.astype(BF16), proj, x,
      ln_g.astype(F32).reshape(1, -1), ln_b.astype(F32).reshape(1, -1))


def kernel(x, p, positions, w_in, s5_lambda_re, s5_lambda_im, s5_log_dt, s5_b_re, s5_b_im, s5_c_re, s5_c_im, s5_d, s5_w_glu, s5_b_glu, hy_conv_w, hy_conv_b, hy_w1, hy_b1, hy_w2, hy_b2, hy_freq, hy_w3, hy_b3, hy_bias, mla_q_norm, mla_w_uq, mla_kv_norm, mla_w_ukv, w_lift, w_out, w_ple, ln_g, ln_b):
    bsz, seq, _ = x.shape
    m = bsz * seq
    depth = w_in.shape[0]
    dft = _dft_matrices(seq // 2)
    rope = _rope_tables(positions)
    s5_tiled = _s5_tiled_params(s5_lambda_re, s5_lambda_im, s5_log_dt, s5_b_re, s5_b_im,
                                s5_c_re, s5_c_im, s5_d)
    xf = x.reshape(m, D_MODEL).astype(F32)
    w_t = jnp.swapaxes(w_in, 1, 2)
    for i in range(depth):
        proj, proj_b = _proj_both(xf, w_t, i)
        y_a = _s5_branch(proj, bsz, seq, [a[i] for a in s5_tiled], s5_w_glu[i], s5_b_glu[i])
        y_b = _hyena_branch(proj, bsz, seq, dft, hy_conv_w[i], hy_conv_b[i], hy_w1[i], hy_b1[i],
                            hy_w2[i], hy_b2[i], hy_freq[i], hy_w3[i], hy_b3[i], hy_bias[i])
        y_c = _mla_branch(proj, proj_b, bsz, seq, rope, mla_q_norm[i], mla_w_uq[i], mla_kv_norm[i], mla_w_ukv[i])
        mix = _lift(y_a, y_b, y_c, w_lift, i, proj_b)
        xf = _out_norm(mix, w_out[i], p[i].reshape(m, PLE_DIM), w_ple[i], proj_b, xf, ln_g[i], ln_b[i])
    return xf.reshape(bsz, seq, D_MODEL).astype(x.dtype)
```

```python
import functools
import math

import numpy as np
import jax
import jax.numpy as jnp
from jax import lax
from jax.experimental import pallas as pl
from jax.experimental.pallas import tpu as pltpu

F32 = jnp.float32
BF16 = jnp.bfloat16

D_MODEL = 2048
PLE_DIM = 256
N_BRANCH = 3
BRANCH_W = 1024

S5_GROUP = 16
S5_GROUPS = BRANCH_W // S5_GROUP
S5_STATE = 64
S5_CHUNK = 16
S5_ROW = S5_CHUNK * S5_GROUP

HY_EMB = 33
HY_FF = 64
HY_PAD = 128
HY_DECAY_TARGET = 0.01
HY_FAST_DECAY = 0.3
HY_SLOW_DECAY = 1.5
DFT_ROWS = 64

MLA_HEADS = 8
MLA_NOPE = 128
MLA_ROPE = 64
MLA_V = 128
MLA_LORA = 512
MLA_QK_PAD = 256
ROPE_BASE = 10000.0

LN_EPS = 1e-5
RMS_EPS = 1e-6
DEPTH = 2
DEEPNORM_ALPHA = (2 * DEPTH) ** 0.25

COL_AX = 0
COL_AZ = 1024
COL_BU = 2048
COL_BZ = 5120
COL_CQ = 6144
COL_CKV = 6656
PROJ_A_N = 7168
COL_PLE = 0
COL_GATE = 2048
COL_CZ = 8192
COL_CKR = 9216
PROJ_B_N = 9728
PROJ_TN = 512
W_IN_CKR = 7168
W_IN_CZ = 7232
W_IN_GATE = 8256
W_IN_PLE = 14400

VMEM_LIMIT = 56 * 1024 * 1024


def _params(*sem):
    return pltpu.CompilerParams(dimension_semantics=sem, vmem_limit_bytes=VMEM_LIMIT)


def _sigmoid(x):
    return 0.5 * jnp.tanh(0.5 * x) + 0.5


def _in_tiles(j, ranges):
    hit = None
    for lo, hi in ranges:
        cur = (j >= lo // PROJ_TN) & (j < hi // PROJ_TN)
        hit = cur if hit is None else (hit | cur)
    return hit


def _proj_kernel(x_ref, w_ref, o_ref, xb_ref, *, silu_cols, sigm_cols):
    j = pl.program_id(1)

    @pl.when(j == 0)
    def _():
        xb_ref[...] = x_ref[...].astype(BF16)

    acc = lax.dot_general(xb_ref[...], w_ref[...].astype(BF16), (((1,), (1,)), ((), ())),
                          preferred_element_type=F32)
    is_silu = _in_tiles(j, silu_cols)
    plain = jnp.logical_not(is_silu)

    @pl.when(is_silu)
    def _():
        o_ref[...] = acc * _sigmoid(acc)

    if sigm_cols:
        is_sigm = _in_tiles(j, sigm_cols)
        plain = jnp.logical_not(is_silu | is_sigm)

        @pl.when(is_sigm)
        def _():
            o_ref[...] = _sigmoid(acc)

    @pl.when(plain)
    def _():
        o_ref[...] = acc


def _proj(x, w, layer, n_out, silu_cols, sigm_cols, name):
    m, k = x.shape
    tm = min(1024, m)
    return pl.pallas_call(
        functools.partial(_proj_kernel, silu_cols=silu_cols, sigm_cols=sigm_cols),
        out_shape=jax.ShapeDtypeStruct((m, n_out), F32),
        grid=(m // tm, n_out // PROJ_TN),
        in_specs=[pl.BlockSpec((tm, k), lambda i, j: (i, 0)),
                  pl.BlockSpec((None, PROJ_TN, k), lambda i, j: (layer, j, 0))],
        out_specs=pl.BlockSpec((tm, PROJ_TN), lambda i, j: (i, j)),
        scratch_shapes=[pltpu.VMEM((tm, k), BF16)],
        compiler_params=_params("parallel", "arbitrary"),
        name=name,
    )(x, w)


def _repack_src_tile(j):
    t = lambda col: col // PROJ_TN
    return jnp.where(j < t(COL_GATE), j + t(W_IN_PLE),
                     jnp.where(j < t(COL_CZ), j - t(COL_GATE) + t(W_IN_GATE),
                               jnp.where(j < t(COL_CKR), j - t(COL_CZ) + t(W_IN_CZ), t(W_IN_CKR))))


def _repack_kernel(a_ref, b_ref, o_ref):
    is_last = pl.program_id(0) == pl.num_programs(0) - 1
    a = a_ref[...]
    shifted = jnp.concatenate([a[MLA_ROPE:], b_ref[...]], axis=0)
    plain = jnp.concatenate([a[:MLA_ROPE], jnp.zeros_like(a[MLA_ROPE:])], axis=0)
    o_ref[...] = jnp.where(is_last, plain, shifted).astype(BF16)


def _repack_b(w_t, layer):
    k = w_t.shape[2]
    sub = PROJ_TN // MLA_ROPE
    return pl.pallas_call(
        _repack_kernel,
        out_shape=jax.ShapeDtypeStruct((1, PROJ_B_N, k), BF16),
        grid=(PROJ_B_N // PROJ_TN,),
        in_specs=[pl.BlockSpec((None, PROJ_TN, k), lambda j: (layer, _repack_src_tile(j), 0)),
                  pl.BlockSpec((None, MLA_ROPE, k), lambda j: (layer, (_repack_src_tile(j) + 1) * sub, 0))],
        out_specs=pl.BlockSpec((None, PROJ_TN, k), lambda j: (0, j, 0)),
        compiler_params=_params("parallel"),
        name="repack_b",
    )(w_t, w_t)


def _proj_both(x, w_t, layer):
    proj_a = _proj(x, w_t, layer, PROJ_A_N, ((COL_AZ, COL_BU), (COL_BZ, COL_CQ)), (), "proj_a")
    proj_b = _proj(x, _repack_b(w_t, layer), 0, PROJ_B_N, ((COL_CZ, COL_CKR),), ((COL_PLE, COL_CZ),),
                   "proj_b")
    return proj_a, proj_b


S5_LANES = 8 * S5_STATE
S5_SLAB = 8
S5_RELAYOUT_ROWS = 32


def _s5_tile_lanes(a):
    return jnp.concatenate([a[:, 0]] * 4 + [a[:, 1]] * 4, axis=-1)


def _s5_tiled_params(lam_re, lam_im, log_dt, b_re, b_im, c_re, c_im, d):
    f = lambda a: a.astype(F32)
    depth = lam_re.shape[0]
    ldt = jnp.broadcast_to(f(log_dt)[..., None, None], lam_re.shape[:3] + (1, S5_STATE))
    return (_s5_tile_lanes(f(lam_re)[:, :, :, None, :]), _s5_tile_lanes(f(lam_im)[:, :, :, None, :]),
            _s5_tile_lanes(ldt),
            _s5_tile_lanes(jnp.swapaxes(f(b_re), -1, -2)), _s5_tile_lanes(jnp.swapaxes(f(b_im), -1, -2)),
            _s5_tile_lanes(f(c_re)), _s5_tile_lanes(f(c_im)),
            f(d).reshape(depth, S5_GROUPS, S5_GROUP, 1))


def _s5_mats_kernel(lr_ref, li_ref, ldt_ref, br_ref, bi_ref, cr_ref, ci_ref, d_ref,
                    ms_ref, mi_ref, mo_ref, ar_ref, ai_ref):
    t_n, h_n = S5_CHUNK, S5_GROUP
    hi = lax.Precision.HIGHEST
    nt = (((1,), (1,)), ((), ()))
    blk = lax.broadcasted_iota(jnp.int32, (1, S5_LANES), 1) // S5_STATE
    is_im = (blk // 2) % 2 == 1
    is_fwd = blk < 4
    steps = lax.broadcasted_iota(jnp.int32, (24, S5_LANES), 0).astype(F32)
    lane_k = lax.broadcasted_iota(jnp.int32, (h_n, S5_ROW), 1)
    sub_k = lax.broadcasted_iota(jnp.int32, (h_n, S5_ROW), 0)

    def per_group(gi, carry):
        lr, li = lr_ref[gi], li_ref[gi]
        dt = jnp.exp(ldt_ref[gi])
        zr, zi = lr * dt, li * dt
        mag = jnp.exp(steps * zr)
        tr, ti = mag * jnp.cos(steps * zi), mag * jnp.sin(steps * zi)
        lbr, lbi = tr[1:2], ti[1:2]
        n2 = lr * lr + li * li
        qr = ((lbr - 1.0) * lr + lbi * li) / n2
        qi = (lbi * lr - (lbr - 1.0) * li) / n2
        br, bi = br_ref[gi], bi_ref[gi]
        bbr, bbi = qr * br - qi * bi, qr * bi + qi * br
        y1, y2 = jnp.where(is_im, bbi, bbr), jnp.where(is_im, bbr, bbi)
        cr, ci = cr_ref[gi], ci_ref[gi]

        def pick(tab, t_fwd, t_bwd):
            return jnp.where(is_fwd, tab[t_fwd:t_fwd + 1], tab[t_bwd:t_bwd + 1])

        def c_times(p_r, p_i):
            return jnp.where(is_im, -(cr * p_i + ci * p_r), cr * p_r - ci * p_i)

        q_rows = []
        for t in range(t_n):
            a_r, a_i = pick(tr, t_n - 1 - t, t), pick(ti, t_n - 1 - t, t)
            rows = pl.ds(t * h_n, h_n)
            ms_ref[gi, rows, :] = (a_r * y1 + jnp.where(is_im, a_i, -a_i) * y2).astype(BF16)
            mo_ref[gi, rows, :] = c_times(pick(tr, t + 1, t_n - t), pick(ti, t + 1, t_n - t)).astype(BF16)
            q_rows.append(c_times(pick(tr, t, t_n - 1 - t), pick(ti, t, t_n - 1 - t)))
        q = jnp.concatenate(q_rows, axis=0)
        half = S5_LANES // 2
        kf = 0.5 * lax.dot_general(y1[:, :half], q[:, :half], nt, precision=hi, preferred_element_type=F32)
        kb = 0.5 * lax.dot_general(y1[:, half:], q[:, half:], nt, precision=hi, preferred_element_type=F32)
        kf = kf + jnp.where(lane_k == sub_k, d_ref[gi], 0.0)
        for t in range(t_n):
            fwd = kf if t == 0 else jnp.where(lane_k >= h_n * t, pltpu.roll(kf, h_n * t, 1), 0.0)
            sh = h_n * (t_n - 1 - t)
            bwd = kb if sh == 0 else jnp.where(lane_k < S5_ROW - sh, pltpu.roll(kb, S5_ROW - sh, 1), 0.0)
            mi_ref[gi, pl.ds(t * h_n, h_n), :] = (fwd + bwd).astype(BF16)
        ar_ref[gi] = tr[t_n:t_n + 1]
        ai_ref[gi] = ti[t_n:t_n + 1]
        return carry

    lax.fori_loop(0, lr_ref.shape[0], per_group, 0)


def _s5_mats(tiled):
    g_n = S5_GROUPS
    gb = S5_SLAB
    spec = lambda r, w: pl.BlockSpec((gb, r, w), lambda j: (j, 0, 0))
    in_rows = (1, 1, 1, S5_GROUP, S5_GROUP, S5_GROUP, S5_GROUP)
    return pl.pallas_call(
        _s5_mats_kernel,
        out_shape=(jax.ShapeDtypeStruct((g_n, S5_ROW, S5_LANES), BF16),
                   jax.ShapeDtypeStruct((g_n, S5_ROW, S5_ROW), BF16),
                   jax.ShapeDtypeStruct((g_n, S5_ROW, S5_LANES), BF16),
                   jax.ShapeDtypeStruct((g_n, 1, S5_LANES), F32),
                   jax.ShapeDtypeStruct((g_n, 1, S5_LANES), F32)),
        grid=(g_n // gb,),
        in_specs=[spec(r, S5_LANES) for r in in_rows] + [spec(S5_GROUP, 1)],
        out_specs=(spec(S5_ROW, S5_LANES), spec(S5_ROW, S5_ROW), spec(S5_ROW, S5_LANES),
                   spec(1, S5_LANES), spec(1, S5_LANES)),
        compiler_params=_params("parallel"),
        name="s5_mats",
    )(*tiled)


def _seg_transpose(vs, seg):
    vs = list(vs)
    for s in (4, 2, 1):
        keep = (seg & s) == 0
        for i in range(8):
            if i & s:
                continue
            a, b = vs[i], vs[i + s]
            vs[i] = jnp.where(keep, a, pltpu.roll(b, s * S5_GROUP, 1))
            vs[i + s] = jnp.where(keep, pltpu.roll(a, 128 - s * S5_GROUP, 1), b)
    return vs


def _s5_main_kernel(x_ref, ms_ref, mi_ref, mo_ref, ar_ref, ai_ref, y_ref,
                    u_ref, sl_ref, st_ref, yg_ref, *, bsz):
    rows = x_ref.shape[0]
    n_chunks = rows // bsz
    rc = S5_RELAYOUT_ROWS
    seg = lax.broadcasted_iota(jnp.int32, (rc, 128), 1) // S5_GROUP
    slot = (lax.broadcasted_iota(jnp.int32, (1, S5_LANES), 1) // S5_STATE) % 2
    slot128 = slot[:, :128]

    def relayout_in(r, carry):
        r0 = pl.multiple_of(r * rc, rc)
        for th in range(2):
            src = [x_ref[pl.ds(r0, rc), th * 8 + t8, :] for t8 in range(8)]
            for gi, out in enumerate(_seg_transpose(src, seg)):
                u_ref[gi, pl.ds(r0, rc), th * 128:(th + 1) * 128] = out.astype(BF16)
        return carry

    lax.fori_loop(0, rows // rc, relayout_in, 0, unroll=2)

    n_pairs = S5_SLAB // 2
    trans = []
    for jp in range(n_pairs):
        g0, g1 = 2 * jp, 2 * jp + 1
        r0 = jnp.dot(u_ref[g0], ms_ref[g0], preferred_element_type=F32)
        r1 = jnp.dot(u_ref[g1], ms_ref[g1], preferred_element_type=F32)
        loc = jnp.where(slot == 0, r0, r1)
        for k in range(4):
            for b in range(bsz):
                sl_ref[k, jp, pl.ds(b, n_chunks, stride=bsz), :] = (
                    loc[b * n_chunks:(b + 1) * n_chunks, k * 128:(k + 1) * 128])
        trans.append([jnp.where(slot128 == 0, ref[g0][:, off:off + 128], ref[g1][:, off:off + 128])
                      for ref, off in ((ar_ref, 0), (ai_ref, 0), (ar_ref, 256), (ai_ref, 256))])

    zero = jnp.zeros((bsz, 128), F32)

    def scan(c, carry):
        rf = pl.ds(pl.multiple_of(c * bsz, bsz), bsz)
        rb = pl.ds(pl.multiple_of((n_chunks - 1 - c) * bsz, bsz), bsz)
        new = []
        for jp in range(n_pairs):
            s_fr, s_fi, s_br, s_bi = carry[jp]
            a_fr, a_fi, a_br, a_bi = trans[jp]
            st_ref[0, jp, rf, :] = s_fr
            st_ref[1, jp, rf, :] = s_fi
            st_ref[2, jp, rb, :] = s_br
            st_ref[3, jp, rb, :] = s_bi
            new.append((a_fr * s_fr - a_fi * s_fi + sl_ref[0, jp, rf, :],
                        a_fr * s_fi + a_fi * s_fr + sl_ref[1, jp, rf, :],
                        a_br * s_br - a_bi * s_bi + sl_ref[2, jp, rb, :],
                        a_br * s_bi + a_bi * s_br + sl_ref[3, jp, rb, :]))
        return tuple(new)

    lax.fori_loop(0, n_chunks, scan, tuple((zero,) * 4 for _ in range(n_pairs)))

    nt = (((1,), (1,)), ((), ()))
    for jp in range(n_pairs):
        st = jnp.concatenate(
            [jnp.concatenate([st_ref[k, jp, pl.ds(b, n_chunks, stride=bsz), :] for b in range(bsz)], axis=0)
             for k in range(4)], axis=1)
        for e in range(2):
            g = 2 * jp + e
            st_g = jnp.where(slot == e, st, 0.0).astype(BF16)
            y = (jnp.dot(u_ref[g], mi_ref[g], preferred_element_type=F32)
                 + lax.dot_general(st_g, mo_ref[g], nt, preferred_element_type=F32))
            yg_ref[g] = jax.nn.gelu(y)

    def relayout_out(r, carry):
        r0 = pl.multiple_of(r * rc, rc)
        for th in range(2):
            src = [yg_ref[gi, pl.ds(r0, rc), th * 128:(th + 1) * 128] for gi in range(S5_SLAB)]
            for t8, out in enumerate(_seg_transpose(src, seg)):
                y_ref[pl.ds(r0, rc), th * 8 + t8, :] = out
        return carry

    lax.fori_loop(0, rows // rc, relayout_out, 0, unroll=2)


def _s5_glu_kernel(g_ref, w_ref, b_ref, z_ref, o_ref):
    g = g_ref[...]
    acc = jnp.dot(g.astype(BF16), w_ref[...], preferred_element_type=F32) + b_ref[...]
    o_ref[...] = (g * _sigmoid(acc) * z_ref[...]).astype(o_ref.dtype)


def _s5_branch(proj, bsz, seq, tiled, w_glu, b_glu):
    m_state, m_intra, m_out, a_re, a_im = _s5_mats(tiled)
    t_n = S5_CHUNK
    rows = bsz * (seq // t_n)
    m = bsz * seq
    gb = S5_SLAB
    x3 = proj.reshape(rows, t_n, proj.shape[1])
    mat = lambda r, w: pl.BlockSpec((gb, r, w), lambda s: (s, 0, 0))
    io_spec = pl.BlockSpec((rows, t_n, 128), lambda s: (0, 0, s + COL_AX // 128))
    y = pl.pallas_call(
        functools.partial(_s5_main_kernel, bsz=bsz),
        out_shape=jax.ShapeDtypeStruct((rows, t_n, BRANCH_W), F32),
        grid=(S5_GROUPS // gb,),
        in_specs=[io_spec, mat(S5_ROW, S5_LANES), mat(S5_ROW, S5_ROW), mat(S5_ROW, S5_LANES),
                  mat(1, S5_LANES), mat(1, S5_LANES)],
        out_specs=pl.BlockSpec((rows, t_n, 128), lambda s: (0, 0, s)),
        scratch_shapes=[pltpu.VMEM((gb, rows, S5_ROW), BF16),
                        pltpu.VMEM((4, gb // 2, rows, 128), F32),
                        pltpu.VMEM((4, gb // 2, rows, 128), F32),
                        pltpu.VMEM((gb, rows, S5_ROW), F32)],
        compiler_params=_params("parallel"),
        name="s5_main",
    )(x3, m_state, m_intra, m_out, a_re, a_im)
    y = y.reshape(m, BRANCH_W)
    tm = min(512, m)
    return pl.pallas_call(
        _s5_glu_kernel,
        out_shape=jax.ShapeDtypeStruct((m, BRANCH_W), BF16),
        grid=(m // tm,),
        in_specs=[pl.BlockSpec((tm, BRANCH_W), lambda i: (i, 0)),
                  pl.BlockSpec((BRANCH_W, BRANCH_W), lambda i: (0, 0)),
                  pl.BlockSpec((1, BRANCH_W), lambda i: (0, 0)),
                  pl.BlockSpec((tm, BRANCH_W), lambda i: (i, COL_AZ // BRANCH_W))],
        out_specs=pl.BlockSpec((tm, BRANCH_W), lambda i: (i, 0)),
        compiler_params=_params("parallel"),
        name="s5_glu",
    )(y, w_glu.astype(BF16), b_glu.astype(F32).reshape(1, BRANCH_W), proj)


def _dft_tables(seq):
    n = 2 * seq
    mm = np.arange(seq, dtype=np.int64)
    k1 = np.arange(seq // DFT_ROWS, dtype=np.int64)[:, None] * DFT_ROWS
    k0 = np.arange(DFT_ROWS, dtype=np.int64)[:, None]
    ang_a = 2.0 * np.pi * ((k1 * mm) % n).astype(np.float64) / n
    ang_b = 2.0 * np.pi * ((k0 * mm) % n).astype(np.float64) / n
    return tuple(jnp.asarray(t, F32) for t in (np.cos(ang_a), np.sin(ang_a), np.cos(ang_b), np.sin(ang_b)))


def _dft_gen_kernel(ac_ref, as_ref, bc_ref, bs_ref, c_ref, s1_ref, s2_ref):
    i = pl.program_id(0)
    a_c = ac_ref[pl.ds(i, 1), :]
    a_s = as_ref[pl.ds(i, 1), :]
    b_c, b_s = bc_ref[...], bs_ref[...]
    cos_t = a_c * b_c - a_s * b_s
    sin_t = a_s * b_c + a_c * b_s
    rows = lax.broadcasted_iota(jnp.int32, cos_t.shape, 0) + i * DFT_ROWS
    cols = lax.broadcasted_iota(jnp.int32, cos_t.shape, 1)
    alt_cols = jnp.where((cols & 1) == 0, 1.0, -1.0).astype(F32)
    alt_rows = jnp.where((rows & 1) == 0, 1.0, -1.0).astype(F32)
    c_ref[...] = cos_t.astype(BF16)
    s1_ref[...] = jnp.where(rows == 0, alt_cols, sin_t).astype(BF16)
    s2_ref[...] = jnp.where(cols == 0, alt_rows, sin_t).astype(BF16)


def _dft_matrices(seq):
    tabs = _dft_tables(seq)
    n_steps = seq // DFT_ROWS
    tab_spec = pl.BlockSpec(tabs[0].shape, lambda i: (0, 0))
    b_spec = pl.BlockSpec((DFT_ROWS, seq), lambda i: (0, 0))
    o_spec = pl.BlockSpec((DFT_ROWS, seq), lambda i: (i, 0))
    return pl.pallas_call(
        _dft_gen_kernel,
        out_shape=(jax.ShapeDtypeStruct((seq, seq), BF16),) * 3,
        grid=(n_steps,),
        in_specs=[tab_spec, tab_spec, b_spec, b_spec],
        out_specs=(o_spec,) * 3,
        compiler_params=_params("parallel"),
        name="dft_gen",
    )(*tabs)


def _hy_filter_kernel(feat_ref, w1_ref, b1_ref, w2_ref, b2_ref, f0_ref, f1_ref,
                      w3p_ref, w3n_ref, b3p_ref, b3n_ref, dl_ref, t_ref,
                      hs_ref, hd_ref, hp_ref, hn_ref, r0_ref, h_ref, split_ref, *, inv_n):
    hi = lax.Precision.HIGHEST

    @pl.when(pl.program_id(0) == 0)
    def _():
        h1 = jnp.sin(f0_ref[...] * (jnp.dot(feat_ref[...], w1_ref[...], precision=hi,
                                            preferred_element_type=F32) + b1_ref[...]))
        h_ref[...] = jnp.sin(f1_ref[...] * (jnp.dot(h1, w2_ref[...], precision=hi,
                                                    preferred_element_type=F32) + b2_ref[...]))

    h = h_ref[...].astype(BF16)
    win = jnp.exp(-t_ref[...] * jnp.abs(dl_ref[...]))
    hpos = (jnp.dot(h, w3p_ref[...].astype(BF16), preferred_element_type=F32) + b3p_ref[...]) * win
    hneg = (jnp.dot(h, w3n_ref[...].astype(BF16), preferred_element_type=F32) + b3n_ref[...]) * win
    rows = lax.broadcasted_iota(jnp.int32, hpos.shape, 0)
    hneg = jnp.where(rows == 0, 0.0, hneg)
    norm = (jnp.sum(jnp.abs(hpos), axis=0, keepdims=True)
            + jnp.sum(jnp.abs(hneg), axis=0, keepdims=True))
    hpos = hpos / norm
    hneg = hneg / norm
    hsum = hpos + hneg
    hdiff = hpos - hneg
    even = (rows & 1) == 0
    alt2 = jnp.where(((rows >> 1) & 1) == 0, 1.0, -1.0).astype(F32)
    col_sum = lambda a: jnp.sum(a, axis=0, keepdims=True)
    a0 = 2.0 * col_sum(jnp.where(even, hsum, 0.0))
    d0 = 2.0 * col_sum(jnp.where(even, 0.0, hsum))
    hr2 = 2.0 * col_sum(jnp.where(even, alt2 * hsum, 0.0))
    hi2 = -2.0 * col_sum(jnp.where(even, 0.0, alt2 * hdiff))
    r0_ref[...] = jnp.concatenate([a0, d0, hr2, hi2, jnp.zeros((4, a0.shape[1]), F32)], axis=0) * inv_n
    def lags(x, par):
        _stage_rows(split_ref, x)
        return _rows_of_parity(split_ref, par).astype(BF16)

    hs_ref[...] = lags(hsum, 0)
    hd_ref[...] = lags(hdiff, 0)
    hp_ref[...] = lags(hpos, 1)
    hn_ref[...] = lags(hneg, 1)


def _hy_filter_taps(seq, w1, b1, w2, b2, freq, w3, b3):
    n_ch = 2 * BRANCH_W
    bands = (HY_EMB - 1) // 2
    t = jnp.linspace(0.0, 1.0, seq, dtype=F32)[:, None]
    w = 2.0 * math.pi * jnp.arange(seq, dtype=F32)[:, None] / seq
    f = jnp.linspace(1e-4, bands - 1, bands, dtype=F32)[None, :]
    feats = jnp.concatenate([t, jnp.cos(f * w), -jnp.sin(f * w),
                             jnp.zeros((seq, HY_PAD - HY_EMB), F32)], axis=-1)
    deltas = jnp.linspace(math.log(HY_DECAY_TARGET) / HY_SLOW_DECAY,
                          math.log(HY_DECAY_TARGET) / HY_FAST_DECAY, n_ch, dtype=F32)[None, :]

    def pad2(a, r, c):
        a = a.astype(F32)
        return jnp.pad(a, ((0, r - a.shape[0]), (0, c - a.shape[1])))

    w1p = pad2(w1, HY_PAD, HY_PAD)
    w2p = pad2(w2, HY_PAD, HY_PAD)
    b1p = pad2(b1[None], 1, HY_PAD)
    b2p = pad2(b2[None], 1, HY_PAD)
    f0p = pad2(freq[0][None], 1, HY_PAD)
    f1p = pad2(freq[1][None], 1, HY_PAD)
    w3p = pad2(w3, HY_PAD, 2 * n_ch)
    b3r = b3.astype(F32)[None]
    tn = 256
    nt = n_ch // tn
    full = lambda shape: pl.BlockSpec(shape, lambda j: (0, 0))
    tap_spec = pl.BlockSpec((seq // 2, tn), lambda j: (0, j))
    return pl.pallas_call(
        functools.partial(_hy_filter_kernel, inv_n=1.0 / (2 * seq)),
        out_shape=(jax.ShapeDtypeStruct((seq // 2, n_ch), BF16),) * 4
                  + (jax.ShapeDtypeStruct((8, n_ch), F32),),
        grid=(nt,),
        in_specs=[full((seq, HY_PAD)), full((HY_PAD, HY_PAD)), full((1, HY_PAD)),
                  full((HY_PAD, HY_PAD)), full((1, HY_PAD)), full((1, HY_PAD)), full((1, HY_PAD)),
                  pl.BlockSpec((HY_PAD, tn), lambda j: (0, j)),
                  pl.BlockSpec((HY_PAD, tn), lambda j: (0, j + nt)),
                  pl.BlockSpec((1, tn), lambda j: (0, j)),
                  pl.BlockSpec((1, tn), lambda j: (0, j + nt)),
                  pl.BlockSpec((1, tn), lambda j: (0, j)),
                  full((seq, 1))],
        out_specs=(tap_spec,) * 4 + (pl.BlockSpec((8, tn), lambda j: (0, j)),),
        scratch_shapes=[pltpu.VMEM((seq, HY_PAD), F32), pltpu.VMEM((tn // 128, seq, 128), F32)],
        compiler_params=_params("arbitrary"),
        name="hy_filter",
    )(feats, w1p, b1p, w2p, b2p, f0p, f1p, w3p, w3p, b3r, b3r, deltas, t)


def _hy_spectrum_kernel(c_ref, s_ref, hse_ref, hde_ref, hpo_ref, hno_ref,
                        ac_ref, as_ref, bc_ref, bs_ref, gc_ref, gs_ref, *, n_half):
    i = pl.program_id(1)
    dot = lambda w, h: jnp.dot(w[...], h[...], preferred_element_type=F32)
    hec, hes = dot(c_ref, hse_ref), dot(s_ref, hde_ref)
    upc, ups = dot(c_ref, hpo_ref), dot(s_ref, hpo_ref)
    umc, ums = dot(c_ref, hno_ref), dot(s_ref, hno_ref)
    tm = hec.shape[0]
    k = (lax.broadcasted_iota(jnp.int32, (tm, 128), 0) + i * tm).astype(F32)
    psi = k * (math.pi / n_half)
    reps = hec.shape[1] // 128
    cp = jnp.concatenate([jnp.cos(psi)] * reps, axis=1)
    sp = jnp.concatenate([jnp.sin(psi)] * reps, axis=1)
    w = 1.0 / n_half
    ac_ref[...] = w * hec
    as_ref[...] = w * hes
    bc_ref[...] = w * (cp * upc - sp * ups + umc)
    bs_ref[...] = w * (cp * ups + sp * upc - ums)
    gc_ref[...] = w * (upc + cp * umc - sp * ums)
    gs_ref[...] = w * (ups - cp * ums - sp * umc)


def _hy_spectrum(cm, s1, taps):
    half, n_ch = taps[0].shape
    tm = min(512, half)
    tn = 512
    w_spec = pl.BlockSpec((tm, half), lambda j, i: (i, 0))
    tap_spec = pl.BlockSpec((half, tn), lambda j, i: (0, j))
    o_spec = pl.BlockSpec((tm, tn), lambda j, i: (i, j))
    return pl.pallas_call(
        functools.partial(_hy_spectrum_kernel, n_half=half),
        out_shape=(jax.ShapeDtypeStruct((half, n_ch), F32),) * 6,
        grid=(n_ch // tn, half // tm),
        in_specs=[w_spec, w_spec] + [tap_spec] * 4,
        out_specs=(o_spec,) * 6,
        compiler_params=_params("parallel", "arbitrary"),
        name="hy_spectrum",
    )(cm, s1, *taps)


def _stage_rows(scr_ref, x):
    for c in range(x.shape[1] // 128):
        scr_ref[c] = x[:, c * 128:(c + 1) * 128]


def _rows_of_parity(scr_ref, par):
    n = scr_ref.shape[1] // 2
    return jnp.concatenate([scr_ref[c, pl.ds(par, n, stride=2), :] for c in range(scr_ref.shape[0])], axis=1)


def _hy_conv3_kernel(u_ref, w_ref, b_ref, o_ref, vb_ref, s_ref):
    u = u_ref[...]
    n = u.shape[0]
    rows = lax.broadcasted_iota(jnp.int32, u.shape, 0)
    prev = jnp.where(rows == 0, 0.0, pltpu.roll(u, 1, 0))
    nxt = jnp.where(rows == n - 1, 0.0, pltpu.roll(u, n - 1, 0))
    w = w_ref[...]
    _stage_rows(s_ref, prev * w[0:1] + u * w[1:2] + nxt * w[2:3] + b_ref[...])
    for par in range(2):
        part = _rows_of_parity(s_ref, par)
        o_ref[par] = part
        vb_ref[par] = part.astype(BF16)


def _hy_conv3(proj, bsz, seq, conv_w, conv_b):
    half = seq // 2
    tn = 256
    nb = 3 * BRANCH_W // tn
    o_spec = pl.BlockSpec((2, half, tn), lambda b, j: (0, b, j))
    return pl.pallas_call(
        _hy_conv3_kernel,
        out_shape=(jax.ShapeDtypeStruct((2, bsz * half, 3 * BRANCH_W), F32),
                   jax.ShapeDtypeStruct((2, bsz * half, 3 * BRANCH_W), BF16)),
        grid=(bsz, nb),
        in_specs=[pl.BlockSpec((seq, tn), lambda b, j: (b, j + COL_BU // tn)),
                  pl.BlockSpec((3, tn), lambda b, j: (0, j)),
                  pl.BlockSpec((1, tn), lambda b, j: (0, j))],
        out_specs=(o_spec, o_spec),
        scratch_shapes=[pltpu.VMEM((tn // 128, seq, 128), F32)],
        compiler_params=_params("parallel", "parallel"),
        name="hy_conv3",
    )(proj, conv_w.astype(F32), conv_b.astype(F32).reshape(1, -1))


def _hy_fwd_kernel(c_ref, s_ref, ze_ref, zo_ref, ac_ref, as_ref, bc_ref, bs_ref, gc_ref, gs_ref, r0_ref,
                   pc_ref, ps_ref, qc_ref, qs_ref):
    dot = lambda w, z: jnp.dot(w[...], z[...], preferred_element_type=F32)
    ec, es = dot(c_ref, ze_ref), dot(s_ref, ze_ref)
    oc, os_ = dot(c_ref, zo_ref), dot(s_ref, zo_ref)
    a_c, a_s = ac_ref[...], as_ref[...]
    b_c, b_s = bc_ref[...], bs_ref[...]
    g_c, g_s = gc_ref[...], gs_ref[...]
    outs = (ec * a_c - es * a_s + oc * b_c - os_ * b_s,
            ec * a_s + es * a_c + oc * b_s + os_ * b_c,
            ec * g_c - es * g_s + oc * a_c - os_ * a_s,
            ec * g_s + es * g_c + oc * a_s + os_ * a_c)
    r0 = r0_ref[...]
    a0, d0, hr2, hi2 = r0[0:1], r0[1:2], r0[2:3], r0[3:4]
    e0, eh, o0, oh = ec[0:1], es[0:1], oc[0:1], os_[0:1]
    first = (e0 * a0 + o0 * d0, eh * hr2 + oh * hi2, e0 * d0 + o0 * a0, oh * hr2 - eh * hi2)
    top = 16
    is_row0 = (lax.broadcasted_iota(jnp.int32, (top, ec.shape[1]), 0) == 0) & (pl.program_id(2) == 0)
    for ref, val, row0 in zip((pc_ref, ps_ref, qc_ref, qs_ref), outs, first):
        ref[...] = val.astype(BF16)
        ref[0:top, :] = jnp.where(is_row0, row0, val[0:top]).astype(BF16)


HY_TN = 512


def _hy_fwd(cm, s1, z_even, z_odd, tables, r0, h_col, bsz, half):
    tm = min(512, half)
    tn = HY_TN
    mt = half // tm
    w_spec = pl.BlockSpec((tm, half), lambda b, j, i: (i, 0))
    h_spec = pl.BlockSpec((tm, tn), lambda b, j, i: (i, j + h_col // tn))
    o_spec = pl.BlockSpec((tm, tn), lambda b, j, i: (b * mt + i, j))
    return pl.pallas_call(
        _hy_fwd_kernel,
        out_shape=(jax.ShapeDtypeStruct((bsz * half, BRANCH_W), BF16),) * 4,
        grid=(bsz, BRANCH_W // tn, mt),
        in_specs=[w_spec, w_spec, z_even[1], z_odd[1]] + [h_spec] * 6
                 + [pl.BlockSpec((8, tn), lambda b, j, i: (0, j + h_col // tn))],
        out_specs=(o_spec,) * 4,
        compiler_params=_params("parallel", "parallel", "arbitrary"),
        name="hy_fwd",
    )(cm, s1, z_even[0], z_odd[0], *tables, r0)


def _hy_inv_convs(c_ref, s_ref, pc_ref, ps_ref, qc_ref, qs_ref):
    dot = lambda w, y: jnp.dot(w[...], y[...], preferred_element_type=F32)
    return dot(c_ref, pc_ref) + dot(s_ref, ps_ref), dot(c_ref, qc_ref) + dot(s_ref, qs_ref)


def _hy_inv_mid_kernel(c_ref, s_ref, pc_ref, ps_ref, qc_ref, qs_ref, ge_ref, go_ref, ze_ref, zo_ref,
                       bias_ref, o_ref, ob_ref):
    convs = _hy_inv_convs(c_ref, s_ref, pc_ref, ps_ref, qc_ref, qs_ref)
    for par, (conv, g_ref, z_ref) in enumerate(zip(convs, (ge_ref, go_ref), (ze_ref, zo_ref))):
        out = g_ref[...] * (conv + bias_ref[...] * z_ref[...])
        o_ref[par] = out
        ob_ref[par] = out.astype(BF16)


def _hy_inv_last_kernel(c_ref, s_ref, pc_ref, ps_ref, qc_ref, qs_ref, ge_ref, go_ref, ze_ref, zo_ref,
                        bias_ref, sz_ref, o_ref, mix_ref):
    convs = _hy_inv_convs(c_ref, s_ref, pc_ref, ps_ref, qc_ref, qs_ref)
    tm = ge_ref.shape[0]
    _stage_rows(mix_ref, sz_ref[...])
    outs = [g_ref[...] * (conv + bias_ref[...] * z_ref[...]) * _rows_of_parity(mix_ref, par)
            for par, (conv, g_ref, z_ref) in enumerate(zip(convs, (ge_ref, go_ref), (ze_ref, zo_ref)))]
    for par, out in enumerate(outs):
        for c in range(mix_ref.shape[0]):
            mix_ref[c, pl.ds(par, tm, stride=2), :] = out[:, c * 128:(c + 1) * 128]
    for c in range(mix_ref.shape[0]):
        o_ref[:, c * 128:(c + 1) * 128] = mix_ref[c]


def _hy_inv(cm, s2, spectra, gates, zprev, bias_row, bsz, half, silu=None):
    tm = min(512, half)
    tn = HY_TN
    mt = half // tm
    w_spec = pl.BlockSpec((tm, half), lambda b, j, i: (i, 0))
    y_spec = pl.BlockSpec((half, tn), lambda b, j, i: (b, j))
    pairs = list(gates) + list(zprev)
    in_specs = [w_spec, w_spec] + [y_spec] * 4 + [s for _, s in pairs] + [pl.BlockSpec((1, tn), lambda b, j, i: (0, j))]
    args = [cm, s2, *spectra] + [a for a, _ in pairs] + [bias_row]
    if silu is None:
        o_spec = pl.BlockSpec((2, tm, tn), lambda b, j, i: (0, b * mt + i, j))
        shape = lambda dt: jax.ShapeDtypeStruct((2, bsz * half, BRANCH_W), dt)
        body, out_shape, out_specs = _hy_inv_mid_kernel, (shape(F32), shape(BF16)), (o_spec, o_spec)
        scratch = []
    else:
        silu_arr, silu_col = silu
        body = _hy_inv_last_kernel
        out_shape = jax.ShapeDtypeStruct((2 * bsz * half, BRANCH_W), F32)
        out_specs = pl.BlockSpec((2 * tm, tn), lambda b, j, i: (b * mt + i, j))
        in_specs.append(pl.BlockSpec((2 * tm, tn), lambda b, j, i: (b * mt + i, j + silu_col // tn)))
        args.append(silu_arr)
        scratch = [pltpu.VMEM((tn // 128, 2 * tm, 128), F32)]
    return pl.pallas_call(
        body, out_shape=out_shape, grid=(bsz, BRANCH_W // tn, mt), in_specs=in_specs, out_specs=out_specs,
        scratch_shapes=scratch, compiler_params=_params("parallel", "parallel", "arbitrary"),
        name="hy_inv",
    )(*args)


def _hyena_branch(proj, bsz, seq, dft, conv_w, conv_b, w1, b1, w2, b2, freq, w3, b3, bias):
    cm, s1, s2 = dft
    half = seq // 2
    tm = min(512, half)
    tn = HY_TN
    mt = half // tm
    full = lambda par, col=0: pl.BlockSpec((None, half, tn), lambda b, j, i: (par, b, j + col // tn))
    row = lambda par, col=0: pl.BlockSpec((None, tm, tn), lambda b, j, i: (par, b * mt + i, j + col // tn))
    both = lambda arr, spec, col=0: ((arr, spec(0, col)), (arr, spec(1, col)))
    *taps, r0 = _hy_filter_taps(seq, w1, b1, w2, b2, freq, w3, b3)
    tables = _hy_spectrum(cm, s1, taps)
    uc, ucb = _hy_conv3(proj, bsz, seq, conv_w, conv_b)
    bias = bias.astype(F32)
    spectra = _hy_fwd(cm, s1, *both(ucb, full), tables, r0, 0, bsz, half)
    z1, z1b = _hy_inv(cm, s2, spectra, gates=both(uc, row, BRANCH_W), zprev=both(uc, row),
                      bias_row=bias[0:1], bsz=bsz, half=half)
    spectra = _hy_fwd(cm, s1, *both(z1b, full), tables, r0, BRANCH_W, bsz, half)
    return _hy_inv(cm, s2, spectra, gates=both(uc, row, 2 * BRANCH_W), zprev=both(z1, row),
                   bias_row=bias[1:2], bsz=bsz, half=half, silu=(proj, COL_BZ))


def _rope_table_kernel(pos_ref, inv_ref, cos_ref, sin_ref):
    ang = pos_ref[...] * inv_ref[...]
    lane = lax.broadcasted_iota(jnp.int32, ang.shape, 1)
    live = lane < MLA_ROPE
    cos_ref[...] = jnp.where(live, jnp.cos(ang), 0.0)
    sin_ref[...] = jnp.where(live, jnp.where(lane < MLA_ROPE // 2, -1.0, 1.0) * jnp.sin(ang), 0.0)


def _rope_tables(positions):
    m = positions.size
    half = MLA_ROPE // 2
    inv = ROPE_BASE ** (-jnp.arange(half, dtype=F32) / half)
    inv = jnp.concatenate([inv, inv, jnp.zeros((128 - MLA_ROPE,), F32)])[None]
    pos = positions.astype(F32).reshape(m, 1)
    tm = min(1024, m)
    spec = pl.BlockSpec((tm, 128), lambda i: (i, 0))
    return pl.pallas_call(
        _rope_table_kernel,
        out_shape=(jax.ShapeDtypeStruct((m, 128), F32),) * 2,
        grid=(m // tm,),
        in_specs=[pl.BlockSpec((tm, 1), lambda i: (i, 0)), pl.BlockSpec((1, 128), lambda i: (0, 0))],
        out_specs=(spec, spec),
        compiler_params=_params("parallel"),
        name="rope_table",
    )(pos, inv)


def _rope128(x, cos_t, sin_t):
    lane = lax.broadcasted_iota(jnp.int32, x.shape, 1)
    half = MLA_ROPE // 2
    partner = jnp.where(lane < half, pltpu.roll(x, 128 - half, 1), pltpu.roll(x, half, 1))
    return x * cos_t + partner * sin_t


def _rms(x, g):
    ms = jnp.mean(jnp.square(x), axis=-1, keepdims=True)
    return x * lax.rsqrt(ms + RMS_EPS) * g


def _mla_q_kernel(cq_ref, g_ref, w_ref, cos_ref, sin_ref, q_ref, *, scale):
    xn = _rms(cq_ref[...], g_ref[...]).astype(BF16)
    q = jnp.dot(xn, w_ref[...], preferred_element_type=F32) * scale
    cos_t, sin_t = cos_ref[...], sin_ref[...]
    for h in range(MLA_HEADS):
        base = h * MLA_QK_PAD
        q_ref[:, base:base + MLA_NOPE] = q[:, base:base + MLA_NOPE].astype(BF16)
        q_ref[:, base + MLA_NOPE:base + MLA_QK_PAD] = _rope128(
            q[:, base + MLA_NOPE:base + MLA_QK_PAD], cos_t, sin_t).astype(BF16)


def _mla_kv_kernel(ckv_ref, g_ref, wk_ref, wv_ref, kr_ref, cos_ref, sin_ref, k_ref, v_ref):
    xn = _rms(ckv_ref[...], g_ref[...]).astype(BF16)
    kn = jnp.dot(xn, wk_ref[...], preferred_element_type=F32)
    v_ref[...] = jnp.dot(xn, wv_ref[...], preferred_element_type=F32).astype(BF16)
    kr = _rope128(kr_ref[:, 0:128], cos_ref[...], sin_ref[...]).astype(BF16)
    for h in range(MLA_HEADS):
        base = h * MLA_QK_PAD
        k_ref[:, base:base + MLA_NOPE] = kn[:, h * MLA_NOPE:(h + 1) * MLA_NOPE].astype(BF16)
        k_ref[:, base + MLA_NOPE:base + MLA_QK_PAD] = kr


def _mla_attn_kernel(q_ref, k_ref, v_ref, z_ref, o_ref):
    k, v = k_ref[...], v_ref[...]
    half = q_ref.shape[0] // 2
    for r in range(2):
        rows = pl.ds(r * half, half)
        s = lax.dot_general(q_ref[rows, :], k, (((1,), (1,)), ((), ())),
                            preferred_element_type=F32)
        p = jnp.exp2(s - jnp.max(s, axis=-1, keepdims=True))
        l = jnp.sum(p, axis=-1, keepdims=True)
        o = jnp.dot(p.astype(BF16), v, preferred_element_type=F32)
        o_ref[rows, :] = (o / l * z_ref[rows, :]).astype(BF16)


def _mla_branch(proj, proj_b, bsz, seq, rope, q_norm_g, w_uq, kv_norm_g, w_ukv):
    m = bsz * seq
    cos_t, sin_t = rope
    dqk = MLA_NOPE + MLA_ROPE
    hq = MLA_HEADS * MLA_QK_PAD
    w_q = w_uq.reshape(MLA_LORA, MLA_HEADS, dqk)
    w_q = jnp.pad(w_q, ((0, 0), (0, 0), (0, MLA_QK_PAD - dqk))).reshape(MLA_LORA, hq).astype(BF16)
    w_kv = w_ukv.reshape(MLA_LORA, MLA_HEADS, MLA_NOPE + MLA_V)
    w_k = w_kv[:, :, :MLA_NOPE].reshape(MLA_LORA, MLA_HEADS * MLA_NOPE).astype(BF16)
    w_v = w_kv[:, :, MLA_NOPE:].reshape(MLA_LORA, MLA_HEADS * MLA_V).astype(BF16)
    tm = min(512, m)
    row = lambda shape, col=0: pl.BlockSpec(shape, lambda i: (i, col))
    full = lambda shape: pl.BlockSpec(shape, lambda i: (0, 0))
    qp = pl.pallas_call(
        functools.partial(_mla_q_kernel, scale=dqk ** -0.5 * math.log2(math.e)),
        out_shape=jax.ShapeDtypeStruct((m, hq), BF16),
        grid=(m // tm,),
        in_specs=[row((tm, MLA_LORA), COL_CQ // MLA_LORA), full((1, MLA_LORA)), full((MLA_LORA, hq)),
                  row((tm, 128)), row((tm, 128))],
        out_specs=row((tm, hq)),
        compiler_params=_params("parallel"),
        name="mla_q",
    )(proj, q_norm_g.astype(F32).reshape(1, -1), w_q, cos_t, sin_t)
    kp, vp = pl.pallas_call(
        _mla_kv_kernel,
        out_shape=(jax.ShapeDtypeStruct((m, hq), BF16),
                   jax.ShapeDtypeStruct((m, MLA_HEADS * MLA_V), BF16)),
        grid=(m // tm,),
        in_specs=[row((tm, MLA_LORA), COL_CKV // MLA_LORA), full((1, MLA_LORA)),
                  full((MLA_LORA, MLA_HEADS * MLA_NOPE)), full((MLA_LORA, MLA_HEADS * MLA_V)),
                  row((tm, 512), COL_CKR // 512), row((tm, 128)), row((tm, 128))],
        out_specs=(row((tm, hq)), row((tm, MLA_HEADS * MLA_V))),
        compiler_params=_params("parallel"),
        name="mla_kv",
    )(proj, kv_norm_g.astype(F32).reshape(1, -1), w_k, w_v, proj_b, cos_t, sin_t)
    tq = min(512, seq)
    qt = seq // tq
    return pl.pallas_call(
        _mla_attn_kernel,
        out_shape=jax.ShapeDtypeStruct((m, MLA_HEADS * MLA_V), BF16),
        grid=(bsz, MLA_HEADS, qt),
        in_specs=[pl.BlockSpec((tq, MLA_QK_PAD), lambda b, h, i: (b * qt + i, h)),
                  pl.BlockSpec((seq, MLA_QK_PAD), lambda b, h, i: (b, h)),
                  pl.BlockSpec((seq, MLA_V), lambda b, h, i: (b, h)),
                  pl.BlockSpec((tq, MLA_V), lambda b, h, i: (b * qt + i, h + COL_CZ // MLA_V))],
        out_specs=pl.BlockSpec((tq, MLA_V), lambda b, h, i: (b * qt + i, h)),
        compiler_params=_params("parallel", "parallel", "arbitrary"),
        name="mla_attn",
    )(qp, kp, vp, proj_b)


def _lift_kernel(ya_ref, yb_ref, yc_ref, w_ref, ga_ref, gb_ref, gc_ref, o_ref, wb_ref):
    @pl.when(pl.program_id(1) == 0)
    def _():
        wb_ref[...] = w_ref[...].astype(BF16)

    acc = ga_ref[...] * jnp.dot(ya_ref[...], wb_ref[0], preferred_element_type=F32)
    acc += gb_ref[...] * jnp.dot(yb_ref[...].astype(BF16), wb_ref[1], preferred_element_type=F32)
    acc += gc_ref[...] * jnp.dot(yc_ref[...], wb_ref[2], preferred_element_type=F32)
    o_ref[...] = acc.astype(BF16)


def _lift(ya, yb, yc, w_lift, layer, proj_b):
    m = ya.shape[0]
    tm = min(1024, m)
    tn = 512
    y_spec = pl.BlockSpec((tm, BRANCH_W), lambda j, i: (i, 0))
    gate = lambda n: pl.BlockSpec((tm, tn), lambda j, i: (i, j + (COL_GATE + n * D_MODEL) // tn))
    return pl.pallas_call(
        _lift_kernel,
        out_shape=jax.ShapeDtypeStruct((m, D_MODEL), BF16),
        grid=(D_MODEL // tn, m // tm),
        in_specs=[y_spec, y_spec, y_spec,
                  pl.BlockSpec((None, N_BRANCH, BRANCH_W, tn), lambda j, i: (layer, 0, 0, j)),
                  gate(0), gate(1), gate(2)],
        out_specs=pl.BlockSpec((tm, tn), lambda j, i: (i, j)),
        scratch_shapes=[pltpu.VMEM((N_BRANCH, BRANCH_W, tn), BF16)],
        compiler_params=_params("parallel", "arbitrary"),
        name="lift",
    )(ya, yb, yc, w_lift, proj_b, proj_b, proj_b)


def _out_kernel(mix_ref, wo_ref, p_ref, wp_ref, sp_ref, x_ref, g_ref, b_ref, o_ref):
    mixed = jnp.dot(mix_ref[...], wo_ref[...], preferred_element_type=F32)
    ple = jnp.dot(p_ref[...].astype(BF16), wp_ref[...], preferred_element_type=F32) * sp_ref[...]
    r = DEEPNORM_ALPHA * x_ref[...] + mixed + ple
    mu = jnp.mean(r, axis=-1, keepdims=True)
    var = jnp.mean(jnp.square(r - mu), axis=-1, keepdims=True)
    o_ref[...] = (r - mu) * lax.rsqrt(var + LN_EPS) * g_ref[...] + b_ref[...]


def _out_norm(mix, w_out, p, w_ple, proj, x, ln_g, ln_b):
    m = mix.shape[0]
    tm = min(512, m)
    row = lambda w, col=0: pl.BlockSpec((tm, w), lambda i: (i, col))
    full = lambda shape: pl.BlockSpec(shape, lambda i: (0, 0))
    return pl.pallas_call(
        _out_kernel,
        out_shape=jax.ShapeDtypeStruct((m, D_MODEL), F32),
        grid=(m // tm,),
        in_specs=[row(D_MODEL), full((D_MODEL, D_MODEL)), row(PLE_DIM), full((PLE_DIM, D_MODEL)),
                  row(D_MODEL, COL_PLE // D_MODEL), row(D_MODEL), full((1, D_MODEL)), full((1, D_MODEL))],
        out_specs=row(D_MODEL),
        compiler_params=_params("parallel"),
        name="out_norm",
    )(mix, w_out.astype(BF16), p, w_ple.astype(BF16), proj, x,
      ln_g.astype(F32).reshape(1, -1), ln_b.astype(F32).reshape(1, -1))


def kernel(x, p, positions, w_in, s5_lambda_re, s5_lambda_im, s5_log_dt, s5_b_re, s5_b_im, s5_c_re, s5_c_im, s5_d, s5_w_glu, s5_b_glu, hy_conv_w, hy_conv_b, hy_w1, hy_b1, hy_w2, hy_b2, hy_freq, hy_w3, hy_b3, hy_bias, mla_q_norm, mla_w_uq, mla_kv_norm, mla_w_ukv, w_lift, w_out, w_ple, ln_g, ln_b):
    bsz, seq, _ = x.shape
    m = bsz * seq
    depth = w_in.shape[0]
    dft = _dft_matrices(seq // 2)
    rope = _rope_tables(positions)
    s5_tiled = _s5_tiled_params(s5_lambda_re, s5_lambda_im, s5_log_dt, s5_b_re, s5_b_im,
                                s5_c_re, s5_c_im, s5_d)
    xf = x.reshape(m, D_MODEL).astype(F32)
    w_t = jnp.swapaxes(w_in, 1, 2)
    for i in range(depth):
        proj, proj_b = _proj_both(xf, w_t, i)
        y_a = _s5_branch(proj, bsz, seq, [a[i] for a in s5_tiled], s5_w_glu[i], s5_b_glu[i])
        y_b = _hyena_branch(proj, bsz, seq, dft, hy_conv_w[i], hy_conv_b[i], hy_w1[i], hy_b1[i],
                            hy_w2[i], hy_b2[i], hy_freq[i], hy_w3[i], hy_b3[i], hy_bias[i])
        y_c = _mla_branch(proj, proj_b, bsz, seq, rope, mla_q_norm[i], mla_w_uq[i], mla_kv_norm[i], mla_w_ukv[i])
        mix = _lift(y_a, y_b, y_c, w_lift, i, proj_b)
        xf = _out_norm(mix, w_out[i], p[i].reshape(m, PLE_DIM), w_ple[i], proj_b, xf, ln_g[i], ln_b[i])
    return xf.reshape(bsz, seq, D_MODEL).astype(x.dtype)
```

```python
import functools
import math

import numpy as np
import jax
import jax.numpy as jnp
from jax import lax
from jax.experimental import pallas as pl
from jax.experimental.pallas import tpu as pltpu

F32 = jnp.float32
BF16 = jnp.bfloat16

D_MODEL = 2048
PLE_DIM = 256
N_BRANCH = 3
BRANCH_W = 1024

S5_GROUP = 16
S5_GROUPS = BRANCH_W // S5_GROUP
S5_STATE = 64
S5_CHUNK = 16
S5_ROW = S5_CHUNK * S5_GROUP

HY_EMB = 33
HY_FF = 64
HY_PAD = 128
HY_DECAY_TARGET = 0.01
HY_FAST_DECAY = 0.3
HY_SLOW_DECAY = 1.5
DFT_ROWS = 64

MLA_HEADS = 8
MLA_NOPE = 128
MLA_ROPE = 64
MLA_V = 128
MLA_LORA = 512
MLA_QK_PAD = 256
ROPE_BASE = 10000.0

LN_EPS = 1e-5
RMS_EPS = 1e-6
DEPTH = 2
DEEPNORM_ALPHA = (2 * DEPTH) ** 0.25

COL_AX = 0
COL_AZ = 1024
COL_BU = 2048
COL_BZ = 5120
COL_CQ = 6144
COL_CKV = 6656
PROJ_A_N = 7168
COL_PLE = 0
COL_GATE = 2048
COL_CZ = 8192
COL_CKR = 9216
PROJ_B_N = 9728
PROJ_TN = 512
W_IN_CKR = 7168
W_IN_CZ = 7232
W_IN_GATE = 8256
W_IN_PLE = 14400

VMEM_LIMIT = 56 * 1024 * 1024


def _params(*sem):
    return pltpu.CompilerParams(dimension_semantics=sem, vmem_limit_bytes=VMEM_LIMIT)


def _sigmoid(x):
    return 0.5 * jnp.tanh(0.5 * x) + 0.5


def _in_tiles(j, ranges):
    hit = None
    for lo, hi in ranges:
        cur = (j >= lo // PROJ_TN) & (j < hi // PROJ_TN)
        hit = cur if hit is None else (hit | cur)
    return hit


def _proj_kernel(x_ref, w_ref, o_ref, xb_ref, *, silu_cols, sigm_cols):
    j = pl.program_id(1)

    @pl.when(j == 0)
    def _():
        xb_ref[...] = x_ref[...].astype(BF16)

    acc = lax.dot_general(xb_ref[...], w_ref[...].astype(BF16), (((1,), (1,)), ((), ())),
                          preferred_element_type=F32)
    is_silu = _in_tiles(j, silu_cols)
    plain = jnp.logical_not(is_silu)

    @pl.when(is_silu)
    def _():
        o_ref[...] = acc * _sigmoid(acc)

    if sigm_cols:
        is_sigm = _in_tiles(j, sigm_cols)
        plain = jnp.logical_not(is_silu | is_sigm)

        @pl.when(is_sigm)
        def _():
            o_ref[...] = _sigmoid(acc)

    @pl.when(plain)
    def _():
        o_ref[...] = acc


def _proj(x, w, layer, n_out, silu_cols, sigm_cols, name):
    m, k = x.shape
    tm = min(1024, m)
    return pl.pallas_call(
        functools.partial(_proj_kernel, silu_cols=silu_cols, sigm_cols=sigm_cols),
        out_shape=jax.ShapeDtypeStruct((m, n_out), F32),
        grid=(m // tm, n_out // PROJ_TN),
        in_specs=[pl.BlockSpec((tm, k), lambda i, j: (i, 0)),
                  pl.BlockSpec((None, PROJ_TN, k), lambda i, j: (layer, j, 0))],
        out_specs=pl.BlockSpec((tm, PROJ_TN), lambda i, j: (i, j)),
        scratch_shapes=[pltpu.VMEM((tm, k), BF16)],
        compiler_params=_params("parallel", "arbitrary"),
        name=name,
    )(x, w)


def _repack_src_tile(j):
    t = lambda col: col // PROJ_TN
    return jnp.where(j < t(COL_GATE), j + t(W_IN_PLE),
                     jnp.where(j < t(COL_CZ), j - t(COL_GATE) + t(W_IN_GATE),
                               jnp.where(j < t(COL_CKR), j - t(COL_CZ) + t(W_IN_CZ), t(W_IN_CKR))))


def _repack_kernel(a_ref, b_ref, o_ref):
    is_last = pl.program_id(0) == pl.num_programs(0) - 1
    a = a_ref[...]
    shifted = jnp.concatenate([a[MLA_ROPE:], b_ref[...]], axis=0)
    plain = jnp.concatenate([a[:MLA_ROPE], jnp.zeros_like(a[MLA_ROPE:])], axis=0)
    o_ref[...] = jnp.where(is_last, plain, shifted).astype(BF16)


def _repack_b(w_t, layer):
    k = w_t.shape[2]
    sub = PROJ_TN // MLA_ROPE
    return pl.pallas_call(
        _repack_kernel,
        out_shape=jax.ShapeDtypeStruct((1, PROJ_B_N, k), BF16),
        grid=(PROJ_B_N // PROJ_TN,),
        in_specs=[pl.BlockSpec((None, PROJ_TN, k), lambda j: (layer, _repack_src_tile(j), 0)),
                  pl.BlockSpec((None, MLA_ROPE, k), lambda j: (layer, (_repack_src_tile(j) + 1) * sub, 0))],
        out_specs=pl.BlockSpec((None, PROJ_TN, k), lambda j: (0, j, 0)),
        compiler_params=_params("parallel"),
        name="repack_b",
    )(w_t, w_t)


def _proj_both(x, w_t, layer):
    proj_a = _proj(x, w_t, layer, PROJ_A_N, ((COL_AZ, COL_BU), (COL_BZ, COL_CQ)), (), "proj_a")
    proj_b = _proj(x, _repack_b(w_t, layer), 0, PROJ_B_N, ((COL_CZ, COL_CKR),), ((COL_PLE, COL_CZ),),
                   "proj_b")
    return proj_a, proj_b


S5_LANES = 8 * S5_STATE
S5_SLAB = 8
S5_RELAYOUT_ROWS = 32


def _s5_tile_lanes(a):
    return jnp.concatenate([a[:, 0]] * 4 + [a[:, 1]] * 4, axis=-1)


def _s5_tiled_params(lam_re, lam_im, log_dt, b_re, b_im, c_re, c_im, d):
    f = lambda a: a.astype(F32)
    depth = lam_re.shape[0]
    ldt = jnp.broadcast_to(f(log_dt)[..., None, None], lam_re.shape[:3] + (1, S5_STATE))
    return (_s5_tile_lanes(f(lam_re)[:, :, :, None, :]), _s5_tile_lanes(f(lam_im)[:, :, :, None, :]),
            _s5_tile_lanes(ldt),
            _s5_tile_lanes(jnp.swapaxes(f(b_re), -1, -2)), _s5_tile_lanes(jnp.swapaxes(f(b_im), -1, -2)),
            _s5_tile_lanes(f(c_re)), _s5_tile_lanes(f(c_im)),
            f(d).reshape(depth, S5_GROUPS, S5_GROUP, 1))


def _s5_mats_kernel(lr_ref, li_ref, ldt_ref, br_ref, bi_ref, cr_ref, ci_ref, d_ref,
                    ms_ref, mi_ref, mo_ref, ar_ref, ai_ref):
    t_n, h_n = S5_CHUNK, S5_GROUP
    hi = lax.Precision.HIGHEST
    nt = (((1,), (1,)), ((), ()))
    blk = lax.broadcasted_iota(jnp.int32, (1, S5_LANES), 1) // S5_STATE
    is_im = (blk // 2) % 2 == 1
    is_fwd = blk < 4
    steps = lax.broadcasted_iota(jnp.int32, (24, S5_LANES), 0).astype(F32)
    lane_k = lax.broadcasted_iota(jnp.int32, (h_n, S5_ROW), 1)
    sub_k = lax.broadcasted_iota(jnp.int32, (h_n, S5_ROW), 0)

    def per_group(gi, carry):
        lr, li = lr_ref[gi], li_ref[gi]
        dt = jnp.exp(ldt_ref[gi])
        zr, zi = lr * dt, li * dt
        mag = jnp.exp(steps * zr)
        tr, ti = mag * jnp.cos(steps * zi), mag * jnp.sin(steps * zi)
        lbr, lbi = tr[1:2], ti[1:2]
        n2 = lr * lr + li * li
        qr = ((lbr - 1.0) * lr + lbi * li) / n2
        qi = (lbi * lr - (lbr - 1.0) * li) / n2
        br, bi = br_ref[gi], bi_ref[gi]
        bbr, bbi = qr * br - qi * bi, qr * bi + qi * br
        y1, y2 = jnp.where(is_im, bbi, bbr), jnp.where(is_im, bbr, bbi)
        cr, ci = cr_ref[gi], ci_ref[gi]

        def pick(tab, t_fwd, t_bwd):
            return jnp.where(is_fwd, tab[t_fwd:t_fwd + 1], tab[t_bwd:t_bwd + 1])

        def c_times(p_r, p_i):
            return jnp.where(is_im, -(cr * p_i + ci * p_r), cr * p_r - ci * p_i)

        ct = [c_times(tr[s:s + 1], ti[s:s + 1]) for s in range(t_n + 1)]
        q_rows = []
        for t in range(t_n):
            a_r, a_i = pick(tr, t_n - 1 - t, t), pick(ti, t_n - 1 - t, t)
            rows = pl.ds(t * h_n, h_n)
            ms_ref[gi, rows, :] = (a_r * y1 + jnp.where(is_im, a_i, -a_i) * y2).astype(BF16)
            mo_ref[gi, rows, :] = jnp.where(is_fwd, ct[t + 1], ct[t_n - t]).astype(BF16)
            q_rows.append(jnp.where(is_fwd, ct[t], ct[t_n - 1 - t]))
        q = jnp.concatenate(q_rows, axis=0)
        half = S5_LANES // 2
        kf = 0.5 * lax.dot_general(y1[:, :half], q[:, :half], nt, precision=hi, preferred_element_type=F32)
        kb = 0.5 * lax.dot_general(y1[:, half:], q[:, half:], nt, precision=hi, preferred_element_type=F32)
        kf = kf + jnp.where(lane_k == sub_k, d_ref[gi], 0.0)
        for t in range(t_n):
            fwd = kf if t == 0 else jnp.where(lane_k >= h_n * t, pltpu.roll(kf, h_n * t, 1), 0.0)
            sh = h_n * (t_n - 1 - t)
            bwd = kb if sh == 0 else jnp.where(lane_k < S5_ROW - sh, pltpu.roll(kb, S5_ROW - sh, 1), 0.0)
            mi_ref[gi, pl.ds(t * h_n, h_n), :] = (fwd + bwd).astype(BF16)
        ar_ref[gi] = tr[t_n:t_n + 1]
        ai_ref[gi] = ti[t_n:t_n + 1]
        return carry

    lax.fori_loop(0, lr_ref.shape[0], per_group, 0)


def _s5_mats(tiled):
    g_n = S5_GROUPS
    gb = S5_SLAB
    spec = lambda r, w: pl.BlockSpec((gb, r, w), lambda j: (j, 0, 0))
    in_rows = (1, 1, 1, S5_GROUP, S5_GROUP, S5_GROUP, S5_GROUP)
    return pl.pallas_call(
        _s5_mats_kernel,
        out_shape=(jax.ShapeDtypeStruct((g_n, S5_ROW, S5_LANES), BF16),
                   jax.ShapeDtypeStruct((g_n, S5_ROW, S5_ROW), BF16),
                   jax.ShapeDtypeStruct((g_n, S5_ROW, S5_LANES), BF16),
                   jax.ShapeDtypeStruct((g_n, 1, S5_LANES), F32),
                   jax.ShapeDtypeStruct((g_n, 1, S5_LANES), F32)),
        grid=(g_n // gb,),
        in_specs=[spec(r, S5_LANES) for r in in_rows] + [spec(S5_GROUP, 1)],
        out_specs=(spec(S5_ROW, S5_LANES), spec(S5_ROW, S5_ROW), spec(S5_ROW, S5_LANES),
                   spec(1, S5_LANES), spec(1, S5_LANES)),
        compiler_params=_params("parallel"),
        name="s5_mats",
    )(*tiled)


def _seg_transpose(vs, seg):
    vs = list(vs)
    for s in (4, 2, 1):
        keep = (seg & s) == 0
        for i in range(8):
            if i & s:
                continue
            a, b = vs[i], vs[i + s]
            vs[i] = jnp.where(keep, a, pltpu.roll(b, s * S5_GROUP, 1))
            vs[i + s] = jnp.where(keep, pltpu.roll(a, 128 - s * S5_GROUP, 1), b)
    return vs


def _s5_main_kernel(x_ref, ms_ref, mi_ref, mo_ref, ar_ref, ai_ref, y_ref,
                    u_ref, sl_ref, st_ref, yg_ref, *, bsz):
    rows = x_ref.shape[0]
    n_chunks = rows // bsz
    rc = S5_RELAYOUT_ROWS
    seg = lax.broadcasted_iota(jnp.int32, (rc, 128), 1) // S5_GROUP
    slot = (lax.broadcasted_iota(jnp.int32, (1, S5_LANES), 1) // S5_STATE) % 2
    slot128 = slot[:, :128]

    def relayout_in(r, carry):
        r0 = pl.multiple_of(r * rc, rc)
        for th in range(2):
            src = [x_ref[pl.ds(r0, rc), th * 8 + t8, :] for t8 in range(8)]
            for gi, out in enumerate(_seg_transpose(src, seg)):
                u_ref[gi, pl.ds(r0, rc), th * 128:(th + 1) * 128] = out.astype(BF16)
        return carry

    lax.fori_loop(0, rows // rc, relayout_in, 0, unroll=2)

    n_pairs = S5_SLAB // 2
    trans = []
    for jp in range(n_pairs):
        g0, g1 = 2 * jp, 2 * jp + 1
        r0 = jnp.dot(u_ref[g0], ms_ref[g0], preferred_element_type=F32)
        r1 = jnp.dot(u_ref[g1], ms_ref[g1], preferred_element_type=F32)
        loc = jnp.where(slot == 0, r0, r1)
        for k in range(4):
            for b in range(bsz):
                sl_ref[k, jp, pl.ds(b, n_chunks, stride=bsz), :] = (
                    loc[b * n_chunks:(b + 1) * n_chunks, k * 128:(k + 1) * 128])
        trans.append([jnp.where(slot128 == 0, ref[g0][:, off:off + 128], ref[g1][:, off:off + 128])
                      for ref, off in ((ar_ref, 0), (ai_ref, 0), (ar_ref, 256), (ai_ref, 256))])

    zero = jnp.zeros((bsz, 128), F32)

    def scan(c, carry):
        rf = pl.ds(pl.multiple_of(c * bsz, bsz), bsz)
        rb = pl.ds(pl.multiple_of((n_chunks - 1 - c) * bsz, bsz), bsz)
        new = []
        for jp in range(n_pairs):
            s_fr, s_fi, s_br, s_bi = carry[jp]
            a_fr, a_fi, a_br, a_bi = trans[jp]
            st_ref[0, jp, rf, :] = s_fr
            st_ref[1, jp, rf, :] = s_fi
            st_ref[2, jp, rb, :] = s_br
            st_ref[3, jp, rb, :] = s_bi
            new.append((a_fr * s_fr - a_fi * s_fi + sl_ref[0, jp, rf, :],
                        a_fr * s_fi + a_fi * s_fr + sl_ref[1, jp, rf, :],
                        a_br * s_br - a_bi * s_bi + sl_ref[2, jp, rb, :],
                        a_br * s_bi + a_bi * s_br + sl_ref[3, jp, rb, :]))
        return tuple(new)

    lax.fori_loop(0, n_chunks, scan, tuple((zero,) * 4 for _ in range(n_pairs)))

    nt = (((1,), (1,)), ((), ()))
    for jp in range(n_pairs):
        st = jnp.concatenate(
            [jnp.concatenate([st_ref[k, jp, pl.ds(b, n_chunks, stride=bsz), :] for b in range(bsz)], axis=0)
             for k in range(4)], axis=1)
        for e in range(2):
            g = 2 * jp + e
            st_g = jnp.where(slot == e, st, 0.0).astype(BF16)
            y = (jnp.dot(u_ref[g], mi_ref[g], preferred_element_type=F32)
                 + lax.dot_general(st_g, mo_ref[g], nt, preferred_element_type=F32))
            yg_ref[g] = jax.nn.gelu(y)

    def relayout_out(r, carry):
        r0 = pl.multiple_of(r * rc, rc)
        for th in range(2):
            src = [yg_ref[gi, pl.ds(r0, rc), th * 128:(th + 1) * 128] for gi in range(S5_SLAB)]
            for t8, out in enumerate(_seg_transpose(src, seg)):
                y_ref[pl.ds(r0, rc), th * 8 + t8, :] = out
        return carry

    lax.fori_loop(0, rows // rc, relayout_out, 0, unroll=2)


def _s5_glu_kernel(g_ref, w_ref, b_ref, z_ref, o_ref):
    g = g_ref[...]
    acc = jnp.dot(g.astype(BF16), w_ref[...], preferred_element_type=F32) + b_ref[...]
    o_ref[...] = (g * _sigmoid(acc) * z_ref[...]).astype(o_ref.dtype)


def _s5_branch(proj, bsz, seq, tiled, w_glu, b_glu):
    m_state, m_intra, m_out, a_re, a_im = _s5_mats(tiled)
    t_n = S5_CHUNK
    rows = bsz * (seq // t_n)
    m = bsz * seq
    gb = S5_SLAB
    x3 = proj.reshape(rows, t_n, proj.shape[1])
    mat = lambda r, w: pl.BlockSpec((gb, r, w), lambda s: (s, 0, 0))
    io_spec = pl.BlockSpec((rows, t_n, 128), lambda s: (0, 0, s + COL_AX // 128))
    y = pl.pallas_call(
        functools.partial(_s5_main_kernel, bsz=bsz),
        out_shape=jax.ShapeDtypeStruct((rows, t_n, BRANCH_W), F32),
        grid=(S5_GROUPS // gb,),
        in_specs=[io_spec, mat(S5_ROW, S5_LANES), mat(S5_ROW, S5_ROW), mat(S5_ROW, S5_LANES),
                  mat(1, S5_LANES), mat(1, S5_LANES)],
        out_specs=pl.BlockSpec((rows, t_n, 128), lambda s: (0, 0, s)),
        scratch_shapes=[pltpu.VMEM((gb, rows, S5_ROW), BF16),
                        pltpu.VMEM((4, gb // 2, rows, 128), F32),
                        pltpu.VMEM((4, gb // 2, rows, 128), F32),
                        pltpu.VMEM((gb, rows, S5_ROW), F32)],
        compiler_params=_params("parallel"),
        name="s5_main",
    )(x3, m_state, m_intra, m_out, a_re, a_im)
    y = y.reshape(m, BRANCH_W)
    tm = min(512, m)
    return pl.pallas_call(
        _s5_glu_kernel,
        out_shape=jax.ShapeDtypeStruct((m, BRANCH_W), BF16),
        grid=(m // tm,),
        in_specs=[pl.BlockSpec((tm, BRANCH_W), lambda i: (i, 0)),
                  pl.BlockSpec((BRANCH_W, BRANCH_W), lambda i: (0, 0)),
                  pl.BlockSpec((1, BRANCH_W), lambda i: (0, 0)),
                  pl.BlockSpec((tm, BRANCH_W), lambda i: (i, COL_AZ // BRANCH_W))],
        out_specs=pl.BlockSpec((tm, BRANCH_W), lambda i: (i, 0)),
        compiler_params=_params("parallel"),
        name="s5_glu",
    )(y, w_glu.astype(BF16), b_glu.astype(F32).reshape(1, BRANCH_W), proj)


def _dft_tables(seq):
    n = 2 * seq
    mm = np.arange(seq, dtype=np.int64)
    k1 = np.arange(seq // DFT_ROWS, dtype=np.int64)[:, None] * DFT_ROWS
    k0 = np.arange(DFT_ROWS, dtype=np.int64)[:, None]
    ang_a = 2.0 * np.pi * ((k1 * mm) % n).astype(np.float64) / n
    ang_b = 2.0 * np.pi * ((k0 * mm) % n).astype(np.float64) / n
    return tuple(jnp.asarray(t, F32) for t in (np.cos(ang_a), np.sin(ang_a), np.cos(ang_b), np.sin(ang_b)))


def _dft_gen_kernel(ac_ref, as_ref, bc_ref, bs_ref, c_ref, s1_ref, s2_ref):
    i = pl.program_id(0)
    a_c = ac_ref[pl.ds(i, 1), :]
    a_s = as_ref[pl.ds(i, 1), :]
    b_c, b_s = bc_ref[...], bs_ref[...]
    cos_t = a_c * b_c - a_s * b_s
    sin_t = a_s * b_c + a_c * b_s
    rows = lax.broadcasted_iota(jnp.int32, cos_t.shape, 0) + i * DFT_ROWS
    cols = lax.broadcasted_iota(jnp.int32, cos_t.shape, 1)
    alt_cols = jnp.where((cols & 1) == 0, 1.0, -1.0).astype(F32)
    alt_rows = jnp.where((rows & 1) == 0, 1.0, -1.0).astype(F32)
    c_ref[...] = cos_t.astype(BF16)
    s1_ref[...] = jnp.where(rows == 0, alt_cols, sin_t).astype(BF16)
    s2_ref[...] = jnp.where(cols == 0, alt_rows, sin_t).astype(BF16)


def _dft_matrices(seq):
    tabs = _dft_tables(seq)
    n_steps = seq // DFT_ROWS
    tab_spec = pl.BlockSpec(tabs[0].shape, lambda i: (0, 0))
    b_spec = pl.BlockSpec((DFT_ROWS, seq), lambda i: (0, 0))
    o_spec = pl.BlockSpec((DFT_ROWS, seq), lambda i: (i, 0))
    return pl.pallas_call(
        _dft_gen_kernel,
        out_shape=(jax.ShapeDtypeStruct((seq, seq), BF16),) * 3,
        grid=(n_steps,),
        in_specs=[tab_spec, tab_spec, b_spec, b_spec],
        out_specs=(o_spec,) * 3,
        compiler_params=_params("parallel"),
        name="dft_gen",
    )(*tabs)


def _hy_filter_kernel(feat_ref, w1_ref, b1_ref, w2_ref, b2_ref, f0_ref, f1_ref,
                      w3p_ref, w3n_ref, b3p_ref, b3n_ref, dl_ref, t_ref,
                      hs_ref, hd_ref, hp_ref, hn_ref, r0_ref, h_ref, split_ref, *, inv_n):
    hi = lax.Precision.HIGHEST

    @pl.when(pl.program_id(0) == 0)
    def _():
        h1 = jnp.sin(f0_ref[...] * (jnp.dot(feat_ref[...], w1_ref[...], precision=hi,
                                            preferred_element_type=F32) + b1_ref[...]))
        h_ref[...] = jnp.sin(f1_ref[...] * (jnp.dot(h1, w2_ref[...], precision=hi,
                                                    preferred_element_type=F32) + b2_ref[...]))

    h = h_ref[...].astype(BF16)
    win = jnp.exp(-t_ref[...] * jnp.abs(dl_ref[...]))
    hpos = (jnp.dot(h, w3p_ref[...].astype(BF16), preferred_element_type=F32) + b3p_ref[...]) * win
    hneg = (jnp.dot(h, w3n_ref[...].astype(BF16), preferred_element_type=F32) + b3n_ref[...]) * win
    rows = lax.broadcasted_iota(jnp.int32, hpos.shape, 0)
    hneg = jnp.where(rows == 0, 0.0, hneg)
    norm = (jnp.sum(jnp.abs(hpos), axis=0, keepdims=True)
            + jnp.sum(jnp.abs(hneg), axis=0, keepdims=True))
    hpos = hpos / norm
    hneg = hneg / norm
    hsum = hpos + hneg
    hdiff = hpos - hneg
    even = (rows & 1) == 0
    alt2 = jnp.where(((rows >> 1) & 1) == 0, 1.0, -1.0).astype(F32)
    col_sum = lambda a: jnp.sum(a, axis=0, keepdims=True)
    a0 = 2.0 * col_sum(jnp.where(even, hsum, 0.0))
    d0 = 2.0 * col_sum(jnp.where(even, 0.0, hsum))
    hr2 = 2.0 * col_sum(jnp.where(even, alt2 * hsum, 0.0))
    hi2 = -2.0 * col_sum(jnp.where(even, 0.0, alt2 * hdiff))
    r0_ref[...] = jnp.concatenate([a0, d0, hr2, hi2, jnp.zeros((4, a0.shape[1]), F32)], axis=0) * inv_n
    def lags(x, par):
        _stage_rows(split_ref, x)
        return _rows_of_parity(split_ref, par).astype(BF16)

    hs_ref[...] = lags(hsum, 0)
    hd_ref[...] = lags(hdiff, 0)
    hp_ref[...] = lags(hpos, 1)
    hn_ref[...] = lags(hneg, 1)


def _hy_filter_taps(seq, w1, b1, w2, b2, freq, w3, b3):
    n_ch = 2 * BRANCH_W
    bands = (HY_EMB - 1) // 2
    t = jnp.linspace(0.0, 1.0, seq, dtype=F32)[:, None]
    w = 2.0 * math.pi * jnp.arange(seq, dtype=F32)[:, None] / seq
    f = jnp.linspace(1e-4, bands - 1, bands, dtype=F32)[None, :]
    feats = jnp.concatenate([t, jnp.cos(f * w), -jnp.sin(f * w),
                             jnp.zeros((seq, HY_PAD - HY_EMB), F32)], axis=-1)
    deltas = jnp.linspace(math.log(HY_DECAY_TARGET) / HY_SLOW_DECAY,
                          math.log(HY_DECAY_TARGET) / HY_FAST_DECAY, n_ch, dtype=F32)[None, :]

    def pad2(a, r, c):
        a = a.astype(F32)
        return jnp.pad(a, ((0, r - a.shape[0]), (0, c - a.shape[1])))

    w1p = pad2(w1, HY_PAD, HY_PAD)
    w2p = pad2(w2, HY_PAD, HY_PAD)
    b1p = pad2(b1[None], 1, HY_PAD)
    b2p = pad2(b2[None], 1, HY_PAD)
    f0p = pad2(freq[0][None], 1, HY_PAD)
    f1p = pad2(freq[1][None], 1, HY_PAD)
    w3p = pad2(w3, HY_PAD, 2 * n_ch)
    b3r = b3.astype(F32)[None]
    tn = 256
    nt = n_ch // tn
    full = lambda shape: pl.BlockSpec(shape, lambda j: (0, 0))
    tap_spec = pl.BlockSpec((seq // 2, tn), lambda j: (0, j))
    return pl.pallas_call(
        functools.partial(_hy_filter_kernel, inv_n=1.0 / (2 * seq)),
        out_shape=(jax.ShapeDtypeStruct((seq // 2, n_ch), BF16),) * 4
                  + (jax.ShapeDtypeStruct((8, n_ch), F32),),
        grid=(nt,),
        in_specs=[full((seq, HY_PAD)), full((HY_PAD, HY_PAD)), full((1, HY_PAD)),
                  full((HY_PAD, HY_PAD)), full((1, HY_PAD)), full((1, HY_PAD)), full((1, HY_PAD)),
                  pl.BlockSpec((HY_PAD, tn), lambda j: (0, j)),
                  pl.BlockSpec((HY_PAD, tn), lambda j: (0, j + nt)),
                  pl.BlockSpec((1, tn), lambda j: (0, j)),
                  pl.BlockSpec((1, tn), lambda j: (0, j + nt)),
                  pl.BlockSpec((1, tn), lambda j: (0, j)),
                  full((seq, 1))],
        out_specs=(tap_spec,) * 4 + (pl.BlockSpec((8, tn), lambda j: (0, j)),),
        scratch_shapes=[pltpu.VMEM((seq, HY_PAD), F32), pltpu.VMEM((tn // 128, seq, 128), F32)],
        compiler_params=_params("arbitrary"),
        name="hy_filter",
    )(feats, w1p, b1p, w2p, b2p, f0p, f1p, w3p, w3p, b3r, b3r, deltas, t)


def _hy_spectrum_kernel(c_ref, s_ref, hse_ref, hde_ref, hpo_ref, hno_ref,
                        ac_ref, as_ref, bc_ref, bs_ref, gc_ref, gs_ref, *, n_half):
    i = pl.program_id(1)
    dot = lambda w, h: jnp.dot(w[...], h[...], preferred_element_type=F32)
    hec, hes = dot(c_ref, hse_ref), dot(s_ref, hde_ref)
    upc, ups = dot(c_ref, hpo_ref), dot(s_ref, hpo_ref)
    umc, ums = dot(c_ref, hno_ref), dot(s_ref, hno_ref)
    tm = hec.shape[0]
    k = (lax.broadcasted_iota(jnp.int32, (tm, 128), 0) + i * tm).astype(F32)
    psi = k * (math.pi / n_half)
    reps = hec.shape[1] // 128
    cp = jnp.concatenate([jnp.cos(psi)] * reps, axis=1)
    sp = jnp.concatenate([jnp.sin(psi)] * reps, axis=1)
    w = 1.0 / n_half
    ac_ref[...] = w * hec
    as_ref[...] = w * hes
    bc_ref[...] = w * (cp * upc - sp * ups + umc)
    bs_ref[...] = w * (cp * ups + sp * upc - ums)
    gc_ref[...] = w * (upc + cp * umc - sp * ums)
    gs_ref[...] = w * (ups - cp * ums - sp * umc)


def _hy_spectrum(cm, s1, taps):
    half, n_ch = taps[0].shape
    tm = min(512, half)
    tn = 512
    w_spec = pl.BlockSpec((tm, half), lambda j, i: (i, 0))
    tap_spec = pl.BlockSpec((half, tn), lambda j, i: (0, j))
    o_spec = pl.BlockSpec((tm, tn), lambda j, i: (i, j))
    return pl.pallas_call(
        functools.partial(_hy_spectrum_kernel, n_half=half),
        out_shape=(jax.ShapeDtypeStruct((half, n_ch), F32),) * 6,
        grid=(n_ch // tn, half // tm),
        in_specs=[w_spec, w_spec] + [tap_spec] * 4,
        out_specs=(o_spec,) * 6,
        compiler_params=_params("parallel", "arbitrary"),
        name="hy_spectrum",
    )(cm, s1, *taps)


def _stage_rows(scr_ref, x):
    for c in range(x.shape[1] // 128):
        scr_ref[c] = x[:, c * 128:(c + 1) * 128]


def _rows_of_parity(scr_ref, par):
    n = scr_ref.shape[1] // 2
    return jnp.concatenate([scr_ref[c, pl.ds(par, n, stride=2), :] for c in range(scr_ref.shape[0])], axis=1)


def _hy_conv3_kernel(u_ref, w_ref, b_ref, o_ref, *rest):
    vb_ref, s_ref = rest if len(rest) == 2 else (None, rest[0])
    u = u_ref[...]
    n = u.shape[0]
    rows = lax.broadcasted_iota(jnp.int32, u.shape, 0)
    prev = jnp.where(rows == 0, 0.0, pltpu.roll(u, 1, 0))
    nxt = jnp.where(rows == n - 1, 0.0, pltpu.roll(u, n - 1, 0))
    w = w_ref[...]
    _stage_rows(s_ref, prev * w[0:1] + u * w[1:2] + nxt * w[2:3] + b_ref[...])
    for par in range(2):
        part = _rows_of_parity(s_ref, par)
        o_ref[par] = part
        if vb_ref is not None:
            vb_ref[par] = part.astype(BF16)


def _hy_conv3(proj, bsz, seq, conv_w, conv_b, col, width, with_bf16):
    half = seq // 2
    tn = 256
    o_spec = pl.BlockSpec((2, half, tn), lambda b, j: (0, b, j))
    shape = lambda dt: jax.ShapeDtypeStruct((2, bsz * half, width), dt)
    return pl.pallas_call(
        _hy_conv3_kernel,
        out_shape=(shape(F32), shape(BF16)) if with_bf16 else shape(F32),
        grid=(bsz, width // tn),
        in_specs=[pl.BlockSpec((seq, tn), lambda b, j: (b, j + (COL_BU + col) // tn)),
                  pl.BlockSpec((3, tn), lambda b, j: (0, j + col // tn)),
                  pl.BlockSpec((1, tn), lambda b, j: (0, j + col // tn))],
        out_specs=(o_spec, o_spec) if with_bf16 else o_spec,
        scratch_shapes=[pltpu.VMEM((tn // 128, seq, 128), F32)],
        compiler_params=_params("parallel", "parallel"),
        name="hy_conv3",
    )(proj, conv_w.astype(F32), conv_b.astype(F32).reshape(1, -1))


def _hy_fwd_kernel(c_ref, s_ref, ze_ref, zo_ref, ac_ref, as_ref, bc_ref, bs_ref, gc_ref, gs_ref, r0_ref,
                   pc_ref, ps_ref, qc_ref, qs_ref):
    dot = lambda w, z: jnp.dot(w[...], z[...], preferred_element_type=F32)
    ec, es = dot(c_ref, ze_ref), dot(s_ref, ze_ref)
    oc, os_ = dot(c_ref, zo_ref), dot(s_ref, zo_ref)
    a_c, a_s = ac_ref[...], as_ref[...]
    b_c, b_s = bc_ref[...], bs_ref[...]
    g_c, g_s = gc_ref[...], gs_ref[...]
    outs = (ec * a_c - es * a_s + oc * b_c - os_ * b_s,
            ec * a_s + es * a_c + oc * b_s + os_ * b_c,
            ec * g_c - es * g_s + oc * a_c - os_ * a_s,
            ec * g_s + es * g_c + oc * a_s + os_ * a_c)
    r0 = r0_ref[...]
    a0, d0, hr2, hi2 = r0[0:1], r0[1:2], r0[2:3], r0[3:4]
    e0, eh, o0, oh = ec[0:1], es[0:1], oc[0:1], os_[0:1]
    first = (e0 * a0 + o0 * d0, eh * hr2 + oh * hi2, e0 * d0 + o0 * a0, oh * hr2 - eh * hi2)
    top = 16
    is_row0 = (lax.broadcasted_iota(jnp.int32, (top, ec.shape[1]), 0) == 0) & (pl.program_id(2) == 0)
    for ref, val, row0 in zip((pc_ref, ps_ref, qc_ref, qs_ref), outs, first):
        ref[...] = val.astype(BF16)
        ref[0:top, :] = jnp.where(is_row0, row0, val[0:top]).astype(BF16)


HY_TN = 512


def _hy_fwd(cm, s1, z_even, z_odd, tables, r0, h_col, bsz, half):
    tm = min(512, half)
    tn = HY_TN
    mt = half // tm
    w_spec = pl.BlockSpec((tm, half), lambda b, j, i: (i, 0))
    h_spec = pl.BlockSpec((tm, tn), lambda b, j, i: (i, j + h_col // tn))
    o_spec = pl.BlockSpec((tm, tn), lambda b, j, i: (b * mt + i, j))
    return pl.pallas_call(
        _hy_fwd_kernel,
        out_shape=(jax.ShapeDtypeStruct((bsz * half, BRANCH_W), BF16),) * 4,
        grid=(bsz, BRANCH_W // tn, mt),
        in_specs=[w_spec, w_spec, z_even[1], z_odd[1]] + [h_spec] * 6
                 + [pl.BlockSpec((8, tn), lambda b, j, i: (0, j + h_col // tn))],
        out_specs=(o_spec,) * 4,
        compiler_params=_params("parallel", "parallel", "arbitrary"),
        name="hy_fwd",
    )(cm, s1, z_even[0], z_odd[0], *tables, r0)


def _hy_inv_convs(c_ref, s_ref, pc_ref, ps_ref, qc_ref, qs_ref):
    dot = lambda w, y: jnp.dot(w[...], y[...], preferred_element_type=F32)
    return dot(c_ref, pc_ref) + dot(s_ref, ps_ref), dot(c_ref, qc_ref) + dot(s_ref, qs_ref)


def _hy_inv_mid_kernel(c_ref, s_ref, pc_ref, ps_ref, qc_ref, qs_ref, ge_ref, go_ref, ze_ref, zo_ref,
                       bias_ref, o_ref, ob_ref):
    convs = _hy_inv_convs(c_ref, s_ref, pc_ref, ps_ref, qc_ref, qs_ref)
    for par, (conv, g_ref, z_ref) in enumerate(zip(convs, (ge_ref, go_ref), (ze_ref, zo_ref))):
        out = g_ref[...] * (conv + bias_ref[...] * z_ref[...])
        o_ref[par] = out
        ob_ref[par] = out.astype(BF16)


def _hy_inv_last_kernel(c_ref, s_ref, pc_ref, ps_ref, qc_ref, qs_ref, ge_ref, go_ref, ze_ref, zo_ref,
                        bias_ref, sz_ref, o_ref, mix_ref):
    convs = _hy_inv_convs(c_ref, s_ref, pc_ref, ps_ref, qc_ref, qs_ref)
    tm = ge_ref.shape[0]
    _stage_rows(mix_ref, sz_ref[...])
    outs = [g_ref[...] * (conv + bias_ref[...] * z_ref[...]) * _rows_of_parity(mix_ref, par)
            for par, (conv, g_ref, z_ref) in enumerate(zip(convs, (ge_ref, go_ref), (ze_ref, zo_ref)))]
    for par, out in enumerate(outs):
        for c in range(mix_ref.shape[0]):
            mix_ref[c, pl.ds(par, tm, stride=2), :] = out[:, c * 128:(c + 1) * 128]
    for c in range(mix_ref.shape[0]):
        o_ref[:, c * 128:(c + 1) * 128] = mix_ref[c]


def _hy_inv(cm, s2, spectra, gates, zprev, bias_row, bsz, half, silu=None):
    tm = min(512, half)
    tn = HY_TN
    mt = half // tm
    w_spec = pl.BlockSpec((tm, half), lambda b, j, i: (i, 0))
    y_spec = pl.BlockSpec((half, tn), lambda b, j, i: (b, j))
    pairs = list(gates) + list(zprev)
    in_specs = [w_spec, w_spec] + [y_spec] * 4 + [s for _, s in pairs] + [pl.BlockSpec((1, tn), lambda b, j, i: (0, j))]
    args = [cm, s2, *spectra] + [a for a, _ in pairs] + [bias_row]
    if silu is None:
        o_spec = pl.BlockSpec((2, tm, tn), lambda b, j, i: (0, b * mt + i, j))
        shape = lambda dt: jax.ShapeDtypeStruct((2, bsz * half, BRANCH_W), dt)
        body, out_shape, out_specs = _hy_inv_mid_kernel, (shape(F32), shape(BF16)), (o_spec, o_spec)
        scratch = []
    else:
        silu_arr, silu_col = silu
        body = _hy_inv_last_kernel
        out_shape = jax.ShapeDtypeStruct((2 * bsz * half, BRANCH_W), F32)
        out_specs = pl.BlockSpec((2 * tm, tn), lambda b, j, i: (b * mt + i, j))
        in_specs.append(pl.BlockSpec((2 * tm, tn), lambda b, j, i: (b * mt + i, j + silu_col // tn)))
        args.append(silu_arr)
        scratch = [pltpu.VMEM((tn // 128, 2 * tm, 128), F32)]
    return pl.pallas_call(
        body, out_shape=out_shape, grid=(bsz, BRANCH_W // tn, mt), in_specs=in_specs, out_specs=out_specs,
        scratch_shapes=scratch, compiler_params=_params("parallel", "parallel", "arbitrary"),
        name="hy_inv",
    )(*args)


def _hyena_branch(proj, bsz, seq, dft, conv_w, conv_b, w1, b1, w2, b2, freq, w3, b3, bias):
    cm, s1, s2 = dft
    half = seq // 2
    tm = min(512, half)
    tn = HY_TN
    mt = half // tm
    full = lambda par, col=0: pl.BlockSpec((None, half, tn), lambda b, j, i: (par, b, j + col // tn))
    row = lambda par, col=0: pl.BlockSpec((None, tm, tn), lambda b, j, i: (par, b * mt + i, j + col // tn))
    both = lambda arr, spec, col=0: ((arr, spec(0, col)), (arr, spec(1, col)))
    *taps, r0 = _hy_filter_taps(seq, w1, b1, w2, b2, freq, w3, b3)
    tables = _hy_spectrum(cm, s1, taps)
    v, vb = _hy_conv3(proj, bsz, seq, conv_w, conv_b, 0, BRANCH_W, True)
    x12 = _hy_conv3(proj, bsz, seq, conv_w, conv_b, BRANCH_W, 2 * BRANCH_W, False)
    bias = bias.astype(F32)
    spectra = _hy_fwd(cm, s1, *both(vb, full), tables, r0, 0, bsz, half)
    z1, z1b = _hy_inv(cm, s2, spectra, gates=both(x12, row), zprev=both(v, row),
                      bias_row=bias[0:1], bsz=bsz, half=half)
    spectra = _hy_fwd(cm, s1, *both(z1b, full), tables, r0, BRANCH_W, bsz, half)
    return _hy_inv(cm, s2, spectra, gates=both(x12, row, BRANCH_W), zprev=both(z1, row),
                   bias_row=bias[1:2], bsz=bsz, half=half, silu=(proj, COL_BZ))


def _rope_table_kernel(pos_ref, inv_ref, cos_ref, sin_ref):
    ang = pos_ref[...] * inv_ref[...]
    lane = lax.broadcasted_iota(jnp.int32, ang.shape, 1)
    live = lane < MLA_ROPE
    cos_ref[...] = jnp.where(live, jnp.cos(ang), 0.0)
    sin_ref[...] = jnp.where(live, jnp.where(lane < MLA_ROPE // 2, -1.0, 1.0) * jnp.sin(ang), 0.0)


def _rope_tables(positions):
    m = positions.size
    half = MLA_ROPE // 2
    inv = ROPE_BASE ** (-jnp.arange(half, dtype=F32) / half)
    inv = jnp.concatenate([inv, inv, jnp.zeros((128 - MLA_ROPE,), F32)])[None]
    pos = positions.astype(F32).reshape(m, 1)
    tm = min(1024, m)
    spec = pl.BlockSpec((tm, 128), lambda i: (i, 0))
    return pl.pallas_call(
        _rope_table_kernel,
        out_shape=(jax.ShapeDtypeStruct((m, 128), F32),) * 2,
        grid=(m // tm,),
        in_specs=[pl.BlockSpec((tm, 1), lambda i: (i, 0)), pl.BlockSpec((1, 128), lambda i: (0, 0))],
        out_specs=(spec, spec),
        compiler_params=_params("parallel"),
        name="rope_table",
    )(pos, inv)


def _rope128(x, cos_t, sin_t):
    lane = lax.broadcasted_iota(jnp.int32, x.shape, 1)
    half = MLA_ROPE // 2
    partner = jnp.where(lane < half, pltpu.roll(x, 128 - half, 1), pltpu.roll(x, half, 1))
    return x * cos_t + partner * sin_t


def _rms(x, g):
    ms = jnp.mean(jnp.square(x), axis=-1, keepdims=True)
    return x * lax.rsqrt(ms + RMS_EPS) * g


def _mla_q_kernel(cq_ref, g_ref, w_ref, cos_ref, sin_ref, q_ref, *, scale):
    xn = _rms(cq_ref[...], g_ref[...]).astype(BF16)
    q = jnp.dot(xn, w_ref[...], preferred_element_type=F32) * scale
    cos_t, sin_t = cos_ref[...], sin_ref[...]
    for h in range(MLA_HEADS):
        base = h * MLA_QK_PAD
        q_ref[:, base:base + MLA_NOPE] = q[:, base:base + MLA_NOPE].astype(BF16)
        q_ref[:, base + MLA_NOPE:base + MLA_QK_PAD] = _rope128(
            q[:, base + MLA_NOPE:base + MLA_QK_PAD], cos_t, sin_t).astype(BF16)


def _mla_kv_kernel(ckv_ref, g_ref, wk_ref, wv_ref, kr_ref, cos_ref, sin_ref, k_ref, v_ref):
    xn = _rms(ckv_ref[...], g_ref[...]).astype(BF16)
    kn = jnp.dot(xn, wk_ref[...], preferred_element_type=F32)
    v_ref[...] = jnp.dot(xn, wv_ref[...], preferred_element_type=F32).astype(BF16)
    kr = _rope128(kr_ref[:, 0:128], cos_ref[...], sin_ref[...]).astype(BF16)
    for h in range(MLA_HEADS):
        base = h * MLA_QK_PAD
        k_ref[:, base:base + MLA_NOPE] = kn[:, h * MLA_NOPE:(h + 1) * MLA_NOPE].astype(BF16)
        k_ref[:, base + MLA_NOPE:base + MLA_QK_PAD] = kr


def _mla_attn_kernel(q_ref, k_ref, v_ref, z_ref, o_ref):
    k, v = k_ref[...], v_ref[...]
    half = q_ref.shape[0] // 2
    for r in range(2):
        rows = pl.ds(r * half, half)
        s = lax.dot_general(q_ref[rows, :], k, (((1,), (1,)), ((), ())),
                            preferred_element_type=F32)
        p = jnp.exp2(s - jnp.max(s, axis=-1, keepdims=True))
        l = jnp.sum(p, axis=-1, keepdims=True)
        o = jnp.dot(p.astype(BF16), v, preferred_element_type=F32)
        o_ref[rows, :] = (o / l * z_ref[rows, :]).astype(BF16)


def _mla_branch(proj, proj_b, bsz, seq, rope, q_norm_g, w_uq, kv_norm_g, w_ukv):
    m = bsz * seq
    cos_t, sin_t = rope
    dqk = MLA_NOPE + MLA_ROPE
    hq = MLA_HEADS * MLA_QK_PAD
    w_q = w_uq.reshape(MLA_LORA, MLA_HEADS, dqk)
    w_q = jnp.pad(w_q, ((0, 0), (0, 0), (0, MLA_QK_PAD - dqk))).reshape(MLA_LORA, hq).astype(BF16)
    w_kv = w_ukv.reshape(MLA_LORA, MLA_HEADS, MLA_NOPE + MLA_V)
    w_k = w_kv[:, :, :MLA_NOPE].reshape(MLA_LORA, MLA_HEADS * MLA_NOPE).astype(BF16)
    w_v = w_kv[:, :, MLA_NOPE:].reshape(MLA_LORA, MLA_HEADS * MLA_V).astype(BF16)
    tm = min(512, m)
    row = lambda shape, col=0: pl.BlockSpec(shape, lambda i: (i, col))
    full = lambda shape: pl.BlockSpec(shape, lambda i: (0, 0))
    qp = pl.pallas_call(
        functools.partial(_mla_q_kernel, scale=dqk ** -0.5 * math.log2(math.e)),
        out_shape=jax.ShapeDtypeStruct((m, hq), BF16),
        grid=(m // tm,),
        in_specs=[row((tm, MLA_LORA), COL_CQ // MLA_LORA), full((1, MLA_LORA)), full((MLA_LORA, hq)),
                  row((tm, 128)), row((tm, 128))],
        out_specs=row((tm, hq)),
        compiler_params=_params("parallel"),
        name="mla_q",
    )(proj, q_norm_g.astype(F32).reshape(1, -1), w_q, cos_t, sin_t)
    kp, vp = pl.pallas_call(
        _mla_kv_kernel,
        out_shape=(jax.ShapeDtypeStruct((m, hq), BF16),
                   jax.ShapeDtypeStruct((m, MLA_HEADS * MLA_V), BF16)),
        grid=(m // tm,),
        in_specs=[row((tm, MLA_LORA), COL_CKV // MLA_LORA), full((1, MLA_LORA)),
                  full((MLA_LORA, MLA_HEADS * MLA_NOPE)), full((MLA_LORA, MLA_HEADS * MLA_V)),
                  row((tm, 512), COL_CKR // 512), row((tm, 128)), row((tm, 128))],
        out_specs=(row((tm, hq)), row((tm, MLA_HEADS * MLA_V))),
        compiler_params=_params("parallel"),
        name="mla_kv",
    )(proj, kv_norm_g.astype(F32).reshape(1, -1), w_k, w_v, proj_b, cos_t, sin_t)
    tq = min(512, seq)
    qt = seq // tq
    return pl.pallas_call(
        _mla_attn_kernel,
        out_shape=jax.ShapeDtypeStruct((m, MLA_HEADS * MLA_V), BF16),
        grid=(bsz, MLA_HEADS, qt),
        in_specs=[pl.BlockSpec((tq, MLA_QK_PAD), lambda b, h, i: (b * qt + i, h)),
                  pl.BlockSpec((seq, MLA_QK_PAD), lambda b, h, i: (b, h)),
                  pl.BlockSpec((seq, MLA_V), lambda b, h, i: (b, h)),
                  pl.BlockSpec((tq, MLA_V), lambda b, h, i: (b * qt + i, h + COL_CZ // MLA_V))],
        out_specs=pl.BlockSpec((tq, MLA_V), lambda b, h, i: (b * qt + i, h)),
        compiler_params=_params("parallel", "parallel", "arbitrary"),
        name="mla_attn",
    )(qp, kp, vp, proj_b)


def _lift_kernel(ya_ref, yb_ref, yc_ref, w_ref, ga_ref, gb_ref, gc_ref, o_ref, wb_ref):
    @pl.when(pl.program_id(1) == 0)
    def _():
        wb_ref[...] = w_ref[...].astype(BF16)

    acc = ga_ref[...] * jnp.dot(ya_ref[...], wb_ref[0], preferred_element_type=F32)
    acc += gb_ref[...] * jnp.dot(yb_ref[...].astype(BF16), wb_ref[1], preferred_element_type=F32)
    acc += gc_ref[...] * jnp.dot(yc_ref[...], wb_ref[2], preferred_element_type=F32)
    o_ref[...] = acc.astype(BF16)


def _lift(ya, yb, yc, w_lift, layer, proj_b):
    m = ya.shape[0]
    tm = min(1024, m)
    tn = 512
    y_spec = pl.BlockSpec((tm, BRANCH_W), lambda j, i: (i, 0))
    gate = lambda n: pl.BlockSpec((tm, tn), lambda j, i: (i, j + (COL_GATE + n * D_MODEL) // tn))
    return pl.pallas_call(
        _lift_kernel,
        out_shape=jax.ShapeDtypeStruct((m, D_MODEL), BF16),
        grid=(D_MODEL // tn, m // tm),
        in_specs=[y_spec, y_spec, y_spec,
                  pl.BlockSpec((None, N_BRANCH, BRANCH_W, tn), lambda j, i: (layer, 0, 0, j)),
                  gate(0), gate(1), gate(2)],
        out_specs=pl.BlockSpec((tm, tn), lambda j, i: (i, j)),
        scratch_shapes=[pltpu.VMEM((N_BRANCH, BRANCH_W, tn), BF16)],
        compiler_params=_params("parallel", "arbitrary"),
        name="lift",
    )(ya, yb, yc, w_lift, proj_b, proj_b, proj_b)


def _out_kernel(mix_ref, wo_ref, p_ref, wp_ref, sp_ref, x_ref, g_ref, b_ref, o_ref):
    mixed = jnp.dot(mix_ref[...], wo_ref[...], preferred_element_type=F32)
    ple = jnp.dot(p_ref[...].astype(BF16), wp_ref[...], preferred_element_type=F32) * sp_ref[...]
    r = DEEPNORM_ALPHA * x_ref[...] + mixed + ple
    mu = jnp.mean(r, axis=-1, keepdims=True)
    var = jnp.mean(jnp.square(r - mu), axis=-1, keepdims=True)
    o_ref[...] = (r - mu) * lax.rsqrt(var + LN_EPS) * g_ref[...] + b_ref[...]


def _out_norm(mix, w_out, p, w_ple, proj, x, ln_g, ln_b):
    m = mix.shape[0]
    tm = min(512, m)
    row = lambda w, col=0: pl.BlockSpec((tm, w), lambda i: (i, col))
    full = lambda shape: pl.BlockSpec(shape, lambda i: (0, 0))
    return pl.pallas_call(
        _out_kernel,
        out_shape=jax.ShapeDtypeStruct((m, D_MODEL), F32),
        grid=(m // tm,),
        in_specs=[row(D_MODEL), full((D_MODEL, D_MODEL)), row(PLE_DIM), full((PLE_DIM, D_MODEL)),
                  row(D_MODEL, COL_PLE // D_MODEL), row(D_MODEL), full((1, D_MODEL)), full((1, D_MODEL))],
        out_specs=row(D_MODEL),
        compiler_params=_params("parallel"),
        name="out_norm",
    )(mix, w_out.astype(BF16), p, w_ple.astype(BF16), proj, x,
      ln_g.astype(F32).reshape(1, -1), ln_b.astype(F32).reshape(1, -1))


def kernel(x, p, positions, w_in, s5_lambda_re, s5_lambda_im, s5_log_dt, s5_b_re, s5_b_im, s5_c_re, s5_c_im, s5_d, s5_w_glu, s5_b_glu, hy_conv_w, hy_conv_b, hy_w1, hy_b1, hy_w2, hy_b2, hy_freq, hy_w3, hy_b3, hy_bias, mla_q_norm, mla_w_uq, mla_kv_norm, mla_w_ukv, w_lift, w_out, w_ple, ln_g, ln_b):
    bsz, seq, _ = x.shape
    m = bsz * seq
    depth = w_in.shape[0]
    dft = _dft_matrices(seq // 2)
    rope = _rope_tables(positions)
    s5_tiled = _s5_tiled_params(s5_lambda_re, s5_lambda_im, s5_log_dt, s5_b_re, s5_b_im,
                                s5_c_re, s5_c_im, s5_d)
    xf = x.reshape(m, D_MODEL).astype(F32)
    w_t = jnp.swapaxes(w_in, 1, 2)
    for i in range(depth):
        proj, proj_b = _proj_both(xf, w_t, i)
        y_a = _s5_branch(proj, bsz, seq, [a[i] for a in s5_tiled], s5_w_glu[i], s5_b_glu[i])
        y_b = _hyena_branch(proj, bsz, seq, dft, hy_conv_w[i], hy_conv_b[i], hy_w1[i], hy_b1[i],
                            hy_w2[i], hy_b2[i], hy_freq[i], hy_w3[i], hy_b3[i], hy_bias[i])
        y_c = _mla_branch(proj, proj_b, bsz, seq, rope, mla_q_norm[i], mla_w_uq[i], mla_kv_norm[i], mla_w_ukv[i])
        mix = _lift(y_a, y_b, y_c, w_lift, i, proj_b)
        xf = _out_norm(mix, w_out[i], p[i].reshape(m, PLE_DIM), w_ple[i], proj_b, xf, ln_g[i], ln_b[i])
    return xf.reshape(bsz, seq, D_MODEL).astype(x.dtype)
```

```python
import functools
import math

import numpy as np
import jax
import jax.numpy as jnp
from jax import lax
from jax.experimental import pallas as pl
from jax.experimental.pallas import tpu as pltpu

F32 = jnp.float32
BF16 = jnp.bfloat16

D_MODEL = 2048
PLE_DIM = 256
N_BRANCH = 3
BRANCH_W = 1024

S5_GROUP = 16
S5_GROUPS = BRANCH_W // S5_GROUP
S5_STATE = 64
S5_CHUNK = 16
S5_ROW = S5_CHUNK * S5_GROUP

HY_EMB = 33
HY_FF = 64
HY_PAD = 128
HY_DECAY_TARGET = 0.01
HY_FAST_DECAY = 0.3
HY_SLOW_DECAY = 1.5
DFT_ROWS = 64

MLA_HEADS = 8
MLA_NOPE = 128
MLA_ROPE = 64
MLA_V = 128
MLA_LORA = 512
MLA_QK_PAD = 256
ROPE_BASE = 10000.0

LN_EPS = 1e-5
RMS_EPS = 1e-6
DEPTH = 2
DEEPNORM_ALPHA = (2 * DEPTH) ** 0.25

COL_AX = 0
COL_AZ = 1024
COL_BU = 2048
COL_BZ = 5120
COL_CQ = 6144
COL_CKV = 6656
PROJ_A_N = 7168
COL_PLE = 0
COL_GATE = 2048
COL_CZ = 8192
COL_CKR = 9216
PROJ_B_N = 9728
PROJ_TN = 512
W_IN_CKR = 7168
W_IN_CZ = 7232
W_IN_GATE = 8256
W_IN_PLE = 14400

VMEM_LIMIT = 56 * 1024 * 1024


def _params(*sem):
    return pltpu.CompilerParams(dimension_semantics=sem, vmem_limit_bytes=VMEM_LIMIT)


def _sigmoid(x):
    return 0.5 * jnp.tanh(0.5 * x) + 0.5


def _in_tiles(j, ranges):
    hit = None
    for lo, hi in ranges:
        cur = (j >= lo // PROJ_TN) & (j < hi // PROJ_TN)
        hit = cur if hit is None else (hit | cur)
    return hit


def _proj_kernel(x_ref, w_ref, o_ref, xb_ref, *, silu_cols, sigm_cols):
    j = pl.program_id(1)

    @pl.when(j == 0)
    def _():
        xb_ref[...] = x_ref[...].astype(BF16)

    acc = lax.dot_general(xb_ref[...], w_ref[...].astype(BF16), (((1,), (1,)), ((), ())),
                          preferred_element_type=F32)
    is_silu = _in_tiles(j, silu_cols)
    plain = jnp.logical_not(is_silu)

    @pl.when(is_silu)
    def _():
        o_ref[...] = acc * _sigmoid(acc)

    if sigm_cols:
        is_sigm = _in_tiles(j, sigm_cols)
        plain = jnp.logical_not(is_silu | is_sigm)

        @pl.when(is_sigm)
        def _():
            o_ref[...] = _sigmoid(acc)

    @pl.when(plain)
    def _():
        o_ref[...] = acc


def _proj(x, w, layer, n_out, silu_cols, sigm_cols, name):
    m, k = x.shape
    tm = min(1024, m)
    return pl.pallas_call(
        functools.partial(_proj_kernel, silu_cols=silu_cols, sigm_cols=sigm_cols),
        out_shape=jax.ShapeDtypeStruct((m, n_out), F32),
        grid=(m // tm, n_out // PROJ_TN),
        in_specs=[pl.BlockSpec((tm, k), lambda i, j: (i, 0)),
                  pl.BlockSpec((None, PROJ_TN, k), lambda i, j: (layer, j, 0))],
        out_specs=pl.BlockSpec((tm, PROJ_TN), lambda i, j: (i, j)),
        scratch_shapes=[pltpu.VMEM((tm, k), BF16)],
        compiler_params=_params("parallel", "arbitrary"),
        name=name,
    )(x, w)


def _repack_src_tile(j):
    t = lambda col: col // PROJ_TN
    return jnp.where(j < t(COL_GATE), j + t(W_IN_PLE),
                     jnp.where(j < t(COL_CZ), j - t(COL_GATE) + t(W_IN_GATE),
                               jnp.where(j < t(COL_CKR), j - t(COL_CZ) + t(W_IN_CZ), t(W_IN_CKR))))


def _repack_kernel(a_ref, b_ref, o_ref):
    is_last = pl.program_id(0) == pl.num_programs(0) - 1
    a = a_ref[...]
    shifted = jnp.concatenate([a[MLA_ROPE:], b_ref[...]], axis=0)
    plain = jnp.concatenate([a[:MLA_ROPE], jnp.zeros_like(a[MLA_ROPE:])], axis=0)
    o_ref[...] = jnp.where(is_last, plain, shifted).astype(BF16)


def _repack_b(w_t, layer):
    k = w_t.shape[2]
    sub = PROJ_TN // MLA_ROPE
    return pl.pallas_call(
        _repack_kernel,
        out_shape=jax.ShapeDtypeStruct((1, PROJ_B_N, k), BF16),
        grid=(PROJ_B_N // PROJ_TN,),
        in_specs=[pl.BlockSpec((None, PROJ_TN, k), lambda j: (layer, _repack_src_tile(j), 0)),
                  pl.BlockSpec((None, MLA_ROPE, k), lambda j: (layer, (_repack_src_tile(j) + 1) * sub, 0))],
        out_specs=pl.BlockSpec((None, PROJ_TN, k), lambda j: (0, j, 0)),
        compiler_params=_params("parallel"),
        name="repack_b",
    )(w_t, w_t)


def _proj_both(x, w_t, layer):
    proj_a = _proj(x, w_t, layer, PROJ_A_N, ((COL_AZ, COL_BU), (COL_BZ, COL_CQ)), (), "proj_a")
    proj_b = _proj(x, _repack_b(w_t, layer), 0, PROJ_B_N, ((COL_CZ, COL_CKR),), ((COL_PLE, COL_CZ),),
                   "proj_b")
    return proj_a, proj_b


S5_LANES = 8 * S5_STATE
S5_SLAB = 8
S5_RELAYOUT_ROWS = 32


def _s5_tile_lanes(a):
    return jnp.concatenate([a[:, 0]] * 4 + [a[:, 1]] * 4, axis=-1)


def _s5_tiled_params(lam_re, lam_im, log_dt, b_re, b_im, c_re, c_im, d):
    f = lambda a: a.astype(F32)
    depth = lam_re.shape[0]
    ldt = jnp.broadcast_to(f(log_dt)[..., None, None], lam_re.shape[:3] + (1, S5_STATE))
    return (_s5_tile_lanes(f(lam_re)[:, :, :, None, :]), _s5_tile_lanes(f(lam_im)[:, :, :, None, :]),
            _s5_tile_lanes(ldt),
            _s5_tile_lanes(jnp.swapaxes(f(b_re), -1, -2)), _s5_tile_lanes(jnp.swapaxes(f(b_im), -1, -2)),
            _s5_tile_lanes(f(c_re)), _s5_tile_lanes(f(c_im)),
            f(d).reshape(depth, S5_GROUPS, S5_GROUP, 1))


def _s5_mats_kernel(lr_ref, li_ref, ldt_ref, br_ref, bi_ref, cr_ref, ci_ref, d_ref,
                    ms_ref, mi_ref, mo_ref, ar_ref, ai_ref):
    t_n, h_n = S5_CHUNK, S5_GROUP
    hi = lax.Precision.HIGHEST
    nt = (((1,), (1,)), ((), ()))
    blk = lax.broadcasted_iota(jnp.int32, (1, S5_LANES), 1) // S5_STATE
    is_im = (blk // 2) % 2 == 1
    is_fwd = blk < 4
    steps = lax.broadcasted_iota(jnp.int32, (24, S5_LANES), 0).astype(F32)
    lane_k = lax.broadcasted_iota(jnp.int32, (h_n, S5_ROW), 1)
    sub_k = lax.broadcasted_iota(jnp.int32, (h_n, S5_ROW), 0)

    def per_group(gi, carry):
        lr, li = lr_ref[gi], li_ref[gi]
        dt = jnp.exp(ldt_ref[gi])
        zr, zi = lr * dt, li * dt
        mag = jnp.exp(steps * zr)
        tr, ti = mag * jnp.cos(steps * zi), mag * jnp.sin(steps * zi)
        lbr, lbi = tr[1:2], ti[1:2]
        n2 = lr * lr + li * li
        qr = ((lbr - 1.0) * lr + lbi * li) / n2
        qi = (lbi * lr - (lbr - 1.0) * li) / n2
        br, bi = br_ref[gi], bi_ref[gi]
        bbr, bbi = qr * br - qi * bi, qr * bi + qi * br
        y1, y2 = jnp.where(is_im, bbi, bbr), jnp.where(is_im, bbr, bbi)
        cr, ci = cr_ref[gi], ci_ref[gi]

        def pick(tab, t_fwd, t_bwd):
            return jnp.where(is_fwd, tab[t_fwd:t_fwd + 1], tab[t_bwd:t_bwd + 1])

        def c_times(p_r, p_i):
            return jnp.where(is_im, -(cr * p_i + ci * p_r), cr * p_r - ci * p_i)

        ct = [c_times(tr[s:s + 1], ti[s:s + 1]) for s in range(t_n + 1)]
        q_rows = []
        for t in range(t_n):
            a_r, a_i = pick(tr, t_n - 1 - t, t), pick(ti, t_n - 1 - t, t)
            rows = pl.ds(t * h_n, h_n)
            ms_ref[gi, rows, :] = (a_r * y1 + jnp.where(is_im, a_i, -a_i) * y2).astype(BF16)
            mo_ref[gi, rows, :] = jnp.where(is_fwd, ct[t + 1], ct[t_n - t]).astype(BF16)
            q_rows.append(jnp.where(is_fwd, ct[t], ct[t_n - 1 - t]))
        q = jnp.concatenate(q_rows, axis=0)
        half = S5_LANES // 2
        kf = 0.5 * lax.dot_general(y1[:, :half], q[:, :half], nt, precision=hi, preferred_element_type=F32)
        kb = 0.5 * lax.dot_general(y1[:, half:], q[:, half:], nt, precision=hi, preferred_element_type=F32)
        kf = kf + jnp.where(lane_k == sub_k, d_ref[gi], 0.0)
        for t in range(t_n):
            fwd = kf if t == 0 else jnp.where(lane_k >= h_n * t, pltpu.roll(kf, h_n * t, 1), 0.0)
            sh = h_n * (t_n - 1 - t)
            bwd = kb if sh == 0 else jnp.where(lane_k < S5_ROW - sh, pltpu.roll(kb, S5_ROW - sh, 1), 0.0)
            mi_ref[gi, pl.ds(t * h_n, h_n), :] = (fwd + bwd).astype(BF16)
        ar_ref[gi] = tr[t_n:t_n + 1]
        ai_ref[gi] = ti[t_n:t_n + 1]
        return carry

    lax.fori_loop(0, lr_ref.shape[0], per_group, 0)


def _s5_mats(tiled):
    g_n = S5_GROUPS
    gb = S5_SLAB
    spec = lambda r, w: pl.BlockSpec((gb, r, w), lambda j: (j, 0, 0))
    in_rows = (1, 1, 1, S5_GROUP, S5_GROUP, S5_GROUP, S5_GROUP)
    return pl.pallas_call(
        _s5_mats_kernel,
        out_shape=(jax.ShapeDtypeStruct((g_n, S5_ROW, S5_LANES), BF16),
                   jax.ShapeDtypeStruct((g_n, S5_ROW, S5_ROW), BF16),
                   jax.ShapeDtypeStruct((g_n, S5_ROW, S5_LANES), BF16),
                   jax.ShapeDtypeStruct((g_n, 1, S5_LANES), F32),
                   jax.ShapeDtypeStruct((g_n, 1, S5_LANES), F32)),
        grid=(g_n // gb,),
        in_specs=[spec(r, S5_LANES) for r in in_rows] + [spec(S5_GROUP, 1)],
        out_specs=(spec(S5_ROW, S5_LANES), spec(S5_ROW, S5_ROW), spec(S5_ROW, S5_LANES),
                   spec(1, S5_LANES), spec(1, S5_LANES)),
        compiler_params=_params("parallel"),
        name="s5_mats",
    )(*tiled)


def _seg_transpose(vs, seg):
    vs = list(vs)
    for s in (4, 2, 1):
        keep = (seg & s) == 0
        for i in range(8):
            if i & s:
                continue
            a, b = vs[i], vs[i + s]
            vs[i] = jnp.where(keep, a, pltpu.roll(b, s * S5_GROUP, 1))
            vs[i + s] = jnp.where(keep, pltpu.roll(a, 128 - s * S5_GROUP, 1), b)
    return vs


def _s5_main_kernel(x_ref, ms_ref, mi_ref, mo_ref, ar_ref, ai_ref, y_ref,
                    u_ref, sl_ref, st_ref, yg_ref, *, bsz):
    rows = x_ref.shape[0]
    n_chunks = rows // bsz
    rc = S5_RELAYOUT_ROWS
    seg = lax.broadcasted_iota(jnp.int32, (rc, 128), 1) // S5_GROUP
    slot = (lax.broadcasted_iota(jnp.int32, (1, S5_LANES), 1) // S5_STATE) % 2
    slot128 = slot[:, :128]

    def relayout_in(r, carry):
        r0 = pl.multiple_of(r * rc, rc)
        for th in range(2):
            src = [x_ref[pl.ds(r0, rc), th * 8 + t8, :] for t8 in range(8)]
            for gi, out in enumerate(_seg_transpose(src, seg)):
                u_ref[gi, pl.ds(r0, rc), th * 128:(th + 1) * 128] = out.astype(BF16)
        return carry

    lax.fori_loop(0, rows // rc, relayout_in, 0, unroll=2)

    n_pairs = S5_SLAB // 2
    trans = []
    for jp in range(n_pairs):
        g0, g1 = 2 * jp, 2 * jp + 1
        r0 = jnp.dot(u_ref[g0], ms_ref[g0], preferred_element_type=F32)
        r1 = jnp.dot(u_ref[g1], ms_ref[g1], preferred_element_type=F32)
        loc = jnp.where(slot == 0, r0, r1)
        for k in range(4):
            for b in range(bsz):
                sl_ref[k, jp, pl.ds(b, n_chunks, stride=bsz), :] = (
                    loc[b * n_chunks:(b + 1) * n_chunks, k * 128:(k + 1) * 128])
        trans.append([jnp.where(slot128 == 0, ref[g0][:, off:off + 128], ref[g1][:, off:off + 128])
                      for ref, off in ((ar_ref, 0), (ai_ref, 0), (ar_ref, 256), (ai_ref, 256))])

    zero = jnp.zeros((bsz, 128), F32)

    def scan(c, carry):
        rf = pl.ds(pl.multiple_of(c * bsz, bsz), bsz)
        rb = pl.ds(pl.multiple_of((n_chunks - 1 - c) * bsz, bsz), bsz)
        new = []
        for jp in range(n_pairs):
            s_fr, s_fi, s_br, s_bi = carry[jp]
            a_fr, a_fi, a_br, a_bi = trans[jp]
            st_ref[0, jp, rf, :] = s_fr
            st_ref[1, jp, rf, :] = s_fi
            st_ref[2, jp, rb, :] = s_br
            st_ref[3, jp, rb, :] = s_bi
            new.append((a_fr * s_fr - a_fi * s_fi + sl_ref[0, jp, rf, :],
                        a_fr * s_fi + a_fi * s_fr + sl_ref[1, jp, rf, :],
                        a_br * s_br - a_bi * s_bi + sl_ref[2, jp, rb, :],
                        a_br * s_bi + a_bi * s_br + sl_ref[3, jp, rb, :]))
        return tuple(new)

    lax.fori_loop(0, n_chunks, scan, tuple((zero,) * 4 for _ in range(n_pairs)))

    nt = (((1,), (1,)), ((), ()))
    for jp in range(n_pairs):
        st = jnp.concatenate(
            [jnp.concatenate([st_ref[k, jp, pl.ds(b, n_chunks, stride=bsz), :] for b in range(bsz)], axis=0)
             for k in range(4)], axis=1)
        for e in range(2):
            g = 2 * jp + e
            st_g = jnp.where(slot == e, st, 0.0).astype(BF16)
            y = (jnp.dot(u_ref[g], mi_ref[g], preferred_element_type=F32)
                 + lax.dot_general(st_g, mo_ref[g], nt, preferred_element_type=F32))
            yg_ref[g] = jax.nn.gelu(y)

    def relayout_out(r, carry):
        r0 = pl.multiple_of(r * rc, rc)
        for th in range(2):
            src = [yg_ref[gi, pl.ds(r0, rc), th * 128:(th + 1) * 128] for gi in range(S5_SLAB)]
            for t8, out in enumerate(_seg_transpose(src, seg)):
                y_ref[pl.ds(r0, rc), th * 8 + t8, :] = out
        return carry

    lax.fori_loop(0, rows // rc, relayout_out, 0, unroll=2)


def _s5_glu_kernel(g_ref, w_ref, b_ref, z_ref, o_ref):
    g = g_ref[...]
    acc = jnp.dot(g.astype(BF16), w_ref[...], preferred_element_type=F32) + b_ref[...]
    o_ref[...] = (g * _sigmoid(acc) * z_ref[...]).astype(o_ref.dtype)


def _s5_branch(proj, bsz, seq, tiled, w_glu, b_glu):
    m_state, m_intra, m_out, a_re, a_im = _s5_mats(tiled)
    t_n = S5_CHUNK
    rows = bsz * (seq // t_n)
    m = bsz * seq
    gb = S5_SLAB
    x3 = proj.reshape(rows, t_n, proj.shape[1])
    mat = lambda r, w: pl.BlockSpec((gb, r, w), lambda s: (s, 0, 0))
    io_spec = pl.BlockSpec((rows, t_n, 128), lambda s: (0, 0, s + COL_AX // 128))
    y = pl.pallas_call(
        functools.partial(_s5_main_kernel, bsz=bsz),
        out_shape=jax.ShapeDtypeStruct((rows, t_n, BRANCH_W), F32),
        grid=(S5_GROUPS // gb,),
        in_specs=[io_spec, mat(S5_ROW, S5_LANES), mat(S5_ROW, S5_ROW), mat(S5_ROW, S5_LANES),
                  mat(1, S5_LANES), mat(1, S5_LANES)],
        out_specs=pl.BlockSpec((rows, t_n, 128), lambda s: (0, 0, s)),
        scratch_shapes=[pltpu.VMEM((gb, rows, S5_ROW), BF16),
                        pltpu.VMEM((4, gb // 2, rows, 128), F32),
                        pltpu.VMEM((4, gb // 2, rows, 128), F32),
                        pltpu.VMEM((gb, rows, S5_ROW), F32)],
        compiler_params=_params("parallel"),
        name="s5_main",
    )(x3, m_state, m_intra, m_out, a_re, a_im)
    y = y.reshape(m, BRANCH_W)
    tm = min(512, m)
    return pl.pallas_call(
        _s5_glu_kernel,
        out_shape=jax.ShapeDtypeStruct((m, BRANCH_W), BF16),
        grid=(m // tm,),
        in_specs=[pl.BlockSpec((tm, BRANCH_W), lambda i: (i, 0)),
                  pl.BlockSpec((BRANCH_W, BRANCH_W), lambda i: (0, 0)),
                  pl.BlockSpec((1, BRANCH_W), lambda i: (0, 0)),
                  pl.BlockSpec((tm, BRANCH_W), lambda i: (i, COL_AZ // BRANCH_W))],
        out_specs=pl.BlockSpec((tm, BRANCH_W), lambda i: (i, 0)),
        compiler_params=_params("parallel"),
        name="s5_glu",
    )(y, w_glu.astype(BF16), b_glu.astype(F32).reshape(1, BRANCH_W), proj)


def _dft_tables(seq):
    n = 2 * seq
    mm = np.arange(seq, dtype=np.int64)
    k1 = np.arange(seq // DFT_ROWS, dtype=np.int64)[:, None] * DFT_ROWS
    k0 = np.arange(DFT_ROWS, dtype=np.int64)[:, None]
    ang_a = 2.0 * np.pi * ((k1 * mm) % n).astype(np.float64) / n
    ang_b = 2.0 * np.pi * ((k0 * mm) % n).astype(np.float64) / n
    return tuple(jnp.asarray(t, F32) for t in (np.cos(ang_a), np.sin(ang_a), np.cos(ang_b), np.sin(ang_b)))


def _dft_gen_kernel(ac_ref, as_ref, bc_ref, bs_ref, c_ref, s1_ref, s2_ref):
    i = pl.program_id(0)
    a_c = ac_ref[pl.ds(i, 1), :]
    a_s = as_ref[pl.ds(i, 1), :]
    b_c, b_s = bc_ref[...], bs_ref[...]
    cos_t = a_c * b_c - a_s * b_s
    sin_t = a_s * b_c + a_c * b_s
    rows = lax.broadcasted_iota(jnp.int32, cos_t.shape, 0) + i * DFT_ROWS
    cols = lax.broadcasted_iota(jnp.int32, cos_t.shape, 1)
    alt_cols = jnp.where((cols & 1) == 0, 1.0, -1.0).astype(F32)
    alt_rows = jnp.where((rows & 1) == 0, 1.0, -1.0).astype(F32)
    c_ref[...] = cos_t.astype(BF16)
    s1_ref[...] = jnp.where(rows == 0, alt_cols, sin_t).astype(BF16)
    s2_ref[...] = jnp.where(cols == 0, alt_rows, sin_t).astype(BF16)


def _dft_matrices(seq):
    tabs = _dft_tables(seq)
    n_steps = seq // DFT_ROWS
    tab_spec = pl.BlockSpec(tabs[0].shape, lambda i: (0, 0))
    b_spec = pl.BlockSpec((DFT_ROWS, seq), lambda i: (0, 0))
    o_spec = pl.BlockSpec((DFT_ROWS, seq), lambda i: (i, 0))
    return pl.pallas_call(
        _dft_gen_kernel,
        out_shape=(jax.ShapeDtypeStruct((seq, seq), BF16),) * 3,
        grid=(n_steps,),
        in_specs=[tab_spec, tab_spec, b_spec, b_spec],
        out_specs=(o_spec,) * 3,
        compiler_params=_params("parallel"),
        name="dft_gen",
    )(*tabs)


def _hy_filter_kernel(feat_ref, w1_ref, b1_ref, w2_ref, b2_ref, f0_ref, f1_ref,
                      w3p_ref, w3n_ref, b3p_ref, b3n_ref, dl_ref, t_ref,
                      hs_ref, hd_ref, hp_ref, hn_ref, r0_ref, h_ref, split_ref, *, inv_n):
    hi = lax.Precision.HIGHEST

    @pl.when(pl.program_id(0) == 0)
    def _():
        h1 = jnp.sin(f0_ref[...] * (jnp.dot(feat_ref[...], w1_ref[...], precision=hi,
                                            preferred_element_type=F32) + b1_ref[...]))
        h_ref[...] = jnp.sin(f1_ref[...] * (jnp.dot(h1, w2_ref[...], precision=hi,
                                                    preferred_element_type=F32) + b2_ref[...]))

    h = h_ref[...].astype(BF16)
    win = jnp.exp(-t_ref[...] * jnp.abs(dl_ref[...]))
    hpos = (jnp.dot(h, w3p_ref[...].astype(BF16), preferred_element_type=F32) + b3p_ref[...]) * win
    hneg = (jnp.dot(h, w3n_ref[...].astype(BF16), preferred_element_type=F32) + b3n_ref[...]) * win
    rows = lax.broadcasted_iota(jnp.int32, hpos.shape, 0)
    hneg = jnp.where(rows == 0, 0.0, hneg)
    norm = (jnp.sum(jnp.abs(hpos), axis=0, keepdims=True)
            + jnp.sum(jnp.abs(hneg), axis=0, keepdims=True))
    hpos = hpos / norm
    hneg = hneg / norm
    hsum = hpos + hneg
    hdiff = hpos - hneg
    even = (rows & 1) == 0
    alt2 = jnp.where(((rows >> 1) & 1) == 0, 1.0, -1.0).astype(F32)
    col_sum = lambda a: jnp.sum(a, axis=0, keepdims=True)
    a0 = 2.0 * col_sum(jnp.where(even, hsum, 0.0))
    d0 = 2.0 * col_sum(jnp.where(even, 0.0, hsum))
    hr2 = 2.0 * col_sum(jnp.where(even, alt2 * hsum, 0.0))
    hi2 = -2.0 * col_sum(jnp.where(even, 0.0, alt2 * hdiff))
    r0_ref[...] = jnp.concatenate([a0, d0, hr2, hi2, jnp.zeros((4, a0.shape[1]), F32)], axis=0) * inv_n
    def lags(x, par):
        _stage_rows(split_ref, x)
        return _rows_of_parity(split_ref, par).astype(BF16)

    hs_ref[...] = lags(hsum, 0)
    hd_ref[...] = lags(hdiff, 0)
    hp_ref[...] = lags(hpos, 1)
    hn_ref[...] = lags(hneg, 1)


def _hy_filter_taps(seq, w1, b1, w2, b2, freq, w3, b3):
    n_ch = 2 * BRANCH_W
    bands = (HY_EMB - 1) // 2
    t = jnp.linspace(0.0, 1.0, seq, dtype=F32)[:, None]
    w = 2.0 * math.pi * jnp.arange(seq, dtype=F32)[:, None] / seq
    f = jnp.linspace(1e-4, bands - 1, bands, dtype=F32)[None, :]
    feats = jnp.concatenate([t, jnp.cos(f * w), -jnp.sin(f * w),
                             jnp.zeros((seq, HY_PAD - HY_EMB), F32)], axis=-1)
    deltas = jnp.linspace(math.log(HY_DECAY_TARGET) / HY_SLOW_DECAY,
                          math.log(HY_DECAY_TARGET) / HY_FAST_DECAY, n_ch, dtype=F32)[None, :]

    def pad2(a, r, c):
        a = a.astype(F32)
        return jnp.pad(a, ((0, r - a.shape[0]), (0, c - a.shape[1])))

    w1p = pad2(w1, HY_PAD, HY_PAD)
    w2p = pad2(w2, HY_PAD, HY_PAD)
    b1p = pad2(b1[None], 1, HY_PAD)
    b2p = pad2(b2[None], 1, HY_PAD)
    f0p = pad2(freq[0][None], 1, HY_PAD)
    f1p = pad2(freq[1][None], 1, HY_PAD)
    w3p = pad2(w3, HY_PAD, 2 * n_ch)
    b3r = b3.astype(F32)[None]
    tn = 256
    nt = n_ch // tn
    full = lambda shape: pl.BlockSpec(shape, lambda j: (0, 0))
    tap_spec = pl.BlockSpec((seq // 2, tn), lambda j: (0, j))
    return pl.pallas_call(
        functools.partial(_hy_filter_kernel, inv_n=1.0 / (2 * seq)),
        out_shape=(jax.ShapeDtypeStruct((seq // 2, n_ch), BF16),) * 4
                  + (jax.ShapeDtypeStruct((8, n_ch), F32),),
        grid=(nt,),
        in_specs=[full((seq, HY_PAD)), full((HY_PAD, HY_PAD)), full((1, HY_PAD)),
                  full((HY_PAD, HY_PAD)), full((1, HY_PAD)), full((1, HY_PAD)), full((1, HY_PAD)),
                  pl.BlockSpec((HY_PAD, tn), lambda j: (0, j)),
                  pl.BlockSpec((HY_PAD, tn), lambda j: (0, j + nt)),
                  pl.BlockSpec((1, tn), lambda j: (0, j)),
                  pl.BlockSpec((1, tn), lambda j: (0, j + nt)),
                  pl.BlockSpec((1, tn), lambda j: (0, j)),
                  full((seq, 1))],
        out_specs=(tap_spec,) * 4 + (pl.BlockSpec((8, tn), lambda j: (0, j)),),
        scratch_shapes=[pltpu.VMEM((seq, HY_PAD), F32), pltpu.VMEM((tn // 128, seq, 128), F32)],
        compiler_params=_params("arbitrary"),
        name="hy_filter",
    )(feats, w1p, b1p, w2p, b2p, f0p, f1p, w3p, w3p, b3r, b3r, deltas, t)


def _hy_spectrum_kernel(c_ref, s_ref, hse_ref, hde_ref, hpo_ref, hno_ref,
                        ac_ref, as_ref, bc_ref, bs_ref, gc_ref, gs_ref, *, n_half):
    i = pl.program_id(1)
    dot = lambda w, h: jnp.dot(w[...], h[...], preferred_element_type=F32)
    hec, hes = dot(c_ref, hse_ref), dot(s_ref, hde_ref)
    upc, ups = dot(c_ref, hpo_ref), dot(s_ref, hpo_ref)
    umc, ums = dot(c_ref, hno_ref), dot(s_ref, hno_ref)
    tm = hec.shape[0]
    k = (lax.broadcasted_iota(jnp.int32, (tm, 128), 0) + i * tm).astype(F32)
    psi = k * (math.pi / n_half)
    reps = hec.shape[1] // 128
    cp = jnp.concatenate([jnp.cos(psi)] * reps, axis=1)
    sp = jnp.concatenate([jnp.sin(psi)] * reps, axis=1)
    w = 1.0 / n_half
    ac_ref[...] = w * hec
    as_ref[...] = w * hes
    bc_ref[...] = w * (cp * upc - sp * ups + umc)
    bs_ref[...] = w * (cp * ups + sp * upc - ums)
    gc_ref[...] = w * (upc + cp * umc - sp * ums)
    gs_ref[...] = w * (ups - cp * ums - sp * umc)


def _hy_spectrum(cm, s1, taps):
    half, n_ch = taps[0].shape
    tm = min(512, half)
    tn = 512
    w_spec = pl.BlockSpec((tm, half), lambda j, i: (i, 0))
    tap_spec = pl.BlockSpec((half, tn), lambda j, i: (0, j))
    o_spec = pl.BlockSpec((tm, tn), lambda j, i: (i, j))
    return pl.pallas_call(
        functools.partial(_hy_spectrum_kernel, n_half=half),
        out_shape=(jax.ShapeDtypeStruct((half, n_ch), F32),) * 6,
        grid=(n_ch // tn, half // tm),
        in_specs=[w_spec, w_spec] + [tap_spec] * 4,
        out_specs=(o_spec,) * 6,
        compiler_params=_params("parallel", "arbitrary"),
        name="hy_spectrum",
    )(cm, s1, *taps)


def _stage_rows(scr_ref, x):
    for c in range(x.shape[1] // 128):
        scr_ref[c] = x[:, c * 128:(c + 1) * 128]


def _rows_of_parity(scr_ref, par):
    n = scr_ref.shape[1] // 2
    return jnp.concatenate([scr_ref[c, pl.ds(par, n, stride=2), :] for c in range(scr_ref.shape[0])], axis=1)


def _hy_conv3_kernel(u_ref, w_ref, b_ref, o_ref, *rest):
    vb_ref, s_ref = rest if len(rest) == 2 else (None, rest[0])
    u = u_ref[...]
    n = u.shape[0]
    rows = lax.broadcasted_iota(jnp.int32, u.shape, 0)
    prev = jnp.where(rows == 0, 0.0, pltpu.roll(u, 1, 0))
    nxt = jnp.where(rows == n - 1, 0.0, pltpu.roll(u, n - 1, 0))
    w = w_ref[...]
    _stage_rows(s_ref, prev * w[0:1] + u * w[1:2] + nxt * w[2:3] + b_ref[...])
    for par in range(2):
        part = _rows_of_parity(s_ref, par)
        o_ref[par] = part
        if vb_ref is not None:
            vb_ref[par] = part.astype(BF16)


def _hy_conv3(proj, bsz, seq, conv_w, conv_b, col, width, with_bf16):
    half = seq // 2
    tn = 256
    o_spec = pl.BlockSpec((2, half, tn), lambda b, j: (0, b, j))
    shape = lambda dt: jax.ShapeDtypeStruct((2, bsz * half, width), dt)
    return pl.pallas_call(
        _hy_conv3_kernel,
        out_shape=(shape(F32), shape(BF16)) if with_bf16 else shape(F32),
        grid=(bsz, width // tn),
        in_specs=[pl.BlockSpec((seq, tn), lambda b, j: (b, j + (COL_BU + col) // tn)),
                  pl.BlockSpec((3, tn), lambda b, j: (0, j + col // tn)),
                  pl.BlockSpec((1, tn), lambda b, j: (0, j + col // tn))],
        out_specs=(o_spec, o_spec) if with_bf16 else o_spec,
        scratch_shapes=[pltpu.VMEM((tn // 128, seq, 128), F32)],
        compiler_params=_params("parallel", "parallel"),
        name="hy_conv3",
    )(proj, conv_w.astype(F32), conv_b.astype(F32).reshape(1, -1))


def _hy_fwd_kernel(c_ref, s_ref, ze_ref, zo_ref, ac_ref, as_ref, bc_ref, bs_ref, gc_ref, gs_ref, r0_ref,
                   pc_ref, ps_ref, qc_ref, qs_ref):
    dot = lambda w, z: jnp.dot(w[...], z[...], preferred_element_type=F32)
    ec, es = dot(c_ref, ze_ref), dot(s_ref, ze_ref)
    oc, os_ = dot(c_ref, zo_ref), dot(s_ref, zo_ref)
    a_c, a_s = ac_ref[...], as_ref[...]
    b_c, b_s = bc_ref[...], bs_ref[...]
    g_c, g_s = gc_ref[...], gs_ref[...]
    outs = (ec * a_c - es * a_s + oc * b_c - os_ * b_s,
            ec * a_s + es * a_c + oc * b_s + os_ * b_c,
            ec * g_c - es * g_s + oc * a_c - os_ * a_s,
            ec * g_s + es * g_c + oc * a_s + os_ * a_c)
    r0 = r0_ref[...]
    a0, d0, hr2, hi2 = r0[0:1], r0[1:2], r0[2:3], r0[3:4]
    e0, eh, o0, oh = ec[0:1], es[0:1], oc[0:1], os_[0:1]
    first = (e0 * a0 + o0 * d0, eh * hr2 + oh * hi2, e0 * d0 + o0 * a0, oh * hr2 - eh * hi2)
    top = 16
    is_row0 = (lax.broadcasted_iota(jnp.int32, (top, ec.shape[1]), 0) == 0) & (pl.program_id(2) == 0)
    for ref, val, row0 in zip((pc_ref, ps_ref, qc_ref, qs_ref), outs, first):
        ref[...] = val.astype(BF16)
        ref[0:top, :] = jnp.where(is_row0, row0, val[0:top]).astype(BF16)


HY_TN = 256
HY_TM = 1024


def _hy_fwd(cm, s1, z_even, z_odd, tables, r0, h_col, bsz, half):
    tm = min(HY_TM, half)
    tn = HY_TN
    mt = half // tm
    w_spec = pl.BlockSpec((tm, half), lambda b, j, i: (i, 0))
    h_spec = pl.BlockSpec((tm, tn), lambda b, j, i: (i, j + h_col // tn))
    o_spec = pl.BlockSpec((tm, tn), lambda b, j, i: (b * mt + i, j))
    return pl.pallas_call(
        _hy_fwd_kernel,
        out_shape=(jax.ShapeDtypeStruct((bsz * half, BRANCH_W), BF16),) * 4,
        grid=(bsz, BRANCH_W // tn, mt),
        in_specs=[w_spec, w_spec, z_even[1], z_odd[1]] + [h_spec] * 6
                 + [pl.BlockSpec((8, tn), lambda b, j, i: (0, j + h_col // tn))],
        out_specs=(o_spec,) * 4,
        compiler_params=_params("parallel", "parallel", "arbitrary"),
        name="hy_fwd",
    )(cm, s1, z_even[0], z_odd[0], *tables, r0)


def _hy_inv_convs(c_ref, s_ref, pc_ref, ps_ref, qc_ref, qs_ref):
    dot = lambda w, y: jnp.dot(w[...], y[...], preferred_element_type=F32)
    return dot(c_ref, pc_ref) + dot(s_ref, ps_ref), dot(c_ref, qc_ref) + dot(s_ref, qs_ref)


def _hy_inv_mid_kernel(c_ref, s_ref, pc_ref, ps_ref, qc_ref, qs_ref, ge_ref, go_ref, ze_ref, zo_ref,
                       bias_ref, o_ref, ob_ref):
    convs = _hy_inv_convs(c_ref, s_ref, pc_ref, ps_ref, qc_ref, qs_ref)
    for par, (conv, g_ref, z_ref) in enumerate(zip(convs, (ge_ref, go_ref), (ze_ref, zo_ref))):
        out = g_ref[...] * (conv + bias_ref[...] * z_ref[...])
        o_ref[par] = out
        ob_ref[par] = out.astype(BF16)


def _hy_inv_last_kernel(c_ref, s_ref, pc_ref, ps_ref, qc_ref, qs_ref, ge_ref, go_ref, ze_ref, zo_ref,
                        bias_ref, sz_ref, o_ref, mix_ref):
    convs = _hy_inv_convs(c_ref, s_ref, pc_ref, ps_ref, qc_ref, qs_ref)
    tm = ge_ref.shape[0]
    _stage_rows(mix_ref, sz_ref[...])
    outs = [g_ref[...] * (conv + bias_ref[...] * z_ref[...]) * _rows_of_parity(mix_ref, par)
            for par, (conv, g_ref, z_ref) in enumerate(zip(convs, (ge_ref, go_ref), (ze_ref, zo_ref)))]
    for par, out in enumerate(outs):
        for c in range(mix_ref.shape[0]):
            mix_ref[c, pl.ds(par, tm, stride=2), :] = out[:, c * 128:(c + 1) * 128]
    for c in range(mix_ref.shape[0]):
        o_ref[:, c * 128:(c + 1) * 128] = mix_ref[c]


def _hy_inv(cm, s2, spectra, gates, zprev, bias_row, bsz, half, silu=None):
    tm = min(HY_TM, half)
    tn = HY_TN
    mt = half // tm
    w_spec = pl.BlockSpec((tm, half), lambda b, j, i: (i, 0))
    y_spec = pl.BlockSpec((half, tn), lambda b, j, i: (b, j))
    pairs = list(gates) + list(zprev)
    in_specs = [w_spec, w_spec] + [y_spec] * 4 + [s for _, s in pairs] + [pl.BlockSpec((1, tn), lambda b, j, i: (0, j))]
    args = [cm, s2, *spectra] + [a for a, _ in pairs] + [bias_row]
    if silu is None:
        o_spec = pl.BlockSpec((2, tm, tn), lambda b, j, i: (0, b * mt + i, j))
        shape = lambda dt: jax.ShapeDtypeStruct((2, bsz * half, BRANCH_W), dt)
        body, out_shape, out_specs = _hy_inv_mid_kernel, (shape(F32), shape(BF16)), (o_spec, o_spec)
        scratch = []
    else:
        silu_arr, silu_col = silu
        body = _hy_inv_last_kernel
        out_shape = jax.ShapeDtypeStruct((2 * bsz * half, BRANCH_W), F32)
        out_specs = pl.BlockSpec((2 * tm, tn), lambda b, j, i: (b * mt + i, j))
        in_specs.append(pl.BlockSpec((2 * tm, tn), lambda b, j, i: (b * mt + i, j + silu_col // tn)))
        args.append(silu_arr)
        scratch = [pltpu.VMEM((tn // 128, 2 * tm, 128), F32)]
    return pl.pallas_call(
        body, out_shape=out_shape, grid=(bsz, BRANCH_W // tn, mt), in_specs=in_specs, out_specs=out_specs,
        scratch_shapes=scratch, compiler_params=_params("parallel", "parallel", "arbitrary"),
        name="hy_inv",
    )(*args)


def _hyena_branch(proj, bsz, seq, dft, conv_w, conv_b, w1, b1, w2, b2, freq, w3, b3, bias):
    cm, s1, s2 = dft
    half = seq // 2
    tm = min(HY_TM, half)
    tn = HY_TN
    mt = half // tm
    full = lambda par, col=0: pl.BlockSpec((None, half, tn), lambda b, j, i: (par, b, j + col // tn))
    row = lambda par, col=0: pl.BlockSpec((None, tm, tn), lambda b, j, i: (par, b * mt + i, j + col // tn))
    both = lambda arr, spec, col=0: ((arr, spec(0, col)), (arr, spec(1, col)))
    *taps, r0 = _hy_filter_taps(seq, w1, b1, w2, b2, freq, w3, b3)
    tables = _hy_spectrum(cm, s1, taps)
    v, vb = _hy_conv3(proj, bsz, seq, conv_w, conv_b, 0, BRANCH_W, True)
    x12 = _hy_conv3(proj, bsz, seq, conv_w, conv_b, BRANCH_W, 2 * BRANCH_W, False)
    bias = bias.astype(F32)
    spectra = _hy_fwd(cm, s1, *both(vb, full), tables, r0, 0, bsz, half)
    z1, z1b = _hy_inv(cm, s2, spectra, gates=both(x12, row), zprev=both(v, row),
                      bias_row=bias[0:1], bsz=bsz, half=half)
    spectra = _hy_fwd(cm, s1, *both(z1b, full), tables, r0, BRANCH_W, bsz, half)
    return _hy_inv(cm, s2, spectra, gates=both(x12, row, BRANCH_W), zprev=both(z1, row),
                   bias_row=bias[1:2], bsz=bsz, half=half, silu=(proj, COL_BZ))


def _rope_table_kernel(pos_ref, inv_ref, cos_ref, sin_ref):
    ang = pos_ref[...] * inv_ref[...]
    lane = lax.broadcasted_iota(jnp.int32, ang.shape, 1)
    live = lane < MLA_ROPE
    cos_ref[...] = jnp.where(live, jnp.cos(ang), 0.0)
    sin_ref[...] = jnp.where(live, jnp.where(lane < MLA_ROPE // 2, -1.0, 1.0) * jnp.sin(ang), 0.0)


def _rope_tables(positions):
    m = positions.size
    half = MLA_ROPE // 2
    inv = ROPE_BASE ** (-jnp.arange(half, dtype=F32) / half)
    inv = jnp.concatenate([inv, inv, jnp.zeros((128 - MLA_ROPE,), F32)])[None]
    pos = positions.astype(F32).reshape(m, 1)
    tm = min(1024, m)
    spec = pl.BlockSpec((tm, 128), lambda i: (i, 0))
    return pl.pallas_call(
        _rope_table_kernel,
        out_shape=(jax.ShapeDtypeStruct((m, 128), F32),) * 2,
        grid=(m // tm,),
        in_specs=[pl.BlockSpec((tm, 1), lambda i: (i, 0)), pl.BlockSpec((1, 128), lambda i: (0, 0))],
        out_specs=(spec, spec),
        compiler_params=_params("parallel"),
        name="rope_table",
    )(pos, inv)


def _rope128(x, cos_t, sin_t):
    lane = lax.broadcasted_iota(jnp.int32, x.shape, 1)
    half = MLA_ROPE // 2
    partner = jnp.where(lane < half, pltpu.roll(x, 128 - half, 1), pltpu.roll(x, half, 1))
    return x * cos_t + partner * sin_t


def _rms(x, g):
    ms = jnp.mean(jnp.square(x), axis=-1, keepdims=True)
    return x * lax.rsqrt(ms + RMS_EPS) * g


def _mla_q_kernel(cq_ref, g_ref, w_ref, cos_ref, sin_ref, q_ref, *, scale):
    xn = _rms(cq_ref[...], g_ref[...]).astype(BF16)
    q = jnp.dot(xn, w_ref[...], preferred_element_type=F32) * scale
    cos_t, sin_t = cos_ref[...], sin_ref[...]
    for h in range(MLA_HEADS):
        base = h * MLA_QK_PAD
        q_ref[:, base:base + MLA_NOPE] = q[:, base:base + MLA_NOPE].astype(BF16)
        q_ref[:, base + MLA_NOPE:base + MLA_QK_PAD] = _rope128(
            q[:, base + MLA_NOPE:base + MLA_QK_PAD], cos_t, sin_t).astype(BF16)


def _mla_kv_kernel(ckv_ref, g_ref, wk_ref, wv_ref, kr_ref, cos_ref, sin_ref, k_ref, v_ref):
    xn = _rms(ckv_ref[...], g_ref[...]).astype(BF16)
    kn = jnp.dot(xn, wk_ref[...], preferred_element_type=F32)
    v_ref[...] = jnp.dot(xn, wv_ref[...], preferred_element_type=F32).astype(BF16)
    kr = _rope128(kr_ref[:, 0:128], cos_ref[...], sin_ref[...]).astype(BF16)
    for h in range(MLA_HEADS):
        base = h * MLA_QK_PAD
        k_ref[:, base:base + MLA_NOPE] = kn[:, h * MLA_NOPE:(h + 1) * MLA_NOPE].astype(BF16)
        k_ref[:, base + MLA_NOPE:base + MLA_QK_PAD] = kr


def _mla_attn_kernel(q_ref, k_ref, v_ref, z_ref, o_ref):
    k, v = k_ref[...], v_ref[...]
    half = q_ref.shape[0] // 2
    for r in range(2):
        rows = pl.ds(r * half, half)
        s = lax.dot_general(q_ref[rows, :], k, (((1,), (1,)), ((), ())),
                            preferred_element_type=F32)
        p = jnp.exp2(s - jnp.max(s, axis=-1, keepdims=True))
        l = jnp.sum(p, axis=-1, keepdims=True)
        o = jnp.dot(p.astype(BF16), v, preferred_element_type=F32)
        o_ref[rows, :] = (o / l * z_ref[rows, :]).astype(BF16)


def _mla_branch(proj, proj_b, bsz, seq, rope, q_norm_g, w_uq, kv_norm_g, w_ukv):
    m = bsz * seq
    cos_t, sin_t = rope
    dqk = MLA_NOPE + MLA_ROPE
    hq = MLA_HEADS * MLA_QK_PAD
    w_q = w_uq.reshape(MLA_LORA, MLA_HEADS, dqk)
    w_q = jnp.pad(w_q, ((0, 0), (0, 0), (0, MLA_QK_PAD - dqk))).reshape(MLA_LORA, hq).astype(BF16)
    w_kv = w_ukv.reshape(MLA_LORA, MLA_HEADS, MLA_NOPE + MLA_V)
    w_k = w_kv[:, :, :MLA_NOPE].reshape(MLA_LORA, MLA_HEADS * MLA_NOPE).astype(BF16)
    w_v = w_kv[:, :, MLA_NOPE:].reshape(MLA_LORA, MLA_HEADS * MLA_V).astype(BF16)
    tm = min(512, m)
    row = lambda shape, col=0: pl.BlockSpec(shape, lambda i: (i, col))
    full = lambda shape: pl.BlockSpec(shape, lambda i: (0, 0))
    qp = pl.pallas_call(
        functools.partial(_mla_q_kernel, scale=dqk ** -0.5 * math.log2(math.e)),
        out_shape=jax.ShapeDtypeStruct((m, hq), BF16),
        grid=(m // tm,),
        in_specs=[row((tm, MLA_LORA), COL_CQ // MLA_LORA), full((1, MLA_LORA)), full((MLA_LORA, hq)),
                  row((tm, 128)), row((tm, 128))],
        out_specs=row((tm, hq)),
        compiler_params=_params("parallel"),
        name="mla_q",
    )(proj, q_norm_g.astype(F32).reshape(1, -1), w_q, cos_t, sin_t)
    kp, vp = pl.pallas_call(
        _mla_kv_kernel,
        out_shape=(jax.ShapeDtypeStruct((m, hq), BF16),
                   jax.ShapeDtypeStruct((m, MLA_HEADS * MLA_V), BF16)),
        grid=(m // tm,),
        in_specs=[row((tm, MLA_LORA), COL_CKV // MLA_LORA), full((1, MLA_LORA)),
                  full((MLA_LORA, MLA_HEADS * MLA_NOPE)), full((MLA_LORA, MLA_HEADS * MLA_V)),
                  row((tm, 512), COL_CKR // 512), row((tm, 128)), row((tm, 128))],
        out_specs=(row((tm, hq)), row((tm, MLA_HEADS * MLA_V))),
        compiler_params=_params("parallel"),
        name="mla_kv",
    )(proj, kv_norm_g.astype(F32).reshape(1, -1), w_k, w_v, proj_b, cos_t, sin_t)
    tq = min(512, seq)
    qt = seq // tq
    return pl.pallas_call(
        _mla_attn_kernel,
        out_shape=jax.ShapeDtypeStruct((m, MLA_HEADS * MLA_V), BF16),
        grid=(bsz, MLA_HEADS, qt),
        in_specs=[pl.BlockSpec((tq, MLA_QK_PAD), lambda b, h, i: (b * qt + i, h)),
                  pl.BlockSpec((seq, MLA_QK_PAD), lambda b, h, i: (b, h)),
                  pl.BlockSpec((seq, MLA_V), lambda b, h, i: (b, h)),
                  pl.BlockSpec((tq, MLA_V), lambda b, h, i: (b * qt + i, h + COL_CZ // MLA_V))],
        out_specs=pl.BlockSpec((tq, MLA_V), lambda b, h, i: (b * qt + i, h)),
        compiler_params=_params("parallel", "parallel", "arbitrary"),
        name="mla_attn",
    )(qp, kp, vp, proj_b)


def _lift_kernel(ya_ref, yb_ref, yc_ref, w_ref, ga_ref, gb_ref, gc_ref, o_ref, wb_ref):
    @pl.when(pl.program_id(1) == 0)
    def _():
        wb_ref[...] = w_ref[...].astype(BF16)

    acc = ga_ref[...] * jnp.dot(ya_ref[...], wb_ref[0], preferred_element_type=F32)
    acc += gb_ref[...] * jnp.dot(yb_ref[...].astype(BF16), wb_ref[1], preferred_element_type=F32)
    acc += gc_ref[...] * jnp.dot(yc_ref[...], wb_ref[2], preferred_element_type=F32)
    o_ref[...] = acc.astype(BF16)


def _lift(ya, yb, yc, w_lift, layer, proj_b):
    m = ya.shape[0]
    tm = min(1024, m)
    tn = 512
    y_spec = pl.BlockSpec((tm, BRANCH_W), lambda j, i: (i, 0))
    gate = lambda n: pl.BlockSpec((tm, tn), lambda j, i: (i, j + (COL_GATE + n * D_MODEL) // tn))
    return pl.pallas_call(
        _lift_kernel,
        out_shape=jax.ShapeDtypeStruct((m, D_MODEL), BF16),
        grid=(D_MODEL // tn, m // tm),
        in_specs=[y_spec, y_spec, y_spec,
                  pl.BlockSpec((None, N_BRANCH, BRANCH_W, tn), lambda j, i: (layer, 0, 0, j)),
                  gate(0), gate(1), gate(2)],
        out_specs=pl.BlockSpec((tm, tn), lambda j, i: (i, j)),
        scratch_shapes=[pltpu.VMEM((N_BRANCH, BRANCH_W, tn), BF16)],
        compiler_params=_params("parallel", "arbitrary"),
        name="lift",
    )(ya, yb, yc, w_lift, proj_b, proj_b, proj_b)


def _out_kernel(mix_ref, wo_ref, p_ref, wp_ref, sp_ref, x_ref, g_ref, b_ref, o_ref):
    half = mix_ref.shape[0] // 2
    for h in range(2):
        rows = pl.ds(h * half, half)
        mixed = jnp.dot(mix_ref[rows, :], wo_ref[...], preferred_element_type=F32)
        ple = jnp.dot(p_ref[rows, :].astype(BF16), wp_ref[...], preferred_element_type=F32) * sp_ref[rows, :]
        r = DEEPNORM_ALPHA * x_ref[rows, :] + mixed + ple
        mu = jnp.mean(r, axis=-1, keepdims=True)
        var = jnp.mean(jnp.square(r - mu), axis=-1, keepdims=True)
        o_ref[rows, :] = (r - mu) * lax.rsqrt(var + LN_EPS) * g_ref[...] + b_ref[...]


def _out_norm(mix, w_out, p, w_ple, proj, x, ln_g, ln_b):
    m = mix.shape[0]
    tm = min(512, m)
    row = lambda w, col=0: pl.BlockSpec((tm, w), lambda i: (i, col))
    full = lambda shape: pl.BlockSpec(shape, lambda i: (0, 0))
    return pl.pallas_call(
        _out_kernel,
        out_shape=jax.ShapeDtypeStruct((m, D_MODEL), F32),
        grid=(m // tm,),
        in_specs=[row(D_MODEL), full((D_MODEL, D_MODEL)), row(PLE_DIM), full((PLE_DIM, D_MODEL)),
                  row(D_MODEL, COL_PLE // D_MODEL), row(D_MODEL), full((1, D_MODEL)), full((1, D_MODEL))],
        out_specs=row(D_MODEL),
        compiler_params=_params("parallel"),
        name="out_norm",
    )(mix, w_out.astype(BF16), p, w_ple.astype(BF16), proj, x,
      ln_g.astype(F32).reshape(1, -1), ln_b.astype(F32).reshape(1, -1))


def kernel(x, p, positions, w_in, s5_lambda_re, s5_lambda_im, s5_log_dt, s5_b_re, s5_b_im, s5_c_re, s5_c_im, s5_d, s5_w_glu, s5_b_glu, hy_conv_w, hy_conv_b, hy_w1, hy_b1, hy_w2, hy_b2, hy_freq, hy_w3, hy_b3, hy_bias, mla_q_norm, mla_w_uq, mla_kv_norm, mla_w_ukv, w_lift, w_out, w_ple, ln_g, ln_b):
    bsz, seq, _ = x.shape
    m = bsz * seq
    depth = w_in.shape[0]
    dft = _dft_matrices(seq // 2)
    rope = _rope_tables(positions)
    s5_tiled = _s5_tiled_params(s5_lambda_re, s5_lambda_im, s5_log_dt, s5_b_re, s5_b_im,
                                s5_c_re, s5_c_im, s5_d)
    xf = x.reshape(m, D_MODEL).astype(F32)
    w_t = jnp.swapaxes(w_in, 1, 2)
    for i in range(depth):
        proj, proj_b = _proj_both(xf, w_t, i)
        y_a = _s5_branch(proj, bsz, seq, [a[i] for a in s5_tiled], s5_w_glu[i], s5_b_glu[i])
        y_b = _hyena_branch(proj, bsz, seq, dft, hy_conv_w[i], hy_conv_b[i], hy_w1[i], hy_b1[i],
                            hy_w2[i], hy_b2[i], hy_freq[i], hy_w3[i], hy_b3[i], hy_bias[i])
        y_c = _mla_branch(proj, proj_b, bsz, seq, rope, mla_q_norm[i], mla_w_uq[i], mla_kv_norm[i], mla_w_ukv[i])
        mix = _lift(y_a, y_b, y_c, w_lift, i, proj_b)
        xf = _out_norm(mix, w_out[i], p[i].reshape(m, PLE_DIM), w_ple[i], proj_b, xf, ln_g[i], ln_b[i])
    return xf.reshape(bsz, seq, D_MODEL).astype(x.dtype)
```

```python
import functools
import math

import numpy as np
import jax
import jax.numpy as jnp
from jax import lax
from jax.experimental import pallas as pl
from jax.experimental.pallas import tpu as pltpu

F32 = jnp.float32
BF16 = jnp.bfloat16

D_MODEL = 2048
PLE_DIM = 256
N_BRANCH = 3
BRANCH_W = 1024

S5_GROUP = 16
S5_GROUPS = BRANCH_W // S5_GROUP
S5_STATE = 64
S5_CHUNK = 16
S5_ROW = S5_CHUNK * S5_GROUP

HY_EMB = 33
HY_FF = 64
HY_PAD = 128
HY_DECAY_TARGET = 0.01
HY_FAST_DECAY = 0.3
HY_SLOW_DECAY = 1.5
DFT_ROWS = 64

MLA_HEADS = 8
MLA_NOPE = 128
MLA_ROPE = 64
MLA_V = 128
MLA_LORA = 512
MLA_QK_PAD = 256
ROPE_BASE = 10000.0

LN_EPS = 1e-5
RMS_EPS = 1e-6
DEPTH = 2
DEEPNORM_ALPHA = (2 * DEPTH) ** 0.25

COL_AX = 0
COL_AZ = 1024
COL_BU = 2048
COL_BZ = 5120
COL_CQ = 6144
COL_CKV = 6656
PROJ_KEEP = 7168
COL_CZ = 7168
COL_PLE = 8192
COL_GATE = 10240
COL_CKR = 16384
PROJ_N = 16896
PROJ_TN = 512
W_IN_CKR = 7168
W_IN_CZ = 7232
W_IN_GATE = 8256
W_IN_PLE = 14400

VMEM_LIMIT = 56 * 1024 * 1024


def _params(*sem):
    return pltpu.CompilerParams(dimension_semantics=sem, vmem_limit_bytes=VMEM_LIMIT)


def _sigmoid(x):
    return 0.5 * jnp.tanh(0.5 * x) + 0.5


def _in_tiles(j, ranges):
    hit = None
    for lo, hi in ranges:
        cur = (j >= lo // PROJ_TN) & (j < hi // PROJ_TN)
        hit = cur if hit is None else (hit | cur)
    return hit


def _proj_kernel(x_ref, w_ref, o_ref, xb_ref, *, silu_cols, sigm_cols):
    j = pl.program_id(1)

    @pl.when(j == 0)
    def _():
        xb_ref[...] = x_ref[...].astype(BF16)

    acc = lax.dot_general(xb_ref[...], w_ref[...].astype(BF16), (((1,), (1,)), ((), ())),
                          preferred_element_type=F32)
    is_silu = _in_tiles(j, silu_cols)
    plain = jnp.logical_not(is_silu)

    @pl.when(is_silu)
    def _():
        o_ref[...] = acc * _sigmoid(acc)

    if sigm_cols:
        is_sigm = _in_tiles(j, sigm_cols)
        plain = jnp.logical_not(is_silu | is_sigm)

        @pl.when(is_sigm)
        def _():
            o_ref[...] = _sigmoid(acc)

    @pl.when(plain)
    def _():
        o_ref[...] = acc


def _proj(x, w, layer, n_out, silu_cols, sigm_cols, name):
    m, k = x.shape
    tm = min(1024, m)
    return pl.pallas_call(
        functools.partial(_proj_kernel, silu_cols=silu_cols, sigm_cols=sigm_cols),
        out_shape=jax.ShapeDtypeStruct((m, n_out), F32),
        grid=(m // tm, n_out // PROJ_TN),
        in_specs=[pl.BlockSpec((tm, k), lambda i, j: (i, 0)),
                  pl.BlockSpec((None, PROJ_TN, k), lambda i, j: (layer, j, 0))],
        out_specs=pl.BlockSpec((tm, PROJ_TN), lambda i, j: (i, j)),
        scratch_shapes=[pltpu.VMEM((tm, k), BF16)],
        compiler_params=_params("parallel", "arbitrary"),
        name=name,
    )(x, w)


def _repack_src_tile(j):
    t = lambda col: col // PROJ_TN
    return jnp.where(j < t(COL_CZ), j,
                     jnp.where(j < t(COL_PLE), j - t(COL_CZ) + t(W_IN_CZ),
                               jnp.where(j < t(COL_GATE), j - t(COL_PLE) + t(W_IN_PLE),
                                         jnp.where(j < t(COL_CKR), j - t(COL_GATE) + t(W_IN_GATE),
                                                   t(W_IN_CKR)))))


def _repack_kernel(a_ref, b_ref, o_ref):
    j = pl.program_id(0)
    a = a_ref[...]
    shifted = jnp.concatenate([a[MLA_ROPE:], b_ref[...]], axis=0)
    key = jnp.concatenate([a[:MLA_ROPE], jnp.zeros_like(a[MLA_ROPE:])], axis=0)
    out = jnp.where(j < PROJ_KEEP // PROJ_TN, a, jnp.where(j == COL_CKR // PROJ_TN, key, shifted))
    o_ref[...] = out.astype(BF16)


def _repack(w_t, layer):
    k = w_t.shape[2]
    sub = PROJ_TN // MLA_ROPE
    return pl.pallas_call(
        _repack_kernel,
        out_shape=jax.ShapeDtypeStruct((1, PROJ_N, k), BF16),
        grid=(PROJ_N // PROJ_TN,),
        in_specs=[pl.BlockSpec((None, PROJ_TN, k), lambda j: (layer, _repack_src_tile(j), 0)),
                  pl.BlockSpec((None, MLA_ROPE, k), lambda j: (layer, (_repack_src_tile(j) + 1) * sub, 0))],
        out_specs=pl.BlockSpec((None, PROJ_TN, k), lambda j: (0, j, 0)),
        compiler_params=_params("parallel"),
        name="repack",
    )(w_t, w_t)


def _proj_all(x, w_t, layer):
    return _proj(x, _repack(w_t, layer), 0, PROJ_N,
                 ((COL_AZ, COL_BU), (COL_BZ, COL_CQ), (COL_CZ, COL_PLE)), ((COL_PLE, COL_CKR),), "proj")


S5_LANES = 8 * S5_STATE
S5_SLAB = 8
S5_RELAYOUT_ROWS = 32


def _s5_tile_lanes(a):
    return jnp.concatenate([a[:, 0]] * 4 + [a[:, 1]] * 4, axis=-1)


def _s5_tiled_params(lam_re, lam_im, log_dt, b_re, b_im, c_re, c_im, d):
    f = lambda a: a.astype(F32)
    depth = lam_re.shape[0]
    ldt = jnp.broadcast_to(f(log_dt)[..., None, None], lam_re.shape[:3] + (1, S5_STATE))
    return (_s5_tile_lanes(f(lam_re)[:, :, :, None, :]), _s5_tile_lanes(f(lam_im)[:, :, :, None, :]),
            _s5_tile_lanes(ldt),
            _s5_tile_lanes(jnp.swapaxes(f(b_re), -1, -2)), _s5_tile_lanes(jnp.swapaxes(f(b_im), -1, -2)),
            _s5_tile_lanes(f(c_re)), _s5_tile_lanes(f(c_im)),
            f(d).reshape(depth, S5_GROUPS, S5_GROUP, 1))


def _s5_mats_kernel(lr_ref, li_ref, ldt_ref, br_ref, bi_ref, cr_ref, ci_ref, d_ref,
                    ms_ref, mi_ref, mo_ref, ar_ref, ai_ref):
    t_n, h_n = S5_CHUNK, S5_GROUP
    hi = lax.Precision.HIGHEST
    nt = (((1,), (1,)), ((), ()))
    blk = lax.broadcasted_iota(jnp.int32, (1, S5_LANES), 1) // S5_STATE
    is_im = (blk // 2) % 2 == 1
    is_fwd = blk < 4
    steps = lax.broadcasted_iota(jnp.int32, (24, S5_LANES), 0).astype(F32)
    lane_k = lax.broadcasted_iota(jnp.int32, (h_n, S5_ROW), 1)
    sub_k = lax.broadcasted_iota(jnp.int32, (h_n, S5_ROW), 0)

    def per_group(gi, carry):
        lr, li = lr_ref[gi], li_ref[gi]
        dt = jnp.exp(ldt_ref[gi])
        zr, zi = lr * dt, li * dt
        mag = jnp.exp(steps * zr)
        tr, ti = mag * jnp.cos(steps * zi), mag * jnp.sin(steps * zi)
        lbr, lbi = tr[1:2], ti[1:2]
        n2 = lr * lr + li * li
        qr = ((lbr - 1.0) * lr + lbi * li) / n2
        qi = (lbi * lr - (lbr - 1.0) * li) / n2
        br, bi = br_ref[gi], bi_ref[gi]
        bbr, bbi = qr * br - qi * bi, qr * bi + qi * br
        y1, y2 = jnp.where(is_im, bbi, bbr), jnp.where(is_im, bbr, bbi)
        cr, ci = cr_ref[gi], ci_ref[gi]

        def pick(tab, t_fwd, t_bwd):
            return jnp.where(is_fwd, tab[t_fwd:t_fwd + 1], tab[t_bwd:t_bwd + 1])

        def c_times(p_r, p_i):
            return jnp.where(is_im, -(cr * p_i + ci * p_r), cr * p_r - ci * p_i)

        ct = [c_times(tr[s:s + 1], ti[s:s + 1]) for s in range(t_n + 1)]
        q_rows = []
        for t in range(t_n):
            a_r, a_i = pick(tr, t_n - 1 - t, t), pick(ti, t_n - 1 - t, t)
            rows = pl.ds(t * h_n, h_n)
            ms_ref[gi, rows, :] = (a_r * y1 + jnp.where(is_im, a_i, -a_i) * y2).astype(BF16)
            mo_ref[gi, rows, :] = jnp.where(is_fwd, ct[t + 1], ct[t_n - t]).astype(BF16)
            q_rows.append(jnp.where(is_fwd, ct[t], ct[t_n - 1 - t]))
        q = jnp.concatenate(q_rows, axis=0)
        half = S5_LANES // 2
        kf = 0.5 * lax.dot_general(y1[:, :half], q[:, :half], nt, precision=hi, preferred_element_type=F32)
        kb = 0.5 * lax.dot_general(y1[:, half:], q[:, half:], nt, precision=hi, preferred_element_type=F32)
        kf = kf + jnp.where(lane_k == sub_k, d_ref[gi], 0.0)
        for t in range(t_n):
            fwd = kf if t == 0 else jnp.where(lane_k >= h_n * t, pltpu.roll(kf, h_n * t, 1), 0.0)
            sh = h_n * (t_n - 1 - t)
            bwd = kb if sh == 0 else jnp.where(lane_k < S5_ROW - sh, pltpu.roll(kb, S5_ROW - sh, 1), 0.0)
            mi_ref[gi, pl.ds(t * h_n, h_n), :] = (fwd + bwd).astype(BF16)
        ar_ref[gi] = tr[t_n:t_n + 1]
        ai_ref[gi] = ti[t_n:t_n + 1]
        return carry

    lax.fori_loop(0, lr_ref.shape[0], per_group, 0)


def _s5_mats(tiled):
    g_n = S5_GROUPS
    gb = S5_SLAB
    spec = lambda r, w: pl.BlockSpec((gb, r, w), lambda j: (j, 0, 0))
    in_rows = (1, 1, 1, S5_GROUP, S5_GROUP, S5_GROUP, S5_GROUP)
    return pl.pallas_call(
        _s5_mats_kernel,
        out_shape=(jax.ShapeDtypeStruct((g_n, S5_ROW, S5_LANES), BF16),
                   jax.ShapeDtypeStruct((g_n, S5_ROW, S5_ROW), BF16),
                   jax.ShapeDtypeStruct((g_n, S5_ROW, S5_LANES), BF16),
                   jax.ShapeDtypeStruct((g_n, 1, S5_LANES), F32),
                   jax.ShapeDtypeStruct((g_n, 1, S5_LANES), F32)),
        grid=(g_n // gb,),
        in_specs=[spec(r, S5_LANES) for r in in_rows] + [spec(S5_GROUP, 1)],
        out_specs=(spec(S5_ROW, S5_LANES), spec(S5_ROW, S5_ROW), spec(S5_ROW, S5_LANES),
                   spec(1, S5_LANES), spec(1, S5_LANES)),
        compiler_params=_params("parallel"),
        name="s5_mats",
    )(*tiled)


def _seg_transpose(vs, seg):
    vs = list(vs)
    for s in (4, 2, 1):
        keep = (seg & s) == 0
        for i in range(8):
            if i & s:
                continue
            a, b = vs[i], vs[i + s]
            vs[i] = jnp.where(keep, a, pltpu.roll(b, s * S5_GROUP, 1))
            vs[i + s] = jnp.where(keep, pltpu.roll(a, 128 - s * S5_GROUP, 1), b)
    return vs


def _s5_main_kernel(x_ref, ms_ref, mi_ref, mo_ref, ar_ref, ai_ref, y_ref,
                    u_ref, sl_ref, st_ref, yg_ref, *, bsz):
    rows = x_ref.shape[0]
    n_chunks = rows // bsz
    rc = S5_RELAYOUT_ROWS
    seg = lax.broadcasted_iota(jnp.int32, (rc, 128), 1) // S5_GROUP
    slot = (lax.broadcasted_iota(jnp.int32, (1, S5_LANES), 1) // S5_STATE) % 2
    slot128 = slot[:, :128]

    def relayout_in(r, carry):
        r0 = pl.multiple_of(r * rc, rc)
        for th in range(2):
            src = [x_ref[pl.ds(r0, rc), th * 8 + t8, :] for t8 in range(8)]
            for gi, out in enumerate(_seg_transpose(src, seg)):
                u_ref[gi, pl.ds(r0, rc), th * 128:(th + 1) * 128] = out.astype(BF16)
        return carry

    lax.fori_loop(0, rows // rc, relayout_in, 0, unroll=2)

    n_pairs = S5_SLAB // 2
    trans = []
    for jp in range(n_pairs):
        g0, g1 = 2 * jp, 2 * jp + 1
        r0 = jnp.dot(u_ref[g0], ms_ref[g0], preferred_element_type=F32)
        r1 = jnp.dot(u_ref[g1], ms_ref[g1], preferred_element_type=F32)
        loc = jnp.where(slot == 0, r0, r1)
        for k in range(4):
            for b in range(bsz):
                sl_ref[k, jp, pl.ds(b, n_chunks, stride=bsz), :] = (
                    loc[b * n_chunks:(b + 1) * n_chunks, k * 128:(k + 1) * 128])
        trans.append([jnp.where(slot128 == 0, ref[g0][:, off:off + 128], ref[g1][:, off:off + 128])
                      for ref, off in ((ar_ref, 0), (ai_ref, 0), (ar_ref, 256), (ai_ref, 256))])

    zero = jnp.zeros((bsz, 128), F32)

    def scan(c, carry):
        rf = pl.ds(pl.multiple_of(c * bsz, bsz), bsz)
        rb = pl.ds(pl.multiple_of((n_chunks - 1 - c) * bsz, bsz), bsz)
        new = []
        for jp in range(n_pairs):
            s_fr, s_fi, s_br, s_bi = carry[jp]
            a_fr, a_fi, a_br, a_bi = trans[jp]
            st_ref[0, jp, rf, :] = s_fr
            st_ref[1, jp, rf, :] = s_fi
            st_ref[2, jp, rb, :] = s_br
            st_ref[3, jp, rb, :] = s_bi
            new.append((a_fr * s_fr - a_fi * s_fi + sl_ref[0, jp, rf, :],
                        a_fr * s_fi + a_fi * s_fr + sl_ref[1, jp, rf, :],
                        a_br * s_br - a_bi * s_bi + sl_ref[2, jp, rb, :],
                        a_br * s_bi + a_bi * s_br + sl_ref[3, jp, rb, :]))
        return tuple(new)

    lax.fori_loop(0, n_chunks, scan, tuple((zero,) * 4 for _ in range(n_pairs)))

    nt = (((1,), (1,)), ((), ()))
    for jp in range(n_pairs):
        st = jnp.concatenate(
            [jnp.concatenate([st_ref[k, jp, pl.ds(b, n_chunks, stride=bsz), :] for b in range(bsz)], axis=0)
             for k in range(4)], axis=1)
        for e in range(2):
            g = 2 * jp + e
            st_g = jnp.where(slot == e, st, 0.0).astype(BF16)
            y = (jnp.dot(u_ref[g], mi_ref[g], preferred_element_type=F32)
                 + lax.dot_general(st_g, mo_ref[g], nt, preferred_element_type=F32))
            yg_ref[g] = jax.nn.gelu(y)

    def relayout_out(r, carry):
        r0 = pl.multiple_of(r * rc, rc)
        for th in range(2):
            src = [yg_ref[gi, pl.ds(r0, rc), th * 128:(th + 1) * 128] for gi in range(S5_SLAB)]
            for t8, out in enumerate(_seg_transpose(src, seg)):
                y_ref[pl.ds(r0, rc), th * 8 + t8, :] = out
        return carry

    lax.fori_loop(0, rows // rc, relayout_out, 0, unroll=2)


def _s5_glu_kernel(g_ref, w_ref, b_ref, z_ref, o_ref):
    g = g_ref[...]
    acc = jnp.dot(g.astype(BF16), w_ref[...], preferred_element_type=F32) + b_ref[...]
    o_ref[...] = (g * _sigmoid(acc) * z_ref[...]).astype(o_ref.dtype)


def _s5_branch(proj, bsz, seq, tiled, w_glu, b_glu):
    m_state, m_intra, m_out, a_re, a_im = _s5_mats(tiled)
    t_n = S5_CHUNK
    rows = bsz * (seq // t_n)
    m = bsz * seq
    gb = S5_SLAB
    x3 = proj.reshape(rows, t_n, proj.shape[1])
    mat = lambda r, w: pl.BlockSpec((gb, r, w), lambda s: (s, 0, 0))
    io_spec = pl.BlockSpec((rows, t_n, 128), lambda s: (0, 0, s + COL_AX // 128))
    y = pl.pallas_call(
        functools.partial(_s5_main_kernel, bsz=bsz),
        out_shape=jax.ShapeDtypeStruct((rows, t_n, BRANCH_W), F32),
        grid=(S5_GROUPS // gb,),
        in_specs=[io_spec, mat(S5_ROW, S5_LANES), mat(S5_ROW, S5_ROW), mat(S5_ROW, S5_LANES),
                  mat(1, S5_LANES), mat(1, S5_LANES)],
        out_specs=pl.BlockSpec((rows, t_n, 128), lambda s: (0, 0, s)),
        scratch_shapes=[pltpu.VMEM((gb, rows, S5_ROW), BF16),
                        pltpu.VMEM((4, gb // 2, rows, 128), F32),
                        pltpu.VMEM((4, gb // 2, rows, 128), F32),
                        pltpu.VMEM((gb, rows, S5_ROW), F32)],
        compiler_params=_params("parallel"),
        name="s5_main",
    )(x3, m_state, m_intra, m_out, a_re, a_im)
    y = y.reshape(m, BRANCH_W)
    tm = min(512, m)
    return pl.pallas_call(
        _s5_glu_kernel,
        out_shape=jax.ShapeDtypeStruct((m, BRANCH_W), BF16),
        grid=(m // tm,),
        in_specs=[pl.BlockSpec((tm, BRANCH_W), lambda i: (i, 0)),
                  pl.BlockSpec((BRANCH_W, BRANCH_W), lambda i: (0, 0)),
                  pl.BlockSpec((1, BRANCH_W), lambda i: (0, 0)),
                  pl.BlockSpec((tm, BRANCH_W), lambda i: (i, COL_AZ // BRANCH_W))],
        out_specs=pl.BlockSpec((tm, BRANCH_W), lambda i: (i, 0)),
        compiler_params=_params("parallel"),
        name="s5_glu",
    )(y, w_glu.astype(BF16), b_glu.astype(F32).reshape(1, BRANCH_W), proj)


def _dft_tables(seq):
    n = 2 * seq
    mm = np.arange(seq, dtype=np.int64)
    k1 = np.arange(seq // DFT_ROWS, dtype=np.int64)[:, None] * DFT_ROWS
    k0 = np.arange(DFT_ROWS, dtype=np.int64)[:, None]
    ang_a = 2.0 * np.pi * ((k1 * mm) % n).astype(np.float64) / n
    ang_b = 2.0 * np.pi * ((k0 * mm) % n).astype(np.float64) / n
    return tuple(jnp.asarray(t, F32) for t in (np.cos(ang_a), np.sin(ang_a), np.cos(ang_b), np.sin(ang_b)))


def _dft_gen_kernel(ac_ref, as_ref, bc_ref, bs_ref, c_ref, s1_ref, s2_ref):
    i = pl.program_id(0)
    a_c = ac_ref[pl.ds(i, 1), :]
    a_s = as_ref[pl.ds(i, 1), :]
    b_c, b_s = bc_ref[...], bs_ref[...]
    cos_t = a_c * b_c - a_s * b_s
    sin_t = a_s * b_c + a_c * b_s
    rows = lax.broadcasted_iota(jnp.int32, cos_t.shape, 0) + i * DFT_ROWS
    cols = lax.broadcasted_iota(jnp.int32, cos_t.shape, 1)
    alt_cols = jnp.where((cols & 1) == 0, 1.0, -1.0).astype(F32)
    alt_rows = jnp.where((rows & 1) == 0, 1.0, -1.0).astype(F32)
    c_ref[...] = cos_t.astype(BF16)
    s1_ref[...] = jnp.where(rows == 0, alt_cols, sin_t).astype(BF16)
    s2_ref[...] = jnp.where(cols == 0, alt_rows, sin_t).astype(BF16)


def _dft_matrices(seq):
    tabs = _dft_tables(seq)
    n_steps = seq // DFT_ROWS
    tab_spec = pl.BlockSpec(tabs[0].shape, lambda i: (0, 0))
    b_spec = pl.BlockSpec((DFT_ROWS, seq), lambda i: (0, 0))
    o_spec = pl.BlockSpec((DFT_ROWS, seq), lambda i: (i, 0))
    return pl.pallas_call(
        _dft_gen_kernel,
        out_shape=(jax.ShapeDtypeStruct((seq, seq), BF16),) * 3,
        grid=(n_steps,),
        in_specs=[tab_spec, tab_spec, b_spec, b_spec],
        out_specs=(o_spec,) * 3,
        compiler_params=_params("parallel"),
        name="dft_gen",
    )(*tabs)


def _hy_filter_kernel(feat_ref, w1_ref, b1_ref, w2_ref, b2_ref, f0_ref, f1_ref,
                      w3p_ref, w3n_ref, b3p_ref, b3n_ref, dl_ref, t_ref,
                      hs_ref, hd_ref, hp_ref, hn_ref, r0_ref, h_ref, split_ref, *, inv_n):
    hi = lax.Precision.HIGHEST

    @pl.when(pl.program_id(0) == 0)
    def _():
        h1 = jnp.sin(f0_ref[...] * (jnp.dot(feat_ref[...], w1_ref[...], precision=hi,
                                            preferred_element_type=F32) + b1_ref[...]))
        h_ref[...] = jnp.sin(f1_ref[...] * (jnp.dot(h1, w2_ref[...], precision=hi,
                                                    preferred_element_type=F32) + b2_ref[...]))

    h = h_ref[...].astype(BF16)
    win = jnp.exp(-t_ref[...] * jnp.abs(dl_ref[...]))
    hpos = (jnp.dot(h, w3p_ref[...].astype(BF16), preferred_element_type=F32) + b3p_ref[...]) * win
    hneg = (jnp.dot(h, w3n_ref[...].astype(BF16), preferred_element_type=F32) + b3n_ref[...]) * win
    rows = lax.broadcasted_iota(jnp.int32, hpos.shape, 0)
    hneg = jnp.where(rows == 0, 0.0, hneg)
    norm = (jnp.sum(jnp.abs(hpos), axis=0, keepdims=True)
            + jnp.sum(jnp.abs(hneg), axis=0, keepdims=True))
    hpos = hpos / norm
    hneg = hneg / norm
    hsum = hpos + hneg
    hdiff = hpos - hneg
    even = (rows & 1) == 0
    alt2 = jnp.where(((rows >> 1) & 1) == 0, 1.0, -1.0).astype(F32)
    col_sum = lambda a: jnp.sum(a, axis=0, keepdims=True)
    a0 = 2.0 * col_sum(jnp.where(even, hsum, 0.0))
    d0 = 2.0 * col_sum(jnp.where(even, 0.0, hsum))
    hr2 = 2.0 * col_sum(jnp.where(even, alt2 * hsum, 0.0))
    hi2 = -2.0 * col_sum(jnp.where(even, 0.0, alt2 * hdiff))
    r0_ref[...] = jnp.concatenate([a0, d0, hr2, hi2, jnp.zeros((4, a0.shape[1]), F32)], axis=0) * inv_n
    def lags(x, par):
        _stage_rows(split_ref, x)
        return _rows_of_parity(split_ref, par).astype(BF16)

    hs_ref[...] = lags(hsum, 0)
    hd_ref[...] = lags(hdiff, 0)
    hp_ref[...] = lags(hpos, 1)
    hn_ref[...] = lags(hneg, 1)


def _hy_filter_taps(seq, w1, b1, w2, b2, freq, w3, b3):
    n_ch = 2 * BRANCH_W
    bands = (HY_EMB - 1) // 2
    t = jnp.linspace(0.0, 1.0, seq, dtype=F32)[:, None]
    w = 2.0 * math.pi * jnp.arange(seq, dtype=F32)[:, None] / seq
    f = jnp.linspace(1e-4, bands - 1, bands, dtype=F32)[None, :]
    feats = jnp.concatenate([t, jnp.cos(f * w), -jnp.sin(f * w),
                             jnp.zeros((seq, HY_PAD - HY_EMB), F32)], axis=-1)
    deltas = jnp.linspace(math.log(HY_DECAY_TARGET) / HY_SLOW_DECAY,
                          math.log(HY_DECAY_TARGET) / HY_FAST_DECAY, n_ch, dtype=F32)[None, :]

    def pad2(a, r, c):
        a = a.astype(F32)
        return jnp.pad(a, ((0, r - a.shape[0]), (0, c - a.shape[1])))

    w1p = pad2(w1, HY_PAD, HY_PAD)
    w2p = pad2(w2, HY_PAD, HY_PAD)
    b1p = pad2(b1[None], 1, HY_PAD)
    b2p = pad2(b2[None], 1, HY_PAD)
    f0p = pad2(freq[0][None], 1, HY_PAD)
    f1p = pad2(freq[1][None], 1, HY_PAD)
    w3p = pad2(w3, HY_PAD, 2 * n_ch)
    b3r = b3.astype(F32)[None]
    tn = 256
    nt = n_ch // tn
    full = lambda shape: pl.BlockSpec(shape, lambda j: (0, 0))
    tap_spec = pl.BlockSpec((seq // 2, tn), lambda j: (0, j))
    return pl.pallas_call(
        functools.partial(_hy_filter_kernel, inv_n=1.0 / (2 * seq)),
        out_shape=(jax.ShapeDtypeStruct((seq // 2, n_ch), BF16),) * 4
                  + (jax.ShapeDtypeStruct((8, n_ch), F32),),
        grid=(nt,),
        in_specs=[full((seq, HY_PAD)), full((HY_PAD, HY_PAD)), full((1, HY_PAD)),
                  full((HY_PAD, HY_PAD)), full((1, HY_PAD)), full((1, HY_PAD)), full((1, HY_PAD)),
                  pl.BlockSpec((HY_PAD, tn), lambda j: (0, j)),
                  pl.BlockSpec((HY_PAD, tn), lambda j: (0, j + nt)),
                  pl.BlockSpec((1, tn), lambda j: (0, j)),
                  pl.BlockSpec((1, tn), lambda j: (0, j + nt)),
                  pl.BlockSpec((1, tn), lambda j: (0, j)),
                  full((seq, 1))],
        out_specs=(tap_spec,) * 4 + (pl.BlockSpec((8, tn), lambda j: (0, j)),),
        scratch_shapes=[pltpu.VMEM((seq, HY_PAD), F32), pltpu.VMEM((tn // 128, seq, 128), F32)],
        compiler_params=_params("arbitrary"),
        name="hy_filter",
    )(feats, w1p, b1p, w2p, b2p, f0p, f1p, w3p, w3p, b3r, b3r, deltas, t)


def _hy_spectrum_kernel(c_ref, s_ref, hse_ref, hde_ref, hpo_ref, hno_ref,
                        ac_ref, as_ref, bc_ref, bs_ref, gc_ref, gs_ref, *, n_half):
    i = pl.program_id(1)
    dot = lambda w, h: jnp.dot(w[...], h[...], preferred_element_type=F32)
    hec, hes = dot(c_ref, hse_ref), dot(s_ref, hde_ref)
    upc, ups = dot(c_ref, hpo_ref), dot(s_ref, hpo_ref)
    umc, ums = dot(c_ref, hno_ref), dot(s_ref, hno_ref)
    tm = hec.shape[0]
    k = (lax.broadcasted_iota(jnp.int32, (tm, 128), 0) + i * tm).astype(F32)
    psi = k * (math.pi / n_half)
    reps = hec.shape[1] // 128
    cp = jnp.concatenate([jnp.cos(psi)] * reps, axis=1)
    sp = jnp.concatenate([jnp.sin(psi)] * reps, axis=1)
    w = 1.0 / n_half
    ac_ref[...] = w * hec
    as_ref[...] = w * hes
    bc_ref[...] = w * (cp * upc - sp * ups + umc)
    bs_ref[...] = w * (cp * ups + sp * upc - ums)
    gc_ref[...] = w * (upc + cp * umc - sp * ums)
    gs_ref[...] = w * (ups - cp * ums - sp * umc)


def _hy_spectrum(cm, s1, taps):
    half, n_ch = taps[0].shape
    tm = min(512, half)
    tn = 512
    w_spec = pl.BlockSpec((tm, half), lambda j, i: (i, 0))
    tap_spec = pl.BlockSpec((half, tn), lambda j, i: (0, j))
    o_spec = pl.BlockSpec((tm, tn), lambda j, i: (i, j))
    return pl.pallas_call(
        functools.partial(_hy_spectrum_kernel, n_half=half),
        out_shape=(jax.ShapeDtypeStruct((half, n_ch), F32),) * 6,
        grid=(n_ch // tn, half // tm),
        in_specs=[w_spec, w_spec] + [tap_spec] * 4,
        out_specs=(o_spec,) * 6,
        compiler_params=_params("parallel", "arbitrary"),
        name="hy_spectrum",
    )(cm, s1, *taps)


def _stage_rows(scr_ref, x):
    for c in range(x.shape[1] // 128):
        scr_ref[c] = x[:, c * 128:(c + 1) * 128]


def _rows_of_parity(scr_ref, par):
    n = scr_ref.shape[1] // 2
    return jnp.concatenate([scr_ref[c, pl.ds(par, n, stride=2), :] for c in range(scr_ref.shape[0])], axis=1)


def _hy_conv3_kernel(u_ref, w_ref, b_ref, o_ref, *rest):
    vb_ref, s_ref = rest if len(rest) == 2 else (None, rest[0])
    u = u_ref[...]
    n = u.shape[0]
    rows = lax.broadcasted_iota(jnp.int32, u.shape, 0)
    prev = jnp.where(rows == 0, 0.0, pltpu.roll(u, 1, 0))
    nxt = jnp.where(rows == n - 1, 0.0, pltpu.roll(u, n - 1, 0))
    w = w_ref[...]
    _stage_rows(s_ref, prev * w[0:1] + u * w[1:2] + nxt * w[2:3] + b_ref[...])
    for par in range(2):
        part = _rows_of_parity(s_ref, par)
        o_ref[par] = part
        if vb_ref is not None:
            vb_ref[par] = part.astype(BF16)


def _hy_conv3(proj, bsz, seq, conv_w, conv_b, col, width, with_bf16):
    half = seq // 2
    tn = 256
    o_spec = pl.BlockSpec((2, half, tn), lambda b, j: (0, b, j))
    shape = lambda dt: jax.ShapeDtypeStruct((2, bsz * half, width), dt)
    return pl.pallas_call(
        _hy_conv3_kernel,
        out_shape=(shape(F32), shape(BF16)) if with_bf16 else shape(F32),
        grid=(bsz, width // tn),
        in_specs=[pl.BlockSpec((seq, tn), lambda b, j: (b, j + (COL_BU + col) // tn)),
                  pl.BlockSpec((3, tn), lambda b, j: (0, j + col // tn)),
                  pl.BlockSpec((1, tn), lambda b, j: (0, j + col // tn))],
        out_specs=(o_spec, o_spec) if with_bf16 else o_spec,
        scratch_shapes=[pltpu.VMEM((tn // 128, seq, 128), F32)],
        compiler_params=_params("parallel", "parallel"),
        name="hy_conv3",
    )(proj, conv_w.astype(F32), conv_b.astype(F32).reshape(1, -1))


def _hy_fwd_kernel(c_ref, s_ref, ze_ref, zo_ref, ac_ref, as_ref, bc_ref, bs_ref, gc_ref, gs_ref, r0_ref,
                   pc_ref, ps_ref, qc_ref, qs_ref):
    dot = lambda w, z: jnp.dot(w[...], z[...], preferred_element_type=F32)
    ec, es = dot(c_ref, ze_ref), dot(s_ref, ze_ref)
    oc, os_ = dot(c_ref, zo_ref), dot(s_ref, zo_ref)
    a_c, a_s = ac_ref[...], as_ref[...]
    b_c, b_s = bc_ref[...], bs_ref[...]
    g_c, g_s = gc_ref[...], gs_ref[...]
    outs = (ec * a_c - es * a_s + oc * b_c - os_ * b_s,
            ec * a_s + es * a_c + oc * b_s + os_ * b_c,
            ec * g_c - es * g_s + oc * a_c - os_ * a_s,
            ec * g_s + es * g_c + oc * a_s + os_ * a_c)
    r0 = r0_ref[...]
    a0, d0, hr2, hi2 = r0[0:1], r0[1:2], r0[2:3], r0[3:4]
    e0, eh, o0, oh = ec[0:1], es[0:1], oc[0:1], os_[0:1]
    first = (e0 * a0 + o0 * d0, eh * hr2 + oh * hi2, e0 * d0 + o0 * a0, oh * hr2 - eh * hi2)
    top = 16
    is_row0 = (lax.broadcasted_iota(jnp.int32, (top, ec.shape[1]), 0) == 0) & (pl.program_id(2) == 0)
    for ref, val, row0 in zip((pc_ref, ps_ref, qc_ref, qs_ref), outs, first):
        ref[...] = val.astype(BF16)
        ref[0:top, :] = jnp.where(is_row0, row0, val[0:top]).astype(BF16)


HY_TN = 512


def _hy_fwd(cm, s1, z_even, z_odd, tables, r0, h_col, bsz, half):
    tm = min(512, half)
    tn = HY_TN
    mt = half // tm
    w_spec = pl.BlockSpec((tm, half), lambda b, j, i: (i, 0))
    h_spec = pl.BlockSpec((tm, tn), lambda b, j, i: (i, j + h_col // tn))
    o_spec = pl.BlockSpec((tm, tn), lambda b, j, i: (b * mt + i, j))
    return pl.pallas_call(
        _hy_fwd_kernel,
        out_shape=(jax.ShapeDtypeStruct((bsz * half, BRANCH_W), BF16),) * 4,
        grid=(bsz, BRANCH_W // tn, mt),
        in_specs=[w_spec, w_spec, z_even[1], z_odd[1]] + [h_spec] * 6
                 + [pl.BlockSpec((8, tn), lambda b, j, i: (0, j + h_col // tn))],
        out_specs=(o_spec,) * 4,
        compiler_params=_params("parallel", "parallel", "arbitrary"),
        name="hy_fwd",
    )(cm, s1, z_even[0], z_odd[0], *tables, r0)


def _hy_inv_convs(c_ref, s_ref, pc_ref, ps_ref, qc_ref, qs_ref):
    dot = lambda w, y: jnp.dot(w[...], y[...], preferred_element_type=F32)
    return dot(c_ref, pc_ref) + dot(s_ref, ps_ref), dot(c_ref, qc_ref) + dot(s_ref, qs_ref)


def _hy_inv_mid_kernel(c_ref, s_ref, pc_ref, ps_ref, qc_ref, qs_ref, ge_ref, go_ref, ze_ref, zo_ref,
                       bias_ref, o_ref, ob_ref):
    convs = _hy_inv_convs(c_ref, s_ref, pc_ref, ps_ref, qc_ref, qs_ref)
    for par, (conv, g_ref, z_ref) in enumerate(zip(convs, (ge_ref, go_ref), (ze_ref, zo_ref))):
        out = g_ref[...] * (conv + bias_ref[...] * z_ref[...])
        o_ref[par] = out
        ob_ref[par] = out.astype(BF16)


def _hy_inv_last_kernel(c_ref, s_ref, pc_ref, ps_ref, qc_ref, qs_ref, ge_ref, go_ref, ze_ref, zo_ref,
                        bias_ref, sz_ref, o_ref, mix_ref):
    convs = _hy_inv_convs(c_ref, s_ref, pc_ref, ps_ref, qc_ref, qs_ref)
    tm = ge_ref.shape[0]
    _stage_rows(mix_ref, sz_ref[...])
    outs = [g_ref[...] * (conv + bias_ref[...] * z_ref[...]) * _rows_of_parity(mix_ref, par)
            for par, (conv, g_ref, z_ref) in enumerate(zip(convs, (ge_ref, go_ref), (ze_ref, zo_ref)))]
    for par, out in enumerate(outs):
        for c in range(mix_ref.shape[0]):
            mix_ref[c, pl.ds(par, tm, stride=2), :] = out[:, c * 128:(c + 1) * 128]
    for c in range(mix_ref.shape[0]):
        o_ref[:, c * 128:(c + 1) * 128] = mix_ref[c]


def _hy_inv(cm, s2, spectra, gates, zprev, bias_row, bsz, half, silu=None):
    tm = min(512, half)
    tn = HY_TN
    mt = half // tm
    w_spec = pl.BlockSpec((tm, half), lambda b, j, i: (i, 0))
    y_spec = pl.BlockSpec((half, tn), lambda b, j, i: (b, j))
    pairs = list(gates) + list(zprev)
    in_specs = [w_spec, w_spec] + [y_spec] * 4 + [s for _, s in pairs] + [pl.BlockSpec((1, tn), lambda b, j, i: (0, j))]
    args = [cm, s2, *spectra] + [a for a, _ in pairs] + [bias_row]
    if silu is None:
        o_spec = pl.BlockSpec((2, tm, tn), lambda b, j, i: (0, b * mt + i, j))
        shape = lambda dt: jax.ShapeDtypeStruct((2, bsz * half, BRANCH_W), dt)
        body, out_shape, out_specs = _hy_inv_mid_kernel, (shape(F32), shape(BF16)), (o_spec, o_spec)
        scratch = []
    else:
        silu_arr, silu_col = silu
        body = _hy_inv_last_kernel
        out_shape = jax.ShapeDtypeStruct((2 * bsz * half, BRANCH_W), F32)
        out_specs = pl.BlockSpec((2 * tm, tn), lambda b, j, i: (b * mt + i, j))
        in_specs.append(pl.BlockSpec((2 * tm, tn), lambda b, j, i: (b * mt + i, j + silu_col // tn)))
        args.append(silu_arr)
        scratch = [pltpu.VMEM((tn // 128, 2 * tm, 128), F32)]
    return pl.pallas_call(
        body, out_shape=out_shape, grid=(bsz, BRANCH_W // tn, mt), in_specs=in_specs, out_specs=out_specs,
        scratch_shapes=scratch, compiler_params=_params("parallel", "parallel", "arbitrary"),
        name="hy_inv",
    )(*args)


def _hyena_branch(proj, bsz, seq, dft, conv_w, conv_b, w1, b1, w2, b2, freq, w3, b3, bias):
    cm, s1, s2 = dft
    half = seq // 2
    tm = min(512, half)
    tn = HY_TN
    mt = half // tm
    full = lambda par, col=0: pl.BlockSpec((None, half, tn), lambda b, j, i: (par, b, j + col // tn))
    row = lambda par, col=0: pl.BlockSpec((None, tm, tn), lambda b, j, i: (par, b * mt + i, j + col // tn))
    both = lambda arr, spec, col=0: ((arr, spec(0, col)), (arr, spec(1, col)))
    *taps, r0 = _hy_filter_taps(seq, w1, b1, w2, b2, freq, w3, b3)
    tables = _hy_spectrum(cm, s1, taps)
    v, vb = _hy_conv3(proj, bsz, seq, conv_w, conv_b, 0, BRANCH_W, True)
    x12 = _hy_conv3(proj, bsz, seq, conv_w, conv_b, BRANCH_W, 2 * BRANCH_W, False)
    bias = bias.astype(F32)
    spectra = _hy_fwd(cm, s1, *both(vb, full), tables, r0, 0, bsz, half)
    z1, z1b = _hy_inv(cm, s2, spectra, gates=both(x12, row), zprev=both(v, row),
                      bias_row=bias[0:1], bsz=bsz, half=half)
    spectra = _hy_fwd(cm, s1, *both(z1b, full), tables, r0, BRANCH_W, bsz, half)
    return _hy_inv(cm, s2, spectra, gates=both(x12, row, BRANCH_W), zprev=both(z1, row),
                   bias_row=bias[1:2], bsz=bsz, half=half, silu=(proj, COL_BZ))


def _rope_table_kernel(pos_ref, inv_ref, cos_ref, sin_ref):
    ang = pos_ref[...] * inv_ref[...]
    lane = lax.broadcasted_iota(jnp.int32, ang.shape, 1)
    live = lane < MLA_ROPE
    cos_ref[...] = jnp.where(live, jnp.cos(ang), 0.0)
    sin_ref[...] = jnp.where(live, jnp.where(lane < MLA_ROPE // 2, -1.0, 1.0) * jnp.sin(ang), 0.0)


def _rope_tables(positions):
    m = positions.size
    half = MLA_ROPE // 2
    inv = ROPE_BASE ** (-jnp.arange(half, dtype=F32) / half)
    inv = jnp.concatenate([inv, inv, jnp.zeros((128 - MLA_ROPE,), F32)])[None]
    pos = positions.astype(F32).reshape(m, 1)
    tm = min(1024, m)
    spec = pl.BlockSpec((tm, 128), lambda i: (i, 0))
    return pl.pallas_call(
        _rope_table_kernel,
        out_shape=(jax.ShapeDtypeStruct((m, 128), F32),) * 2,
        grid=(m // tm,),
        in_specs=[pl.BlockSpec((tm, 1), lambda i: (i, 0)), pl.BlockSpec((1, 128), lambda i: (0, 0))],
        out_specs=(spec, spec),
        compiler_params=_params("parallel"),
        name="rope_table",
    )(pos, inv)


def _rope128(x, cos_t, sin_t):
    lane = lax.broadcasted_iota(jnp.int32, x.shape, 1)
    half = MLA_ROPE // 2
    partner = jnp.where(lane < half, pltpu.roll(x, 128 - half, 1), pltpu.roll(x, half, 1))
    return x * cos_t + partner * sin_t


def _rms(x, g):
    ms = jnp.mean(jnp.square(x), axis=-1, keepdims=True)
    return x * lax.rsqrt(ms + RMS_EPS) * g


def _mla_q_kernel(cq_ref, g_ref, w_ref, cos_ref, sin_ref, q_ref, *, scale):
    xn = _rms(cq_ref[...], g_ref[...]).astype(BF16)
    q = jnp.dot(xn, w_ref[...], preferred_element_type=F32) * scale
    cos_t, sin_t = cos_ref[...], sin_ref[...]
    for h in range(MLA_HEADS):
        base = h * MLA_QK_PAD
        q_ref[:, base:base + MLA_NOPE] = q[:, base:base + MLA_NOPE].astype(BF16)
        q_ref[:, base + MLA_NOPE:base + MLA_QK_PAD] = _rope128(
            q[:, base + MLA_NOPE:base + MLA_QK_PAD], cos_t, sin_t).astype(BF16)


def _mla_kv_kernel(ckv_ref, g_ref, wk_ref, wv_ref, kr_ref, cos_ref, sin_ref, k_ref, v_ref):
    xn = _rms(ckv_ref[...], g_ref[...]).astype(BF16)
    kn = jnp.dot(xn, wk_ref[...], preferred_element_type=F32)
    v_ref[...] = jnp.dot(xn, wv_ref[...], preferred_element_type=F32).astype(BF16)
    kr = _rope128(kr_ref[:, 0:128], cos_ref[...], sin_ref[...]).astype(BF16)
    for h in range(MLA_HEADS):
        base = h * MLA_QK_PAD
        k_ref[:, base:base + MLA_NOPE] = kn[:, h * MLA_NOPE:(h + 1) * MLA_NOPE].astype(BF16)
        k_ref[:, base + MLA_NOPE:base + MLA_QK_PAD] = kr


def _mla_attn_kernel(q_ref, k_ref, v_ref, z_ref, o_ref):
    k, v = k_ref[...], v_ref[...]
    half = q_ref.shape[0] // 2
    for r in range(2):
        rows = pl.ds(r * half, half)
        s = lax.dot_general(q_ref[rows, :], k, (((1,), (1,)), ((), ())),
                            preferred_element_type=F32)
        p = jnp.exp2(s - jnp.max(s, axis=-1, keepdims=True))
        l = jnp.sum(p, axis=-1, keepdims=True)
        o = jnp.dot(p.astype(BF16), v, preferred_element_type=F32)
        o_ref[rows, :] = (o / l * z_ref[rows, :]).astype(BF16)


def _mla_branch(proj, bsz, seq, rope, q_norm_g, w_uq, kv_norm_g, w_ukv):
    m = bsz * seq
    cos_t, sin_t = rope
    dqk = MLA_NOPE + MLA_ROPE
    hq = MLA_HEADS * MLA_QK_PAD
    w_q = w_uq.reshape(MLA_LORA, MLA_HEADS, dqk)
    w_q = jnp.pad(w_q, ((0, 0), (0, 0), (0, MLA_QK_PAD - dqk))).reshape(MLA_LORA, hq).astype(BF16)
    w_kv = w_ukv.reshape(MLA_LORA, MLA_HEADS, MLA_NOPE + MLA_V)
    w_k = w_kv[:, :, :MLA_NOPE].reshape(MLA_LORA, MLA_HEADS * MLA_NOPE).astype(BF16)
    w_v = w_kv[:, :, MLA_NOPE:].reshape(MLA_LORA, MLA_HEADS * MLA_V).astype(BF16)
    tm = min(512, m)
    row = lambda shape, col=0: pl.BlockSpec(shape, lambda i: (i, col))
    full = lambda shape: pl.BlockSpec(shape, lambda i: (0, 0))
    qp = pl.pallas_call(
        functools.partial(_mla_q_kernel, scale=dqk ** -0.5 * math.log2(math.e)),
        out_shape=jax.ShapeDtypeStruct((m, hq), BF16),
        grid=(m // tm,),
        in_specs=[row((tm, MLA_LORA), COL_CQ // MLA_LORA), full((1, MLA_LORA)), full((MLA_LORA, hq)),
                  row((tm, 128)), row((tm, 128))],
        out_specs=row((tm, hq)),
        compiler_params=_params("parallel"),
        name="mla_q",
    )(proj, q_norm_g.astype(F32).reshape(1, -1), w_q, cos_t, sin_t)
    kp, vp = pl.pallas_call(
        _mla_kv_kernel,
        out_shape=(jax.ShapeDtypeStruct((m, hq), BF16),
                   jax.ShapeDtypeStruct((m, MLA_HEADS * MLA_V), BF16)),
        grid=(m // tm,),
        in_specs=[row((tm, MLA_LORA), COL_CKV // MLA_LORA), full((1, MLA_LORA)),
                  full((MLA_LORA, MLA_HEADS * MLA_NOPE)), full((MLA_LORA, MLA_HEADS * MLA_V)),
                  row((tm, 512), COL_CKR // 512), row((tm, 128)), row((tm, 128))],
        out_specs=(row((tm, hq)), row((tm, MLA_HEADS * MLA_V))),
        compiler_params=_params("parallel"),
        name="mla_kv",
    )(proj, kv_norm_g.astype(F32).reshape(1, -1), w_k, w_v, proj, cos_t, sin_t)
    tq = min(512, seq)
    qt = seq // tq
    return pl.pallas_call(
        _mla_attn_kernel,
        out_shape=jax.ShapeDtypeStruct((m, MLA_HEADS * MLA_V), BF16),
        grid=(bsz, MLA_HEADS, qt),
        in_specs=[pl.BlockSpec((tq, MLA_QK_PAD), lambda b, h, i: (b * qt + i, h)),
                  pl.BlockSpec((seq, MLA_QK_PAD), lambda b, h, i: (b, h)),
                  pl.BlockSpec((seq, MLA_V), lambda b, h, i: (b, h)),
                  pl.BlockSpec((tq, MLA_V), lambda b, h, i: (b * qt + i, h + COL_CZ // MLA_V))],
        out_specs=pl.BlockSpec((tq, MLA_V), lambda b, h, i: (b * qt + i, h)),
        compiler_params=_params("parallel", "parallel", "arbitrary"),
        name="mla_attn",
    )(qp, kp, vp, proj)


def _lift_kernel(ya_ref, yb_ref, yc_ref, w_ref, ga_ref, gb_ref, gc_ref, o_ref, wb_ref):
    @pl.when(pl.program_id(1) == 0)
    def _():
        wb_ref[...] = w_ref[...].astype(BF16)

    acc = ga_ref[...] * jnp.dot(ya_ref[...], wb_ref[0], preferred_element_type=F32)
    acc += gb_ref[...] * jnp.dot(yb_ref[...].astype(BF16), wb_ref[1], preferred_element_type=F32)
    acc += gc_ref[...] * jnp.dot(yc_ref[...], wb_ref[2], preferred_element_type=F32)
    o_ref[...] = acc.astype(BF16)


def _lift(ya, yb, yc, w_lift, layer, proj):
    m = ya.shape[0]
    tm = min(1024, m)
    tn = 512
    y_spec = pl.BlockSpec((tm, BRANCH_W), lambda j, i: (i, 0))
    gate = lambda n: pl.BlockSpec((tm, tn), lambda j, i: (i, j + (COL_GATE + n * D_MODEL) // tn))
    return pl.pallas_call(
        _lift_kernel,
        out_shape=jax.ShapeDtypeStruct((m, D_MODEL), BF16),
        grid=(D_MODEL // tn, m // tm),
        in_specs=[y_spec, y_spec, y_spec,
                  pl.BlockSpec((None, N_BRANCH, BRANCH_W, tn), lambda j, i: (layer, 0, 0, j)),
                  gate(0), gate(1), gate(2)],
        out_specs=pl.BlockSpec((tm, tn), lambda j, i: (i, j)),
        scratch_shapes=[pltpu.VMEM((N_BRANCH, BRANCH_W, tn), BF16)],
        compiler_params=_params("parallel", "arbitrary"),
        name="lift",
    )(ya, yb, yc, w_lift, proj, proj, proj)


def _out_kernel(mix_ref, wo_ref, p_ref, wp_ref, sp_ref, x_ref, g_ref, b_ref, o_ref):
    mixed = jnp.dot(mix_ref[...], wo_ref[...], preferred_element_type=F32)
    ple = jnp.dot(p_ref[...].astype(BF16), wp_ref[...], preferred_element_type=F32) * sp_ref[...]
    r = DEEPNORM_ALPHA * x_ref[...] + mixed + ple
    mu = jnp.mean(r, axis=-1, keepdims=True)
    var = jnp.mean(jnp.square(r - mu), axis=-1, keepdims=True)
    o_ref[...] = (r - mu) * lax.rsqrt(var + LN_EPS) * g_ref[...] + b_ref[...]


def _out_norm(mix, w_out, p, w_ple, proj, x, ln_g, ln_b):
    m = mix.shape[0]
    tm = min(512, m)
    row = lambda w, col=0: pl.BlockSpec((tm, w), lambda i: (i, col))
    full = lambda shape: pl.BlockSpec(shape, lambda i: (0, 0))
    return pl.pallas_call(
        _out_kernel,
        out_shape=jax.ShapeDtypeStruct((m, D_MODEL), F32),
        grid=(m // tm,),
        in_specs=[row(D_MODEL), full((D_MODEL, D_MODEL)), row(PLE_DIM), full((PLE_DIM, D_MODEL)),
                  row(D_MODEL, COL_PLE // D_MODEL), row(D_MODEL), full((1, D_MODEL)), full((1, D_MODEL))],
        out_specs=row(D_MODEL),
        compiler_params=_params("parallel"),
        name="out_norm",
    )(mix, w_out.astype(BF16), p, w_ple.astype(BF16), proj, x,
      ln_g.astype(F32).reshape(1, -1), ln_b.astype(F32).reshape(1, -1))


def kernel(x, p, positions, w_in, s5_lambda_re, s5_lambda_im, s5_log_dt, s5_b_re, s5_b_im, s5_c_re, s5_c_im, s5_d, s5_w_glu, s5_b_glu, hy_conv_w, hy_conv_b, hy_w1, hy_b1, hy_w2, hy_b2, hy_freq, hy_w3, hy_b3, hy_bias, mla_q_norm, mla_w_uq, mla_kv_norm, mla_w_ukv, w_lift, w_out, w_ple, ln_g, ln_b):
    bsz, seq, _ = x.shape
    m = bsz * seq
    depth = w_in.shape[0]
    dft = _dft_matrices(seq // 2)
    rope = _rope_tables(positions)
    s5_tiled = _s5_tiled_params(s5_lambda_re, s5_lambda_im, s5_log_dt, s5_b_re, s5_b_im,
                                s5_c_re, s5_c_im, s5_d)
    xf = x.reshape(m, D_MODEL).astype(F32)
    w_t = jnp.swapaxes(w_in, 1, 2)
    for i in range(depth):
        proj = _proj_all(xf, w_t, i)
        y_a = _s5_branch(proj, bsz, seq, [a[i] for a in s5_tiled], s5_w_glu[i], s5_b_glu[i])
        y_b = _hyena_branch(proj, bsz, seq, dft, hy_conv_w[i], hy_conv_b[i], hy_w1[i], hy_b1[i],
                            hy_w2[i], hy_b2[i], hy_freq[i], hy_w3[i], hy_b3[i], hy_bias[i])
        y_c = _mla_branch(proj, bsz, seq, rope, mla_q_norm[i], mla_w_uq[i], mla_kv_norm[i], mla_w_ukv[i])
        mix = _lift(y_a, y_b, y_c, w_lift, i, proj)
        xf = _out_norm(mix, w_out[i], p[i].reshape(m, PLE_DIM), w_ple[i], proj, xf, ln_g[i], ln_b[i])
    return xf.reshape(bsz, seq, D_MODEL).astype(x.dtype)
```

```python
import functools
import math

import numpy as np
import jax
import jax.numpy as jnp
from jax import lax
from jax.experimental import pallas as pl
from jax.experimental.pallas import tpu as pltpu

F32 = jnp.float32
BF16 = jnp.bfloat16

D_MODEL = 2048
PLE_DIM = 256
N_BRANCH = 3
BRANCH_W = 1024

S5_GROUP = 16
S5_GROUPS = BRANCH_W // S5_GROUP
S5_STATE = 64
S5_CHUNK = 16
S5_ROW = S5_CHUNK * S5_GROUP

HY_EMB = 33
HY_FF = 64
HY_PAD = 128
HY_DECAY_TARGET = 0.01
HY_FAST_DECAY = 0.3
HY_SLOW_DECAY = 1.5
DFT_ROWS = 64

MLA_HEADS = 8
MLA_NOPE = 128
MLA_ROPE = 64
MLA_V = 128
MLA_LORA = 512
MLA_QK_PAD = 256
ROPE_BASE = 10000.0

LN_EPS = 1e-5
RMS_EPS = 1e-6
DEPTH = 2
DEEPNORM_ALPHA = (2 * DEPTH) ** 0.25

COL_AX = 0
COL_AZ = 1024
COL_BU = 2048
COL_BZ = 5120
COL_CQ = 6144
COL_CKV = 6656
PROJ_KEEP = 7168
COL_CZ = 7168
COL_PLE = 8192
COL_GATE = 10240
COL_CKR = 16384
PROJ_N = 16896
PROJ_TN = 512
W_IN_CKR = 7168
W_IN_CZ = 7232
W_IN_GATE = 8256
W_IN_PLE = 14400

VMEM_LIMIT = 56 * 1024 * 1024


def _params(*sem):
    return pltpu.CompilerParams(dimension_semantics=sem, vmem_limit_bytes=VMEM_LIMIT)


def _sigmoid(x):
    return 0.5 * jnp.tanh(0.5 * x) + 0.5


def _silu(x):
    return x * _sigmoid(x)


def _proj_kernel(x_ref, w_ref, o_ref, xb_ref):
    @pl.when(pl.program_id(1) == 0)
    def _():
        xb_ref[...] = x_ref[...].astype(BF16)

    o_ref[...] = lax.dot_general(xb_ref[...], w_ref[...].astype(BF16), (((1,), (1,)), ((), ())),
                                 preferred_element_type=F32)


def _proj(x, w, layer, n_out, name):
    m, k = x.shape
    tm = min(1024, m)
    return pl.pallas_call(
        _proj_kernel,
        out_shape=jax.ShapeDtypeStruct((m, n_out), F32),
        grid=(m // tm, n_out // PROJ_TN),
        in_specs=[pl.BlockSpec((tm, k), lambda i, j: (i, 0)),
                  pl.BlockSpec((None, PROJ_TN, k), lambda i, j: (layer, j, 0))],
        out_specs=pl.BlockSpec((tm, PROJ_TN), lambda i, j: (i, j)),
        scratch_shapes=[pltpu.VMEM((tm, k), BF16)],
        compiler_params=_params("parallel", "arbitrary"),
        name=name,
    )(x, w)


def _repack_src_tile(j):
    t = lambda col: col // PROJ_TN
    return jnp.where(j < t(COL_CZ), j,
                     jnp.where(j < t(COL_PLE), j - t(COL_CZ) + t(W_IN_CZ),
                               jnp.where(j < t(COL_GATE), j - t(COL_PLE) + t(W_IN_PLE),
                                         jnp.where(j < t(COL_CKR), j - t(COL_GATE) + t(W_IN_GATE),
                                                   t(W_IN_CKR)))))


def _repack_kernel(a_ref, b_ref, o_ref):
    j = pl.program_id(0)
    a = a_ref[...]
    shifted = jnp.concatenate([a[MLA_ROPE:], b_ref[...]], axis=0)
    key = jnp.concatenate([a[:MLA_ROPE], jnp.zeros_like(a[MLA_ROPE:])], axis=0)
    out = jnp.where(j < PROJ_KEEP // PROJ_TN, a, jnp.where(j == COL_CKR // PROJ_TN, key, shifted))
    o_ref[...] = out.astype(BF16)


def _repack(w_t, layer):
    k = w_t.shape[2]
    sub = PROJ_TN // MLA_ROPE
    return pl.pallas_call(
        _repack_kernel,
        out_shape=jax.ShapeDtypeStruct((1, PROJ_N, k), BF16),
        grid=(PROJ_N // PROJ_TN,),
        in_specs=[pl.BlockSpec((None, PROJ_TN, k), lambda j: (layer, _repack_src_tile(j), 0)),
                  pl.BlockSpec((None, MLA_ROPE, k), lambda j: (layer, (_repack_src_tile(j) + 1) * sub, 0))],
        out_specs=pl.BlockSpec((None, PROJ_TN, k), lambda j: (0, j, 0)),
        compiler_params=_params("parallel"),
        name="repack",
    )(w_t, w_t)


def _proj_all(x, w_t, layer):
    return _proj(x, _repack(w_t, layer), 0, PROJ_N, "proj")


S5_LANES = 8 * S5_STATE
S5_SLAB = 8
S5_RELAYOUT_ROWS = 32


def _s5_tile_lanes(a):
    return jnp.concatenate([a[:, 0]] * 4 + [a[:, 1]] * 4, axis=-1)


def _s5_tiled_params(lam_re, lam_im, log_dt, b_re, b_im, c_re, c_im, d):
    f = lambda a: a.astype(F32)
    depth = lam_re.shape[0]
    ldt = jnp.broadcast_to(f(log_dt)[..., None, None], lam_re.shape[:3] + (1, S5_STATE))
    return (_s5_tile_lanes(f(lam_re)[:, :, :, None, :]), _s5_tile_lanes(f(lam_im)[:, :, :, None, :]),
            _s5_tile_lanes(ldt),
            _s5_tile_lanes(jnp.swapaxes(f(b_re), -1, -2)), _s5_tile_lanes(jnp.swapaxes(f(b_im), -1, -2)),
            _s5_tile_lanes(f(c_re)), _s5_tile_lanes(f(c_im)),
            f(d).reshape(depth, S5_GROUPS, S5_GROUP, 1))


def _s5_mats_kernel(lr_ref, li_ref, ldt_ref, br_ref, bi_ref, cr_ref, ci_ref, d_ref,
                    ms_ref, mi_ref, mo_ref, ar_ref, ai_ref):
    t_n, h_n = S5_CHUNK, S5_GROUP
    hi = lax.Precision.HIGHEST
    nt = (((1,), (1,)), ((), ()))
    blk = lax.broadcasted_iota(jnp.int32, (1, S5_LANES), 1) // S5_STATE
    is_im = (blk // 2) % 2 == 1
    is_fwd = blk < 4
    steps = lax.broadcasted_iota(jnp.int32, (24, S5_LANES), 0).astype(F32)
    lane_k = lax.broadcasted_iota(jnp.int32, (h_n, S5_ROW), 1)
    sub_k = lax.broadcasted_iota(jnp.int32, (h_n, S5_ROW), 0)

    def per_group(gi, carry):
        lr, li = lr_ref[gi], li_ref[gi]
        dt = jnp.exp(ldt_ref[gi])
        zr, zi = lr * dt, li * dt
        mag = jnp.exp(steps * zr)
        tr, ti = mag * jnp.cos(steps * zi), mag * jnp.sin(steps * zi)
        lbr, lbi = tr[1:2], ti[1:2]
        n2 = lr * lr + li * li
        qr = ((lbr - 1.0) * lr + lbi * li) / n2
        qi = (lbi * lr - (lbr - 1.0) * li) / n2
        br, bi = br_ref[gi], bi_ref[gi]
        bbr, bbi = qr * br - qi * bi, qr * bi + qi * br
        y1, y2 = jnp.where(is_im, bbi, bbr), jnp.where(is_im, bbr, bbi)
        cr, ci = cr_ref[gi], ci_ref[gi]

        def pick(tab, t_fwd, t_bwd):
            return jnp.where(is_fwd, tab[t_fwd:t_fwd + 1], tab[t_bwd:t_bwd + 1])

        def c_times(p_r, p_i):
            return jnp.where(is_im, -(cr * p_i + ci * p_r), cr * p_r - ci * p_i)

        ct = [c_times(tr[s:s + 1], ti[s:s + 1]) for s in range(t_n + 1)]
        q_rows = []
        for t in range(t_n):
            a_r, a_i = pick(tr, t_n - 1 - t, t), pick(ti, t_n - 1 - t, t)
            rows = pl.ds(t * h_n, h_n)
            ms_ref[gi, rows, :] = (a_r * y1 + jnp.where(is_im, a_i, -a_i) * y2).astype(BF16)
            mo_ref[gi, rows, :] = jnp.where(is_fwd, ct[t + 1], ct[t_n - t]).astype(BF16)
            q_rows.append(jnp.where(is_fwd, ct[t], ct[t_n - 1 - t]))
        q = jnp.concatenate(q_rows, axis=0)
        half = S5_LANES // 2
        kf = 0.5 * lax.dot_general(y1[:, :half], q[:, :half], nt, precision=hi, preferred_element_type=F32)
        kb = 0.5 * lax.dot_general(y1[:, half:], q[:, half:], nt, precision=hi, preferred_element_type=F32)
        kf = kf + jnp.where(lane_k == sub_k, d_ref[gi], 0.0)
        for t in range(t_n):
            fwd = kf if t == 0 else jnp.where(lane_k >= h_n * t, pltpu.roll(kf, h_n * t, 1), 0.0)
            sh = h_n * (t_n - 1 - t)
            bwd = kb if sh == 0 else jnp.where(lane_k < S5_ROW - sh, pltpu.roll(kb, S5_ROW - sh, 1), 0.0)
            mi_ref[gi, pl.ds(t * h_n, h_n), :] = (fwd + bwd).astype(BF16)
        ar_ref[gi] = tr[t_n:t_n + 1]
        ai_ref[gi] = ti[t_n:t_n + 1]
        return carry

    lax.fori_loop(0, lr_ref.shape[0], per_group, 0)


def _s5_mats(tiled):
    g_n = S5_GROUPS
    gb = S5_SLAB
    spec = lambda r, w: pl.BlockSpec((gb, r, w), lambda j: (j, 0, 0))
    in_rows = (1, 1, 1, S5_GROUP, S5_GROUP, S5_GROUP, S5_GROUP)
    return pl.pallas_call(
        _s5_mats_kernel,
        out_shape=(jax.ShapeDtypeStruct((g_n, S5_ROW, S5_LANES), BF16),
                   jax.ShapeDtypeStruct((g_n, S5_ROW, S5_ROW), BF16),
                   jax.ShapeDtypeStruct((g_n, S5_ROW, S5_LANES), BF16),
                   jax.ShapeDtypeStruct((g_n, 1, S5_LANES), F32),
                   jax.ShapeDtypeStruct((g_n, 1, S5_LANES), F32)),
        grid=(g_n // gb,),
        in_specs=[spec(r, S5_LANES) for r in in_rows] + [spec(S5_GROUP, 1)],
        out_specs=(spec(S5_ROW, S5_LANES), spec(S5_ROW, S5_ROW), spec(S5_ROW, S5_LANES),
                   spec(1, S5_LANES), spec(1, S5_LANES)),
        compiler_params=_params("parallel"),
        name="s5_mats",
    )(*tiled)


def _seg_transpose(vs, seg):
    vs = list(vs)
    for s in (4, 2, 1):
        keep = (seg & s) == 0
        for i in range(8):
            if i & s:
                continue
            a, b = vs[i], vs[i + s]
            vs[i] = jnp.where(keep, a, pltpu.roll(b, s * S5_GROUP, 1))
            vs[i + s] = jnp.where(keep, pltpu.roll(a, 128 - s * S5_GROUP, 1), b)
    return vs


def _s5_main_kernel(x_ref, ms_ref, mi_ref, mo_ref, ar_ref, ai_ref, y_ref,
                    u_ref, sl_ref, st_ref, yg_ref, *, bsz):
    rows = x_ref.shape[0]
    n_chunks = rows // bsz
    rc = S5_RELAYOUT_ROWS
    seg = lax.broadcasted_iota(jnp.int32, (rc, 128), 1) // S5_GROUP
    slot = (lax.broadcasted_iota(jnp.int32, (1, S5_LANES), 1) // S5_STATE) % 2
    slot128 = slot[:, :128]

    def relayout_in(r, carry):
        r0 = pl.multiple_of(r * rc, rc)
        for th in range(2):
            src = [x_ref[pl.ds(r0, rc), th * 8 + t8, :] for t8 in range(8)]
            for gi, out in enumerate(_seg_transpose(src, seg)):
                u_ref[gi, pl.ds(r0, rc), th * 128:(th + 1) * 128] = out.astype(BF16)
        return carry

    lax.fori_loop(0, rows // rc, relayout_in, 0, unroll=2)

    n_pairs = S5_SLAB // 2
    trans = []
    for jp in range(n_pairs):
        g0, g1 = 2 * jp, 2 * jp + 1
        r0 = jnp.dot(u_ref[g0], ms_ref[g0], preferred_element_type=F32)
        r1 = jnp.dot(u_ref[g1], ms_ref[g1], preferred_element_type=F32)
        loc = jnp.where(slot == 0, r0, r1)
        for k in range(4):
            for b in range(bsz):
                sl_ref[k, jp, pl.ds(b, n_chunks, stride=bsz), :] = (
                    loc[b * n_chunks:(b + 1) * n_chunks, k * 128:(k + 1) * 128])
        trans.append([jnp.where(slot128 == 0, ref[g0][:, off:off + 128], ref[g1][:, off:off + 128])
                      for ref, off in ((ar_ref, 0), (ai_ref, 0), (ar_ref, 256), (ai_ref, 256))])

    zero = jnp.zeros((bsz, 128), F32)

    def scan(c, carry):
        rf = pl.ds(pl.multiple_of(c * bsz, bsz), bsz)
        rb = pl.ds(pl.multiple_of((n_chunks - 1 - c) * bsz, bsz), bsz)
        new = []
        for jp in range(n_pairs):
            s_fr, s_fi, s_br, s_bi = carry[jp]
            a_fr, a_fi, a_br, a_bi = trans[jp]
            st_ref[0, jp, rf, :] = s_fr
            st_ref[1, jp, rf, :] = s_fi
            st_ref[2, jp, rb, :] = s_br
            st_ref[3, jp, rb, :] = s_bi
            new.append((a_fr * s_fr - a_fi * s_fi + sl_ref[0, jp, rf, :],
                        a_fr * s_fi + a_fi * s_fr + sl_ref[1, jp, rf, :],
                        a_br * s_br - a_bi * s_bi + sl_ref[2, jp, rb, :],
                        a_br * s_bi + a_bi * s_br + sl_ref[3, jp, rb, :]))
        return tuple(new)

    lax.fori_loop(0, n_chunks, scan, tuple((zero,) * 4 for _ in range(n_pairs)))

    nt = (((1,), (1,)), ((), ()))
    for jp in range(n_pairs):
        st = jnp.concatenate(
            [jnp.concatenate([st_ref[k, jp, pl.ds(b, n_chunks, stride=bsz), :] for b in range(bsz)], axis=0)
             for k in range(4)], axis=1)
        for e in range(2):
            g = 2 * jp + e
            st_g = jnp.where(slot == e, st, 0.0).astype(BF16)
            y = (jnp.dot(u_ref[g], mi_ref[g], preferred_element_type=F32)
                 + lax.dot_general(st_g, mo_ref[g], nt, preferred_element_type=F32))
            yg_ref[g] = jax.nn.gelu(y)

    def relayout_out(r, carry):
        r0 = pl.multiple_of(r * rc, rc)
        for th in range(2):
            src = [yg_ref[gi, pl.ds(r0, rc), th * 128:(th + 1) * 128] for gi in range(S5_SLAB)]
            for t8, out in enumerate(_seg_transpose(src, seg)):
                y_ref[pl.ds(r0, rc), th * 8 + t8, :] = out
        return carry

    lax.fori_loop(0, rows // rc, relayout_out, 0, unroll=2)


def _s5_glu_kernel(g_ref, w_ref, b_ref, z_ref, o_ref):
    g = g_ref[...]
    acc = jnp.dot(g.astype(BF16), w_ref[...], preferred_element_type=F32) + b_ref[...]
    o_ref[...] = (g * _sigmoid(acc) * _silu(z_ref[...])).astype(o_ref.dtype)


def _s5_branch(proj, bsz, seq, tiled, w_glu, b_glu):
    m_state, m_intra, m_out, a_re, a_im = _s5_mats(tiled)
    t_n = S5_CHUNK
    rows = bsz * (seq // t_n)
    m = bsz * seq
    gb = S5_SLAB
    x3 = proj.reshape(rows, t_n, proj.shape[1])
    mat = lambda r, w: pl.BlockSpec((gb, r, w), lambda s: (s, 0, 0))
    io_spec = pl.BlockSpec((rows, t_n, 128), lambda s: (0, 0, s + COL_AX // 128))
    y = pl.pallas_call(
        functools.partial(_s5_main_kernel, bsz=bsz),
        out_shape=jax.ShapeDtypeStruct((rows, t_n, BRANCH_W), F32),
        grid=(S5_GROUPS // gb,),
        in_specs=[io_spec, mat(S5_ROW, S5_LANES), mat(S5_ROW, S5_ROW), mat(S5_ROW, S5_LANES),
                  mat(1, S5_LANES), mat(1, S5_LANES)],
        out_specs=pl.BlockSpec((rows, t_n, 128), lambda s: (0, 0, s)),
        scratch_shapes=[pltpu.VMEM((gb, rows, S5_ROW), BF16),
                        pltpu.VMEM((4, gb // 2, rows, 128), F32),
                        pltpu.VMEM((4, gb // 2, rows, 128), F32),
                        pltpu.VMEM((gb, rows, S5_ROW), F32)],
        compiler_params=_params("parallel"),
        name="s5_main",
    )(x3, m_state, m_intra, m_out, a_re, a_im)
    y = y.reshape(m, BRANCH_W)
    tm = min(512, m)
    return pl.pallas_call(
        _s5_glu_kernel,
        out_shape=jax.ShapeDtypeStruct((m, BRANCH_W), BF16),
        grid=(m // tm,),
        in_specs=[pl.BlockSpec((tm, BRANCH_W), lambda i: (i, 0)),
                  pl.BlockSpec((BRANCH_W, BRANCH_W), lambda i: (0, 0)),
                  pl.BlockSpec((1, BRANCH_W), lambda i: (0, 0)),
                  pl.BlockSpec((tm, BRANCH_W), lambda i: (i, COL_AZ // BRANCH_W))],
        out_specs=pl.BlockSpec((tm, BRANCH_W), lambda i: (i, 0)),
        compiler_params=_params("parallel"),
        name="s5_glu",
    )(y, w_glu.astype(BF16), b_glu.astype(F32).reshape(1, BRANCH_W), proj)


def _dft_tables(seq):
    n = 2 * seq
    mm = np.arange(seq, dtype=np.int64)
    k1 = np.arange(seq // DFT_ROWS, dtype=np.int64)[:, None] * DFT_ROWS
    k0 = np.arange(DFT_ROWS, dtype=np.int64)[:, None]
    ang_a = 2.0 * np.pi * ((k1 * mm) % n).astype(np.float64) / n
    ang_b = 2.0 * np.pi * ((k0 * mm) % n).astype(np.float64) / n
    return tuple(jnp.asarray(t, F32) for t in (np.cos(ang_a), np.sin(ang_a), np.cos(ang_b), np.sin(ang_b)))


def _dft_gen_kernel(ac_ref, as_ref, bc_ref, bs_ref, c_ref, s1_ref, s2_ref):
    i = pl.program_id(0)
    a_c = ac_ref[pl.ds(i, 1), :]
    a_s = as_ref[pl.ds(i, 1), :]
    b_c, b_s = bc_ref[...], bs_ref[...]
    cos_t = a_c * b_c - a_s * b_s
    sin_t = a_s * b_c + a_c * b_s
    rows = lax.broadcasted_iota(jnp.int32, cos_t.shape, 0) + i * DFT_ROWS
    cols = lax.broadcasted_iota(jnp.int32, cos_t.shape, 1)
    alt_cols = jnp.where((cols & 1) == 0, 1.0, -1.0).astype(F32)
    alt_rows = jnp.where((rows & 1) == 0, 1.0, -1.0).astype(F32)
    c_ref[...] = cos_t.astype(BF16)
    s1_ref[...] = jnp.where(rows == 0, alt_cols, sin_t).astype(BF16)
    s2_ref[...] = jnp.where(cols == 0, alt_rows, sin_t).astype(BF16)


def _dft_matrices(seq):
    tabs = _dft_tables(seq)
    n_steps = seq // DFT_ROWS
    tab_spec = pl.BlockSpec(tabs[0].shape, lambda i: (0, 0))
    b_spec = pl.BlockSpec((DFT_ROWS, seq), lambda i: (0, 0))
    o_spec = pl.BlockSpec((DFT_ROWS, seq), lambda i: (i, 0))
    return pl.pallas_call(
        _dft_gen_kernel,
        out_shape=(jax.ShapeDtypeStruct((seq, seq), BF16),) * 3,
        grid=(n_steps,),
        in_specs=[tab_spec, tab_spec, b_spec, b_spec],
        out_specs=(o_spec,) * 3,
        compiler_params=_params("parallel"),
        name="dft_gen",
    )(*tabs)


def _hy_filter_kernel(feat_ref, w1_ref, b1_ref, w2_ref, b2_ref, f0_ref, f1_ref,
                      w3p_ref, w3n_ref, b3p_ref, b3n_ref, dl_ref, t_ref,
                      hs_ref, hd_ref, hp_ref, hn_ref, r0_ref, h_ref, split_ref, *, inv_n):
    hi = lax.Precision.HIGHEST

    @pl.when(pl.program_id(0) == 0)
    def _():
        h1 = jnp.sin(f0_ref[...] * (jnp.dot(feat_ref[...], w1_ref[...], precision=hi,
                                            preferred_element_type=F32) + b1_ref[...]))
        h_ref[...] = jnp.sin(f1_ref[...] * (jnp.dot(h1, w2_ref[...], precision=hi,
                                                    preferred_element_type=F32) + b2_ref[...]))

    h = h_ref[...].astype(BF16)
    win = jnp.exp(-t_ref[...] * jnp.abs(dl_ref[...]))
    hpos = (jnp.dot(h, w3p_ref[...].astype(BF16), preferred_element_type=F32) + b3p_ref[...]) * win
    hneg = (jnp.dot(h, w3n_ref[...].astype(BF16), preferred_element_type=F32) + b3n_ref[...]) * win
    rows = lax.broadcasted_iota(jnp.int32, hpos.shape, 0)
    hneg = jnp.where(rows == 0, 0.0, hneg)
    norm = (jnp.sum(jnp.abs(hpos), axis=0, keepdims=True)
            + jnp.sum(jnp.abs(hneg), axis=0, keepdims=True))
    hpos = hpos / norm
    hneg = hneg / norm
    hsum = hpos + hneg
    hdiff = hpos - hneg
    even = (rows & 1) == 0
    alt2 = jnp.where(((rows >> 1) & 1) == 0, 1.0, -1.0).astype(F32)
    col_sum = lambda a: jnp.sum(a, axis=0, keepdims=True)
    a0 = 2.0 * col_sum(jnp.where(even, hsum, 0.0))
    d0 = 2.0 * col_sum(jnp.where(even, 0.0, hsum))
    hr2 = 2.0 * col_sum(jnp.where(even, alt2 * hsum, 0.0))
    hi2 = -2.0 * col_sum(jnp.where(even, 0.0, alt2 * hdiff))
    r0_ref[...] = jnp.concatenate([a0, d0, hr2, hi2, jnp.zeros((4, a0.shape[1]), F32)], axis=0) * inv_n
    def lags(x, par):
        _stage_rows(split_ref, x)
        return _rows_of_parity(split_ref, par).astype(BF16)

    hs_ref[...] = lags(hsum, 0)
    hd_ref[...] = lags(hdiff, 0)
    hp_ref[...] = lags(hpos, 1)
    hn_ref[...] = lags(hneg, 1)


def _hy_filter_taps(seq, w1, b1, w2, b2, freq, w3, b3):
    n_ch = 2 * BRANCH_W
    bands = (HY_EMB - 1) // 2
    t = jnp.linspace(0.0, 1.0, seq, dtype=F32)[:, None]
    w = 2.0 * math.pi * jnp.arange(seq, dtype=F32)[:, None] / seq
    f = jnp.linspace(1e-4, bands - 1, bands, dtype=F32)[None, :]
    feats = jnp.concatenate([t, jnp.cos(f * w), -jnp.sin(f * w),
                             jnp.zeros((seq, HY_PAD - HY_EMB), F32)], axis=-1)
    deltas = jnp.linspace(math.log(HY_DECAY_TARGET) / HY_SLOW_DECAY,
                          math.log(HY_DECAY_TARGET) / HY_FAST_DECAY, n_ch, dtype=F32)[None, :]

    def pad2(a, r, c):
        a = a.astype(F32)
        return jnp.pad(a, ((0, r - a.shape[0]), (0, c - a.shape[1])))

    w1p = pad2(w1, HY_PAD, HY_PAD)
    w2p = pad2(w2, HY_PAD, HY_PAD)
    b1p = pad2(b1[None], 1, HY_PAD)
    b2p = pad2(b2[None], 1, HY_PAD)
    f0p = pad2(freq[0][None], 1, HY_PAD)
    f1p = pad2(freq[1][None], 1, HY_PAD)
    w3p = pad2(w3, HY_PAD, 2 * n_ch)
    b3r = b3.astype(F32)[None]
    tn = 256
    nt = n_ch // tn
    full = lambda shape: pl.BlockSpec(shape, lambda j: (0, 0))
    tap_spec = pl.BlockSpec((seq // 2, tn), lambda j: (0, j))
    return pl.pallas_call(
        functools.partial(_hy_filter_kernel, inv_n=1.0 / (2 * seq)),
        out_shape=(jax.ShapeDtypeStruct((seq // 2, n_ch), BF16),) * 4
                  + (jax.ShapeDtypeStruct((8, n_ch), F32),),
        grid=(nt,),
        in_specs=[full((seq, HY_PAD)), full((HY_PAD, HY_PAD)), full((1, HY_PAD)),
                  full((HY_PAD, HY_PAD)), full((1, HY_PAD)), full((1, HY_PAD)), full((1, HY_PAD)),
                  pl.BlockSpec((HY_PAD, tn), lambda j: (0, j)),
                  pl.BlockSpec((HY_PAD, tn), lambda j: (0, j + nt)),
                  pl.BlockSpec((1, tn), lambda j: (0, j)),
                  pl.BlockSpec((1, tn), lambda j: (0, j + nt)),
                  pl.BlockSpec((1, tn), lambda j: (0, j)),
                  full((seq, 1))],
        out_specs=(tap_spec,) * 4 + (pl.BlockSpec((8, tn), lambda j: (0, j)),),
        scratch_shapes=[pltpu.VMEM((seq, HY_PAD), F32), pltpu.VMEM((tn // 128, seq, 128), F32)],
        compiler_params=_params("arbitrary"),
        name="hy_filter",
    )(feats, w1p, b1p, w2p, b2p, f0p, f1p, w3p, w3p, b3r, b3r, deltas, t)


def _hy_spectrum_kernel(c_ref, s_ref, hse_ref, hde_ref, hpo_ref, hno_ref,
                        ac_ref, as_ref, bc_ref, bs_ref, gc_ref, gs_ref, *, n_half):
    i = pl.program_id(1)
    dot = lambda w, h: jnp.dot(w[...], h[...], preferred_element_type=F32)
    hec, hes = dot(c_ref, hse_ref), dot(s_ref, hde_ref)
    upc, ups = dot(c_ref, hpo_ref), dot(s_ref, hpo_ref)
    umc, ums = dot(c_ref, hno_ref), dot(s_ref, hno_ref)
    tm = hec.shape[0]
    k = (lax.broadcasted_iota(jnp.int32, (tm, 128), 0) + i * tm).astype(F32)
    psi = k * (math.pi / n_half)
    reps = hec.shape[1] // 128
    cp = jnp.concatenate([jnp.cos(psi)] * reps, axis=1)
    sp = jnp.concatenate([jnp.sin(psi)] * reps, axis=1)
    w = 1.0 / n_half
    ac_ref[...] = w * hec
    as_ref[...] = w * hes
    bc_ref[...] = w * (cp * upc - sp * ups + umc)
    bs_ref[...] = w * (cp * ups + sp * upc - ums)
    gc_ref[...] = w * (upc + cp * umc - sp * ums)
    gs_ref[...] = w * (ups - cp * ums - sp * umc)


def _hy_spectrum(cm, s1, taps):
    half, n_ch = taps[0].shape
    tm = min(512, half)
    tn = 512
    w_spec = pl.BlockSpec((tm, half), lambda j, i: (i, 0))
    tap_spec = pl.BlockSpec((half, tn), lambda j, i: (0, j))
    o_spec = pl.BlockSpec((tm, tn), lambda j, i: (i, j))
    return pl.pallas_call(
        functools.partial(_hy_spectrum_kernel, n_half=half),
        out_shape=(jax.ShapeDtypeStruct((half, n_ch), F32),) * 6,
        grid=(n_ch // tn, half // tm),
        in_specs=[w_spec, w_spec] + [tap_spec] * 4,
        out_specs=(o_spec,) * 6,
        compiler_params=_params("parallel", "arbitrary"),
        name="hy_spectrum",
    )(cm, s1, *taps)


def _stage_rows(scr_ref, x):
    for c in range(x.shape[1] // 128):
        scr_ref[c] = x[:, c * 128:(c + 1) * 128]


def _rows_of_parity(scr_ref, par):
    n = scr_ref.shape[1] // 2
    return jnp.concatenate([scr_ref[c, pl.ds(par, n, stride=2), :] for c in range(scr_ref.shape[0])], axis=1)


def _hy_conv3_kernel(u_ref, w_ref, b_ref, o_ref, *rest):
    vb_ref, s_ref = rest if len(rest) == 2 else (None, rest[0])
    u = u_ref[...]
    n = u.shape[0]
    rows = lax.broadcasted_iota(jnp.int32, u.shape, 0)
    prev = jnp.where(rows == 0, 0.0, pltpu.roll(u, 1, 0))
    nxt = jnp.where(rows == n - 1, 0.0, pltpu.roll(u, n - 1, 0))
    w = w_ref[...]
    _stage_rows(s_ref, prev * w[0:1] + u * w[1:2] + nxt * w[2:3] + b_ref[...])
    for par in range(2):
        part = _rows_of_parity(s_ref, par)
        o_ref[par] = part
        if vb_ref is not None:
            vb_ref[par] = part.astype(BF16)


def _hy_conv3(proj, bsz, seq, conv_w, conv_b, col, width, with_bf16):
    half = seq // 2
    tn = 256
    o_spec = pl.BlockSpec((2, half, tn), lambda b, j: (0, b, j))
    shape = lambda dt: jax.ShapeDtypeStruct((2, bsz * half, width), dt)
    return pl.pallas_call(
        _hy_conv3_kernel,
        out_shape=(shape(F32), shape(BF16)) if with_bf16 else shape(F32),
        grid=(bsz, width // tn),
        in_specs=[pl.BlockSpec((seq, tn), lambda b, j: (b, j + (COL_BU + col) // tn)),
                  pl.BlockSpec((3, tn), lambda b, j: (0, j + col // tn)),
                  pl.BlockSpec((1, tn), lambda b, j: (0, j + col // tn))],
        out_specs=(o_spec, o_spec) if with_bf16 else o_spec,
        scratch_shapes=[pltpu.VMEM((tn // 128, seq, 128), F32)],
        compiler_params=_params("parallel", "parallel"),
        name="hy_conv3",
    )(proj, conv_w.astype(F32), conv_b.astype(F32).reshape(1, -1))


def _hy_fwd_kernel(c_ref, s_ref, ze_ref, zo_ref, ac_ref, as_ref, bc_ref, bs_ref, gc_ref, gs_ref, r0_ref,
                   pc_ref, ps_ref, qc_ref, qs_ref):
    dot = lambda w, z: jnp.dot(w[...], z[...], preferred_element_type=F32)
    ec, es = dot(c_ref, ze_ref), dot(s_ref, ze_ref)
    oc, os_ = dot(c_ref, zo_ref), dot(s_ref, zo_ref)
    a_c, a_s = ac_ref[...], as_ref[...]
    b_c, b_s = bc_ref[...], bs_ref[...]
    g_c, g_s = gc_ref[...], gs_ref[...]
    outs = (ec * a_c - es * a_s + oc * b_c - os_ * b_s,
            ec * a_s + es * a_c + oc * b_s + os_ * b_c,
            ec * g_c - es * g_s + oc * a_c - os_ * a_s,
            ec * g_s + es * g_c + oc * a_s + os_ * a_c)
    r0 = r0_ref[...]
    a0, d0, hr2, hi2 = r0[0:1], r0[1:2], r0[2:3], r0[3:4]
    e0, eh, o0, oh = ec[0:1], es[0:1], oc[0:1], os_[0:1]
    first = (e0 * a0 + o0 * d0, eh * hr2 + oh * hi2, e0 * d0 + o0 * a0, oh * hr2 - eh * hi2)
    top = 16
    is_row0 = (lax.broadcasted_iota(jnp.int32, (top, ec.shape[1]), 0) == 0) & (pl.program_id(2) == 0)
    for ref, val, row0 in zip((pc_ref, ps_ref, qc_ref, qs_ref), outs, first):
        ref[...] = val.astype(BF16)
        ref[0:top, :] = jnp.where(is_row0, row0, val[0:top]).astype(BF16)


HY_TN = 512


def _hy_fwd(cm, s1, z_even, z_odd, tables, r0, h_col, bsz, half):
    tm = min(512, half)
    tn = HY_TN
    mt = half // tm
    w_spec = pl.BlockSpec((tm, half), lambda b, j, i: (i, 0))
    h_spec = pl.BlockSpec((tm, tn), lambda b, j, i: (i, j + h_col // tn))
    o_spec = pl.BlockSpec((tm, tn), lambda b, j, i: (b * mt + i, j))
    return pl.pallas_call(
        _hy_fwd_kernel,
        out_shape=(jax.ShapeDtypeStruct((bsz * half, BRANCH_W), BF16),) * 4,
        grid=(bsz, BRANCH_W // tn, mt),
        in_specs=[w_spec, w_spec, z_even[1], z_odd[1]] + [h_spec] * 6
                 + [pl.BlockSpec((8, tn), lambda b, j, i: (0, j + h_col // tn))],
        out_specs=(o_spec,) * 4,
        compiler_params=_params("parallel", "parallel", "arbitrary"),
        name="hy_fwd",
    )(cm, s1, z_even[0], z_odd[0], *tables, r0)


def _hy_inv_convs(c_ref, s_ref, pc_ref, ps_ref, qc_ref, qs_ref):
    dot = lambda w, y: jnp.dot(w[...], y[...], preferred_element_type=F32)
    return dot(c_ref, pc_ref) + dot(s_ref, ps_ref), dot(c_ref, qc_ref) + dot(s_ref, qs_ref)


def _hy_inv_mid_kernel(c_ref, s_ref, pc_ref, ps_ref, qc_ref, qs_ref, ge_ref, go_ref, ze_ref, zo_ref,
                       bias_ref, o_ref, ob_ref):
    convs = _hy_inv_convs(c_ref, s_ref, pc_ref, ps_ref, qc_ref, qs_ref)
    for par, (conv, g_ref, z_ref) in enumerate(zip(convs, (ge_ref, go_ref), (ze_ref, zo_ref))):
        out = g_ref[...] * (conv + bias_ref[...] * z_ref[...])
        o_ref[par] = out
        ob_ref[par] = out.astype(BF16)


def _hy_inv_last_kernel(c_ref, s_ref, pc_ref, ps_ref, qc_ref, qs_ref, ge_ref, go_ref, ze_ref, zo_ref,
                        bias_ref, sz_ref, o_ref, mix_ref):
    convs = _hy_inv_convs(c_ref, s_ref, pc_ref, ps_ref, qc_ref, qs_ref)
    tm = ge_ref.shape[0]
    _stage_rows(mix_ref, _silu(sz_ref[...]))
    outs = [g_ref[...] * (conv + bias_ref[...] * z_ref[...]) * _rows_of_parity(mix_ref, par)
            for par, (conv, g_ref, z_ref) in enumerate(zip(convs, (ge_ref, go_ref), (ze_ref, zo_ref)))]
    for par, out in enumerate(outs):
        for c in range(mix_ref.shape[0]):
            mix_ref[c, pl.ds(par, tm, stride=2), :] = out[:, c * 128:(c + 1) * 128]
    for c in range(mix_ref.shape[0]):
        o_ref[:, c * 128:(c + 1) * 128] = mix_ref[c]


def _hy_inv(cm, s2, spectra, gates, zprev, bias_row, bsz, half, silu=None):
    tm = min(512, half)
    tn = HY_TN
    mt = half // tm
    w_spec = pl.BlockSpec((tm, half), lambda b, j, i: (i, 0))
    y_spec = pl.BlockSpec((half, tn), lambda b, j, i: (b, j))
    pairs = list(gates) + list(zprev)
    in_specs = [w_spec, w_spec] + [y_spec] * 4 + [s for _, s in pairs] + [pl.BlockSpec((1, tn), lambda b, j, i: (0, j))]
    args = [cm, s2, *spectra] + [a for a, _ in pairs] + [bias_row]
    if silu is None:
        o_spec = pl.BlockSpec((2, tm, tn), lambda b, j, i: (0, b * mt + i, j))
        shape = lambda dt: jax.ShapeDtypeStruct((2, bsz * half, BRANCH_W), dt)
        body, out_shape, out_specs = _hy_inv_mid_kernel, (shape(F32), shape(BF16)), (o_spec, o_spec)
        scratch = []
    else:
        silu_arr, silu_col = silu
        body = _hy_inv_last_kernel
        out_shape = jax.ShapeDtypeStruct((2 * bsz * half, BRANCH_W), F32)
        out_specs = pl.BlockSpec((2 * tm, tn), lambda b, j, i: (b * mt + i, j))
        in_specs.append(pl.BlockSpec((2 * tm, tn), lambda b, j, i: (b * mt + i, j + silu_col // tn)))
        args.append(silu_arr)
        scratch = [pltpu.VMEM((tn // 128, 2 * tm, 128), F32)]
    return pl.pallas_call(
        body, out_shape=out_shape, grid=(bsz, BRANCH_W // tn, mt), in_specs=in_specs, out_specs=out_specs,
        scratch_shapes=scratch, compiler_params=_params("parallel", "parallel", "arbitrary"),
        name="hy_inv",
    )(*args)


def _hyena_branch(proj, bsz, seq, dft, conv_w, conv_b, w1, b1, w2, b2, freq, w3, b3, bias):
    cm, s1, s2 = dft
    half = seq // 2
    tm = min(512, half)
    tn = HY_TN
    mt = half // tm
    full = lambda par, col=0: pl.BlockSpec((None, half, tn), lambda b, j, i: (par, b, j + col // tn))
    row = lambda par, col=0: pl.BlockSpec((None, tm, tn), lambda b, j, i: (par, b * mt + i, j + col // tn))
    both = lambda arr, spec, col=0: ((arr, spec(0, col)), (arr, spec(1, col)))
    *taps, r0 = _hy_filter_taps(seq, w1, b1, w2, b2, freq, w3, b3)
    tables = _hy_spectrum(cm, s1, taps)
    v, vb = _hy_conv3(proj, bsz, seq, conv_w, conv_b, 0, BRANCH_W, True)
    x12 = _hy_conv3(proj, bsz, seq, conv_w, conv_b, BRANCH_W, 2 * BRANCH_W, False)
    bias = bias.astype(F32)
    spectra = _hy_fwd(cm, s1, *both(vb, full), tables, r0, 0, bsz, half)
    z1, z1b = _hy_inv(cm, s2, spectra, gates=both(x12, row), zprev=both(v, row),
                      bias_row=bias[0:1], bsz=bsz, half=half)
    spectra = _hy_fwd(cm, s1, *both(z1b, full), tables, r0, BRANCH_W, bsz, half)
    return _hy_inv(cm, s2, spectra, gates=both(x12, row, BRANCH_W), zprev=both(z1, row),
                   bias_row=bias[1:2], bsz=bsz, half=half, silu=(proj, COL_BZ))


def _rope_table_kernel(pos_ref, inv_ref, cos_ref, sin_ref):
    ang = pos_ref[...] * inv_ref[...]
    lane = lax.broadcasted_iota(jnp.int32, ang.shape, 1)
    live = lane < MLA_ROPE
    cos_ref[...] = jnp.where(live, jnp.cos(ang), 0.0)
    sin_ref[...] = jnp.where(live, jnp.where(lane < MLA_ROPE // 2, -1.0, 1.0) * jnp.sin(ang), 0.0)


def _rope_tables(positions):
    m = positions.size
    half = MLA_ROPE // 2
    inv = ROPE_BASE ** (-jnp.arange(half, dtype=F32) / half)
    inv = jnp.concatenate([inv, inv, jnp.zeros((128 - MLA_ROPE,), F32)])[None]
    pos = positions.astype(F32).reshape(m, 1)
    tm = min(1024, m)
    spec = pl.BlockSpec((tm, 128), lambda i: (i, 0))
    return pl.pallas_call(
        _rope_table_kernel,
        out_shape=(jax.ShapeDtypeStruct((m, 128), F32),) * 2,
        grid=(m // tm,),
        in_specs=[pl.BlockSpec((tm, 1), lambda i: (i, 0)), pl.BlockSpec((1, 128), lambda i: (0, 0))],
        out_specs=(spec, spec),
        compiler_params=_params("parallel"),
        name="rope_table",
    )(pos, inv)


def _rope128(x, cos_t, sin_t):
    lane = lax.broadcasted_iota(jnp.int32, x.shape, 1)
    half = MLA_ROPE // 2
    partner = jnp.where(lane < half, pltpu.roll(x, 128 - half, 1), pltpu.roll(x, half, 1))
    return x * cos_t + partner * sin_t


def _rms(x, g):
    ms = jnp.mean(jnp.square(x), axis=-1, keepdims=True)
    return x * lax.rsqrt(ms + RMS_EPS) * g


def _mla_q_kernel(cq_ref, g_ref, w_ref, cos_ref, sin_ref, q_ref, *, scale):
    xn = _rms(cq_ref[...], g_ref[...]).astype(BF16)
    q = jnp.dot(xn, w_ref[...], preferred_element_type=F32) * scale
    cos_t, sin_t = cos_ref[...], sin_ref[...]
    for h in range(MLA_HEADS):
        base = h * MLA_QK_PAD
        q_ref[:, base:base + MLA_NOPE] = q[:, base:base + MLA_NOPE].astype(BF16)
        q_ref[:, base + MLA_NOPE:base + MLA_QK_PAD] = _rope128(
            q[:, base + MLA_NOPE:base + MLA_QK_PAD], cos_t, sin_t).astype(BF16)


def _mla_kv_kernel(ckv_ref, g_ref, wk_ref, wv_ref, kr_ref, cos_ref, sin_ref, k_ref, v_ref):
    xn = _rms(ckv_ref[...], g_ref[...]).astype(BF16)
    kn = jnp.dot(xn, wk_ref[...], preferred_element_type=F32)
    v_ref[...] = jnp.dot(xn, wv_ref[...], preferred_element_type=F32).astype(BF16)
    kr = _rope128(kr_ref[:, 0:128], cos_ref[...], sin_ref[...]).astype(BF16)
    for h in range(MLA_HEADS):
        base = h * MLA_QK_PAD
        k_ref[:, base:base + MLA_NOPE] = kn[:, h * MLA_NOPE:(h + 1) * MLA_NOPE].astype(BF16)
        k_ref[:, base + MLA_NOPE:base + MLA_QK_PAD] = kr


def _mla_attn_kernel(q_ref, k_ref, v_ref, z_ref, o_ref):
    k, v = k_ref[...], v_ref[...]
    half = q_ref.shape[0] // 2
    for r in range(2):
        rows = pl.ds(r * half, half)
        s = lax.dot_general(q_ref[rows, :], k, (((1,), (1,)), ((), ())),
                            preferred_element_type=F32)
        p = jnp.exp2(s - jnp.max(s, axis=-1, keepdims=True))
        l = jnp.sum(p, axis=-1, keepdims=True)
        o = jnp.dot(p.astype(BF16), v, preferred_element_type=F32)
        o_ref[rows, :] = (o / l * _silu(z_ref[rows, :])).astype(BF16)


def _mla_branch(proj, bsz, seq, rope, q_norm_g, w_uq, kv_norm_g, w_ukv):
    m = bsz * seq
    cos_t, sin_t = rope
    dqk = MLA_NOPE + MLA_ROPE
    hq = MLA_HEADS * MLA_QK_PAD
    w_q = w_uq.reshape(MLA_LORA, MLA_HEADS, dqk)
    w_q = jnp.pad(w_q, ((0, 0), (0, 0), (0, MLA_QK_PAD - dqk))).reshape(MLA_LORA, hq).astype(BF16)
    w_kv = w_ukv.reshape(MLA_LORA, MLA_HEADS, MLA_NOPE + MLA_V)
    w_k = w_kv[:, :, :MLA_NOPE].reshape(MLA_LORA, MLA_HEADS * MLA_NOPE).astype(BF16)
    w_v = w_kv[:, :, MLA_NOPE:].reshape(MLA_LORA, MLA_HEADS * MLA_V).astype(BF16)
    tm = min(512, m)
    row = lambda shape, col=0: pl.BlockSpec(shape, lambda i: (i, col))
    full = lambda shape: pl.BlockSpec(shape, lambda i: (0, 0))
    qp = pl.pallas_call(
        functools.partial(_mla_q_kernel, scale=dqk ** -0.5 * math.log2(math.e)),
        out_shape=jax.ShapeDtypeStruct((m, hq), BF16),
        grid=(m // tm,),
        in_specs=[row((tm, MLA_LORA), COL_CQ // MLA_LORA), full((1, MLA_LORA)), full((MLA_LORA, hq)),
                  row((tm, 128)), row((tm, 128))],
        out_specs=row((tm, hq)),
        compiler_params=_params("parallel"),
        name="mla_q",
    )(proj, q_norm_g.astype(F32).reshape(1, -1), w_q, cos_t, sin_t)
    kp, vp = pl.pallas_call(
        _mla_kv_kernel,
        out_shape=(jax.ShapeDtypeStruct((m, hq), BF16),
                   jax.ShapeDtypeStruct((m, MLA_HEADS * MLA_V), BF16)),
        grid=(m // tm,),
        in_specs=[row((tm, MLA_LORA), COL_CKV // MLA_LORA), full((1, MLA_LORA)),
                  full((MLA_LORA, MLA_HEADS * MLA_NOPE)), full((MLA_LORA, MLA_HEADS * MLA_V)),
                  row((tm, 512), COL_CKR // 512), row((tm, 128)), row((tm, 128))],
        out_specs=(row((tm, hq)), row((tm, MLA_HEADS * MLA_V))),
        compiler_params=_params("parallel"),
        name="mla_kv",
    )(proj, kv_norm_g.astype(F32).reshape(1, -1), w_k, w_v, proj, cos_t, sin_t)
    tq = min(512, seq)
    qt = seq // tq
    return pl.pallas_call(
        _mla_attn_kernel,
        out_shape=jax.ShapeDtypeStruct((m, MLA_HEADS * MLA_V), BF16),
        grid=(bsz, MLA_HEADS, qt),
        in_specs=[pl.BlockSpec((tq, MLA_QK_PAD), lambda b, h, i: (b * qt + i, h)),
                  pl.BlockSpec((seq, MLA_QK_PAD), lambda b, h, i: (b, h)),
                  pl.BlockSpec((seq, MLA_V), lambda b, h, i: (b, h)),
                  pl.BlockSpec((tq, MLA_V), lambda b, h, i: (b * qt + i, h + COL_CZ // MLA_V))],
        out_specs=pl.BlockSpec((tq, MLA_V), lambda b, h, i: (b * qt + i, h)),
        compiler_params=_params("parallel", "parallel", "arbitrary"),
        name="mla_attn",
    )(qp, kp, vp, proj)


def _lift_kernel(ya_ref, yb_ref, yc_ref, w_ref, ga_ref, gb_ref, gc_ref, o_ref, wb_ref):
    @pl.when(pl.program_id(1) == 0)
    def _():
        wb_ref[...] = w_ref[...].astype(BF16)

    acc = _sigmoid(ga_ref[...]) * jnp.dot(ya_ref[...], wb_ref[0], preferred_element_type=F32)
    acc += _sigmoid(gb_ref[...]) * jnp.dot(yb_ref[...].astype(BF16), wb_ref[1], preferred_element_type=F32)
    acc += _sigmoid(gc_ref[...]) * jnp.dot(yc_ref[...], wb_ref[2], preferred_element_type=F32)
    o_ref[...] = acc.astype(BF16)


def _lift(ya, yb, yc, w_lift, layer, proj):
    m = ya.shape[0]
    tm = min(1024, m)
    tn = 512
    y_spec = pl.BlockSpec((tm, BRANCH_W), lambda j, i: (i, 0))
    gate = lambda n: pl.BlockSpec((tm, tn), lambda j, i: (i, j + (COL_GATE + n * D_MODEL) // tn))
    return pl.pallas_call(
        _lift_kernel,
        out_shape=jax.ShapeDtypeStruct((m, D_MODEL), BF16),
        grid=(D_MODEL // tn, m // tm),
        in_specs=[y_spec, y_spec, y_spec,
                  pl.BlockSpec((None, N_BRANCH, BRANCH_W, tn), lambda j, i: (layer, 0, 0, j)),
                  gate(0), gate(1), gate(2)],
        out_specs=pl.BlockSpec((tm, tn), lambda j, i: (i, j)),
        scratch_shapes=[pltpu.VMEM((N_BRANCH, BRANCH_W, tn), BF16)],
        compiler_params=_params("parallel", "arbitrary"),
        name="lift",
    )(ya, yb, yc, w_lift, proj, proj, proj)


def _out_kernel(mix_ref, wo_ref, p_ref, wp_ref, sp_ref, x_ref, g_ref, b_ref, o_ref):
    mixed = jnp.dot(mix_ref[...], wo_ref[...], preferred_element_type=F32)
    ple = jnp.dot(p_ref[...].astype(BF16), wp_ref[...], preferred_element_type=F32) * _sigmoid(sp_ref[...])
    r = DEEPNORM_ALPHA * x_ref[...] + mixed + ple
    mu = jnp.mean(r, axis=-1, keepdims=True)
    var = jnp.mean(jnp.square(r - mu), axis=-1, keepdims=True)
    o_ref[...] = (r - mu) * lax.rsqrt(var + LN_EPS) * g_ref[...] + b_ref[...]


def _out_norm(mix, w_out, p, w_ple, proj, x, ln_g, ln_b):
    m = mix.shape[0]
    tm = min(512, m)
    row = lambda w, col=0: pl.BlockSpec((tm, w), lambda i: (i, col))
    full = lambda shape: pl.BlockSpec(shape, lambda i: (0, 0))
    return pl.pallas_call(
        _out_kernel,
        out_shape=jax.ShapeDtypeStruct((m, D_MODEL), F32),
        grid=(m // tm,),
        in_specs=[row(D_MODEL), full((D_MODEL, D_MODEL)), row(PLE_DIM), full((PLE_DIM, D_MODEL)),
                  row(D_MODEL, COL_PLE // D_MODEL), row(D_MODEL), full((1, D_MODEL)), full((1, D_MODEL))],
        out_specs=row(D_MODEL),
        compiler_params=_params("parallel"),
        name="out_norm",
    )(mix, w_out.astype(BF16), p, w_ple.astype(BF16), proj, x,
      ln_g.astype(F32).reshape(1, -1), ln_b.astype(F32).reshape(1, -1))


def kernel(x, p, positions, w_in, s5_lambda_re, s5_lambda_im, s5_log_dt, s5_b_re, s5_b_im, s5_c_re, s5_c_im, s5_d, s5_w_glu, s5_b_glu, hy_conv_w, hy_conv_b, hy_w1, hy_b1, hy_w2, hy_b2, hy_freq, hy_w3, hy_b3, hy_bias, mla_q_norm, mla_w_uq, mla_kv_norm, mla_w_ukv, w_lift, w_out, w_ple, ln_g, ln_b):
    bsz, seq, _ = x.shape
    m = bsz * seq
    depth = w_in.shape[0]
    dft = _dft_matrices(seq // 2)
    rope = _rope_tables(positions)
    s5_tiled = _s5_tiled_params(s5_lambda_re, s5_lambda_im, s5_log_dt, s5_b_re, s5_b_im,
                                s5_c_re, s5_c_im, s5_d)
    xf = x.reshape(m, D_MODEL).astype(F32)
    w_t = jnp.swapaxes(w_in, 1, 2)
    for i in range(depth):
        proj = _proj_all(xf, w_t, i)
        y_a = _s5_branch(proj, bsz, seq, [a[i] for a in s5_tiled], s5_w_glu[i], s5_b_glu[i])
        y_b = _hyena_branch(proj, bsz, seq, dft, hy_conv_w[i], hy_conv_b[i], hy_w1[i], hy_b1[i],
                            hy_w2[i], hy_b2[i], hy_freq[i], hy_w3[i], hy_b3[i], hy_bias[i])
        y_c = _mla_branch(proj, bsz, seq, rope, mla_q_norm[i], mla_w_uq[i], mla_kv_norm[i], mla_w_ukv[i])
        mix = _lift(y_a, y_b, y_c, w_lift, i, proj)
        xf = _out_norm(mix, w_out[i], p[i].reshape(m, PLE_DIM), w_ple[i], proj, xf, ln_g[i], ln_b[i])
    return xf.reshape(bsz, seq, D_MODEL).astype(x.dtype)
```

```python
import functools
import math

import numpy as np
import jax
import jax.numpy as jnp
from jax import lax
from jax.experimental import pallas as pl
from jax.experimental.pallas import tpu as pltpu

F32 = jnp.float32
BF16 = jnp.bfloat16

D_MODEL = 2048
PLE_DIM = 256
N_BRANCH = 3
BRANCH_W = 1024

S5_GROUP = 16
S5_GROUPS = BRANCH_W // S5_GROUP
S5_STATE = 64
S5_CHUNK = 16
S5_ROW = S5_CHUNK * S5_GROUP

HY_EMB = 33
HY_FF = 64
HY_PAD = 128
HY_DECAY_TARGET = 0.01
HY_FAST_DECAY = 0.3
HY_SLOW_DECAY = 1.5
DFT_ROWS = 64

MLA_HEADS = 8
MLA_NOPE = 128
MLA_ROPE = 64
MLA_V = 128
MLA_LORA = 512
MLA_QK_PAD = 256
ROPE_BASE = 10000.0

LN_EPS = 1e-5
RMS_EPS = 1e-6
DEPTH = 2
DEEPNORM_ALPHA = (2 * DEPTH) ** 0.25

COL_AX = 0
COL_AZ = 1024
COL_BU = 2048
COL_BZ = 5120
COL_CQ = 6144
COL_CKV = 6656
PROJ_KEEP = 7168
COL_CZ = 7168
COL_PLE = 8192
COL_GATE = 10240
COL_CKR = 16384
PROJ_N = 16896
PROJ_TN = 512
W_IN_CKR = 7168
W_IN_CZ = 7232
W_IN_GATE = 8256
W_IN_PLE = 14400

VMEM_LIMIT = 56 * 1024 * 1024


def _params(*sem):
    return pltpu.CompilerParams(dimension_semantics=sem, vmem_limit_bytes=VMEM_LIMIT)


def _sigmoid(x):
    return 0.5 * jnp.tanh(0.5 * x) + 0.5


def _silu(x):
    return x * _sigmoid(x)


def _proj_kernel(x_ref, w_ref, o_ref, xb_ref):
    @pl.when(pl.program_id(1) == 0)
    def _():
        xb_ref[...] = x_ref[...].astype(BF16)

    o_ref[...] = lax.dot_general(xb_ref[...], w_ref[...].astype(BF16), (((1,), (1,)), ((), ())),
                                 preferred_element_type=F32)


def _proj(x, w, layer, n_out, name):
    m, k = x.shape
    tm = min(1024, m)
    return pl.pallas_call(
        _proj_kernel,
        out_shape=jax.ShapeDtypeStruct((m, n_out), F32),
        grid=(m // tm, n_out // PROJ_TN),
        in_specs=[pl.BlockSpec((tm, k), lambda i, j: (i, 0)),
                  pl.BlockSpec((None, PROJ_TN, k), lambda i, j: (layer, j, 0))],
        out_specs=pl.BlockSpec((tm, PROJ_TN), lambda i, j: (i, j)),
        scratch_shapes=[pltpu.VMEM((tm, k), BF16)],
        compiler_params=_params("parallel", "arbitrary"),
        name=name,
    )(x, w)


def _repack_src_tile(j):
    t = lambda col: col // PROJ_TN
    return jnp.where(j < t(COL_CZ), j,
                     jnp.where(j < t(COL_PLE), j - t(COL_CZ) + t(W_IN_CZ),
                               jnp.where(j < t(COL_GATE), j - t(COL_PLE) + t(W_IN_PLE),
                                         jnp.where(j < t(COL_CKR), j - t(COL_GATE) + t(W_IN_GATE),
                                                   t(W_IN_CKR)))))


def _repack_kernel(a_ref, b_ref, o_ref):
    j = pl.program_id(0)
    a = a_ref[...]
    shifted = jnp.concatenate([a[MLA_ROPE:], b_ref[...]], axis=0)
    key = jnp.concatenate([a[:MLA_ROPE], jnp.zeros_like(a[MLA_ROPE:])], axis=0)
    out = jnp.where(j < PROJ_KEEP // PROJ_TN, a, jnp.where(j == COL_CKR // PROJ_TN, key, shifted))
    o_ref[...] = out.astype(BF16)


def _repack(w_t, layer):
    k = w_t.shape[2]
    sub = PROJ_TN // MLA_ROPE
    return pl.pallas_call(
        _repack_kernel,
        out_shape=jax.ShapeDtypeStruct((1, PROJ_N, k), BF16),
        grid=(PROJ_N // PROJ_TN,),
        in_specs=[pl.BlockSpec((None, PROJ_TN, k), lambda j: (layer, _repack_src_tile(j), 0)),
                  pl.BlockSpec((None, MLA_ROPE, k), lambda j: (layer, (_repack_src_tile(j) + 1) * sub, 0))],
        out_specs=pl.BlockSpec((None, PROJ_TN, k), lambda j: (0, j, 0)),
        compiler_params=_params("parallel"),
        name="repack",
    )(w_t, w_t)


def _proj_all(x, w_t, layer):
    return _proj(x, _repack(w_t, layer), 0, PROJ_N, "proj")


S5_LANES = 8 * S5_STATE
S5_SLAB = 8
S5_RELAYOUT_ROWS = 32


def _s5_tile_lanes(a):
    return jnp.concatenate([a[:, 0]] * 4 + [a[:, 1]] * 4, axis=-1)


def _s5_tiled_params(lam_re, lam_im, log_dt, b_re, b_im, c_re, c_im, d):
    f = lambda a: a.astype(F32)
    depth = lam_re.shape[0]
    ldt = jnp.broadcast_to(f(log_dt)[..., None, None], lam_re.shape[:3] + (1, S5_STATE))
    return (_s5_tile_lanes(f(lam_re)[:, :, :, None, :]), _s5_tile_lanes(f(lam_im)[:, :, :, None, :]),
            _s5_tile_lanes(ldt),
            _s5_tile_lanes(jnp.swapaxes(f(b_re), -1, -2)), _s5_tile_lanes(jnp.swapaxes(f(b_im), -1, -2)),
            _s5_tile_lanes(f(c_re)), _s5_tile_lanes(f(c_im)),
            f(d).reshape(depth, S5_GROUPS, S5_GROUP, 1))


def _s5_mats_kernel(lr_ref, li_ref, ldt_ref, br_ref, bi_ref, cr_ref, ci_ref, d_ref,
                    ms_ref, mi_ref, mo_ref, ar_ref, ai_ref):
    t_n, h_n = S5_CHUNK, S5_GROUP
    hi = lax.Precision.HIGHEST
    nt = (((1,), (1,)), ((), ()))
    blk = lax.broadcasted_iota(jnp.int32, (1, S5_LANES), 1) // S5_STATE
    is_im = (blk // 2) % 2 == 1
    is_fwd = blk < 4
    steps = lax.broadcasted_iota(jnp.int32, (24, S5_LANES), 0).astype(F32)
    lane_k = lax.broadcasted_iota(jnp.int32, (h_n, S5_ROW), 1)
    sub_k = lax.broadcasted_iota(jnp.int32, (h_n, S5_ROW), 0)

    def per_group(gi, carry):
        lr, li = lr_ref[gi], li_ref[gi]
        dt = jnp.exp(ldt_ref[gi])
        zr, zi = lr * dt, li * dt
        mag = jnp.exp(steps * zr)
        tr, ti = mag * jnp.cos(steps * zi), mag * jnp.sin(steps * zi)
        lbr, lbi = tr[1:2], ti[1:2]
        n2 = lr * lr + li * li
        qr = ((lbr - 1.0) * lr + lbi * li) / n2
        qi = (lbi * lr - (lbr - 1.0) * li) / n2
        br, bi = br_ref[gi], bi_ref[gi]
        bbr, bbi = qr * br - qi * bi, qr * bi + qi * br
        y1, y2 = jnp.where(is_im, bbi, bbr), jnp.where(is_im, bbr, bbi)
        cr, ci = cr_ref[gi], ci_ref[gi]

        def pick(tab, t_fwd, t_bwd):
            return jnp.where(is_fwd, tab[t_fwd:t_fwd + 1], tab[t_bwd:t_bwd + 1])

        def c_times(p_r, p_i):
            return jnp.where(is_im, -(cr * p_i + ci * p_r), cr * p_r - ci * p_i)

        ct = [c_times(tr[s:s + 1], ti[s:s + 1]) for s in range(t_n + 1)]
        q_rows = []
        for t in range(t_n):
            a_r, a_i = pick(tr, t_n - 1 - t, t), pick(ti, t_n - 1 - t, t)
            rows = pl.ds(t * h_n, h_n)
            ms_ref[gi, rows, :] = (a_r * y1 + jnp.where(is_im, a_i, -a_i) * y2).astype(BF16)
            mo_ref[gi, rows, :] = jnp.where(is_fwd, ct[t + 1], ct[t_n - t]).astype(BF16)
            q_rows.append(jnp.where(is_fwd, ct[t], ct[t_n - 1 - t]))
        q = jnp.concatenate(q_rows, axis=0)
        half = S5_LANES // 2
        kf = 0.5 * lax.dot_general(y1[:, :half], q[:, :half], nt, precision=hi, preferred_element_type=F32)
        kb = 0.5 * lax.dot_general(y1[:, half:], q[:, half:], nt, precision=hi, preferred_element_type=F32)
        kf = kf + jnp.where(lane_k == sub_k, d_ref[gi], 0.0)
        for t in range(t_n):
            fwd = kf if t == 0 else jnp.where(lane_k >= h_n * t, pltpu.roll(kf, h_n * t, 1), 0.0)
            sh = h_n * (t_n - 1 - t)
            bwd = kb if sh == 0 else jnp.where(lane_k < S5_ROW - sh, pltpu.roll(kb, S5_ROW - sh, 1), 0.0)
            mi_ref[gi, pl.ds(t * h_n, h_n), :] = (fwd + bwd).astype(BF16)
        ar_ref[gi] = tr[t_n:t_n + 1]
        ai_ref[gi] = ti[t_n:t_n + 1]
        return carry

    lax.fori_loop(0, lr_ref.shape[0], per_group, 0)


def _s5_mats(tiled):
    g_n = S5_GROUPS
    gb = S5_SLAB
    spec = lambda r, w: pl.BlockSpec((gb, r, w), lambda j: (j, 0, 0))
    in_rows = (1, 1, 1, S5_GROUP, S5_GROUP, S5_GROUP, S5_GROUP)
    return pl.pallas_call(
        _s5_mats_kernel,
        out_shape=(jax.ShapeDtypeStruct((g_n, S5_ROW, S5_LANES), BF16),
                   jax.ShapeDtypeStruct((g_n, S5_ROW, S5_ROW), BF16),
                   jax.ShapeDtypeStruct((g_n, S5_ROW, S5_LANES), BF16),
                   jax.ShapeDtypeStruct((g_n, 1, S5_LANES), F32),
                   jax.ShapeDtypeStruct((g_n, 1, S5_LANES), F32)),
        grid=(g_n // gb,),
        in_specs=[spec(r, S5_LANES) for r in in_rows] + [spec(S5_GROUP, 1)],
        out_specs=(spec(S5_ROW, S5_LANES), spec(S5_ROW, S5_ROW), spec(S5_ROW, S5_LANES),
                   spec(1, S5_LANES), spec(1, S5_LANES)),
        compiler_params=_params("parallel"),
        name="s5_mats",
    )(*tiled)


def _seg_transpose(vs, seg):
    vs = list(vs)
    for s in (4, 2, 1):
        keep = (seg & s) == 0
        for i in range(8):
            if i & s:
                continue
            a, b = vs[i], vs[i + s]
            vs[i] = jnp.where(keep, a, pltpu.roll(b, s * S5_GROUP, 1))
            vs[i + s] = jnp.where(keep, pltpu.roll(a, 128 - s * S5_GROUP, 1), b)
    return vs


def _s5_main_kernel(x_ref, ms_ref, mi_ref, mo_ref, ar_ref, ai_ref, y_ref,
                    u_ref, sl_ref, st_ref, yg_ref, *, bsz):
    rows = x_ref.shape[0]
    n_chunks = rows // bsz
    rc = S5_RELAYOUT_ROWS
    seg = lax.broadcasted_iota(jnp.int32, (rc, 128), 1) // S5_GROUP
    slot = (lax.broadcasted_iota(jnp.int32, (1, S5_LANES), 1) // S5_STATE) % 2
    slot128 = slot[:, :128]

    def relayout_in(r, carry):
        r0 = pl.multiple_of(r * rc, rc)
        for th in range(2):
            src = [x_ref[pl.ds(r0, rc), th * 8 + t8, :] for t8 in range(8)]
            for gi, out in enumerate(_seg_transpose(src, seg)):
                u_ref[gi, pl.ds(r0, rc), th * 128:(th + 1) * 128] = out.astype(BF16)
        return carry

    lax.fori_loop(0, rows // rc, relayout_in, 0, unroll=2)

    n_pairs = S5_SLAB // 2
    trans = []
    for jp in range(n_pairs):
        g0, g1 = 2 * jp, 2 * jp + 1
        r0 = jnp.dot(u_ref[g0], ms_ref[g0], preferred_element_type=F32)
        r1 = jnp.dot(u_ref[g1], ms_ref[g1], preferred_element_type=F32)
        loc = jnp.where(slot == 0, r0, r1)
        for k in range(4):
            for b in range(bsz):
                sl_ref[k, jp, pl.ds(b, n_chunks, stride=bsz), :] = (
                    loc[b * n_chunks:(b + 1) * n_chunks, k * 128:(k + 1) * 128])
        trans.append([jnp.where(slot128 == 0, ref[g0][:, off:off + 128], ref[g1][:, off:off + 128])
                      for ref, off in ((ar_ref, 0), (ai_ref, 0), (ar_ref, 256), (ai_ref, 256))])

    zero = jnp.zeros((bsz, 128), F32)

    def scan(c, carry):
        rf = pl.ds(pl.multiple_of(c * bsz, bsz), bsz)
        rb = pl.ds(pl.multiple_of((n_chunks - 1 - c) * bsz, bsz), bsz)
        new = []
        for jp in range(n_pairs):
            s_fr, s_fi, s_br, s_bi = carry[jp]
            a_fr, a_fi, a_br, a_bi = trans[jp]
            st_ref[0, jp, rf, :] = s_fr
            st_ref[1, jp, rf, :] = s_fi
            st_ref[2, jp, rb, :] = s_br
            st_ref[3, jp, rb, :] = s_bi
            new.append((a_fr * s_fr - a_fi * s_fi + sl_ref[0, jp, rf, :],
                        a_fr * s_fi + a_fi * s_fr + sl_ref[1, jp, rf, :],
                        a_br * s_br - a_bi * s_bi + sl_ref[2, jp, rb, :],
                        a_br * s_bi + a_bi * s_br + sl_ref[3, jp, rb, :]))
        return tuple(new)

    lax.fori_loop(0, n_chunks, scan, tuple((zero,) * 4 for _ in range(n_pairs)))

    nt = (((1,), (1,)), ((), ()))
    for jp in range(n_pairs):
        st = jnp.concatenate(
            [jnp.concatenate([st_ref[k, jp, pl.ds(b, n_chunks, stride=bsz), :] for b in range(bsz)], axis=0)
             for k in range(4)], axis=1)
        for e in range(2):
            g = 2 * jp + e
            st_g = jnp.where(slot == e, st, 0.0).astype(BF16)
            y = (jnp.dot(u_ref[g], mi_ref[g], preferred_element_type=F32)
                 + lax.dot_general(st_g, mo_ref[g], nt, preferred_element_type=F32))
            yg_ref[g] = jax.nn.gelu(y)

    def relayout_out(r, carry):
        r0 = pl.multiple_of(r * rc, rc)
        for th in range(2):
            src = [yg_ref[gi, pl.ds(r0, rc), th * 128:(th + 1) * 128] for gi in range(S5_SLAB)]
            for t8, out in enumerate(_seg_transpose(src, seg)):
                y_ref[pl.ds(r0, rc), th * 8 + t8, :] = out
        return carry

    lax.fori_loop(0, rows // rc, relayout_out, 0, unroll=2)


def _s5_glu_kernel(g_ref, w_ref, b_ref, z_ref, o_ref):
    g = g_ref[...]
    acc = jnp.dot(g.astype(BF16), w_ref[...], preferred_element_type=F32) + b_ref[...]
    o_ref[...] = (g * _sigmoid(acc) * _silu(z_ref[...])).astype(o_ref.dtype)


def _s5_branch(proj, bsz, seq, tiled, w_glu, b_glu):
    m_state, m_intra, m_out, a_re, a_im = _s5_mats(tiled)
    t_n = S5_CHUNK
    rows = bsz * (seq // t_n)
    m = bsz * seq
    gb = S5_SLAB
    x3 = proj.reshape(rows, t_n, proj.shape[1])
    mat = lambda r, w: pl.BlockSpec((gb, r, w), lambda s: (s, 0, 0))
    io_spec = pl.BlockSpec((rows, t_n, 128), lambda s: (0, 0, s + COL_AX // 128))
    y = pl.pallas_call(
        functools.partial(_s5_main_kernel, bsz=bsz),
        out_shape=jax.ShapeDtypeStruct((rows, t_n, BRANCH_W), F32),
        grid=(S5_GROUPS // gb,),
        in_specs=[io_spec, mat(S5_ROW, S5_LANES), mat(S5_ROW, S5_ROW), mat(S5_ROW, S5_LANES),
                  mat(1, S5_LANES), mat(1, S5_LANES)],
        out_specs=pl.BlockSpec((rows, t_n, 128), lambda s: (0, 0, s)),
        scratch_shapes=[pltpu.VMEM((gb, rows, S5_ROW), BF16),
                        pltpu.VMEM((4, gb // 2, rows, 128), F32),
                        pltpu.VMEM((4, gb // 2, rows, 128), F32),
                        pltpu.VMEM((gb, rows, S5_ROW), F32)],
        compiler_params=_params("parallel"),
        name="s5_main",
    )(x3, m_state, m_intra, m_out, a_re, a_im)
    y = y.reshape(m, BRANCH_W)
    tm = min(512, m)
    return pl.pallas_call(
        _s5_glu_kernel,
        out_shape=jax.ShapeDtypeStruct((m, BRANCH_W), BF16),
        grid=(m // tm,),
        in_specs=[pl.BlockSpec((tm, BRANCH_W), lambda i: (i, 0)),
                  pl.BlockSpec((BRANCH_W, BRANCH_W), lambda i: (0, 0)),
                  pl.BlockSpec((1, BRANCH_W), lambda i: (0, 0)),
                  pl.BlockSpec((tm, BRANCH_W), lambda i: (i, COL_AZ // BRANCH_W))],
        out_specs=pl.BlockSpec((tm, BRANCH_W), lambda i: (i, 0)),
        compiler_params=_params("parallel"),
        name="s5_glu",
    )(y, w_glu.astype(BF16), b_glu.astype(F32).reshape(1, BRANCH_W), proj)


def _dft_tables(seq):
    n = 2 * seq
    mm = np.arange(seq, dtype=np.int64)
    k1 = np.arange(seq // DFT_ROWS, dtype=np.int64)[:, None] * DFT_ROWS
    k0 = np.arange(DFT_ROWS, dtype=np.int64)[:, None]
    ang_a = 2.0 * np.pi * ((k1 * mm) % n).astype(np.float64) / n
    ang_b = 2.0 * np.pi * ((k0 * mm) % n).astype(np.float64) / n
    return tuple(jnp.asarray(t, F32) for t in (np.cos(ang_a), np.sin(ang_a), np.cos(ang_b), np.sin(ang_b)))


def _dft_gen_kernel(ac_ref, as_ref, bc_ref, bs_ref, c_ref, s1_ref, s2_ref):
    i = pl.program_id(0)
    a_c = ac_ref[pl.ds(i, 1), :]
    a_s = as_ref[pl.ds(i, 1), :]
    b_c, b_s = bc_ref[...], bs_ref[...]
    cos_t = a_c * b_c - a_s * b_s
    sin_t = a_s * b_c + a_c * b_s
    rows = lax.broadcasted_iota(jnp.int32, cos_t.shape, 0) + i * DFT_ROWS
    cols = lax.broadcasted_iota(jnp.int32, cos_t.shape, 1)
    alt_cols = jnp.where((cols & 1) == 0, 1.0, -1.0).astype(F32)
    alt_rows = jnp.where((rows & 1) == 0, 1.0, -1.0).astype(F32)
    c_ref[...] = cos_t.astype(BF16)
    s1_ref[...] = jnp.where(rows == 0, alt_cols, sin_t).astype(BF16)
    s2_ref[...] = jnp.where(cols == 0, alt_rows, sin_t).astype(BF16)


def _dft_matrices(seq):
    tabs = _dft_tables(seq)
    n_steps = seq // DFT_ROWS
    tab_spec = pl.BlockSpec(tabs[0].shape, lambda i: (0, 0))
    b_spec = pl.BlockSpec((DFT_ROWS, seq), lambda i: (0, 0))
    o_spec = pl.BlockSpec((DFT_ROWS, seq), lambda i: (i, 0))
    return pl.pallas_call(
        _dft_gen_kernel,
        out_shape=(jax.ShapeDtypeStruct((seq, seq), BF16),) * 3,
        grid=(n_steps,),
        in_specs=[tab_spec, tab_spec, b_spec, b_spec],
        out_specs=(o_spec,) * 3,
        compiler_params=_params("parallel"),
        name="dft_gen",
    )(*tabs)


def _hy_filter_kernel(feat_ref, w1_ref, b1_ref, w2_ref, b2_ref, f0_ref, f1_ref,
                      w3p_ref, w3n_ref, b3p_ref, b3n_ref, dl_ref, t_ref,
                      hs_ref, hd_ref, hp_ref, hn_ref, r0_ref, h_ref, split_ref, *, inv_n):
    hi = lax.Precision.HIGHEST

    @pl.when(pl.program_id(0) == 0)
    def _():
        h1 = jnp.sin(f0_ref[...] * (jnp.dot(feat_ref[...], w1_ref[...], precision=hi,
                                            preferred_element_type=F32) + b1_ref[...]))
        h_ref[...] = jnp.sin(f1_ref[...] * (jnp.dot(h1, w2_ref[...], precision=hi,
                                                    preferred_element_type=F32) + b2_ref[...]))

    h = h_ref[...].astype(BF16)
    win = jnp.exp(-t_ref[...] * jnp.abs(dl_ref[...]))
    hpos = (jnp.dot(h, w3p_ref[...].astype(BF16), preferred_element_type=F32) + b3p_ref[...]) * win
    hneg = (jnp.dot(h, w3n_ref[...].astype(BF16), preferred_element_type=F32) + b3n_ref[...]) * win
    rows = lax.broadcasted_iota(jnp.int32, hpos.shape, 0)
    hneg = jnp.where(rows == 0, 0.0, hneg)
    norm = (jnp.sum(jnp.abs(hpos), axis=0, keepdims=True)
            + jnp.sum(jnp.abs(hneg), axis=0, keepdims=True))
    hpos = hpos / norm
    hneg = hneg / norm
    hsum = hpos + hneg
    hdiff = hpos - hneg
    even = (rows & 1) == 0
    alt2 = jnp.where(((rows >> 1) & 1) == 0, 1.0, -1.0).astype(F32)
    col_sum = lambda a: jnp.sum(a, axis=0, keepdims=True)
    a0 = 2.0 * col_sum(jnp.where(even, hsum, 0.0))
    d0 = 2.0 * col_sum(jnp.where(even, 0.0, hsum))
    hr2 = 2.0 * col_sum(jnp.where(even, alt2 * hsum, 0.0))
    hi2 = -2.0 * col_sum(jnp.where(even, 0.0, alt2 * hdiff))
    r0_ref[...] = jnp.concatenate([a0, d0, hr2, hi2, jnp.zeros((4, a0.shape[1]), F32)], axis=0) * inv_n
    def lags(x, par):
        _stage_rows(split_ref, x)
        return _rows_of_parity(split_ref, par).astype(BF16)

    hs_ref[...] = lags(hsum, 0)
    hd_ref[...] = lags(hdiff, 0)
    hp_ref[...] = lags(hpos, 1)
    hn_ref[...] = lags(hneg, 1)


def _hy_filter_taps(seq, w1, b1, w2, b2, freq, w3, b3):
    n_ch = 2 * BRANCH_W
    bands = (HY_EMB - 1) // 2
    t = jnp.linspace(0.0, 1.0, seq, dtype=F32)[:, None]
    w = 2.0 * math.pi * jnp.arange(seq, dtype=F32)[:, None] / seq
    f = jnp.linspace(1e-4, bands - 1, bands, dtype=F32)[None, :]
    feats = jnp.concatenate([t, jnp.cos(f * w), -jnp.sin(f * w),
                             jnp.zeros((seq, HY_PAD - HY_EMB), F32)], axis=-1)
    deltas = jnp.linspace(math.log(HY_DECAY_TARGET) / HY_SLOW_DECAY,
                          math.log(HY_DECAY_TARGET) / HY_FAST_DECAY, n_ch, dtype=F32)[None, :]

    def pad2(a, r, c):
        a = a.astype(F32)
        return jnp.pad(a, ((0, r - a.shape[0]), (0, c - a.shape[1])))

    w1p = pad2(w1, HY_PAD, HY_PAD)
    w2p = pad2(w2, HY_PAD, HY_PAD)
    b1p = pad2(b1[None], 1, HY_PAD)
    b2p = pad2(b2[None], 1, HY_PAD)
    f0p = pad2(freq[0][None], 1, HY_PAD)
    f1p = pad2(freq[1][None], 1, HY_PAD)
    w3p = pad2(w3, HY_PAD, 2 * n_ch)
    b3r = b3.astype(F32)[None]
    tn = 256
    nt = n_ch // tn
    full = lambda shape: pl.BlockSpec(shape, lambda j: (0, 0))
    tap_spec = pl.BlockSpec((seq // 2, tn), lambda j: (0, j))
    return pl.pallas_call(
        functools.partial(_hy_filter_kernel, inv_n=1.0 / (2 * seq)),
        out_shape=(jax.ShapeDtypeStruct((seq // 2, n_ch), BF16),) * 4
                  + (jax.ShapeDtypeStruct((8, n_ch), F32),),
        grid=(nt,),
        in_specs=[full((seq, HY_PAD)), full((HY_PAD, HY_PAD)), full((1, HY_PAD)),
                  full((HY_PAD, HY_PAD)), full((1, HY_PAD)), full((1, HY_PAD)), full((1, HY_PAD)),
                  pl.BlockSpec((HY_PAD, tn), lambda j: (0, j)),
                  pl.BlockSpec((HY_PAD, tn), lambda j: (0, j + nt)),
                  pl.BlockSpec((1, tn), lambda j: (0, j)),
                  pl.BlockSpec((1, tn), lambda j: (0, j + nt)),
                  pl.BlockSpec((1, tn), lambda j: (0, j)),
                  full((seq, 1))],
        out_specs=(tap_spec,) * 4 + (pl.BlockSpec((8, tn), lambda j: (0, j)),),
        scratch_shapes=[pltpu.VMEM((seq, HY_PAD), F32), pltpu.VMEM((tn // 128, seq, 128), F32)],
        compiler_params=_params("arbitrary"),
        name="hy_filter",
    )(feats, w1p, b1p, w2p, b2p, f0p, f1p, w3p, w3p, b3r, b3r, deltas, t)


def _hy_spectrum_kernel(c_ref, s_ref, hse_ref, hde_ref, hpo_ref, hno_ref,
                        ac_ref, as_ref, bc_ref, bs_ref, gc_ref, gs_ref, *, n_half):
    i = pl.program_id(1)
    dot = lambda w, h: jnp.dot(w[...], h[...], preferred_element_type=F32)
    hec, hes = dot(c_ref, hse_ref), dot(s_ref, hde_ref)
    upc, ups = dot(c_ref, hpo_ref), dot(s_ref, hpo_ref)
    umc, ums = dot(c_ref, hno_ref), dot(s_ref, hno_ref)
    tm = hec.shape[0]
    k = (lax.broadcasted_iota(jnp.int32, (tm, 128), 0) + i * tm).astype(F32)
    psi = k * (math.pi / n_half)
    reps = hec.shape[1] // 128
    cp = jnp.concatenate([jnp.cos(psi)] * reps, axis=1)
    sp = jnp.concatenate([jnp.sin(psi)] * reps, axis=1)
    w = 1.0 / n_half
    ac_ref[...] = w * hec
    as_ref[...] = w * hes
    bc_ref[...] = w * (cp * upc - sp * ups + umc)
    bs_ref[...] = w * (cp * ups + sp * upc - ums)
    gc_ref[...] = w * (upc + cp * umc - sp * ums)
    gs_ref[...] = w * (ups - cp * ums - sp * umc)


def _hy_spectrum(cm, s1, taps):
    half, n_ch = taps[0].shape
    tm = min(512, half)
    tn = 512
    w_spec = pl.BlockSpec((tm, half), lambda j, i: (i, 0))
    tap_spec = pl.BlockSpec((half, tn), lambda j, i: (0, j))
    o_spec = pl.BlockSpec((tm, tn), lambda j, i: (i, j))
    return pl.pallas_call(
        functools.partial(_hy_spectrum_kernel, n_half=half),
        out_shape=(jax.ShapeDtypeStruct((half, n_ch), F32),) * 6,
        grid=(n_ch // tn, half // tm),
        in_specs=[w_spec, w_spec] + [tap_spec] * 4,
        out_specs=(o_spec,) * 6,
        compiler_params=_params("parallel", "arbitrary"),
        name="hy_spectrum",
    )(cm, s1, *taps)


def _stage_rows(scr_ref, x):
    for c in range(x.shape[1] // 128):
        scr_ref[c] = x[:, c * 128:(c + 1) * 128]


def _rows_of_parity(scr_ref, par):
    n = scr_ref.shape[1] // 2
    return jnp.concatenate([scr_ref[c, pl.ds(par, n, stride=2), :] for c in range(scr_ref.shape[0])], axis=1)


def _hy_conv3_kernel(u_ref, w_ref, b_ref, o_ref, *rest):
    vb_ref, s_ref = rest if len(rest) == 2 else (None, rest[0])
    u = u_ref[...]
    n = u.shape[0]
    rows = lax.broadcasted_iota(jnp.int32, u.shape, 0)
    prev = jnp.where(rows == 0, 0.0, pltpu.roll(u, 1, 0))
    nxt = jnp.where(rows == n - 1, 0.0, pltpu.roll(u, n - 1, 0))
    w = w_ref[...]
    _stage_rows(s_ref, prev * w[0:1] + u * w[1:2] + nxt * w[2:3] + b_ref[...])
    for par in range(2):
        part = _rows_of_parity(s_ref, par)
        o_ref[par] = part
        if vb_ref is not None:
            vb_ref[par] = part.astype(BF16)


def _hy_conv3(proj, bsz, seq, conv_w, conv_b, col, width, with_bf16):
    half = seq // 2
    tn = 256
    o_spec = pl.BlockSpec((2, half, tn), lambda b, j: (0, b, j))
    shape = lambda dt: jax.ShapeDtypeStruct((2, bsz * half, width), dt)
    return pl.pallas_call(
        _hy_conv3_kernel,
        out_shape=(shape(F32), shape(BF16)) if with_bf16 else shape(F32),
        grid=(bsz, width // tn),
        in_specs=[pl.BlockSpec((seq, tn), lambda b, j: (b, j + (COL_BU + col) // tn)),
                  pl.BlockSpec((3, tn), lambda b, j: (0, j + col // tn)),
                  pl.BlockSpec((1, tn), lambda b, j: (0, j + col // tn))],
        out_specs=(o_spec, o_spec) if with_bf16 else o_spec,
        scratch_shapes=[pltpu.VMEM((tn // 128, seq, 128), F32)],
        compiler_params=_params("parallel", "parallel"),
        name="hy_conv3",
    )(proj, conv_w.astype(F32), conv_b.astype(F32).reshape(1, -1))


def _hy_fwd_kernel(c_ref, s_ref, ze_ref, zo_ref, ac_ref, as_ref, bc_ref, bs_ref, gc_ref, gs_ref, r0_ref,
                   pc_ref, ps_ref, qc_ref, qs_ref):
    dot = lambda w, z: jnp.dot(w[...], z[...], preferred_element_type=F32)
    ec, es = dot(c_ref, ze_ref), dot(s_ref, ze_ref)
    oc, os_ = dot(c_ref, zo_ref), dot(s_ref, zo_ref)
    a_c, a_s = ac_ref[...], as_ref[...]
    b_c, b_s = bc_ref[...], bs_ref[...]
    g_c, g_s = gc_ref[...], gs_ref[...]
    outs = (ec * a_c - es * a_s + oc * b_c - os_ * b_s,
            ec * a_s + es * a_c + oc * b_s + os_ * b_c,
            ec * g_c - es * g_s + oc * a_c - os_ * a_s,
            ec * g_s + es * g_c + oc * a_s + os_ * a_c)
    r0 = r0_ref[...]
    a0, d0, hr2, hi2 = r0[0:1], r0[1:2], r0[2:3], r0[3:4]
    e0, eh, o0, oh = ec[0:1], es[0:1], oc[0:1], os_[0:1]
    first = (e0 * a0 + o0 * d0, eh * hr2 + oh * hi2, e0 * d0 + o0 * a0, oh * hr2 - eh * hi2)
    top = 16
    is_row0 = (lax.broadcasted_iota(jnp.int32, (top, ec.shape[1]), 0) == 0) & (pl.program_id(2) == 0)
    for ref, val, row0 in zip((pc_ref, ps_ref, qc_ref, qs_ref), outs, first):
        ref[...] = val.astype(BF16)
        ref[0:top, :] = jnp.where(is_row0, row0, val[0:top]).astype(BF16)


HY_TN = 512


def _hy_fwd(cm, s1, z_even, z_odd, tables, r0, h_col, bsz, half):
    tm = min(512, half)
    tn = HY_TN
    mt = half // tm
    w_spec = pl.BlockSpec((tm, half), lambda b, j, i: (i, 0))
    h_spec = pl.BlockSpec((tm, tn), lambda b, j, i: (i, j + h_col // tn))
    o_spec = pl.BlockSpec((tm, tn), lambda b, j, i: (b * mt + i, j))
    return pl.pallas_call(
        _hy_fwd_kernel,
        out_shape=(jax.ShapeDtypeStruct((bsz * half, BRANCH_W), BF16),) * 4,
        grid=(bsz, BRANCH_W // tn, mt),
        in_specs=[w_spec, w_spec, z_even[1], z_odd[1]] + [h_spec] * 6
                 + [pl.BlockSpec((8, tn), lambda b, j, i: (0, j + h_col // tn))],
        out_specs=(o_spec,) * 4,
        compiler_params=_params("parallel", "parallel", "arbitrary"),
        name="hy_fwd",
    )(cm, s1, z_even[0], z_odd[0], *tables, r0)


def _hy_inv_convs(c_ref, s_ref, pc_ref, ps_ref, qc_ref, qs_ref):
    dot = lambda w, y: jnp.dot(w[...], y[...], preferred_element_type=F32)
    return dot(c_ref, pc_ref) + dot(s_ref, ps_ref), dot(c_ref, qc_ref) + dot(s_ref, qs_ref)


def _hy_inv_mid_kernel(c_ref, s_ref, pc_ref, ps_ref, qc_ref, qs_ref, ge_ref, go_ref, ze_ref, zo_ref,
                       bias_ref, o_ref, ob_ref):
    convs = _hy_inv_convs(c_ref, s_ref, pc_ref, ps_ref, qc_ref, qs_ref)
    for par, (conv, g_ref, z_ref) in enumerate(zip(convs, (ge_ref, go_ref), (ze_ref, zo_ref))):
        out = g_ref[...] * (conv + bias_ref[...] * z_ref[...])
        o_ref[par] = out
        ob_ref[par] = out.astype(BF16)


def _hy_inv_last_kernel(c_ref, s_ref, pc_ref, ps_ref, qc_ref, qs_ref, ge_ref, go_ref, ze_ref, zo_ref,
                        bias_ref, sz_ref, o_ref, mix_ref):
    convs = _hy_inv_convs(c_ref, s_ref, pc_ref, ps_ref, qc_ref, qs_ref)
    tm = ge_ref.shape[0]
    _stage_rows(mix_ref, _silu(sz_ref[...]))
    outs = [g_ref[...] * (conv + bias_ref[...] * z_ref[...]) * _rows_of_parity(mix_ref, par)
            for par, (conv, g_ref, z_ref) in enumerate(zip(convs, (ge_ref, go_ref), (ze_ref, zo_ref)))]
    for par, out in enumerate(outs):
        for c in range(mix_ref.shape[0]):
            mix_ref[c, pl.ds(par, tm, stride=2), :] = out[:, c * 128:(c + 1) * 128]
    for c in range(mix_ref.shape[0]):
        o_ref[:, c * 128:(c + 1) * 128] = mix_ref[c]


def _hy_inv(cm, s2, spectra, gates, zprev, bias_row, bsz, half, silu=None):
    tm = min(512, half)
    tn = HY_TN
    mt = half // tm
    w_spec = pl.BlockSpec((tm, half), lambda b, j, i: (i, 0))
    y_spec = pl.BlockSpec((half, tn), lambda b, j, i: (b, j))
    pairs = list(gates) + list(zprev)
    in_specs = [w_spec, w_spec] + [y_spec] * 4 + [s for _, s in pairs] + [pl.BlockSpec((1, tn), lambda b, j, i: (0, j))]
    args = [cm, s2, *spectra] + [a for a, _ in pairs] + [bias_row]
    if silu is None:
        o_spec = pl.BlockSpec((2, tm, tn), lambda b, j, i: (0, b * mt + i, j))
        shape = lambda dt: jax.ShapeDtypeStruct((2, bsz * half, BRANCH_W), dt)
        body, out_shape, out_specs = _hy_inv_mid_kernel, (shape(F32), shape(BF16)), (o_spec, o_spec)
        scratch = []
    else:
        silu_arr, silu_col = silu
        body = _hy_inv_last_kernel
        out_shape = jax.ShapeDtypeStruct((2 * bsz * half, BRANCH_W), F32)
        out_specs = pl.BlockSpec((2 * tm, tn), lambda b, j, i: (b * mt + i, j))
        in_specs.append(pl.BlockSpec((2 * tm, tn), lambda b, j, i: (b * mt + i, j + silu_col // tn)))
        args.append(silu_arr)
        scratch = [pltpu.VMEM((tn // 128, 2 * tm, 128), F32)]
    return pl.pallas_call(
        body, out_shape=out_shape, grid=(bsz, BRANCH_W // tn, mt), in_specs=in_specs, out_specs=out_specs,
        scratch_shapes=scratch, compiler_params=_params("parallel", "parallel", "arbitrary"),
        name="hy_inv",
    )(*args)


def _hyena_branch(proj, bsz, seq, dft, conv_w, conv_b, w1, b1, w2, b2, freq, w3, b3, bias):
    cm, s1, s2 = dft
    half = seq // 2
    tm = min(512, half)
    tn = HY_TN
    mt = half // tm
    full = lambda par, col=0: pl.BlockSpec((None, half, tn), lambda b, j, i: (par, b, j + col // tn))
    row = lambda par, col=0: pl.BlockSpec((None, tm, tn), lambda b, j, i: (par, b * mt + i, j + col // tn))
    both = lambda arr, spec, col=0: ((arr, spec(0, col)), (arr, spec(1, col)))
    *taps, r0 = _hy_filter_taps(seq, w1, b1, w2, b2, freq, w3, b3)
    tables = _hy_spectrum(cm, s1, taps)
    v, vb = _hy_conv3(proj, bsz, seq, conv_w, conv_b, 0, BRANCH_W, True)
    x12 = _hy_conv3(proj, bsz, seq, conv_w, conv_b, BRANCH_W, 2 * BRANCH_W, False)
    bias = bias.astype(F32)
    spectra = _hy_fwd(cm, s1, *both(vb, full), tables, r0, 0, bsz, half)
    z1, z1b = _hy_inv(cm, s2, spectra, gates=both(x12, row), zprev=both(v, row),
                      bias_row=bias[0:1], bsz=bsz, half=half)
    spectra = _hy_fwd(cm, s1, *both(z1b, full), tables, r0, BRANCH_W, bsz, half)
    return _hy_inv(cm, s2, spectra, gates=both(x12, row, BRANCH_W), zprev=both(z1, row),
                   bias_row=bias[1:2], bsz=bsz, half=half, silu=(proj, COL_BZ))


def _rope_table_kernel(pos_ref, inv_ref, cos_ref, sin_ref):
    ang = pos_ref[...] * inv_ref[...]
    lane = lax.broadcasted_iota(jnp.int32, ang.shape, 1)
    live = lane < MLA_ROPE
    cos_ref[...] = jnp.where(live, jnp.cos(ang), 0.0)
    sin_ref[...] = jnp.where(live, jnp.where(lane < MLA_ROPE // 2, -1.0, 1.0) * jnp.sin(ang), 0.0)


def _rope_tables(positions):
    m = positions.size
    half = MLA_ROPE // 2
    inv = ROPE_BASE ** (-jnp.arange(half, dtype=F32) / half)
    inv = jnp.concatenate([inv, inv, jnp.zeros((128 - MLA_ROPE,), F32)])[None]
    pos = positions.astype(F32).reshape(m, 1)
    tm = min(1024, m)
    spec = pl.BlockSpec((tm, 128), lambda i: (i, 0))
    return pl.pallas_call(
        _rope_table_kernel,
        out_shape=(jax.ShapeDtypeStruct((m, 128), F32),) * 2,
        grid=(m // tm,),
        in_specs=[pl.BlockSpec((tm, 1), lambda i: (i, 0)), pl.BlockSpec((1, 128), lambda i: (0, 0))],
        out_specs=(spec, spec),
        compiler_params=_params("parallel"),
        name="rope_table",
    )(pos, inv)


def _rope128(x, cos_t, sin_t):
    lane = lax.broadcasted_iota(jnp.int32, x.shape, 1)
    half = MLA_ROPE // 2
    partner = jnp.where(lane < half, pltpu.roll(x, 128 - half, 1), pltpu.roll(x, half, 1))
    return x * cos_t + partner * sin_t


def _rms(x, g):
    ms = jnp.mean(jnp.square(x), axis=-1, keepdims=True)
    return x * lax.rsqrt(ms + RMS_EPS) * g


def _mla_q_kernel(cq_ref, g_ref, w_ref, cos_ref, sin_ref, q_ref, *, scale):
    xn = _rms(cq_ref[...], g_ref[...]).astype(BF16)
    q = jnp.dot(xn, w_ref[...], preferred_element_type=F32) * scale
    cos_t, sin_t = cos_ref[...], sin_ref[...]
    for h in range(MLA_HEADS):
        base = h * MLA_QK_PAD
        q_ref[:, base:base + MLA_NOPE] = q[:, base:base + MLA_NOPE].astype(BF16)
        q_ref[:, base + MLA_NOPE:base + MLA_QK_PAD] = _rope128(
            q[:, base + MLA_NOPE:base + MLA_QK_PAD], cos_t, sin_t).astype(BF16)


def _mla_kv_kernel(ckv_ref, g_ref, wk_ref, wv_ref, kr_ref, cos_ref, sin_ref, k_ref, v_ref):
    xn = _rms(ckv_ref[...], g_ref[...]).astype(BF16)
    kn = jnp.dot(xn, wk_ref[...], preferred_element_type=F32)
    v_ref[...] = jnp.dot(xn, wv_ref[...], preferred_element_type=F32).astype(BF16)
    kr = _rope128(kr_ref[:, 0:128], cos_ref[...], sin_ref[...]).astype(BF16)
    for h in range(MLA_HEADS):
        base = h * MLA_QK_PAD
        k_ref[:, base:base + MLA_NOPE] = kn[:, h * MLA_NOPE:(h + 1) * MLA_NOPE].astype(BF16)
        k_ref[:, base + MLA_NOPE:base + MLA_QK_PAD] = kr


def _mla_attn_kernel(q_ref, k_ref, v_ref, z_ref, o_ref):
    k, v = k_ref[...], v_ref[...]
    half = q_ref.shape[0] // 2
    for r in range(2):
        rows = pl.ds(r * half, half)
        s = lax.dot_general(q_ref[rows, :], k, (((1,), (1,)), ((), ())),
                            preferred_element_type=F32)
        p = jnp.exp2(s - jnp.max(s, axis=-1, keepdims=True))
        l = jnp.sum(p, axis=-1, keepdims=True)
        o = jnp.dot(p.astype(BF16), v, preferred_element_type=F32)
        o_ref[rows, :] = (o / l * _silu(z_ref[rows, :])).astype(BF16)


def _mla_branch(proj, bsz, seq, rope, q_norm_g, w_uq, kv_norm_g, w_ukv):
    m = bsz * seq
    cos_t, sin_t = rope
    dqk = MLA_NOPE + MLA_ROPE
    hq = MLA_HEADS * MLA_QK_PAD
    w_q = w_uq.reshape(MLA_LORA, MLA_HEADS, dqk)
    w_q = jnp.pad(w_q, ((0, 0), (0, 0), (0, MLA_QK_PAD - dqk))).reshape(MLA_LORA, hq).astype(BF16)
    w_kv = w_ukv.reshape(MLA_LORA, MLA_HEADS, MLA_NOPE + MLA_V)
    w_k = w_kv[:, :, :MLA_NOPE].reshape(MLA_LORA, MLA_HEADS * MLA_NOPE).astype(BF16)
    w_v = w_kv[:, :, MLA_NOPE:].reshape(MLA_LORA, MLA_HEADS * MLA_V).astype(BF16)
    tm = min(512, m)
    row = lambda shape, col=0: pl.BlockSpec(shape, lambda i: (i, col))
    full = lambda shape: pl.BlockSpec(shape, lambda i: (0, 0))
    qp = pl.pallas_call(
        functools.partial(_mla_q_kernel, scale=dqk ** -0.5 * math.log2(math.e)),
        out_shape=jax.ShapeDtypeStruct((m, hq), BF16),
        grid=(m // tm,),
        in_specs=[row((tm, MLA_LORA), COL_CQ // MLA_LORA), full((1, MLA_LORA)), full((MLA_LORA, hq)),
                  row((tm, 128)), row((tm, 128))],
        out_specs=row((tm, hq)),
        compiler_params=_params("parallel"),
        name="mla_q",
    )(proj, q_norm_g.astype(F32).reshape(1, -1), w_q, cos_t, sin_t)
    kp, vp = pl.pallas_call(
        _mla_kv_kernel,
        out_shape=(jax.ShapeDtypeStruct((m, hq), BF16),
                   jax.ShapeDtypeStruct((m, MLA_HEADS * MLA_V), BF16)),
        grid=(m // tm,),
        in_specs=[row((tm, MLA_LORA), COL_CKV // MLA_LORA), full((1, MLA_LORA)),
                  full((MLA_LORA, MLA_HEADS * MLA_NOPE)), full((MLA_LORA, MLA_HEADS * MLA_V)),
                  row((tm, 512), COL_CKR // 512), row((tm, 128)), row((tm, 128))],
        out_specs=(row((tm, hq)), row((tm, MLA_HEADS * MLA_V))),
        compiler_params=_params("parallel"),
        name="mla_kv",
    )(proj, kv_norm_g.astype(F32).reshape(1, -1), w_k, w_v, proj, cos_t, sin_t)
    tq = min(512, seq)
    qt = seq // tq
    return pl.pallas_call(
        _mla_attn_kernel,
        out_shape=jax.ShapeDtypeStruct((m, MLA_HEADS * MLA_V), BF16),
        grid=(bsz, MLA_HEADS, qt),
        in_specs=[pl.BlockSpec((tq, MLA_QK_PAD), lambda b, h, i: (b * qt + i, h)),
                  pl.BlockSpec((seq, MLA_QK_PAD), lambda b, h, i: (b, h)),
                  pl.BlockSpec((seq, MLA_V), lambda b, h, i: (b, h)),
                  pl.BlockSpec((tq, MLA_V), lambda b, h, i: (b * qt + i, h + COL_CZ // MLA_V))],
        out_specs=pl.BlockSpec((tq, MLA_V), lambda b, h, i: (b * qt + i, h)),
        compiler_params=_params("parallel", "parallel", "arbitrary"),
        name="mla_attn",
    )(qp, kp, vp, proj)


def _lift_kernel(ya_ref, yb_ref, yc_ref, w_ref, ga_ref, gb_ref, gc_ref, o_ref, wb_ref):
    @pl.when(pl.program_id(1) == 0)
    def _():
        wb_ref[...] = w_ref[...].astype(BF16)

    acc = _sigmoid(ga_ref[...]) * jnp.dot(ya_ref[...], wb_ref[0], preferred_element_type=F32)
    acc += _sigmoid(gb_ref[...]) * jnp.dot(yb_ref[...].astype(BF16), wb_ref[1], preferred_element_type=F32)
    acc += _sigmoid(gc_ref[...]) * jnp.dot(yc_ref[...], wb_ref[2], preferred_element_type=F32)
    o_ref[...] = acc.astype(BF16)


def _lift(ya, yb, yc, w_lift, layer, proj):
    m = ya.shape[0]
    tm = min(512, m)
    tn = 1024
    y_spec = pl.BlockSpec((tm, BRANCH_W), lambda j, i: (i, 0))
    gate = lambda n: pl.BlockSpec((tm, tn), lambda j, i: (i, j + (COL_GATE + n * D_MODEL) // tn))
    return pl.pallas_call(
        _lift_kernel,
        out_shape=jax.ShapeDtypeStruct((m, D_MODEL), BF16),
        grid=(D_MODEL // tn, m // tm),
        in_specs=[y_spec, y_spec, y_spec,
                  pl.BlockSpec((None, N_BRANCH, BRANCH_W, tn), lambda j, i: (layer, 0, 0, j),
                               pipeline_mode=pl.Buffered(1)),
                  gate(0), gate(1), gate(2)],
        out_specs=pl.BlockSpec((tm, tn), lambda j, i: (i, j)),
        scratch_shapes=[pltpu.VMEM((N_BRANCH, BRANCH_W, tn), BF16)],
        compiler_params=_params("parallel", "arbitrary"),
        name="lift",
    )(ya, yb, yc, w_lift, proj, proj, proj)


def _out_kernel(mix_ref, wo_ref, p_ref, wp_ref, sp_ref, x_ref, g_ref, b_ref, o_ref):
    mixed = jnp.dot(mix_ref[...], wo_ref[...], preferred_element_type=F32)
    ple = jnp.dot(p_ref[...].astype(BF16), wp_ref[...], preferred_element_type=F32) * _sigmoid(sp_ref[...])
    r = DEEPNORM_ALPHA * x_ref[...] + mixed + ple
    mu = jnp.mean(r, axis=-1, keepdims=True)
    var = jnp.mean(jnp.square(r - mu), axis=-1, keepdims=True)
    o_ref[...] = (r - mu) * lax.rsqrt(var + LN_EPS) * g_ref[...] + b_ref[...]


def _out_norm(mix, w_out, p, w_ple, proj, x, ln_g, ln_b):
    m = mix.shape[0]
    tm = min(512, m)
    row = lambda w, col=0: pl.BlockSpec((tm, w), lambda i: (i, col))
    full = lambda shape: pl.BlockSpec(shape, lambda i: (0, 0))
    return pl.pallas_call(
        _out_kernel,
        out_shape=jax.ShapeDtypeStruct((m, D_MODEL), F32),
        grid=(m // tm,),
        in_specs=[row(D_MODEL), full((D_MODEL, D_MODEL)), row(PLE_DIM), full((PLE_DIM, D_MODEL)),
                  row(D_MODEL, COL_PLE // D_MODEL), row(D_MODEL), full((1, D_MODEL)), full((1, D_MODEL))],
        out_specs=row(D_MODEL),
        compiler_params=_params("parallel"),
        name="out_norm",
    )(mix, w_out.astype(BF16), p, w_ple.astype(BF16), proj, x,
      ln_g.astype(F32).reshape(1, -1), ln_b.astype(F32).reshape(1, -1))


def kernel(x, p, positions, w_in, s5_lambda_re, s5_lambda_im, s5_log_dt, s5_b_re, s5_b_im, s5_c_re, s5_c_im, s5_d, s5_w_glu, s5_b_glu, hy_conv_w, hy_conv_b, hy_w1, hy_b1, hy_w2, hy_b2, hy_freq, hy_w3, hy_b3, hy_bias, mla_q_norm, mla_w_uq, mla_kv_norm, mla_w_ukv, w_lift, w_out, w_ple, ln_g, ln_b):
    bsz, seq, _ = x.shape
    m = bsz * seq
    depth = w_in.shape[0]
    dft = _dft_matrices(seq // 2)
    rope = _rope_tables(positions)
    s5_tiled = _s5_tiled_params(s5_lambda_re, s5_lambda_im, s5_log_dt, s5_b_re, s5_b_im,
                                s5_c_re, s5_c_im, s5_d)
    xf = x.reshape(m, D_MODEL).astype(F32)
    w_t = jnp.swapaxes(w_in, 1, 2)
    for i in range(depth):
        proj = _proj_all(xf, w_t, i)
        y_a = _s5_branch(proj, bsz, seq, [a[i] for a in s5_tiled], s5_w_glu[i], s5_b_glu[i])
        y_b = _hyena_branch(proj, bsz, seq, dft, hy_conv_w[i], hy_conv_b[i], hy_w1[i], hy_b1[i],
                            hy_w2[i], hy_b2[i], hy_freq[i], hy_w3[i], hy_b3[i], hy_bias[i])
        y_c = _mla_branch(proj, bsz, seq, rope, mla_q_norm[i], mla_w_uq[i], mla_kv_norm[i], mla_w_ukv[i])
        mix = _lift(y_a, y_b, y_c, w_lift, i, proj)
        xf = _out_norm(mix, w_out[i], p[i].reshape(m, PLE_DIM), w_ple[i], proj, xf, ln_g[i], ln_b[i])
    return xf.reshape(bsz, seq, D_MODEL).astype(x.dtype)
```

```python
import functools
import math

import numpy as np
import jax
import jax.numpy as jnp
from jax import lax
from jax.experimental import pallas as pl
from jax.experimental.pallas import tpu as pltpu

F32 = jnp.float32
BF16 = jnp.bfloat16

D_MODEL = 2048
PLE_DIM = 256
N_BRANCH = 3
BRANCH_W = 1024

S5_GROUP = 16
S5_GROUPS = BRANCH_W // S5_GROUP
S5_STATE = 64
S5_CHUNK = 16
S5_ROW = S5_CHUNK * S5_GROUP

HY_EMB = 33
HY_FF = 64
HY_PAD = 128
HY_DECAY_TARGET = 0.01
HY_FAST_DECAY = 0.3
HY_SLOW_DECAY = 1.5
DFT_ROWS = 64

MLA_HEADS = 8
MLA_NOPE = 128
MLA_ROPE = 64
MLA_V = 128
MLA_LORA = 512
MLA_QK_PAD = 256
ROPE_BASE = 10000.0

LN_EPS = 1e-5
RMS_EPS = 1e-6
DEPTH = 2
DEEPNORM_ALPHA = (2 * DEPTH) ** 0.25

COL_AX = 0
COL_AZ = 1024
COL_BU = 2048
COL_BZ = 5120
COL_CQ = 6144
COL_CKV = 6656
PROJ_KEEP = 7168
COL_CZ = 7168
COL_PLE = 8192
COL_GATE = 10240
COL_CKR = 16384
PROJ_N = 16896
PROJ_TN = 512
W_IN_CKR = 7168
W_IN_CZ = 7232
W_IN_GATE = 8256
W_IN_PLE = 14400

VMEM_LIMIT = 56 * 1024 * 1024


def _params(*sem):
    return pltpu.CompilerParams(dimension_semantics=sem, vmem_limit_bytes=VMEM_LIMIT)


def _sigmoid(x):
    return 0.5 * jnp.tanh(0.5 * x) + 0.5


def _silu(x):
    return x * _sigmoid(x)


def _proj_kernel(x_ref, w_ref, o_ref, xb_ref):
    @pl.when(pl.program_id(1) == 0)
    def _():
        xb_ref[...] = x_ref[...].astype(BF16)

    o_ref[...] = lax.dot_general(xb_ref[...], w_ref[...].astype(BF16), (((1,), (1,)), ((), ())),
                                 preferred_element_type=F32)


def _proj(x, w, layer, n_out, name):
    m, k = x.shape
    tm = min(1024, m)
    return pl.pallas_call(
        _proj_kernel,
        out_shape=jax.ShapeDtypeStruct((m, n_out), F32),
        grid=(m // tm, n_out // PROJ_TN),
        in_specs=[pl.BlockSpec((tm, k), lambda i, j: (i, 0)),
                  pl.BlockSpec((None, PROJ_TN, k), lambda i, j: (layer, j, 0))],
        out_specs=pl.BlockSpec((tm, PROJ_TN), lambda i, j: (i, j)),
        scratch_shapes=[pltpu.VMEM((tm, k), BF16)],
        compiler_params=_params("parallel", "arbitrary"),
        name=name,
    )(x, w)


def _repack_src_tile(j):
    t = lambda col: col // PROJ_TN
    return jnp.where(j < t(COL_CZ), j,
                     jnp.where(j < t(COL_PLE), j - t(COL_CZ) + t(W_IN_CZ),
                               jnp.where(j < t(COL_GATE), j - t(COL_PLE) + t(W_IN_PLE),
                                         jnp.where(j < t(COL_CKR), j - t(COL_GATE) + t(W_IN_GATE),
                                                   t(W_IN_CKR)))))


def _repack_kernel(a_ref, b_ref, o_ref):
    j = pl.program_id(0)
    a = a_ref[...]
    shifted = jnp.concatenate([a[MLA_ROPE:], b_ref[...]], axis=0)
    key = jnp.concatenate([a[:MLA_ROPE], jnp.zeros_like(a[MLA_ROPE:])], axis=0)
    out = jnp.where(j < PROJ_KEEP // PROJ_TN, a, jnp.where(j == COL_CKR // PROJ_TN, key, shifted))
    o_ref[...] = out.astype(BF16)


def _repack(w_t, layer):
    k = w_t.shape[2]
    sub = PROJ_TN // MLA_ROPE
    return pl.pallas_call(
        _repack_kernel,
        out_shape=jax.ShapeDtypeStruct((1, PROJ_N, k), BF16),
        grid=(PROJ_N // PROJ_TN,),
        in_specs=[pl.BlockSpec((None, PROJ_TN, k), lambda j: (layer, _repack_src_tile(j), 0)),
                  pl.BlockSpec((None, MLA_ROPE, k), lambda j: (layer, (_repack_src_tile(j) + 1) * sub, 0))],
        out_specs=pl.BlockSpec((None, PROJ_TN, k), lambda j: (0, j, 0)),
        compiler_params=_params("parallel"),
        name="repack",
    )(w_t, w_t)


def _proj_all(x, w_t, layer):
    return _proj(x, _repack(w_t, layer), 0, PROJ_N, "proj")


S5_LANES = 8 * S5_STATE
S5_SLAB = 8
S5_RELAYOUT_ROWS = 32


def _s5_tile_lanes(a):
    return jnp.concatenate([a[:, 0]] * 4 + [a[:, 1]] * 4, axis=-1)


def _s5_tiled_params(lam_re, lam_im, log_dt, b_re, b_im, c_re, c_im, d):
    f = lambda a: a.astype(F32)
    depth = lam_re.shape[0]
    ldt = jnp.broadcast_to(f(log_dt)[..., None, None], lam_re.shape[:3] + (1, S5_STATE))
    return (_s5_tile_lanes(f(lam_re)[:, :, :, None, :]), _s5_tile_lanes(f(lam_im)[:, :, :, None, :]),
            _s5_tile_lanes(ldt),
            _s5_tile_lanes(jnp.swapaxes(f(b_re), -1, -2)), _s5_tile_lanes(jnp.swapaxes(f(b_im), -1, -2)),
            _s5_tile_lanes(f(c_re)), _s5_tile_lanes(f(c_im)),
            f(d).reshape(depth, S5_GROUPS, S5_GROUP, 1))


def _s5_mats_kernel(lr_ref, li_ref, ldt_ref, br_ref, bi_ref, cr_ref, ci_ref, d_ref,
                    ms_ref, mi_ref, mo_ref, ar_ref, ai_ref):
    t_n, h_n = S5_CHUNK, S5_GROUP
    hi = lax.Precision.HIGHEST
    nt = (((1,), (1,)), ((), ()))
    blk = lax.broadcasted_iota(jnp.int32, (1, S5_LANES), 1) // S5_STATE
    is_im = (blk // 2) % 2 == 1
    is_fwd = blk < 4
    lane_k = lax.broadcasted_iota(jnp.int32, (h_n, S5_ROW), 1)
    sub_k = lax.broadcasted_iota(jnp.int32, (h_n, S5_ROW), 0)

    def per_group(gi, carry):
        lr, li = lr_ref[gi], li_ref[gi]
        dt = jnp.exp(ldt_ref[gi])
        zr, zi = lr * dt, li * dt
        mag = jnp.exp(zr)
        lbr, lbi = mag * jnp.cos(zi), mag * jnp.sin(zi)
        tr, ti = [jnp.ones_like(lbr), lbr], [jnp.zeros_like(lbi), lbi]
        for _ in range(t_n - 1):
            tr, ti = tr + [tr[-1] * lbr - ti[-1] * lbi], ti + [tr[-1] * lbi + ti[-1] * lbr]
        n2 = lr * lr + li * li
        qr = ((lbr - 1.0) * lr + lbi * li) / n2
        qi = (lbi * lr - (lbr - 1.0) * li) / n2
        br, bi = br_ref[gi], bi_ref[gi]
        bbr, bbi = qr * br - qi * bi, qr * bi + qi * br
        y1, y2 = jnp.where(is_im, bbi, bbr), jnp.where(is_im, bbr, bbi)
        cr, ci = cr_ref[gi], ci_ref[gi]

        def pick(tab, t_fwd, t_bwd):
            return jnp.where(is_fwd, tab[t_fwd], tab[t_bwd])

        def c_times(p_r, p_i):
            return jnp.where(is_im, -(cr * p_i + ci * p_r), cr * p_r - ci * p_i)

        ct = [c_times(tr[s], ti[s]) for s in range(t_n + 1)]
        q_rows = []
        for t in range(t_n):
            a_r, a_i = pick(tr, t_n - 1 - t, t), pick(ti, t_n - 1 - t, t)
            rows = pl.ds(t * h_n, h_n)
            ms_ref[gi, rows, :] = (a_r * y1 + jnp.where(is_im, a_i, -a_i) * y2).astype(BF16)
            mo_ref[gi, rows, :] = jnp.where(is_fwd, ct[t + 1], ct[t_n - t]).astype(BF16)
            q_rows.append(jnp.where(is_fwd, ct[t], ct[t_n - 1 - t]))
        q = jnp.concatenate(q_rows, axis=0)
        half = S5_LANES // 2
        kf = 0.5 * lax.dot_general(y1[:, :half], q[:, :half], nt, precision=hi, preferred_element_type=F32)
        kb = 0.5 * lax.dot_general(y1[:, half:], q[:, half:], nt, precision=hi, preferred_element_type=F32)
        kf = kf + jnp.where(lane_k == sub_k, d_ref[gi], 0.0)
        for t in range(t_n):
            fwd = kf if t == 0 else jnp.where(lane_k >= h_n * t, pltpu.roll(kf, h_n * t, 1), 0.0)
            sh = h_n * (t_n - 1 - t)
            bwd = kb if sh == 0 else jnp.where(lane_k < S5_ROW - sh, pltpu.roll(kb, S5_ROW - sh, 1), 0.0)
            mi_ref[gi, pl.ds(t * h_n, h_n), :] = (fwd + bwd).astype(BF16)
        ar_ref[gi] = tr[t_n]
        ai_ref[gi] = ti[t_n]
        return carry

    lax.fori_loop(0, lr_ref.shape[0], per_group, 0)


def _s5_mats(tiled):
    g_n = S5_GROUPS
    gb = S5_SLAB
    spec = lambda r, w: pl.BlockSpec((gb, r, w), lambda j: (j, 0, 0))
    in_rows = (1, 1, 1, S5_GROUP, S5_GROUP, S5_GROUP, S5_GROUP)
    return pl.pallas_call(
        _s5_mats_kernel,
        out_shape=(jax.ShapeDtypeStruct((g_n, S5_ROW, S5_LANES), BF16),
                   jax.ShapeDtypeStruct((g_n, S5_ROW, S5_ROW), BF16),
                   jax.ShapeDtypeStruct((g_n, S5_ROW, S5_LANES), BF16),
                   jax.ShapeDtypeStruct((g_n, 1, S5_LANES), F32),
                   jax.ShapeDtypeStruct((g_n, 1, S5_LANES), F32)),
        grid=(g_n // gb,),
        in_specs=[spec(r, S5_LANES) for r in in_rows] + [spec(S5_GROUP, 1)],
        out_specs=(spec(S5_ROW, S5_LANES), spec(S5_ROW, S5_ROW), spec(S5_ROW, S5_LANES),
                   spec(1, S5_LANES), spec(1, S5_LANES)),
        compiler_params=_params("parallel"),
        name="s5_mats",
    )(*tiled)


def _seg_transpose(vs, seg):
    vs = list(vs)
    for s in (4, 2, 1):
        keep = (seg & s) == 0
        for i in range(8):
            if i & s:
                continue
            a, b = vs[i], vs[i + s]
            vs[i] = jnp.where(keep, a, pltpu.roll(b, s * S5_GROUP, 1))
            vs[i + s] = jnp.where(keep, pltpu.roll(a, 128 - s * S5_GROUP, 1), b)
    return vs


def _s5_main_kernel(x_ref, ms_ref, mi_ref, mo_ref, ar_ref, ai_ref, y_ref,
                    u_ref, sl_ref, st_ref, yg_ref, *, bsz):
    rows = x_ref.shape[0]
    n_chunks = rows // bsz
    rc = S5_RELAYOUT_ROWS
    seg = lax.broadcasted_iota(jnp.int32, (rc, 128), 1) // S5_GROUP
    slot = (lax.broadcasted_iota(jnp.int32, (1, S5_LANES), 1) // S5_STATE) % 2
    slot128 = slot[:, :128]

    def relayout_in(r, carry):
        r0 = pl.multiple_of(r * rc, rc)
        for th in range(2):
            src = [x_ref[pl.ds(r0, rc), th * 8 + t8, :] for t8 in range(8)]
            for gi, out in enumerate(_seg_transpose(src, seg)):
                u_ref[gi, pl.ds(r0, rc), th * 128:(th + 1) * 128] = out.astype(BF16)
        return carry

    lax.fori_loop(0, rows // rc, relayout_in, 0, unroll=2)

    n_pairs = S5_SLAB // 2
    trans = []
    for jp in range(n_pairs):
        g0, g1 = 2 * jp, 2 * jp + 1
        r0 = jnp.dot(u_ref[g0], ms_ref[g0], preferred_element_type=F32)
        r1 = jnp.dot(u_ref[g1], ms_ref[g1], preferred_element_type=F32)
        loc = jnp.where(slot == 0, r0, r1)
        for k in range(4):
            for b in range(bsz):
                sl_ref[k, jp, pl.ds(b, n_chunks, stride=bsz), :] = (
                    loc[b * n_chunks:(b + 1) * n_chunks, k * 128:(k + 1) * 128])
        trans.append([jnp.where(slot128 == 0, ref[g0][:, off:off + 128], ref[g1][:, off:off + 128])
                      for ref, off in ((ar_ref, 0), (ai_ref, 0), (ar_ref, 256), (ai_ref, 256))])

    zero = jnp.zeros((bsz, 128), F32)

    def scan(c, carry):
        rf = pl.ds(pl.multiple_of(c * bsz, bsz), bsz)
        rb = pl.ds(pl.multiple_of((n_chunks - 1 - c) * bsz, bsz), bsz)
        new = []
        for jp in range(n_pairs):
            s_fr, s_fi, s_br, s_bi = carry[jp]
            a_fr, a_fi, a_br, a_bi = trans[jp]
            st_ref[0, jp, rf, :] = s_fr
            st_ref[1, jp, rf, :] = s_fi
            st_ref[2, jp, rb, :] = s_br
            st_ref[3, jp, rb, :] = s_bi
            new.append((a_fr * s_fr - a_fi * s_fi + sl_ref[0, jp, rf, :],
                        a_fr * s_fi + a_fi * s_fr + sl_ref[1, jp, rf, :],
                        a_br * s_br - a_bi * s_bi + sl_ref[2, jp, rb, :],
                        a_br * s_bi + a_bi * s_br + sl_ref[3, jp, rb, :]))
        return tuple(new)

    lax.fori_loop(0, n_chunks, scan, tuple((zero,) * 4 for _ in range(n_pairs)))

    nt = (((1,), (1,)), ((), ()))
    for jp in range(n_pairs):
        st = jnp.concatenate(
            [jnp.concatenate([st_ref[k, jp, pl.ds(b, n_chunks, stride=bsz), :] for b in range(bsz)], axis=0)
             for k in range(4)], axis=1)
        for e in range(2):
            g = 2 * jp + e
            st_g = jnp.where(slot == e, st, 0.0).astype(BF16)
            y = (jnp.dot(u_ref[g], mi_ref[g], preferred_element_type=F32)
                 + lax.dot_general(st_g, mo_ref[g], nt, preferred_element_type=F32))
            yg_ref[g] = jax.nn.gelu(y)

    def relayout_out(r, carry):
        r0 = pl.multiple_of(r * rc, rc)
        for th in range(2):
            src = [yg_ref[gi, pl.ds(r0, rc), th * 128:(th + 1) * 128] for gi in range(S5_SLAB)]
            for t8, out in enumerate(_seg_transpose(src, seg)):
                y_ref[pl.ds(r0, rc), th * 8 + t8, :] = out
        return carry

    lax.fori_loop(0, rows // rc, relayout_out, 0, unroll=2)


def _s5_glu_kernel(g_ref, w_ref, b_ref, z_ref, o_ref):
    g = g_ref[...]
    acc = jnp.dot(g.astype(BF16), w_ref[...], preferred_element_type=F32) + b_ref[...]
    o_ref[...] = (g * _sigmoid(acc) * _silu(z_ref[...])).astype(o_ref.dtype)


def _s5_branch(proj, bsz, seq, tiled, w_glu, b_glu):
    m_state, m_intra, m_out, a_re, a_im = _s5_mats(tiled)
    t_n = S5_CHUNK
    rows = bsz * (seq // t_n)
    m = bsz * seq
    gb = S5_SLAB
    x3 = proj.reshape(rows, t_n, proj.shape[1])
    mat = lambda r, w: pl.BlockSpec((gb, r, w), lambda s: (s, 0, 0))
    io_spec = pl.BlockSpec((rows, t_n, 128), lambda s: (0, 0, s + COL_AX // 128))
    y = pl.pallas_call(
        functools.partial(_s5_main_kernel, bsz=bsz),
        out_shape=jax.ShapeDtypeStruct((rows, t_n, BRANCH_W), F32),
        grid=(S5_GROUPS // gb,),
        in_specs=[io_spec, mat(S5_ROW, S5_LANES), mat(S5_ROW, S5_ROW), mat(S5_ROW, S5_LANES),
                  mat(1, S5_LANES), mat(1, S5_LANES)],
        out_specs=pl.BlockSpec((rows, t_n, 128), lambda s: (0, 0, s)),
        scratch_shapes=[pltpu.VMEM((gb, rows, S5_ROW), BF16),
                        pltpu.VMEM((4, gb // 2, rows, 128), F32),
                        pltpu.VMEM((4, gb // 2, rows, 128), F32),
                        pltpu.VMEM((gb, rows, S5_ROW), F32)],
        compiler_params=_params("parallel"),
        name="s5_main",
    )(x3, m_state, m_intra, m_out, a_re, a_im)
    y = y.reshape(m, BRANCH_W)
    tm = min(512, m)
    return pl.pallas_call(
        _s5_glu_kernel,
        out_shape=jax.ShapeDtypeStruct((m, BRANCH_W), BF16),
        grid=(m // tm,),
        in_specs=[pl.BlockSpec((tm, BRANCH_W), lambda i: (i, 0)),
                  pl.BlockSpec((BRANCH_W, BRANCH_W), lambda i: (0, 0)),
                  pl.BlockSpec((1, BRANCH_W), lambda i: (0, 0)),
                  pl.BlockSpec((tm, BRANCH_W), lambda i: (i, COL_AZ // BRANCH_W))],
        out_specs=pl.BlockSpec((tm, BRANCH_W), lambda i: (i, 0)),
        compiler_params=_params("parallel"),
        name="s5_glu",
    )(y, w_glu.astype(BF16), b_glu.astype(F32).reshape(1, BRANCH_W), proj)


def _dft_tables(seq):
    n = 2 * seq
    mm = np.arange(seq, dtype=np.int64)
    k1 = np.arange(seq // DFT_ROWS, dtype=np.int64)[:, None] * DFT_ROWS
    k0 = np.arange(DFT_ROWS, dtype=np.int64)[:, None]
    ang_a = 2.0 * np.pi * ((k1 * mm) % n).astype(np.float64) / n
    ang_b = 2.0 * np.pi * ((k0 * mm) % n).astype(np.float64) / n
    return tuple(jnp.asarray(t, F32) for t in (np.cos(ang_a), np.sin(ang_a), np.cos(ang_b), np.sin(ang_b)))


def _dft_gen_kernel(ac_ref, as_ref, bc_ref, bs_ref, c_ref, s1_ref, s2_ref):
    i = pl.program_id(0)
    a_c = ac_ref[pl.ds(i, 1), :]
    a_s = as_ref[pl.ds(i, 1), :]
    b_c, b_s = bc_ref[...], bs_ref[...]
    cos_t = a_c * b_c - a_s * b_s
    sin_t = a_s * b_c + a_c * b_s
    rows = lax.broadcasted_iota(jnp.int32, cos_t.shape, 0) + i * DFT_ROWS
    cols = lax.broadcasted_iota(jnp.int32, cos_t.shape, 1)
    alt_cols = jnp.where((cols & 1) == 0, 1.0, -1.0).astype(F32)
    alt_rows = jnp.where((rows & 1) == 0, 1.0, -1.0).astype(F32)
    c_ref[...] = cos_t.astype(BF16)
    s1_ref[...] = jnp.where(rows == 0, alt_cols, sin_t).astype(BF16)
    s2_ref[...] = jnp.where(cols == 0, alt_rows, sin_t).astype(BF16)


def _dft_matrices(seq):
    tabs = _dft_tables(seq)
    n_steps = seq // DFT_ROWS
    tab_spec = pl.BlockSpec(tabs[0].shape, lambda i: (0, 0))
    b_spec = pl.BlockSpec((DFT_ROWS, seq), lambda i: (0, 0))
    o_spec = pl.BlockSpec((DFT_ROWS, seq), lambda i: (i, 0))
    return pl.pallas_call(
        _dft_gen_kernel,
        out_shape=(jax.ShapeDtypeStruct((seq, seq), BF16),) * 3,
        grid=(n_steps,),
        in_specs=[tab_spec, tab_spec, b_spec, b_spec],
        out_specs=(o_spec,) * 3,
        compiler_params=_params("parallel"),
        name="dft_gen",
    )(*tabs)


def _hy_filter_kernel(feat_ref, w1_ref, b1_ref, w2_ref, b2_ref, f0_ref, f1_ref,
                      w3p_ref, w3n_ref, b3p_ref, b3n_ref, dl_ref, t_ref,
                      hs_ref, hd_ref, hp_ref, hn_ref, r0_ref, h_ref, split_ref, *, inv_n):
    hi = lax.Precision.HIGHEST

    @pl.when(pl.program_id(0) == 0)
    def _():
        h1 = jnp.sin(f0_ref[...] * (jnp.dot(feat_ref[...], w1_ref[...], precision=hi,
                                            preferred_element_type=F32) + b1_ref[...]))
        h_ref[...] = jnp.sin(f1_ref[...] * (jnp.dot(h1, w2_ref[...], precision=hi,
                                                    preferred_element_type=F32) + b2_ref[...]))

    h = h_ref[...].astype(BF16)
    win = jnp.exp(-t_ref[...] * jnp.abs(dl_ref[...]))
    hpos = (jnp.dot(h, w3p_ref[...].astype(BF16), preferred_element_type=F32) + b3p_ref[...]) * win
    hneg = (jnp.dot(h, w3n_ref[...].astype(BF16), preferred_element_type=F32) + b3n_ref[...]) * win
    rows = lax.broadcasted_iota(jnp.int32, hpos.shape, 0)
    hneg = jnp.where(rows == 0, 0.0, hneg)
    norm = (jnp.sum(jnp.abs(hpos), axis=0, keepdims=True)
            + jnp.sum(jnp.abs(hneg), axis=0, keepdims=True))
    hpos = hpos / norm
    hneg = hneg / norm
    hsum = hpos + hneg
    hdiff = hpos - hneg
    even = (rows & 1) == 0
    alt2 = jnp.where(((rows >> 1) & 1) == 0, 1.0, -1.0).astype(F32)
    col_sum = lambda a: jnp.sum(a, axis=0, keepdims=True)
    a0 = 2.0 * col_sum(jnp.where(even, hsum, 0.0))
    d0 = 2.0 * col_sum(jnp.where(even, 0.0, hsum))
    hr2 = 2.0 * col_sum(jnp.where(even, alt2 * hsum, 0.0))
    hi2 = -2.0 * col_sum(jnp.where(even, 0.0, alt2 * hdiff))
    r0_ref[...] = jnp.concatenate([a0, d0, hr2, hi2, jnp.zeros((4, a0.shape[1]), F32)], axis=0) * inv_n
    def lags(x, par):
        _stage_rows(split_ref, x)
        return _rows_of_parity(split_ref, par).astype(BF16)

    hs_ref[...] = lags(hsum, 0)
    hd_ref[...] = lags(hdiff, 0)
    hp_ref[...] = lags(hpos, 1)
    hn_ref[...] = lags(hneg, 1)


def _hy_filter_taps(seq, w1, b1, w2, b2, freq, w3, b3):
    n_ch = 2 * BRANCH_W
    bands = (HY_EMB - 1) // 2
    t = jnp.linspace(0.0, 1.0, seq, dtype=F32)[:, None]
    w = 2.0 * math.pi * jnp.arange(seq, dtype=F32)[:, None] / seq
    f = jnp.linspace(1e-4, bands - 1, bands, dtype=F32)[None, :]
    feats = jnp.concatenate([t, jnp.cos(f * w), -jnp.sin(f * w),
                             jnp.zeros((seq, HY_PAD - HY_EMB), F32)], axis=-1)
    deltas = jnp.linspace(math.log(HY_DECAY_TARGET) / HY_SLOW_DECAY,
                          math.log(HY_DECAY_TARGET) / HY_FAST_DECAY, n_ch, dtype=F32)[None, :]

    def pad2(a, r, c):
        a = a.astype(F32)
        return jnp.pad(a, ((0, r - a.shape[0]), (0, c - a.shape[1])))

    w1p = pad2(w1, HY_PAD, HY_PAD)
    w2p = pad2(w2, HY_PAD, HY_PAD)
    b1p = pad2(b1[None], 1, HY_PAD)
    b2p = pad2(b2[None], 1, HY_PAD)
    f0p = pad2(freq[0][None], 1, HY_PAD)
    f1p = pad2(freq[1][None], 1, HY_PAD)
    w3p = pad2(w3, HY_PAD, 2 * n_ch)
    b3r = b3.astype(F32)[None]
    tn = 256
    nt = n_ch // tn
    full = lambda shape: pl.BlockSpec(shape, lambda j: (0, 0))
    tap_spec = pl.BlockSpec((seq // 2, tn), lambda j: (0, j))
    return pl.pallas_call(
        functools.partial(_hy_filter_kernel, inv_n=1.0 / (2 * seq)),
        out_shape=(jax.ShapeDtypeStruct((seq // 2, n_ch), BF16),) * 4
                  + (jax.ShapeDtypeStruct((8, n_ch), F32),),
        grid=(nt,),
        in_specs=[full((seq, HY_PAD)), full((HY_PAD, HY_PAD)), full((1, HY_PAD)),
                  full((HY_PAD, HY_PAD)), full((1, HY_PAD)), full((1, HY_PAD)), full((1, HY_PAD)),
                  pl.BlockSpec((HY_PAD, tn), lambda j: (0, j)),
                  pl.BlockSpec((HY_PAD, tn), lambda j: (0, j + nt)),
                  pl.BlockSpec((1, tn), lambda j: (0, j)),
                  pl.BlockSpec((1, tn), lambda j: (0, j + nt)),
                  pl.BlockSpec((1, tn), lambda j: (0, j)),
                  full((seq, 1))],
        out_specs=(tap_spec,) * 4 + (pl.BlockSpec((8, tn), lambda j: (0, j)),),
        scratch_shapes=[pltpu.VMEM((seq, HY_PAD), F32), pltpu.VMEM((tn // 128, seq, 128), F32)],
        compiler_params=_params("arbitrary"),
        name="hy_filter",
    )(feats, w1p, b1p, w2p, b2p, f0p, f1p, w3p, w3p, b3r, b3r, deltas, t)


def _hy_spectrum_kernel(c_ref, s_ref, hse_ref, hde_ref, hpo_ref, hno_ref,
                        ac_ref, as_ref, bc_ref, bs_ref, gc_ref, gs_ref, *, n_half):
    i = pl.program_id(1)
    dot = lambda w, h: jnp.dot(w[...], h[...], preferred_element_type=F32)
    hec, hes = dot(c_ref, hse_ref), dot(s_ref, hde_ref)
    upc, ups = dot(c_ref, hpo_ref), dot(s_ref, hpo_ref)
    umc, ums = dot(c_ref, hno_ref), dot(s_ref, hno_ref)
    tm = hec.shape[0]
    k = (lax.broadcasted_iota(jnp.int32, (tm, 128), 0) + i * tm).astype(F32)
    psi = k * (math.pi / n_half)
    reps = hec.shape[1] // 128
    cp = jnp.concatenate([jnp.cos(psi)] * reps, axis=1)
    sp = jnp.concatenate([jnp.sin(psi)] * reps, axis=1)
    w = 1.0 / n_half
    ac_ref[...] = w * hec
    as_ref[...] = w * hes
    bc_ref[...] = w * (cp * upc - sp * ups + umc)
    bs_ref[...] = w * (cp * ups + sp * upc - ums)
    gc_ref[...] = w * (upc + cp * umc - sp * ums)
    gs_ref[...] = w * (ups - cp * ums - sp * umc)


def _hy_spectrum(cm, s1, taps):
    half, n_ch = taps[0].shape
    tm = min(512, half)
    tn = 512
    w_spec = pl.BlockSpec((tm, half), lambda j, i: (i, 0))
    tap_spec = pl.BlockSpec((half, tn), lambda j, i: (0, j))
    o_spec = pl.BlockSpec((tm, tn), lambda j, i: (i, j))
    return pl.pallas_call(
        functools.partial(_hy_spectrum_kernel, n_half=half),
        out_shape=(jax.ShapeDtypeStruct((half, n_ch), F32),) * 6,
        grid=(n_ch // tn, half // tm),
        in_specs=[w_spec, w_spec] + [tap_spec] * 4,
        out_specs=(o_spec,) * 6,
        compiler_params=_params("parallel", "arbitrary"),
        name="hy_spectrum",
    )(cm, s1, *taps)


def _stage_rows(scr_ref, x):
    for c in range(x.shape[1] // 128):
        scr_ref[c] = x[:, c * 128:(c + 1) * 128]


def _rows_of_parity(scr_ref, par):
    n = scr_ref.shape[1] // 2
    return jnp.concatenate([scr_ref[c, pl.ds(par, n, stride=2), :] for c in range(scr_ref.shape[0])], axis=1)


def _hy_conv3_kernel(u_ref, w_ref, b_ref, o_ref, *rest):
    vb_ref, s_ref = rest if len(rest) == 2 else (None, rest[0])
    u = u_ref[...]
    n = u.shape[0]
    rows = lax.broadcasted_iota(jnp.int32, u.shape, 0)
    prev = jnp.where(rows == 0, 0.0, pltpu.roll(u, 1, 0))
    nxt = jnp.where(rows == n - 1, 0.0, pltpu.roll(u, n - 1, 0))
    w = w_ref[...]
    _stage_rows(s_ref, prev * w[0:1] + u * w[1:2] + nxt * w[2:3] + b_ref[...])
    for par in range(2):
        part = _rows_of_parity(s_ref, par)
        o_ref[par] = part
        if vb_ref is not None:
            vb_ref[par] = part.astype(BF16)


def _hy_conv3(proj, bsz, seq, conv_w, conv_b, col, width, with_bf16):
    half = seq // 2
    tn = 256
    o_spec = pl.BlockSpec((2, half, tn), lambda b, j: (0, b, j))
    shape = lambda dt: jax.ShapeDtypeStruct((2, bsz * half, width), dt)
    return pl.pallas_call(
        _hy_conv3_kernel,
        out_shape=(shape(F32), shape(BF16)) if with_bf16 else shape(F32),
        grid=(bsz, width // tn),
        in_specs=[pl.BlockSpec((seq, tn), lambda b, j: (b, j + (COL_BU + col) // tn)),
                  pl.BlockSpec((3, tn), lambda b, j: (0, j + col // tn)),
                  pl.BlockSpec((1, tn), lambda b, j: (0, j + col // tn))],
        out_specs=(o_spec, o_spec) if with_bf16 else o_spec,
        scratch_shapes=[pltpu.VMEM((tn // 128, seq, 128), F32)],
        compiler_params=_params("parallel", "parallel"),
        name="hy_conv3",
    )(proj, conv_w.astype(F32), conv_b.astype(F32).reshape(1, -1))


def _hy_fwd_kernel(c_ref, s_ref, ze_ref, zo_ref, ac_ref, as_ref, bc_ref, bs_ref, gc_ref, gs_ref, r0_ref,
                   pc_ref, ps_ref, qc_ref, qs_ref):
    dot = lambda w, z: jnp.dot(w[...], z[...], preferred_element_type=F32)
    ec, es = dot(c_ref, ze_ref), dot(s_ref, ze_ref)
    oc, os_ = dot(c_ref, zo_ref), dot(s_ref, zo_ref)
    a_c, a_s = ac_ref[...], as_ref[...]
    b_c, b_s = bc_ref[...], bs_ref[...]
    g_c, g_s = gc_ref[...], gs_ref[...]
    outs = (ec * a_c - es * a_s + oc * b_c - os_ * b_s,
            ec * a_s + es * a_c + oc * b_s + os_ * b_c,
            ec * g_c - es * g_s + oc * a_c - os_ * a_s,
            ec * g_s + es * g_c + oc * a_s + os_ * a_c)
    r0 = r0_ref[...]
    a0, d0, hr2, hi2 = r0[0:1], r0[1:2], r0[2:3], r0[3:4]
    e0, eh, o0, oh = ec[0:1], es[0:1], oc[0:1], os_[0:1]
    first = (e0 * a0 + o0 * d0, eh * hr2 + oh * hi2, e0 * d0 + o0 * a0, oh * hr2 - eh * hi2)
    top = 16
    is_row0 = (lax.broadcasted_iota(jnp.int32, (top, ec.shape[1]), 0) == 0) & (pl.program_id(2) == 0)
    for ref, val, row0 in zip((pc_ref, ps_ref, qc_ref, qs_ref), outs, first):
        ref[...] = val.astype(BF16)
        ref[0:top, :] = jnp.where(is_row0, row0, val[0:top]).astype(BF16)


HY_TN = 512


def _hy_fwd(cm, s1, z_even, z_odd, tables, r0, h_col, bsz, half):
    tm = min(512, half)
    tn = HY_TN
    mt = half // tm
    w_spec = pl.BlockSpec((tm, half), lambda b, j, i: (i, 0))
    h_spec = pl.BlockSpec((tm, tn), lambda b, j, i: (i, j + h_col // tn))
    o_spec = pl.BlockSpec((tm, tn), lambda b, j, i: (b * mt + i, j))
    return pl.pallas_call(
        _hy_fwd_kernel,
        out_shape=(jax.ShapeDtypeStruct((bsz * half, BRANCH_W), BF16),) * 4,
        grid=(bsz, BRANCH_W // tn, mt),
        in_specs=[w_spec, w_spec, z_even[1], z_odd[1]] + [h_spec] * 6
                 + [pl.BlockSpec((8, tn), lambda b, j, i: (0, j + h_col // tn))],
        out_specs=(o_spec,) * 4,
        compiler_params=_params("parallel", "parallel", "arbitrary"),
        name="hy_fwd",
    )(cm, s1, z_even[0], z_odd[0], *tables, r0)


def _hy_inv_convs(c_ref, s_ref, pc_ref, ps_ref, qc_ref, qs_ref):
    dot = lambda w, y: jnp.dot(w[...], y[...], preferred_element_type=F32)
    return dot(c_ref, pc_ref) + dot(s_ref, ps_ref), dot(c_ref, qc_ref) + dot(s_ref, qs_ref)


def _hy_inv_mid_kernel(c_ref, s_ref, pc_ref, ps_ref, qc_ref, qs_ref, ge_ref, go_ref, ze_ref, zo_ref,
                       bias_ref, o_ref, ob_ref):
    convs = _hy_inv_convs(c_ref, s_ref, pc_ref, ps_ref, qc_ref, qs_ref)
    for par, (conv, g_ref, z_ref) in enumerate(zip(convs, (ge_ref, go_ref), (ze_ref, zo_ref))):
        out = g_ref[...] * (conv + bias_ref[...] * z_ref[...])
        o_ref[par] = out
        ob_ref[par] = out.astype(BF16)


def _hy_inv_last_kernel(c_ref, s_ref, pc_ref, ps_ref, qc_ref, qs_ref, ge_ref, go_ref, ze_ref, zo_ref,
                        bias_ref, sz_ref, o_ref, mix_ref):
    convs = _hy_inv_convs(c_ref, s_ref, pc_ref, ps_ref, qc_ref, qs_ref)
    tm = ge_ref.shape[0]
    _stage_rows(mix_ref, _silu(sz_ref[...]))
    outs = [g_ref[...] * (conv + bias_ref[...] * z_ref[...]) * _rows_of_parity(mix_ref, par)
            for par, (conv, g_ref, z_ref) in enumerate(zip(convs, (ge_ref, go_ref), (ze_ref, zo_ref)))]
    for par, out in enumerate(outs):
        for c in range(mix_ref.shape[0]):
            mix_ref[c, pl.ds(par, tm, stride=2), :] = out[:, c * 128:(c + 1) * 128]
    for c in range(mix_ref.shape[0]):
        o_ref[:, c * 128:(c + 1) * 128] = mix_ref[c]


def _hy_inv(cm, s2, spectra, gates, zprev, bias_row, bsz, half, silu=None):
    tm = min(512, half)
    tn = HY_TN
    mt = half // tm
    w_spec = pl.BlockSpec((tm, half), lambda b, j, i: (i, 0))
    y_spec = pl.BlockSpec((half, tn), lambda b, j, i: (b, j))
    pairs = list(gates) + list(zprev)
    in_specs = [w_spec, w_spec] + [y_spec] * 4 + [s for _, s in pairs] + [pl.BlockSpec((1, tn), lambda b, j, i: (0, j))]
    args = [cm, s2, *spectra] + [a for a, _ in pairs] + [bias_row]
    if silu is None:
        o_spec = pl.BlockSpec((2, tm, tn), lambda b, j, i: (0, b * mt + i, j))
        shape = lambda dt: jax.ShapeDtypeStruct((2, bsz * half, BRANCH_W), dt)
        body, out_shape, out_specs = _hy_inv_mid_kernel, (shape(F32), shape(BF16)), (o_spec, o_spec)
        scratch = []
    else:
        silu_arr, silu_col = silu
        body = _hy_inv_last_kernel
        out_shape = jax.ShapeDtypeStruct((2 * bsz * half, BRANCH_W), F32)
        out_specs = pl.BlockSpec((2 * tm, tn), lambda b, j, i: (b * mt + i, j))
        in_specs.append(pl.BlockSpec((2 * tm, tn), lambda b, j, i: (b * mt + i, j + silu_col // tn)))
        args.append(silu_arr)
        scratch = [pltpu.VMEM((tn // 128, 2 * tm, 128), F32)]
    return pl.pallas_call(
        body, out_shape=out_shape, grid=(bsz, BRANCH_W // tn, mt), in_specs=in_specs, out_specs=out_specs,
        scratch_shapes=scratch, compiler_params=_params("parallel", "parallel", "arbitrary"),
        name="hy_inv",
    )(*args)


def _hyena_branch(proj, bsz, seq, dft, conv_w, conv_b, w1, b1, w2, b2, freq, w3, b3, bias):
    cm, s1, s2 = dft
    half = seq // 2
    tm = min(512, half)
    tn = HY_TN
    mt = half // tm
    full = lambda par, col=0: pl.BlockSpec((None, half, tn), lambda b, j, i: (par, b, j + col // tn))
    row = lambda par, col=0: pl.BlockSpec((None, tm, tn), lambda b, j, i: (par, b * mt + i, j + col // tn))
    both = lambda arr, spec, col=0: ((arr, spec(0, col)), (arr, spec(1, col)))
    *taps, r0 = _hy_filter_taps(seq, w1, b1, w2, b2, freq, w3, b3)
    tables = _hy_spectrum(cm, s1, taps)
    v, vb = _hy_conv3(proj, bsz, seq, conv_w, conv_b, 0, BRANCH_W, True)
    x12 = _hy_conv3(proj, bsz, seq, conv_w, conv_b, BRANCH_W, 2 * BRANCH_W, False)
    bias = bias.astype(F32)
    spectra = _hy_fwd(cm, s1, *both(vb, full), tables, r0, 0, bsz, half)
    z1, z1b = _hy_inv(cm, s2, spectra, gates=both(x12, row), zprev=both(v, row),
                      bias_row=bias[0:1], bsz=bsz, half=half)
    spectra = _hy_fwd(cm, s1, *both(z1b, full), tables, r0, BRANCH_W, bsz, half)
    return _hy_inv(cm, s2, spectra, gates=both(x12, row, BRANCH_W), zprev=both(z1, row),
                   bias_row=bias[1:2], bsz=bsz, half=half, silu=(proj, COL_BZ))


def _rope_table_kernel(pos_ref, inv_ref, cos_ref, sin_ref):
    ang = pos_ref[...] * inv_ref[...]
    lane = lax.broadcasted_iota(jnp.int32, ang.shape, 1)
    live = lane < MLA_ROPE
    cos_ref[...] = jnp.where(live, jnp.cos(ang), 0.0)
    sin_ref[...] = jnp.where(live, jnp.where(lane < MLA_ROPE // 2, -1.0, 1.0) * jnp.sin(ang), 0.0)


def _rope_tables(positions):
    m = positions.size
    half = MLA_ROPE // 2
    inv = ROPE_BASE ** (-jnp.arange(half, dtype=F32) / half)
    inv = jnp.concatenate([inv, inv, jnp.zeros((128 - MLA_ROPE,), F32)])[None]
    pos = positions.astype(F32).reshape(m, 1)
    tm = min(1024, m)
    spec = pl.BlockSpec((tm, 128), lambda i: (i, 0))
    return pl.pallas_call(
        _rope_table_kernel,
        out_shape=(jax.ShapeDtypeStruct((m, 128), F32),) * 2,
        grid=(m // tm,),
        in_specs=[pl.BlockSpec((tm, 1), lambda i: (i, 0)), pl.BlockSpec((1, 128), lambda i: (0, 0))],
        out_specs=(spec, spec),
        compiler_params=_params("parallel"),
        name="rope_table",
    )(pos, inv)


def _rope128(x, cos_t, sin_t):
    lane = lax.broadcasted_iota(jnp.int32, x.shape, 1)
    half = MLA_ROPE // 2
    partner = jnp.where(lane < half, pltpu.roll(x, 128 - half, 1), pltpu.roll(x, half, 1))
    return x * cos_t + partner * sin_t


def _rms(x, g):
    ms = jnp.mean(jnp.square(x), axis=-1, keepdims=True)
    return x * lax.rsqrt(ms + RMS_EPS) * g


def _mla_q_kernel(cq_ref, g_ref, w_ref, cos_ref, sin_ref, q_ref, *, scale):
    xn = _rms(cq_ref[...], g_ref[...]).astype(BF16)
    q = jnp.dot(xn, w_ref[...], preferred_element_type=F32) * scale
    cos_t, sin_t = cos_ref[...], sin_ref[...]
    for h in range(MLA_HEADS):
        base = h * MLA_QK_PAD
        q_ref[:, base:base + MLA_NOPE] = q[:, base:base + MLA_NOPE].astype(BF16)
        q_ref[:, base + MLA_NOPE:base + MLA_QK_PAD] = _rope128(
            q[:, base + MLA_NOPE:base + MLA_QK_PAD], cos_t, sin_t).astype(BF16)


def _mla_kv_kernel(ckv_ref, g_ref, wk_ref, wv_ref, kr_ref, cos_ref, sin_ref, k_ref, v_ref):
    xn = _rms(ckv_ref[...], g_ref[...]).astype(BF16)
    kn = jnp.dot(xn, wk_ref[...], preferred_element_type=F32)
    v_ref[...] = jnp.dot(xn, wv_ref[...], preferred_element_type=F32).astype(BF16)
    kr = _rope128(kr_ref[:, 0:128], cos_ref[...], sin_ref[...]).astype(BF16)
    for h in range(MLA_HEADS):
        base = h * MLA_QK_PAD
        k_ref[:, base:base + MLA_NOPE] = kn[:, h * MLA_NOPE:(h + 1) * MLA_NOPE].astype(BF16)
        k_ref[:, base + MLA_NOPE:base + MLA_QK_PAD] = kr


def _mla_attn_kernel(q_ref, k_ref, v_ref, z_ref, o_ref):
    k, v = k_ref[...], v_ref[...]
    half = q_ref.shape[0] // 2
    for r in range(2):
        rows = pl.ds(r * half, half)
        s = lax.dot_general(q_ref[rows, :], k, (((1,), (1,)), ((), ())),
                            preferred_element_type=F32)
        p = jnp.exp2(s - jnp.max(s, axis=-1, keepdims=True))
        l = jnp.sum(p, axis=-1, keepdims=True)
        o = jnp.dot(p.astype(BF16), v, preferred_element_type=F32)
        o_ref[rows, :] = (o / l * _silu(z_ref[rows, :])).astype(BF16)


def _mla_branch(proj, bsz, seq, rope, q_norm_g, w_uq, kv_norm_g, w_ukv):
    m = bsz * seq
    cos_t, sin_t = rope
    dqk = MLA_NOPE + MLA_ROPE
    hq = MLA_HEADS * MLA_QK_PAD
    w_q = w_uq.reshape(MLA_LORA, MLA_HEADS, dqk)
    w_q = jnp.pad(w_q, ((0, 0), (0, 0), (0, MLA_QK_PAD - dqk))).reshape(MLA_LORA, hq).astype(BF16)
    w_kv = w_ukv.reshape(MLA_LORA, MLA_HEADS, MLA_NOPE + MLA_V)
    w_k = w_kv[:, :, :MLA_NOPE].reshape(MLA_LORA, MLA_HEADS * MLA_NOPE).astype(BF16)
    w_v = w_kv[:, :, MLA_NOPE:].reshape(MLA_LORA, MLA_HEADS * MLA_V).astype(BF16)
    tm = min(1024, m)
    row = lambda shape, col=0: pl.BlockSpec(shape, lambda i: (i, col))
    full = lambda shape: pl.BlockSpec(shape, lambda i: (0, 0))
    qp = pl.pallas_call(
        functools.partial(_mla_q_kernel, scale=dqk ** -0.5 * math.log2(math.e)),
        out_shape=jax.ShapeDtypeStruct((m, hq), BF16),
        grid=(m // tm,),
        in_specs=[row((tm, MLA_LORA), COL_CQ // MLA_LORA), full((1, MLA_LORA)), full((MLA_LORA, hq)),
                  row((tm, 128)), row((tm, 128))],
        out_specs=row((tm, hq)),
        compiler_params=_params("parallel"),
        name="mla_q",
    )(proj, q_norm_g.astype(F32).reshape(1, -1), w_q, cos_t, sin_t)
    kp, vp = pl.pallas_call(
        _mla_kv_kernel,
        out_shape=(jax.ShapeDtypeStruct((m, hq), BF16),
                   jax.ShapeDtypeStruct((m, MLA_HEADS * MLA_V), BF16)),
        grid=(m // tm,),
        in_specs=[row((tm, MLA_LORA), COL_CKV // MLA_LORA), full((1, MLA_LORA)),
                  full((MLA_LORA, MLA_HEADS * MLA_NOPE)), full((MLA_LORA, MLA_HEADS * MLA_V)),
                  row((tm, 512), COL_CKR // 512), row((tm, 128)), row((tm, 128))],
        out_specs=(row((tm, hq)), row((tm, MLA_HEADS * MLA_V))),
        compiler_params=_params("parallel"),
        name="mla_kv",
    )(proj, kv_norm_g.astype(F32).reshape(1, -1), w_k, w_v, proj, cos_t, sin_t)
    tq = min(512, seq)
    qt = seq // tq
    return pl.pallas_call(
        _mla_attn_kernel,
        out_shape=jax.ShapeDtypeStruct((m, MLA_HEADS * MLA_V), BF16),
        grid=(bsz, MLA_HEADS, qt),
        in_specs=[pl.BlockSpec((tq, MLA_QK_PAD), lambda b, h, i: (b * qt + i, h)),
                  pl.BlockSpec((seq, MLA_QK_PAD), lambda b, h, i: (b, h)),
                  pl.BlockSpec((seq, MLA_V), lambda b, h, i: (b, h)),
                  pl.BlockSpec((tq, MLA_V), lambda b, h, i: (b * qt + i, h + COL_CZ // MLA_V))],
        out_specs=pl.BlockSpec((tq, MLA_V), lambda b, h, i: (b * qt + i, h)),
        compiler_params=_params("parallel", "parallel", "arbitrary"),
        name="mla_attn",
    )(qp, kp, vp, proj)


def _lift_kernel(ya_ref, yb_ref, yc_ref, w_ref, ga_ref, gb_ref, gc_ref, o_ref, wb_ref):
    @pl.when(pl.program_id(1) == 0)
    def _():
        wb_ref[...] = w_ref[...].astype(BF16)

    acc = _sigmoid(ga_ref[...]) * jnp.dot(ya_ref[...], wb_ref[0], preferred_element_type=F32)
    acc += _sigmoid(gb_ref[...]) * jnp.dot(yb_ref[...].astype(BF16), wb_ref[1], preferred_element_type=F32)
    acc += _sigmoid(gc_ref[...]) * jnp.dot(yc_ref[...], wb_ref[2], preferred_element_type=F32)
    o_ref[...] = acc.astype(BF16)


def _lift(ya, yb, yc, w_lift, layer, proj):
    m = ya.shape[0]
    tm = min(512, m)
    tn = 1024
    y_spec = pl.BlockSpec((tm, BRANCH_W), lambda j, i: (i, 0))
    gate = lambda n: pl.BlockSpec((tm, tn), lambda j, i: (i, j + (COL_GATE + n * D_MODEL) // tn))
    return pl.pallas_call(
        _lift_kernel,
        out_shape=jax.ShapeDtypeStruct((m, D_MODEL), BF16),
        grid=(D_MODEL // tn, m // tm),
        in_specs=[y_spec, y_spec, y_spec,
                  pl.BlockSpec((None, N_BRANCH, BRANCH_W, tn), lambda j, i: (layer, 0, 0, j),
                               pipeline_mode=pl.Buffered(1)),
                  gate(0), gate(1), gate(2)],
        out_specs=pl.BlockSpec((tm, tn), lambda j, i: (i, j)),
        scratch_shapes=[pltpu.VMEM((N_BRANCH, BRANCH_W, tn), BF16)],
        compiler_params=_params("parallel", "arbitrary"),
        name="lift",
    )(ya, yb, yc, w_lift, proj, proj, proj)


def _out_kernel(mix_ref, wo_ref, p_ref, wp_ref, sp_ref, x_ref, g_ref, b_ref, o_ref):
    mixed = jnp.dot(mix_ref[...], wo_ref[...], preferred_element_type=F32)
    ple = jnp.dot(p_ref[...].astype(BF16), wp_ref[...], preferred_element_type=F32) * _sigmoid(sp_ref[...])
    r = DEEPNORM_ALPHA * x_ref[...] + mixed + ple
    mu = jnp.mean(r, axis=-1, keepdims=True)
    var = jnp.mean(jnp.square(r - mu), axis=-1, keepdims=True)
    o_ref[...] = (r - mu) * lax.rsqrt(var + LN_EPS) * g_ref[...] + b_ref[...]


def _out_norm(mix, w_out, p, w_ple, proj, x, ln_g, ln_b):
    m = mix.shape[0]
    tm = min(512, m)
    row = lambda w, col=0: pl.BlockSpec((tm, w), lambda i: (i, col))
    full = lambda shape: pl.BlockSpec(shape, lambda i: (0, 0))
    return pl.pallas_call(
        _out_kernel,
        out_shape=jax.ShapeDtypeStruct((m, D_MODEL), F32),
        grid=(m // tm,),
        in_specs=[row(D_MODEL), full((D_MODEL, D_MODEL)), row(PLE_DIM), full((PLE_DIM, D_MODEL)),
                  row(D_MODEL, COL_PLE // D_MODEL), row(D_MODEL), full((1, D_MODEL)), full((1, D_MODEL))],
        out_specs=row(D_MODEL),
        compiler_params=_params("parallel"),
        name="out_norm",
    )(mix, w_out.astype(BF16), p, w_ple.astype(BF16), proj, x,
      ln_g.astype(F32).reshape(1, -1), ln_b.astype(F32).reshape(1, -1))


def kernel(x, p, positions, w_in, s5_lambda_re, s5_lambda_im, s5_log_dt, s5_b_re, s5_b_im, s5_c_re, s5_c_im, s5_d, s5_w_glu, s5_b_glu, hy_conv_w, hy_conv_b, hy_w1, hy_b1, hy_w2, hy_b2, hy_freq, hy_w3, hy_b3, hy_bias, mla_q_norm, mla_w_uq, mla_kv_norm, mla_w_ukv, w_lift, w_out, w_ple, ln_g, ln_b):
    bsz, seq, _ = x.shape
    m = bsz * seq
    depth = w_in.shape[0]
    dft = _dft_matrices(seq // 2)
    rope = _rope_tables(positions)
    s5_tiled = _s5_tiled_params(s5_lambda_re, s5_lambda_im, s5_log_dt, s5_b_re, s5_b_im,
                                s5_c_re, s5_c_im, s5_d)
    xf = x.reshape(m, D_MODEL).astype(F32)
    w_t = jnp.swapaxes(w_in, 1, 2)
    for i in range(depth):
        proj = _proj_all(xf, w_t, i)
        y_a = _s5_branch(proj, bsz, seq, [a[i] for a in s5_tiled], s5_w_glu[i], s5_b_glu[i])
        y_b = _hyena_branch(proj, bsz, seq, dft, hy_conv_w[i], hy_conv_b[i], hy_w1[i], hy_b1[i],
                            hy_w2[i], hy_b2[i], hy_freq[i], hy_w3[i], hy_b3[i], hy_bias[i])
        y_c = _mla_branch(proj, bsz, seq, rope, mla_q_norm[i], mla_w_uq[i], mla_kv_norm[i], mla_w_ukv[i])
        mix = _lift(y_a, y_b, y_c, w_lift, i, proj)
        xf = _out_norm(mix, w_out[i], p[i].reshape(m, PLE_DIM), w_ple[i], proj, xf, ln_g[i], ln_b[i])
    return xf.reshape(bsz, seq, D_MODEL).astype(x.dtype)
```
